```python
import math, functools
import jax, jax.numpy as jnp
from jax import lax
import numpy as np

D_MODEL = 1024
BATCH = 8
SEQ = 8192
DEPTH = 2

GRID_W = 64
CTX_LEN = 256
NORM_EPS = 1e-6
D_FF = 4 * D_MODEL
N_MOD = 6

SSD_HEADDIM = 64
SSD_HEADS = 16
SSD_INNER = SSD_HEADS * SSD_HEADDIM
SSD_GROUPS = 4
SSD_STATE = 128
SSD_CHUNK = 128
SSD_XBC = SSD_INNER + 2 * SSD_GROUPS * SSD_STATE
CONV_K = 3
GLA_HEADS = 8
GLA_DK = 64
GLA_DV = 128
GLA_KEY = GLA_HEADS * GLA_DK
GLA_VAL = GLA_HEADS * GLA_DV
GLA_GATE_RANK = 16
GLA_GATE_NORM = 16.0
HGRN_HEADS = 8
HGRN_DK = 128
HGRN_DV = 128
HGRN_WIDTH = HGRN_HEADS * HGRN_DV
S5_GROUP = 16
S5_GROUPS = 24
S5_WIDTH = S5_GROUPS * S5_GROUP
S5_STATE = 64
LIN_CHUNK = 64

EVEN_SPLITS = (SSD_INNER, SSD_XBC, 2 * SSD_HEADS, GLA_KEY, GLA_KEY, GLA_VAL, 2 * GLA_GATE_RANK, GLA_VAL)
ODD_SPLITS = (HGRN_WIDTH, HGRN_WIDTH, 2 * HGRN_WIDTH, HGRN_WIDTH, S5_WIDTH)
EVEN_IN = sum(EVEN_SPLITS)
ODD_IN = sum(ODD_SPLITS)
EVEN_MIX = SSD_INNER + GLA_VAL
ODD_MIX = HGRN_WIDTH + S5_WIDTH

kernel_name = 'hybrid_ssd_gla_hgrn2_s5_dit'


def _split(a, sizes):
    idx = np.cumsum(sizes)[:-1].tolist()
    return jnp.split(a, idx, axis=-1)


def rmsnorm(x, w):
    xf = x.astype(jnp.float32)
    y = xf * lax.rsqrt(jnp.mean(xf * xf, axis=-1, keepdims=True) + NORM_EPS)
    return (y * w.astype(jnp.float32)).astype(x.dtype)


def modulate(h, shift, scale):
    return h * (1.0 + scale) + shift


def sq_relu_mlp(h, w1, w2):
    return jnp.square(jax.nn.relu(h @ w1)) @ w2


def conv_grid(u, w, b, rows):
    bn, t, ch = u.shape
    img = u.reshape(bn, rows, GRID_W, ch)
    out = lax.conv_general_dilated(img, w[:, :, None, :], window_strides=(1, 1), padding='SAME',
                                   dimension_numbers=('NHWC', 'HWIO', 'NHWC'), feature_group_count=ch)
    return out.reshape(bn, t, ch) + b


def conv_seq(u, w, b):
    ch = u.shape[-1]
    out = lax.conv_general_dilated(u, w[:, None, :], window_strides=(1,), padding='SAME',
                                   dimension_numbers=('NWC', 'WIO', 'NWC'), feature_group_count=ch)
    return out + b


def ssd_chunk_scan(xdt, la, bm, cm, s0):
    f32 = jnp.float32
    bn, t, nh, p = xdt.shape
    g, n = bm.shape[-2:]
    r = nh // g
    c = SSD_CHUNK
    nc = t // c
    xc = xdt.astype(f32).reshape(bn, nc, c, g, r, p).transpose(1, 0, 3, 4, 2, 5)
    ac = la.astype(f32).reshape(bn, nc, c, g, r).transpose(1, 0, 3, 4, 2)
    bc = bm.astype(f32).reshape(bn, nc, c, g, n).transpose(1, 0, 3, 2, 4)
    cc = cm.astype(f32).reshape(bn, nc, c, g, n).transpose(1, 0, 3, 2, 4)
    lower = jnp.tril(jnp.ones((c, c), dtype=bool))

    def step(h, inp):
        xi, ai, bi, ci = inp
        acum = jnp.cumsum(ai, axis=-1)
        seg = jnp.where(lower, acum[..., :, None] - acum[..., None, :], -jnp.inf)
        scores = jnp.einsum('bgin,bgjn->bgij', ci, bi)[:, :, None] * jnp.exp(seg)
        y = jnp.einsum('bgrij,bgrjp->bgrip', scores, xi)
        y = y + jnp.einsum('bgin,bgrpn->bgrip', ci, h) * jnp.exp(acum)[..., None]
        xw = xi * jnp.exp(acum[..., -1:] - acum)[..., None]
        h = h * jnp.exp(acum[..., -1])[..., None, None] + jnp.einsum('bgjn,bgrjp->bgrpn', bi, xw)
        return h, y

    h, y = lax.scan(step, s0.astype(f32).reshape(bn, g, r, p, n), (xc, ac, bc, cc))
    y = y.transpose(1, 0, 4, 2, 3, 5).reshape(bn, t, nh, p)
    return y, h.reshape(bn, nh, p, n)


def gla_chunk_scan(q, k, v, lg, s0):
    f32 = jnp.float32
    bn, t, nh, dk = q.shape
    dv = v.shape[-1]
    nc = t // LIN_CHUNK

    def chunks(a):
        return a.astype(f32).reshape(bn, nc, LIN_CHUNK, nh, a.shape[-1]).transpose(1, 0, 3, 2, 4)

    lower = jnp.tril(jnp.ones((LIN_CHUNK, LIN_CHUNK), dtype=bool))

    def step(s, inp):
        qi, ki, vi, gi = inp
        gcum = jnp.cumsum(gi, axis=-2)
        glast = gcum[..., -1:, :]
        q_dec = qi * jnp.exp(gcum)
        k_inv = ki * jnp.exp(-gcum)
        att = jnp.where(lower, jnp.einsum('bhid,bhjd->bhij', q_dec, k_inv), 0.0)
        o = jnp.einsum('bhij,bhjv->bhiv', att, vi) + jnp.einsum('bhid,bhdv->bhiv', q_dec, s)
        k_end = ki * jnp.exp(glast - gcum)
        s = s * jnp.exp(glast)[..., 0, :, None] + jnp.einsum('bhjd,bhjv->bhdv', k_end, vi)
        return s, o

    s, o = lax.scan(step, s0.astype(f32), (chunks(q), chunks(k), chunks(v), chunks(lg)))
    o = o.transpose(1, 0, 3, 2, 4).reshape(bn, t, nh, dv)
    return o, s


def _cplx_affine_combine(e1, e2):
    a1r, a1i, b1r, b1i = e1
    a2r, a2i, b2r, b2i = e2
    ar = a2r * a1r - a2i * a1i
    ai = a2r * a1i + a2i * a1r
    br = a2r * b1r - a2i * b1i + b2r
    bi = a2r * b1i + a2i * b1r + b2i
    return ar, ai, br, bi


def s5_dir_scan(u, state, lam_re, lam_im, bb_re, bb_im, c_re, c_im):
    t = u.shape[1]
    bu_re = jnp.einsum('gpc,btgc->btgp', bb_re, u)
    bu_im = jnp.einsum('gpc,btgc->btgp', bb_im, u)
    ar = jnp.broadcast_to(lam_re, (1, t) + lam_re.shape)
    ai = jnp.broadcast_to(lam_im, (1, t) + lam_im.shape)
    pr, pi, hr, hi = lax.associative_scan(_cplx_affine_combine, (ar, ai, bu_re, bu_im), axis=1)
    s_re, s_im = state
    hr = hr + pr * s_re[:, None] - pi * s_im[:, None]
    hi = hi + pr * s_im[:, None] + pi * s_re[:, None]
    y = jnp.einsum('gcp,btgp->btgc', c_re, hr) - jnp.einsum('gcp,btgp->btgc', c_im, hi)
    return y, (hr[:, -1], hi[:, -1])


def _bidirectional(scan_f, scan_b, ctx_f, lat_f, ctx_b, lat_b, state0):
    oc_f, sc_f = scan_f(*ctx_f, state0)
    ol_f, _ = scan_f(*lat_f, sc_f)
    oc_b, sc_b = scan_b(*[jnp.flip(a, 1) for a in ctx_b], state0)
    ol_b, _ = scan_b(*[jnp.flip(a, 1) for a in lat_b], sc_b)
    return oc_f + jnp.flip(oc_b, 1), ol_f + jnp.flip(ol_b, 1)


def mixer_ssd_gla(hl, hc, w_in, conv_w, conv_b, dt_bias, a_log, d_skip, ssd_norm_w,
                  gate_w, gate_b, gla_norm_w, w_out, need_ctx):
    f32 = jnp.float32
    bn = hl.shape[0]
    rows = hl.shape[1] // GRID_W
    neg_a = -jnp.exp(a_log.astype(f32))

    def project(h, conv):
        t = h.shape[1]
        z, xbc, dt, q, k, v, lr, r = _split(h @ w_in, EVEN_SPLITS)
        xbc = jax.nn.silu(conv(xbc))
        x, bm, cm = _split(xbc, (SSD_INNER, SSD_GROUPS * SSD_STATE, SSD_GROUPS * SSD_STATE))
        x = x.astype(f32).reshape(bn, t, SSD_HEADS, SSD_HEADDIM)
        bm = bm.reshape(bn, t, SSD_GROUPS, SSD_STATE)
        cm = cm.reshape(bn, t, SSD_GROUPS, SSD_STATE)
        dt = jax.nn.softplus(dt.astype(f32).reshape(bn, t, 2, SSD_HEADS) + dt_bias.astype(f32))
        la = dt * neg_a
        xdt = x[:, :, None] * dt[..., None]
        ssd_f = (xdt[:, :, 0], la[:, :, 0], bm, cm)
        ssd_b = (xdt[:, :, 1], la[:, :, 1], bm, cm)
        q = q.reshape(bn, t, GLA_HEADS, GLA_DK) * (GLA_DK ** -0.5)
        k = k.reshape(bn, t, GLA_HEADS, GLA_DK)
        v = v.reshape(bn, t, GLA_HEADS, GLA_DV)
        gk = jnp.einsum('btdr,drk->btdk', lr.reshape(bn, t, 2, GLA_GATE_RANK), gate_w) + gate_b
        gk = (jax.nn.log_sigmoid(gk.astype(f32)) / GLA_GATE_NORM).reshape(bn, t, 2, GLA_HEADS, GLA_DK)
        gla_f = (q, k, v, gk[:, :, 0])
        gla_b = (q, k, v, gk[:, :, 1])
        return z, x, r, ssd_f, ssd_b, gla_f, gla_b

    zl, xl, rl, sfl, sbl, gfl, gbl = project(hl, lambda u: conv_grid(u, conv_w, conv_b, rows))
    zc, xc, rc, sfc, sbc, gfc, gbc = project(hc, lambda u: conv_seq(u, conv_w[CONV_K // 2], conv_b))

    ssd_s0 = jnp.zeros((bn, SSD_HEADS, SSD_HEADDIM, SSD_STATE), f32)
    yc, yl = _bidirectional(ssd_chunk_scan, ssd_chunk_scan, sfc, sfl, sbc, sbl, ssd_s0)
    gla_s0 = jnp.zeros((bn, GLA_HEADS, GLA_DK, GLA_DV), f32)
    oc, ol = _bidirectional(gla_chunk_scan, gla_chunk_scan, gfc, gfl, gbc, gbl, gla_s0)

    def finish(y, x, z, o, r):
        t = z.shape[1]
        y = (y + d_skip.astype(f32)[:, None] * x).reshape(bn, t, SSD_INNER) * jax.nn.silu(z.astype(f32))
        y = rmsnorm(y.reshape(bn, t, SSD_GROUPS, -1), ssd_norm_w.reshape(SSD_GROUPS, -1)).reshape(bn, t, SSD_INNER)
        o = rmsnorm(o, gla_norm_w).reshape(bn, t, GLA_VAL) * jax.nn.silu(r.astype(f32))
        return jnp.concatenate([y, o], axis=-1).astype(hl.dtype) @ w_out

    out_l = finish(yl, xl, zl, ol, rl)
    out_c = finish(yc, xc, zc, oc, rc) if need_ctx else None
    return out_l, out_c


def mixer_hgrn_s5(hl, hc, w_in, lb, hgrn_norm_w, a_re, a_im, log_dt, b_re, b_im, c_re, c_im,
                  d_skip, glu_w, glu_b, w_out, need_ctx):
    f32 = jnp.float32
    bn = hl.shape[0]
    log_lb = jnp.log(lb).reshape(2, HGRN_HEADS, HGRN_DK)
    log_1mlb = jnp.log1p(-lb).reshape(2, HGRN_HEADS, HGRN_DK)
    are, aim = a_re.astype(f32), a_im.astype(f32)
    delta = jnp.exp(log_dt.astype(f32))[..., None]
    mag = jnp.exp(are * delta)
    lbar_re, lbar_im = mag * jnp.cos(aim * delta), mag * jnp.sin(aim * delta)
    den = are * are + aim * aim
    zr = ((lbar_re - 1.0) * are + lbar_im * aim) / den
    zi = (lbar_im * are - (lbar_re - 1.0) * aim) / den
    bre, bim = b_re.astype(f32), b_im.astype(f32)
    bb_re = zr[..., None] * bre - zi[..., None] * bim
    bb_im = zr[..., None] * bim + zi[..., None] * bre
    cre, cim = c_re.astype(f32), c_im.astype(f32)

    def project(h):
        t = h.shape[1]
        q, i, f, g, u = _split(h @ w_in, ODD_SPLITS)
        q = jax.nn.silu(q.astype(f32)).reshape(bn, t, HGRN_HEADS, HGRN_DK)
        v = i.astype(f32).reshape(bn, t, HGRN_HEADS, HGRN_DV)
        f = f.astype(f32).reshape(bn, t, 2, HGRN_HEADS, HGRN_DK)
        log_f = jnp.logaddexp(log_lb, log_1mlb + jax.nn.log_sigmoid(f))
        k = -jnp.expm1(log_f)
        hg_f = (q, k[:, :, 0], v, log_f[:, :, 0])
        hg_b = (q, k[:, :, 1], v, log_f[:, :, 1])
        u = u.astype(f32).reshape(bn, t, S5_GROUPS, S5_GROUP)
        return g, u, hg_f, hg_b

    gl, ul, hfl, hbl = project(hl)
    gc, uc, hfc, hbc = project(hc)

    hg_s0 = jnp.zeros((bn, HGRN_HEADS, HGRN_DK, HGRN_DV), f32)
    oc, ol = _bidirectional(gla_chunk_scan, gla_chunk_scan, hfc, hfl, hbc, hbl, hg_s0)

    s5_f = functools.partial(s5_dir_scan, lam_re=lbar_re[0], lam_im=lbar_im[0], bb_re=bb_re[0],
                             bb_im=bb_im[0], c_re=cre, c_im=cim)
    s5_b = functools.partial(s5_dir_scan, lam_re=lbar_re[1], lam_im=lbar_im[1], bb_re=bb_re[1],
                             bb_im=bb_im[1], c_re=cre, c_im=cim)
    s5_s0 = (jnp.zeros((bn, S5_GROUPS, S5_STATE), f32), jnp.zeros((bn, S5_GROUPS, S5_STATE), f32))
    yc, yl = _bidirectional(s5_f, s5_b, (uc,), (ul,), (uc,), (ul,), s5_s0)
    dsk = d_skip.astype(f32).reshape(S5_GROUPS, S5_GROUP)

    def finish(o, g, y, u):
        t = g.shape[1]
        o = rmsnorm(o, hgrn_norm_w).reshape(bn, t, HGRN_WIDTH) * jax.nn.silu(g.astype(f32))
        y = jax.nn.gelu((y + dsk * u).reshape(bn, t, S5_WIDTH))
        y = y * jax.nn.sigmoid(y @ glu_w.astype(f32) + glu_b.astype(f32))
        return jnp.concatenate([o, y], axis=-1).astype(hl.dtype) @ w_out

    out_l = finish(ol, gl, yl, ul)
    out_c = finish(oc, gc, yc, uc) if need_ctx else None
    return out_l, out_c


def setup_inputs(seed: int = 0) -> dict:
    key = jax.random.key(seed)
    ks = iter(jax.random.split(key, 40))
    f32 = jnp.float32

    def nrm(shape, s):
        return jax.random.normal(next(ks), shape, f32) * s

    def log_uniform(shape, lo, hi):
        return jax.random.uniform(next(ks), shape, f32, minval=math.log(lo), maxval=math.log(hi))

    n_even, n_odd = (DEPTH + 1) // 2, DEPTH // 2
    dt0 = jnp.exp(log_uniform((n_even, 2, SSD_HEADS), 1e-3, 1e-1))
    n_idx = jnp.arange(S5_STATE, dtype=f32)
    return {
        'x': nrm((BATCH, SEQ, D_MODEL), 1.0),
        'c': nrm((BATCH, D_MODEL), 1.0),
        'ctx': nrm((BATCH, CTX_LEN, D_MODEL), 1.0),
        'c_ctx': nrm((D_MODEL,), 1.0),
        'ada_w': nrm((DEPTH, D_MODEL, N_MOD * D_MODEL), 0.5 * D_MODEL ** -0.5),
        'ada_b': nrm((DEPTH, N_MOD * D_MODEL), 0.02),
        'norm1_w': 1.0 + nrm((DEPTH, D_MODEL), 0.02),
        'norm2_w': 1.0 + nrm((DEPTH, D_MODEL), 0.02),
        'ssd_gla_w_in': nrm((n_even, D_MODEL, EVEN_IN), D_MODEL ** -0.5),
        'ssd_conv_w': nrm((n_even, CONV_K, CONV_K, SSD_XBC), 1.0 / CONV_K),
        'ssd_conv_b': nrm((n_even, SSD_XBC), 0.02),
        'ssd_dt_bias': dt0 + jnp.log(-jnp.expm1(-dt0)),
        'ssd_a_log': jnp.log(jax.random.uniform(next(ks), (n_even, 2, SSD_HEADS), f32, minval=1.0, maxval=16.0)),
        'ssd_d': 1.0 + nrm((n_even, SSD_HEADS), 0.1),
        'ssd_norm_w': 1.0 + nrm((n_even, SSD_INNER), 0.02),
        'gla_gate_w': nrm((n_even, 2, GLA_GATE_RANK, GLA_KEY), GLA_GATE_RANK ** -0.5),
        'gla_gate_b': nrm((n_even, 2, GLA_KEY), 0.1),
        'gla_norm_w': 1.0 + nrm((n_even, GLA_DV), 0.02),
        'ssd_gla_w_out': nrm((n_even, EVEN_MIX, D_MODEL), EVEN_MIX ** -0.5),
        'hgrn_s5_w_in': nrm((n_odd, D_MODEL, ODD_IN), D_MODEL ** -0.5),
        'hgrn_lb_logits': nrm((DEPTH, 2, HGRN_WIDTH), 0.1),
        'hgrn_norm_w': 1.0 + nrm((n_odd, HGRN_DV), 0.02),
        's5_a_re': -0.5 + nrm((n_odd, 2, S5_GROUPS, S5_STATE), 0.01),
        's5_a_im': math.pi * n_idx + nrm((n_odd, 2, S5_GROUPS, S5_STATE), 0.01),
        's5_log_dt': log_uniform((n_odd, 2, S5_GROUPS), 1e-3, 1e-1),
        's5_b_re': nrm((n_odd, S5_GROUPS, S5_STATE, S5_GROUP), (2 * S5_GROUP) ** -0.5),
        's5_b_im': nrm((n_odd, S5_GROUPS, S5_STATE, S5_GROUP), (2 * S5_GROUP) ** -0.5),
        's5_c_re': nrm((n_odd, S5_GROUPS, S5_GROUP, S5_STATE), (2 * S5_STATE) ** -0.5),
        's5_c_im': nrm((n_odd, S5_GROUPS, S5_GROUP, S5_STATE), (2 * S5_STATE) ** -0.5),
        's5_d': nrm((n_odd, S5_WIDTH), 1.0),
        's5_glu_w': nrm((n_odd, S5_WIDTH, S5_WIDTH), S5_WIDTH ** -0.5),
        's5_glu_b': nrm((n_odd, S5_WIDTH), 0.02),
        'hgrn_s5_w_out': nrm((n_odd, ODD_MIX, D_MODEL), ODD_MIX ** -0.5),
        'mlp_w1': nrm((DEPTH, D_MODEL, D_FF), D_MODEL ** -0.5),
        'mlp_w2': nrm((DEPTH, D_FF, D_MODEL), 0.5 * D_FF ** -0.5),
        'final_norm_w': 1.0 + nrm((D_MODEL,), 0.02),
    }


def reference(x, c, ctx, c_ctx, ada_w, ada_b, norm1_w, norm2_w,
              ssd_gla_w_in, ssd_conv_w, ssd_conv_b, ssd_dt_bias, ssd_a_log, ssd_d, ssd_norm_w,
              gla_gate_w, gla_gate_b, gla_norm_w, ssd_gla_w_out,
              hgrn_s5_w_in, hgrn_lb_logits, hgrn_norm_w, s5_a_re, s5_a_im, s5_log_dt,
              s5_b_re, s5_b_im, s5_c_re, s5_c_im, s5_d, s5_glu_w, s5_glu_b, hgrn_s5_w_out,
              mlp_w1, mlp_w2, final_norm_w):
    p_lb = jax.nn.softmax(hgrn_lb_logits.astype(jnp.float32), axis=0)
    lb_all = jnp.cumsum(p_lb, axis=0) - p_lb[0]
    xc = ctx
    for layer in range(DEPTH):
        need_ctx = layer < DEPTH - 1
        j = layer // 2
        mod_l = (jax.nn.silu(c) @ ada_w[layer] + ada_b[layer])[:, None, :]
        mod_c = (jax.nn.silu(c_ctx) @ ada_w[layer] + ada_b[layer])[None, None, :]
        sh1l, sc1l, g1l, sh2l, sc2l, g2l = jnp.split(mod_l, N_MOD, axis=-1)
        sh1c, sc1c, g1c, sh2c, sc2c, g2c = jnp.split(mod_c, N_MOD, axis=-1)
        hl = modulate(rmsnorm(x, norm1_w[layer]), sh1l, sc1l)
        hc = modulate(rmsnorm(xc, norm1_w[layer]), sh1c, sc1c)
        if layer % 2 == 0:
            ol, oc = mixer_ssd_gla(hl, hc, ssd_gla_w_in[j], ssd_conv_w[j], ssd_conv_b[j], ssd_dt_bias[j],
                                   ssd_a_log[j], ssd_d[j], ssd_norm_w[j], gla_gate_w[j], gla_gate_b[j],
                                   gla_norm_w[j], ssd_gla_w_out[j], need_ctx)
        else:
            ol, oc = mixer_hgrn_s5(hl, hc, hgrn_s5_w_in[j], lb_all[layer], hgrn_norm_w[j], s5_a_re[j],
                                   s5_a_im[j], s5_log_dt[j], s5_b_re[j], s5_b_im[j], s5_c_re[j], s5_c_im[j],
                                   s5_d[j], s5_glu_w[j], s5_glu_b[j], hgrn_s5_w_out[j], need_ctx)
        x = x + g1l * ol
        x = x + g2l * sq_relu_mlp(modulate(rmsnorm(x, norm2_w[layer]), sh2l, sc2l), mlp_w1[layer], mlp_w2[layer])
        if need_ctx:
            xc = xc + g1c * oc
            xc = xc + g2c * sq_relu_mlp(modulate(rmsnorm(xc, norm2_w[layer]), sh2c, sc2c),
                                        mlp_w1[layer], mlp_w2[layer])
    return rmsnorm(x, final_norm_w)
```

```python
import functools
import math

import jax
import jax.numpy as jnp
from jax import lax
from jax.experimental import pallas as pl
from jax.experimental.pallas import tpu as pltpu

F32 = jnp.float32
BF16 = jnp.bfloat16
HIGHEST = lax.Precision.HIGHEST

GRID_W = 64
NORM_EPS = 1e-6
N_MOD = 6
SSD_HEADDIM = 64
SSD_HEADS = 16
SSD_GROUPS = 4
SSD_STATE = 128
SSD_CHUNK = 128
GLA_HEADS = 8
GLA_DK = 64
GLA_DV = 128
GLA_GATE_RANK = 16
GLA_GATE_NORM = 16.0
HGRN_HEADS = 8
HGRN_DK = 128
HGRN_DV = 128
S5_GROUP = 16
S5_GROUPS = 24
S5_STATE = 64
LIN_CHUNK = 64

SSD_INNER = SSD_HEADS * SSD_HEADDIM
SSD_BC = SSD_GROUPS * SSD_STATE
GLA_KEY = GLA_HEADS * GLA_DK
GLA_VAL = GLA_HEADS * GLA_DV
HGRN_WIDTH = HGRN_HEADS * HGRN_DV
S5_WIDTH = S5_GROUPS * S5_GROUP
S5_NSTATE = S5_GROUPS * S5_STATE

VMEM_LIMIT_BYTES = 56 * 1024 * 1024
LANE = 128
SCAN_BLOCK = 256
S5_CHUNK = 64


def _cparams(n_axes):
    return pltpu.CompilerParams(dimension_semantics=("arbitrary",) * n_axes,
                                vmem_limit_bytes=VMEM_LIMIT_BYTES)


def _largest_divisor(n, multiple, cap):
    best = None
    for d in range(multiple, min(n, cap) + 1, multiple):
        if n % d == 0:
            best = d
    assert best is not None, (n, multiple, cap)
    return best


def _silu(x):
    return x * (1.0 / (1.0 + jnp.exp(-x)))


def _softplus(x):
    return jnp.maximum(x, 0.0) + jnp.log1p(jnp.exp(-jnp.abs(x)))


def _log_sigmoid(x):
    return -_softplus(-x)


def _rms(x, w):
    return x * lax.rsqrt(jnp.mean(x * x, axis=-1, keepdims=True) + NORM_EPS) * w


def _dot(a, b, dims=(((1,), (0,)), ((), ())), precision=None):
    return lax.dot_general(a, b, dims, precision=precision, preferred_element_type=F32)


_NT = (((1,), (1,)), ((), ()))
_TN = (((0,), (0,)), ((), ()))
_TT = (((0,), (1,)), ((), ()))


def _mod_kernel(c_ref, w_ref, b_ref, o_ref):
    a = _silu(c_ref[...]).astype(BF16)
    o_ref[0] = _dot(a, w_ref[0].astype(BF16)) + b_ref[0]


def _modulation(cvec, ada_w, ada_b):
    depth, d, n = ada_w.shape
    rows = cvec.shape[0]
    tn = _largest_divisor(n, LANE, 1024)
    return pl.pallas_call(
        _mod_kernel,
        grid=(depth, n // tn),
        in_specs=[pl.BlockSpec((rows, d), lambda l, j: (0, 0)),
                  pl.BlockSpec((1, d, tn), lambda l, j: (l, 0, j)),
                  pl.BlockSpec((1, 1, tn), lambda l, j: (l, 0, j))],
        out_specs=pl.BlockSpec((1, rows, tn), lambda l, j: (l, 0, j)),
        out_shape=jax.ShapeDtypeStruct((depth, rows, n), F32),
        compiler_params=_cparams(2),
        name="adaln_mod",
    )(cvec, ada_w, ada_b.reshape(depth, 1, n))


def _row_select(i, tm, ctx_len, ctx_val, lat_val):
    row = i * tm + lax.broadcasted_iota(jnp.int32, (tm, 1), 0)
    return jnp.where(row < ctx_len, ctx_val, lat_val)


def _mod_specs(d, cols, n_grid_axes):
    specs = []
    for k in cols:
        if n_grid_axes == 2:
            specs.append(pl.BlockSpec((1, 1, d), lambda b, i, k=k: (b, 0, k)))
            specs.append(pl.BlockSpec((1, d), lambda b, i, k=k: (0, k)))
        else:
            specs.append(pl.BlockSpec((1, 1, d), lambda b, i, j, k=k: (b, 0, k)))
            specs.append(pl.BlockSpec((1, d), lambda b, i, j, k=k: (0, k)))
    return specs


def _proj_kernel(*refs, tm, ctx_len, has_extra):
    if has_extra:
        (x_ref, nw_ref, shl_ref, shc_ref, scl_ref, scc_ref, w_ref, wx_ref, o_ref, ox_ref, h_scr) = refs
    else:
        (x_ref, nw_ref, shl_ref, shc_ref, scl_ref, scc_ref, w_ref, o_ref, h_scr) = refs
    i = pl.program_id(1)
    j = pl.program_id(2)

    @pl.when(j == 0)
    def _():
        y = _rms(x_ref[0], nw_ref[...])
        shift = _row_select(i, tm, ctx_len, shc_ref[...], shl_ref[0])
        scale = _row_select(i, tm, ctx_len, scc_ref[...], scl_ref[0])
        h = (y * (1.0 + scale) + shift).astype(BF16)
        h_scr[...] = h
        if has_extra:
            ox_ref[...] = _dot(h, wx_ref[...])

    o_ref[0] = _dot(h_scr[...], w_ref[...])


def _project(x_all, norm_w, mod_l, mod_c, w, w_extra, ctx_len, tm, tn):
    bsz, tall, d = x_all.shape
    n = w.shape[1]
    has_extra = w_extra is not None
    in_specs = [pl.BlockSpec((1, tm, d), lambda b, i, j: (b, i, 0)),
                pl.BlockSpec((1, d), lambda b, i, j: (0, 0))]
    in_specs += _mod_specs(d, (0, 1), 3)
    in_specs.append(pl.BlockSpec((d, tn), lambda b, i, j: (0, j)))
    args = [x_all, norm_w.reshape(1, d), mod_l, mod_c, mod_l, mod_c, w]
    out_specs = [pl.BlockSpec((1, tm, tn), lambda b, i, j: (b, i, j))]
    out_shape = [jax.ShapeDtypeStruct((bsz, tall, n), F32)]
    if has_extra:
        nx = w_extra.shape[1]
        in_specs.append(pl.BlockSpec((d, nx), lambda b, i, j: (0, 0)))
        args.append(w_extra)
        out_specs.append(pl.BlockSpec((tm, nx), lambda b, i, j: (i, b)))
        out_shape.append(jax.ShapeDtypeStruct((tall, bsz * nx), F32))
    out = pl.pallas_call(
        functools.partial(_proj_kernel, tm=tm, ctx_len=ctx_len, has_extra=has_extra),
        grid=(bsz, tall // tm, n // tn),
        in_specs=in_specs,
        out_specs=out_specs,
        out_shape=out_shape,
        scratch_shapes=[pltpu.VMEM((tm, d), BF16)],
        compiler_params=_cparams(3),
        name="norm_mod_proj",
    )(*args)
    return out if has_extra else out[0]


def _conv_kernel(main_ref, prev_ref, next_ref, w_ref, b_ref, o_ref, *, tt, ctx_len, n_lat):
    i = pl.program_id(1)
    p = i * tt + lax.broadcasted_iota(jnp.int32, (tt, 1), 0)
    is_ctx = p < ctx_len
    q = p - ctx_len
    col = jnp.bitwise_and(p, GRID_W - 1)
    m_up = q >= GRID_W
    m_dn = jnp.where(is_ctx, n_lat, q) < n_lat - GRID_W
    m_l = jnp.where(is_ctx, p, col) > 0
    m_r = jnp.where(is_ctx, p - (ctx_len - 1), col - (GRID_W - 1)) < 0

    main = main_ref[0]
    rows = {
        -1: jnp.concatenate([prev_ref[0], main[:tt - GRID_W]], axis=0),
        0: main,
        1: jnp.concatenate([main[GRID_W:], next_ref[0]], axis=0),
    }
    w = w_ref[...]
    acc = jnp.zeros_like(main) + b_ref[...]
    for dy in (-1, 0, 1):
        s = rows[dy]
        k0 = 3 * (dy + 1)
        t = (s * w[k0 + 1:k0 + 2]
             + jnp.where(m_l, pltpu.roll(s, 1, 0), 0.0) * w[k0:k0 + 1]
             + jnp.where(m_r, pltpu.roll(s, tt - 1, 0), 0.0) * w[k0 + 2:k0 + 3])
        if dy == -1:
            t = jnp.where(m_up, t, 0.0)
        elif dy == 1:
            t = jnp.where(m_dn, t, 0.0)
        acc = acc + t
    o_ref[0] = _silu(acc)


def _conv_silu(proj, conv_w, conv_b, ctx_len, n_ch):
    bsz, tall, _ = proj.shape
    n_rows = tall // GRID_W
    tt = _largest_divisor(tall, GRID_W, 768)
    assert ctx_len <= tt and ctx_len % GRID_W == 0
    r = tt // GRID_W
    tc = 512
    return pl.pallas_call(
        functools.partial(_conv_kernel, tt=tt, ctx_len=ctx_len, n_lat=tall - ctx_len),
        grid=(bsz, tall // tt, n_ch // tc),
        in_specs=[pl.BlockSpec((1, tt, tc), lambda b, i, c: (b, i, c)),
                  pl.BlockSpec((1, GRID_W, tc), lambda b, i, c: (b, jnp.maximum(i * r - 1, 0), c)),
                  pl.BlockSpec((1, GRID_W, tc), lambda b, i, c: (b, jnp.minimum((i + 1) * r, n_rows - 1), c)),
                  pl.BlockSpec((9, tc), lambda b, i, c: (0, c)),
                  pl.BlockSpec((1, tc), lambda b, i, c: (0, c))],
        out_specs=pl.BlockSpec((1, tt, tc), lambda b, i, c: (b, i, c)),
        out_shape=jax.ShapeDtypeStruct((bsz, tall, n_ch), F32),
        compiler_params=_cparams(3),
        name="dwconv_silu",
    )(proj, proj, proj, conv_w.reshape(9, n_ch), conv_b.reshape(1, n_ch))


def _scan_block(s, n_ctx_blocks, n_blocks, reverse):
    if not reverse:
        return s
    return jnp.where(s < n_ctx_blocks, n_ctx_blocks - 1 - s, n_blocks - 1 - s + n_ctx_blocks)


def _tri_mask(c, reverse):
    ri = lax.broadcasted_iota(jnp.int32, (c, c), 0)
    ci = lax.broadcasted_iota(jnp.int32, (c, c), 1)
    return (ci >= ri) if reverse else (ci <= ri)


def _ssd_kernel(*refs, direction, finish):
    if finish:
        (x_ref, bm_ref, cm_ref, dtlr_ref, dtb_ref, nega_ref, z_ref, yf_ref, dsk_ref, nw_ref,
         o_ref, st_ref) = refs
    else:
        (x_ref, bm_ref, cm_ref, dtlr_ref, dtb_ref, nega_ref, o_ref, st_ref) = refs
    reverse = direction == 1
    c = SSD_CHUNK
    p = SSD_HEADDIM
    gw = SSD_INNER // SSD_GROUPS
    hpg = SSD_HEADS // SSD_GROUPS

    @pl.when(pl.program_id(1) == 0)
    def _():
        st_ref[...] = jnp.zeros_like(st_ref)

    mask = _tri_mask(c, reverse)
    tri = mask.astype(F32)
    n_chunks = x_ref.shape[1] // c
    order = range(n_chunks - 1, -1, -1) if reverse else range(n_chunks)
    for ck in order:
        rs = slice(ck * c, (ck + 1) * c)
        x = x_ref[0, rs, :]
        bm = bm_ref[0, rs, :].astype(BF16)
        cm = cm_ref[0, rs, :].astype(BF16)
        dt_raw = dtlr_ref[0, rs, :][:, direction * SSD_HEADS:(direction + 1) * SSD_HEADS]
        dt = _softplus(dt_raw + dtb_ref[...])
        la = dt * nega_ref[...]
        acum = _dot(tri, la, precision=HIGHEST)
        acum_t = _dot(la, tri, _TT, precision=HIGHEST)
        a_last = acum[0:1] if reverse else acum[c - 1:c]
        e_acum = jnp.exp(acum)
        e_last = jnp.exp(a_last)
        dt_dec = dt * jnp.exp(a_last - acum)
        y_groups = []
        for g in range(SSD_GROUPS):
            cg = cm[:, g * SSD_STATE:(g + 1) * SSD_STATE]
            bg = bm[:, g * SSD_STATE:(g + 1) * SSD_STATE]
            cb = _dot(cg, bg, _NT)
            st = st_ref[:, g * gw:(g + 1) * gw]
            y_state = _dot(cg, st.astype(BF16))
            ys, xws, els = [], [], []
            for r in range(hpg):
                h = g * hpg + r
                xh = x[:, h * p:(h + 1) * p]
                xdt = xh * dt[:, h:h + 1]
                seg = acum[:, h:h + 1] - acum_t[h:h + 1, :]
                decay = jnp.exp(jnp.where(mask, seg, -jnp.inf))
                scores = (cb * decay).astype(BF16)
                y_h = _dot(scores, xdt.astype(BF16))
                y_h = y_h + y_state[:, r * p:(r + 1) * p] * e_acum[:, h:h + 1]
                ys.append(y_h)
                xws.append((xh * dt_dec[:, h:h + 1]).astype(BF16))
                els.append(jnp.broadcast_to(e_last[:, h:h + 1], (1, p)))
            xw = jnp.concatenate(xws, axis=1)
            el = jnp.concatenate(els, axis=1)
            st_ref[:, g * gw:(g + 1) * gw] = st * el + _dot(bg, xw, _TN)
            y_groups.append(jnp.concatenate(ys, axis=1))
        y = jnp.concatenate(y_groups, axis=1)
        if finish:
            z = z_ref[0, rs, :]
            y = (y + yf_ref[0, rs, :] + dsk_ref[...] * x) * _silu(z)
            outs = []
            for g in range(SSD_GROUPS):
                sl = slice(g * gw, (g + 1) * gw)
                outs.append(_rms(y[:, sl], nw_ref[:, sl]))
            o_ref[0, rs, :] = jnp.concatenate(outs, axis=1).astype(o_ref.dtype)
        else:
            o_ref[0, rs, :] = y


def _ssd_scan(xbc, proj, dt_bias, neg_a, z_col, dtlr_col, ctx_len, direction, finish_args=None):
    bsz, tall, _ = xbc.shape
    tb = SCAN_BLOCK
    nb, ncb = tall // tb, ctx_len // tb
    reverse = direction == 1
    finish = finish_args is not None

    def tok(col):
        return lambda b, s: (b, _scan_block(s, ncb, nb, reverse), col)

    in_specs = [pl.BlockSpec((1, tb, SSD_INNER), tok(0)),
                pl.BlockSpec((1, tb, SSD_BC), tok(SSD_INNER // SSD_BC)),
                pl.BlockSpec((1, tb, SSD_BC), tok(SSD_INNER // SSD_BC + 1)),
                pl.BlockSpec((1, tb, LANE), tok(dtlr_col)),
                pl.BlockSpec((1, SSD_HEADS), lambda b, s: (0, 0)),
                pl.BlockSpec((1, SSD_HEADS), lambda b, s: (0, 0))]
    args = [xbc, xbc, xbc, proj, dt_bias[direction:direction + 1], neg_a[direction:direction + 1]]
    if finish:
        y_f, d_skip_wide, norm_w = finish_args
        in_specs += [pl.BlockSpec((1, tb, SSD_INNER), tok(z_col)),
                     pl.BlockSpec((1, tb, SSD_INNER), tok(0)),
                     pl.BlockSpec((1, SSD_INNER), lambda b, s: (0, 0)),
                     pl.BlockSpec((1, SSD_INNER), lambda b, s: (0, 0))]
        args += [proj, y_f, d_skip_wide, norm_w]
    return pl.pallas_call(
        functools.partial(_ssd_kernel, direction=direction, finish=finish),
        grid=(bsz, nb),
        in_specs=in_specs,
        out_specs=pl.BlockSpec((1, tb, SSD_INNER), tok(0)),
        out_shape=jax.ShapeDtypeStruct((bsz, tall, SSD_INNER), BF16 if finish else F32),
        scratch_shapes=[pltpu.VMEM((SSD_STATE, SSD_INNER), F32)],
        compiler_params=_cparams(2),
        name="ssd_scan_bwd" if reverse else "ssd_scan_fwd",
    )(*args)


def _lin_kernel(*refs, mode, direction, finish, heads, dk, dv):
    refs = list(refs)
    if mode == "gla":
        q_ref, k_ref, v_ref, aux_ref, p1_ref, p2_ref = refs[:6]
        rest = refs[6:]
    else:
        q_ref, v_ref, aux_ref, p1_ref, p2_ref = refs[:5]
        k_ref = None
        rest = refs[5:]
    if finish:
        of_ref, gate_ref, nw_ref, o_ref, st_ref = rest
    else:
        o_ref, st_ref = rest
    reverse = direction == 1
    c = LIN_CHUNK

    @pl.when(pl.program_id(1) == 0)
    def _():
        st_ref[...] = jnp.zeros_like(st_ref)

    mask = _tri_mask(c, reverse)
    tri = mask.astype(F32)
    n_chunks = q_ref.shape[1] // c
    order = range(n_chunks - 1, -1, -1) if reverse else range(n_chunks)
    for ck in order:
        rs = slice(ck * c, (ck + 1) * c)
        v = v_ref[0, rs, :].astype(BF16)
        if mode == "gla":
            q = q_ref[0, rs, :] * (dk ** -0.5)
            k = k_ref[0, rs, :]
            off = 2 * SSD_HEADS + direction * GLA_GATE_RANK
            lr = aux_ref[0, rs, :][:, off:off + GLA_GATE_RANK].astype(BF16)
            gk = _dot(lr, p1_ref[...]) + p2_ref[...]
            lg = _log_sigmoid(gk) * (1.0 / GLA_GATE_NORM)
        else:
            q = _silu(q_ref[0, rs, :])
            f_raw = aux_ref[0, rs, :]
            a = p1_ref[...]
            b = p2_ref[...] + _log_sigmoid(f_raw)
            lg = jnp.maximum(a, b) + jnp.log1p(jnp.exp(-jnp.abs(a - b)))
            k = jnp.exp(b - f_raw)
        gcum = _dot(tri, lg, precision=HIGHEST)
        g_last = gcum[0:1] if reverse else gcum[c - 1:c]
        q_dec = (q * jnp.exp(gcum)).astype(BF16)
        k_inv = (k * jnp.exp(-gcum)).astype(BF16)
        k_end = (k * jnp.exp(g_last - gcum)).astype(BF16)
        e_last = jnp.exp(g_last)
        outs = []
        for h in range(heads):
            ks = slice(h * dk, (h + 1) * dk)
            vs = slice(h * dv, (h + 1) * dv)
            att = jnp.where(mask, _dot(q_dec[:, ks], k_inv[:, ks], _NT), 0.0).astype(BF16)
            st = st_ref[h]
            o_h = _dot(att, v[:, vs]) + _dot(q_dec[:, ks], st.astype(BF16), _NT)
            st_ref[h] = st * e_last[:, ks] + _dot(v[:, vs], k_end[:, ks], _TN)
            if finish:
                o_h = _rms(o_h + of_ref[0, rs, vs], nw_ref[...])
            outs.append(o_h)
        o = jnp.concatenate(outs, axis=1)
        if finish:
            o_ref[0, rs, :] = (o * _silu(gate_ref[0, rs, :])).astype(o_ref.dtype)
        else:
            o_ref[0, rs, :] = o


def _lin_scan(mode, proj, cols, params, ctx_len, direction, finish_args=None):
    bsz, tall, _ = proj.shape
    if mode == "gla":
        heads, dk, dv = GLA_HEADS, GLA_DK, GLA_DV
    else:
        heads, dk, dv = HGRN_HEADS, HGRN_DK, HGRN_DV
    kw, vw = heads * dk, heads * dv
    tb = SCAN_BLOCK
    nb, ncb = tall // tb, ctx_len // tb
    reverse = direction == 1
    finish = finish_args is not None

    def tok(col):
        return lambda b, s: (b, _scan_block(s, ncb, nb, reverse), col)

    def const2(shape):
        return pl.BlockSpec(shape, lambda b, s: (0, 0))

    p1, p2 = params
    if mode == "gla":
        in_specs = [pl.BlockSpec((1, tb, kw), tok(cols["q"])),
                    pl.BlockSpec((1, tb, kw), tok(cols["k"])),
                    pl.BlockSpec((1, tb, vw), tok(cols["v"])),
                    pl.BlockSpec((1, tb, LANE), tok(cols["aux"])),
                    const2(p1.shape), const2(p2.shape)]
        args = [proj, proj, proj, proj, p1, p2]
    else:
        in_specs = [pl.BlockSpec((1, tb, kw), tok(cols["q"])),
                    pl.BlockSpec((1, tb, vw), tok(cols["v"])),
                    pl.BlockSpec((1, tb, kw), tok(cols["aux"] + direction)),
                    const2(p1.shape), const2(p2.shape)]
        args = [proj, proj, proj, p1, p2]
    if finish:
        o_f, norm_w = finish_args
        in_specs += [pl.BlockSpec((1, tb, vw), tok(0)),
                     pl.BlockSpec((1, tb, vw), tok(cols["gate"])),
                     const2(norm_w.shape)]
        args += [o_f, proj, norm_w]
    return pl.pallas_call(
        functools.partial(_lin_kernel, mode=mode, direction=direction, finish=finish,
                          heads=heads, dk=dk, dv=dv),
        grid=(bsz, nb),
        in_specs=in_specs,
        out_specs=pl.BlockSpec((1, tb, vw), tok(0)),
        out_shape=jax.ShapeDtypeStruct((bsz, tall, vw), BF16 if finish else F32),
        scratch_shapes=[pltpu.VMEM((heads, dv, dk), F32)],
        compiler_params=_cparams(2),
        name=f"{mode}_scan_{'bwd' if reverse else 'fwd'}",
    )(*args)


def _s5_kernel(u_ref, bmat_ref, lre_ref, lim_ref, cmat_ref, o_ref, h_ref, st_ref, *, bsz, n_state):
    d = pl.program_id(0)
    steps = S5_CHUNK
    cblk = 512

    @pl.when(pl.program_id(1) == 0)
    def _():
        st_ref[...] = jnp.zeros_like(st_ref)

    h_ref[...] = _dot(u_ref[...].astype(BF16), bmat_ref[0])
    for c0 in range(0, n_state, cblk):
        re_cols = slice(c0, c0 + cblk)
        im_cols = slice(n_state + c0, n_state + c0 + cblk)
        lam_re = jnp.broadcast_to(lre_ref[0, :, re_cols], (bsz, cblk))
        lam_im = jnp.broadcast_to(lim_ref[0, :, re_cols], (bsz, cblk))

        def body(tt, carry, re_cols=re_cols, im_cols=im_cols, lam_re=lam_re, lam_im=lam_im):
            hr, hi = carry
            t = jnp.where(d == 0, tt, steps - 1 - tt)
            rows = pl.ds(pl.multiple_of(t * bsz, bsz), bsz)
            nr = lam_re * hr - lam_im * hi + h_ref[rows, re_cols]
            ni = lam_re * hi + lam_im * hr + h_ref[rows, im_cols]
            h_ref[rows, re_cols] = nr
            h_ref[rows, im_cols] = ni
            return nr, ni

        hr, hi = lax.fori_loop(0, steps, body, (st_ref[:, re_cols], st_ref[:, im_cols]), unroll=8)
        st_ref[:, re_cols] = hr
        st_ref[:, im_cols] = hi
    o_ref[0] = _dot(h_ref[...].astype(BF16), cmat_ref[...])


def _s5_scan(u_t, bmat, lam_re, lam_im, cmat, bsz, ctx_len):
    rows_total, width = u_t.shape
    tall = rows_total // bsz
    nch, ncc = tall // S5_CHUNK, ctx_len // S5_CHUNK
    n_state = lam_re.shape[-1]
    rows = S5_CHUNK * bsz

    def chunk(d, s):
        return jnp.where(d == 0, s, _scan_block(s, ncc, nch, True))

    return pl.pallas_call(
        functools.partial(_s5_kernel, bsz=bsz, n_state=n_state),
        grid=(2, nch),
        in_specs=[pl.BlockSpec((rows, width), lambda d, s: (chunk(d, s), 0)),
                  pl.BlockSpec((1, width, 2 * n_state), lambda d, s: (d, 0, 0)),
                  pl.BlockSpec((1, 1, n_state), lambda d, s: (d, 0, 0)),
                  pl.BlockSpec((1, 1, n_state), lambda d, s: (d, 0, 0)),
                  pl.BlockSpec((2 * n_state, width), lambda d, s: (0, 0))],
        out_specs=pl.BlockSpec((1, rows, width), lambda d, s: (d, chunk(d, s), 0)),
        out_shape=jax.ShapeDtypeStruct((2, rows_total, width), F32),
        scratch_shapes=[pltpu.VMEM((rows, 2 * n_state), F32),
                        pltpu.VMEM((bsz, 2 * n_state), F32)],
        compiler_params=_cparams(2),
        name="s5_scan",
    )(u_t, bmat, lam_re, lam_im, cmat)


def _out0_kernel(x_ref, a_ref, b_ref, wa_ref, wb_ref, gl_ref, gc_ref, o_ref, *, tm, ctx_len):
    i = pl.program_id(1)
    o = _dot(a_ref[0], wa_ref[...]) + _dot(b_ref[0], wb_ref[...])
    gate = _row_select(i, tm, ctx_len, gc_ref[...], gl_ref[0])
    o_ref[0] = x_ref[0] + gate * o


def _out_proj0(x_all, mix_a, mix_b, w_a, w_b, mod_l, mod_c, ctx_len, tm):
    bsz, tall, d = x_all.shape
    return pl.pallas_call(
        functools.partial(_out0_kernel, tm=tm, ctx_len=ctx_len),
        grid=(bsz, tall // tm),
        in_specs=[pl.BlockSpec((1, tm, d), lambda b, i: (b, i, 0)),
                  pl.BlockSpec((1, tm, mix_a.shape[2]), lambda b, i: (b, i, 0)),
                  pl.BlockSpec((1, tm, mix_b.shape[2]), lambda b, i: (b, i, 0)),
                  pl.BlockSpec(w_a.shape, lambda b, i: (0, 0)),
                  pl.BlockSpec(w_b.shape, lambda b, i: (0, 0))] + _mod_specs(d, (2,), 2),
        out_specs=pl.BlockSpec((1, tm, d), lambda b, i: (b, i, 0)),
        out_shape=jax.ShapeDtypeStruct((bsz, tall, d), F32),
        compiler_params=_cparams(2),
        name="out_proj_even",
    )(x_all, mix_a, mix_b, w_a, w_b, mod_l, mod_c)


def _gelu_tanh(x):
    return 0.5 * x * (1.0 + jnp.tanh(math.sqrt(2.0 / math.pi) * (x + 0.044715 * (x * x * x))))


def _out1_kernel(x_ref, a_ref, yf_ref, yb_ref, u_ref, dsk_ref, gw_ref, gb_ref, wa_ref, wb_ref,
                 gl_ref, gc_ref, o_ref, *, tm, ctx_len):
    i = pl.program_id(1)
    y = _gelu_tanh(yf_ref[0] + yb_ref[0] + dsk_ref[...] * u_ref[...])
    glu = _dot(y.astype(BF16), gw_ref[...]) + gb_ref[...]
    y = y * (1.0 / (1.0 + jnp.exp(-glu)))
    o = _dot(a_ref[0], wa_ref[...]) + _dot(y.astype(BF16), wb_ref[...])
    gate = _row_select(i, tm, ctx_len, gc_ref[...], gl_ref[0])
    o_ref[0] = x_ref[0] + gate * o


def _out_proj1(x_all, mix_a, y_dirs, u_t, d_skip, glu_w, glu_b, w_a, w_b, mod_l, mod_c, ctx_len, tm):
    bsz, tall, d = x_all.shape
    width = d_skip.shape[1]
    y3 = y_dirs.reshape(2, tall, bsz * width)
    u2 = u_t.reshape(tall, bsz * width)
    return pl.pallas_call(
        functools.partial(_out1_kernel, tm=tm, ctx_len=ctx_len),
        grid=(bsz, tall // tm),
        in_specs=[pl.BlockSpec((1, tm, d), lambda b, i: (b, i, 0)),
                  pl.BlockSpec((1, tm, mix_a.shape[2]), lambda b, i: (b, i, 0)),
                  pl.BlockSpec((1, tm, width), lambda b, i: (0, i, b)),
                  pl.BlockSpec((1, tm, width), lambda b, i: (1, i, b)),
                  pl.BlockSpec((tm, width), lambda b, i: (i, b)),
                  pl.BlockSpec((1, width), lambda b, i: (0, 0)),
                  pl.BlockSpec(glu_w.shape, lambda b, i: (0, 0)),
                  pl.BlockSpec((1, width), lambda b, i: (0, 0)),
                  pl.BlockSpec(w_a.shape, lambda b, i: (0, 0)),
                  pl.BlockSpec(w_b.shape, lambda b, i: (0, 0))] + _mod_specs(d, (2,), 2),
        out_specs=pl.BlockSpec((1, tm, d), lambda b, i: (b, i, 0)),
        out_shape=jax.ShapeDtypeStruct((bsz, tall, d), F32),
        compiler_params=_cparams(2),
        name="out_proj_odd",
    )(x_all, mix_a, y3, y3, u2, d_skip, glu_w, glu_b, w_a, w_b, mod_l, mod_c)


def _mlp_kernel(*refs, tm, ctx_len, final):
    if final:
        (x_ref, nw_ref, shl_ref, shc_ref, scl_ref, scc_ref, gl_ref, gc_ref, w1_ref, w2_ref, fw_ref,
         o_ref, h_scr, acc_scr) = refs
    else:
        (x_ref, nw_ref, shl_ref, shc_ref, scl_ref, scc_ref, gl_ref, gc_ref, w1_ref, w2_ref,
         o_ref, h_scr, acc_scr) = refs
    i = pl.program_id(1)
    j = pl.program_id(2)

    @pl.when(j == 0)
    def _():
        y = _rms(x_ref[0], nw_ref[...])
        shift = _row_select(i, tm, ctx_len, shc_ref[...], shl_ref[0])
        scale = _row_select(i, tm, ctx_len, scc_ref[...], scl_ref[0])
        h_scr[...] = (y * (1.0 + scale) + shift).astype(BF16)
        acc_scr[...] = jnp.zeros_like(acc_scr)

    a = jnp.maximum(_dot(h_scr[...], w1_ref[...]), 0.0)
    acc_scr[...] += _dot((a * a).astype(BF16), w2_ref[...])

    @pl.when(j == pl.num_programs(2) - 1)
    def _():
        gate = _row_select(i, tm, ctx_len, gc_ref[...], gl_ref[0])
        out = x_ref[0] + gate * acc_scr[...]
        if final:
            out = _rms(out, fw_ref[...])
        o_ref[0] = out


def _mlp(x_all, norm_w, mod_l, mod_c, w1, w2, final_w, ctx_len, tm, tf):
    bsz, tall, d = x_all.shape
    ff = w1.shape[1]
    final = final_w is not None
    in_specs = [pl.BlockSpec((1, tm, d), lambda b, i, j: (b, i, 0)),
                pl.BlockSpec((1, d), lambda b, i, j: (0, 0))]
    in_specs += _mod_specs(d, (3, 4, 5), 3)
    in_specs += [pl.BlockSpec((d, tf), lambda b, i, j: (0, j)),
                 pl.BlockSpec((tf, d), lambda b, i, j: (j, 0))]
    args = [x_all, norm_w.reshape(1, d)] + [mod_l, mod_c] * 3 + [w1, w2]
    if final:
        in_specs.append(pl.BlockSpec((1, d), lambda b, i, j: (0, 0)))
        args.append(final_w.reshape(1, d))
    return pl.pallas_call(
        functools.partial(_mlp_kernel, tm=tm, ctx_len=ctx_len, final=final),
        grid=(bsz, tall // tm, ff // tf),
        in_specs=in_specs,
        out_specs=pl.BlockSpec((1, tm, d), lambda b, i, j: (b, i, 0)),
        out_shape=jax.ShapeDtypeStruct((bsz, tall, d), F32),
        scratch_shapes=[pltpu.VMEM((tm, d), BF16), pltpu.VMEM((tm, d), F32)],
        compiler_params=_cparams(3),
        name="sq_relu_mlp",
    )(*args)


def _even_in_weight(w_in):
    sizes = (SSD_INNER, SSD_INNER + 2 * SSD_BC, 2 * SSD_HEADS, GLA_KEY, GLA_KEY, GLA_VAL,
             2 * GLA_GATE_RANK, GLA_VAL)
    offs = [0]
    for s in sizes:
        offs.append(offs[-1] + s)
    z, xbc, dt, q, k, v, lr, r = (w_in[:, offs[n]:offs[n + 1]] for n in range(8))
    pad = jnp.zeros((w_in.shape[0], LANE - dt.shape[1] - lr.shape[1]), w_in.dtype)
    return jnp.concatenate([xbc, z, v, r, q, k, dt, lr, pad], axis=1).astype(BF16)


def _s5_params(a_re, a_im, log_dt, b_re, b_im, c_re, c_im):
    delta = jnp.exp(log_dt.astype(F32))[..., None]
    mag = jnp.exp(a_re * delta)
    lbar_re, lbar_im = mag * jnp.cos(a_im * delta), mag * jnp.sin(a_im * delta)
    den = a_re * a_re + a_im * a_im
    zr = ((lbar_re - 1.0) * a_re + lbar_im * a_im) / den
    zi = (lbar_im * a_re - (lbar_re - 1.0) * a_im) / den
    bb_re = zr[..., None] * b_re - zi[..., None] * b_im
    bb_im = zr[..., None] * b_im + zi[..., None] * b_re
    eye = jnp.eye(S5_GROUPS, dtype=F32)

    def block_in(bb):
        return jnp.einsum("dgpc,gh->dgchp", bb, eye).reshape(2, S5_WIDTH, S5_NSTATE)

    def block_out(cc):
        return jnp.einsum("gcp,gh->gphc", cc, eye).reshape(S5_NSTATE, S5_WIDTH)

    bmat = jnp.concatenate([block_in(bb_re), block_in(bb_im)], axis=2).astype(BF16)
    cmat = jnp.concatenate([block_out(c_re), -block_out(c_im)], axis=0).astype(BF16)
    return (bmat, lbar_re.reshape(2, 1, S5_NSTATE), lbar_im.reshape(2, 1, S5_NSTATE), cmat)


def _layer_even(x_all, mod_l, mod_c, ctx_len, tm, norm1_w, w_in, conv_w, conv_b, dt_bias, a_log,
                d_skip, ssd_norm_w, gate_w, gate_b, gla_norm_w, w_out):
    w = _even_in_weight(w_in)
    n = w.shape[1]
    proj = _project(x_all, norm1_w, mod_l, mod_c, w, None, ctx_len, tm, _largest_divisor(n, LANE, 1024))
    n_xbc = SSD_INNER + 2 * SSD_BC
    c_z, c_v, c_r = n_xbc // SSD_INNER, n_xbc // GLA_VAL + 1, n_xbc // GLA_VAL + 2
    c_q = (n_xbc + 3 * SSD_INNER) // GLA_KEY
    c_aux = (n_xbc + 3 * SSD_INNER + 2 * GLA_KEY) // LANE
    xbc = _conv_silu(proj, conv_w, conv_b, ctx_len, n_xbc)

    neg_a = -jnp.exp(a_log.astype(F32))
    dt_bias = dt_bias.astype(F32)
    d_wide = jnp.repeat(d_skip.astype(F32), SSD_HEADDIM).reshape(1, SSD_INNER)
    y_f = _ssd_scan(xbc, proj, dt_bias, neg_a, c_z, c_aux, ctx_len, 0)
    y_mix = _ssd_scan(xbc, proj, dt_bias, neg_a, c_z, c_aux, ctx_len, 1,
                      (y_f, d_wide, ssd_norm_w.reshape(1, SSD_INNER)))

    cols = {"q": c_q, "k": c_q + 1, "v": c_v, "aux": c_aux, "gate": c_r}
    gparams = [(gate_w[d].astype(BF16), gate_b[d].reshape(1, GLA_KEY).astype(F32)) for d in range(2)]
    o_f = _lin_scan("gla", proj, cols, gparams[0], ctx_len, 0)
    o_mix = _lin_scan("gla", proj, cols, gparams[1], ctx_len, 1, (o_f, gla_norm_w.reshape(1, GLA_DV)))

    w_out = w_out.astype(BF16)
    return _out_proj0(x_all, y_mix, o_mix, w_out[:SSD_INNER], w_out[SSD_INNER:], mod_l, mod_c, ctx_len, tm)


def _layer_odd(x_all, mod_l, mod_c, ctx_len, tm, norm1_w, w_in, lb, hgrn_norm_w, a_re, a_im, log_dt,
               b_re, b_im, c_re, c_im, d_skip, glu_w, glu_b, w_out):
    bsz = x_all.shape[0]
    n_main = 5 * HGRN_WIDTH
    w_main = w_in[:, :n_main].astype(BF16)
    w_u = w_in[:, n_main:].astype(BF16)
    proj, u_t = _project(x_all, norm1_w, mod_l, mod_c, w_main, w_u, ctx_len, tm, HGRN_WIDTH)

    log_lb = jnp.log(lb).reshape(2, 1, HGRN_WIDTH)
    log_1mlb = jnp.log1p(-lb).reshape(2, 1, HGRN_WIDTH)
    cols = {"q": 0, "v": 1, "aux": 2, "gate": 4}
    o_f = _lin_scan("hgrn", proj, cols, (log_lb[0], log_1mlb[0]), ctx_len, 0)
    o_mix = _lin_scan("hgrn", proj, cols, (log_lb[1], log_1mlb[1]), ctx_len, 1,
                      (o_f, hgrn_norm_w.reshape(1, HGRN_DV)))

    bmat, lam_re, lam_im, cmat = _s5_params(a_re.astype(F32), a_im.astype(F32), log_dt, b_re.astype(F32),
                                            b_im.astype(F32), c_re.astype(F32), c_im.astype(F32))
    y_dirs = _s5_scan(u_t.reshape(-1, S5_WIDTH), bmat, lam_re, lam_im, cmat, bsz, ctx_len)

    w_out = w_out.astype(BF16)
    return _out_proj1(x_all, o_mix, y_dirs, u_t, d_skip.astype(F32).reshape(1, S5_WIDTH),
                      glu_w.astype(BF16), glu_b.astype(F32).reshape(1, S5_WIDTH),
                      w_out[:HGRN_WIDTH], w_out[HGRN_WIDTH:], mod_l, mod_c, ctx_len, tm)


def kernel(x, c, ctx, c_ctx, ada_w, ada_b, norm1_w, norm2_w, ssd_gla_w_in, ssd_conv_w, ssd_conv_b, ssd_dt_bias, ssd_a_log, ssd_d, ssd_norm_w, gla_gate_w, gla_gate_b, gla_norm_w, ssd_gla_w_out, hgrn_s5_w_in, hgrn_lb_logits, hgrn_norm_w, s5_a_re, s5_a_im, s5_log_dt, s5_b_re, s5_b_im, s5_c_re, s5_c_im, s5_d, s5_glu_w, s5_glu_b, hgrn_s5_w_out, mlp_w1, mlp_w2, final_norm_w):
    bsz, seq, d = x.shape
    ctx_len = ctx.shape[1]
    depth = ada_w.shape[0]
    tall = ctx_len + seq
    assert bsz % 8 == 0 and ctx_len % SCAN_BLOCK == 0 and seq % SCAN_BLOCK == 0 and seq % GRID_W == 0
    tm = _largest_divisor(tall, 16, 1056)
    tf = 512

    n_rows = -(-(bsz + 1) // 8) * 8
    cvec = jnp.concatenate([c, c_ctx[None, :], jnp.zeros((n_rows - bsz - 1, d), c.dtype)], axis=0)
    mod = _modulation(cvec.astype(F32), ada_w, ada_b)

    p_lb = jax.nn.softmax(hgrn_lb_logits.astype(F32), axis=0)
    lb_all = jnp.cumsum(p_lb, axis=0) - p_lb[0]

    x_all = jnp.concatenate([ctx, x], axis=1).astype(F32)
    for layer in range(depth):
        j = layer // 2
        mod_l = mod[layer, :bsz].reshape(bsz, 1, N_MOD * d)
        mod_c = mod[layer, bsz:bsz + 1]
        if layer % 2 == 0:
            x_all = _layer_even(x_all, mod_l, mod_c, ctx_len, tm, norm1_w[layer], ssd_gla_w_in[j],
                                ssd_conv_w[j], ssd_conv_b[j], ssd_dt_bias[j], ssd_a_log[j], ssd_d[j],
                                ssd_norm_w[j], gla_gate_w[j], gla_gate_b[j], gla_norm_w[j], ssd_gla_w_out[j])
        else:
            x_all = _layer_odd(x_all, mod_l, mod_c, ctx_len, tm, norm1_w[layer], hgrn_s5_w_in[j],
                               lb_all[layer], hgrn_norm_w[j], s5_a_re[j], s5_a_im[j], s5_log_dt[j],
                               s5_b_re[j], s5_b_im[j], s5_c_re[j], s5_c_im[j], s5_d[j], s5_glu_w[j],
                               s5_glu_b[j], hgrn_s5_w_out[j])
        last = layer == depth - 1
        x_all = _mlp(x_all, norm2_w[layer], mod_l, mod_c, mlp_w1[layer].astype(BF16),
                     mlp_w2[layer].astype(BF16), final_norm_w if last else None, ctx_len, tm, tf)
    return x_all[:, ctx_len:].astype(x.dtype)
```

```python
import functools
import math

import jax
import jax.numpy as jnp
from jax import lax
from jax.experimental import pallas as pl
from jax.experimental.pallas import tpu as pltpu

F32 = jnp.float32
BF16 = jnp.bfloat16

GRID_W = 64
NORM_EPS = 1e-6
N_MOD = 6
SSD_HEADDIM = 64
SSD_HEADS = 16
SSD_GROUPS = 4
SSD_STATE = 128
SSD_CHUNK = 128
GLA_HEADS = 8
GLA_DK = 64
GLA_DV = 128
GLA_GATE_RANK = 16
GLA_GATE_NORM = 16.0
HGRN_HEADS = 8
HGRN_DK = 128
HGRN_DV = 128
S5_GROUP = 16
S5_GROUPS = 24
S5_STATE = 64
LIN_CHUNK = 64

SSD_INNER = SSD_HEADS * SSD_HEADDIM
SSD_BC = SSD_GROUPS * SSD_STATE
GLA_KEY = GLA_HEADS * GLA_DK
GLA_VAL = GLA_HEADS * GLA_DV
HGRN_WIDTH = HGRN_HEADS * HGRN_DV
S5_WIDTH = S5_GROUPS * S5_GROUP
S5_NSTATE = S5_GROUPS * S5_STATE

VMEM_LIMIT_BYTES = 56 * 1024 * 1024
LANE = 128
SCAN_BLOCK = 256
S5_CHUNK = 128
S5_SLAB = LANE // S5_GROUP


def _cparams(n_axes):
    return pltpu.CompilerParams(dimension_semantics=("arbitrary",) * n_axes,
                                vmem_limit_bytes=VMEM_LIMIT_BYTES)


def _largest_divisor(n, multiple, cap):
    best = None
    for d in range(multiple, min(n, cap) + 1, multiple):
        if n % d == 0:
            best = d
    assert best is not None, (n, multiple, cap)
    return best


def _silu(x):
    return x * (1.0 / (1.0 + jnp.exp(-x)))


def _softplus(x):
    return jnp.maximum(x, 0.0) + jnp.log1p(jnp.exp(-jnp.abs(x)))


def _log_sigmoid(x):
    return -_softplus(-x)


def _rms(x, w):
    return x * lax.rsqrt(jnp.mean(x * x, axis=-1, keepdims=True) + NORM_EPS) * w


def _dot(a, b, dims=(((1,), (0,)), ((), ())), precision=None):
    return lax.dot_general(a, b, dims, precision=precision, preferred_element_type=F32)


def _split3(v):
    hi = v.astype(BF16)
    r1 = v - hi.astype(F32)
    mid = r1.astype(BF16)
    lo = (r1 - mid.astype(F32)).astype(BF16)
    return hi, mid, lo


def _tri3(mask):
    tri = mask.astype(BF16)
    return jnp.concatenate([tri, tri, tri], axis=1)


def _cumsum_rows(tri3, v):
    return _dot(tri3, jnp.concatenate(_split3(v), axis=0))


_NT = (((1,), (1,)), ((), ()))
_TN = (((0,), (0,)), ((), ()))
_TT = (((0,), (1,)), ((), ()))


def _mod_kernel(c_ref, w_ref, b_ref, o_ref):
    a = _silu(c_ref[...]).astype(BF16)
    o_ref[0] = _dot(a, w_ref[0].astype(BF16)) + b_ref[0]


def _modulation(cvec, ada_w, ada_b):
    depth, d, n = ada_w.shape
    rows = cvec.shape[0]
    tn = _largest_divisor(n, LANE, 1024)
    return pl.pallas_call(
        _mod_kernel,
        grid=(depth, n // tn),
        in_specs=[pl.BlockSpec((rows, d), lambda l, j: (0, 0)),
                  pl.BlockSpec((1, d, tn), lambda l, j: (l, 0, j)),
                  pl.BlockSpec((1, 1, tn), lambda l, j: (l, 0, j))],
        out_specs=pl.BlockSpec((1, rows, tn), lambda l, j: (l, 0, j)),
        out_shape=jax.ShapeDtypeStruct((depth, rows, n), F32),
        compiler_params=_cparams(2),
        name="adaln_mod",
    )(cvec, ada_w, ada_b.reshape(depth, 1, n))


def _row_select(i, tm, ctx_len, ctx_val, lat_val):
    row = i * tm + lax.broadcasted_iota(jnp.int32, (tm, 1), 0)
    return jnp.where(row < ctx_len, ctx_val, lat_val)


def _mod_specs(d, cols, n_grid_axes):
    specs = []
    for k in cols:
        if n_grid_axes == 2:
            specs.append(pl.BlockSpec((1, 1, d), lambda b, i, k=k: (b, 0, k)))
            specs.append(pl.BlockSpec((1, d), lambda b, i, k=k: (0, k)))
        else:
            specs.append(pl.BlockSpec((1, 1, d), lambda b, i, j, k=k: (b, 0, k)))
            specs.append(pl.BlockSpec((1, d), lambda b, i, j, k=k: (0, k)))
    return specs


def _proj_kernel(*refs, tm, ctx_len, has_extra):
    if has_extra:
        (x_ref, nw_ref, shl_ref, shc_ref, scl_ref, scc_ref, w_ref, wx_ref, o_ref, ox_ref, h_scr) = refs
    else:
        (x_ref, nw_ref, shl_ref, shc_ref, scl_ref, scc_ref, w_ref, o_ref, h_scr) = refs
    i = pl.program_id(1)
    j = pl.program_id(2)

    @pl.when(j == 0)
    def _():
        y = _rms(x_ref[0], nw_ref[...])
        shift = _row_select(i, tm, ctx_len, shc_ref[...], shl_ref[0])
        scale = _row_select(i, tm, ctx_len, scc_ref[...], scl_ref[0])
        h = (y * (1.0 + scale) + shift).astype(BF16)
        h_scr[...] = h
        if has_extra:
            ox_ref[...] = _dot(h, wx_ref[...])

    o_ref[0] = _dot(h_scr[...], w_ref[...]).astype(o_ref.dtype)


def _project(x_all, norm_w, mod_l, mod_c, w, w_extra, ctx_len, tm, tn):
    bsz, tall, d = x_all.shape
    n = w.shape[1]
    has_extra = w_extra is not None
    in_specs = [pl.BlockSpec((1, tm, d), lambda b, i, j: (b, i, 0)),
                pl.BlockSpec((1, d), lambda b, i, j: (0, 0))]
    in_specs += _mod_specs(d, (0, 1), 3)
    in_specs.append(pl.BlockSpec((d, tn), lambda b, i, j: (0, j)))
    args = [x_all, norm_w.reshape(1, d), mod_l, mod_c, mod_l, mod_c, w]
    out_specs = [pl.BlockSpec((1, tm, tn), lambda b, i, j: (b, i, j))]
    out_shape = [jax.ShapeDtypeStruct((bsz, tall, n), BF16)]
    if has_extra:
        nx = w_extra.shape[1]
        in_specs.append(pl.BlockSpec((d, nx), lambda b, i, j: (0, 0)))
        args.append(w_extra)
        out_specs.append(pl.BlockSpec((tm, nx), lambda b, i, j: (i, b)))
        out_shape.append(jax.ShapeDtypeStruct((tall, bsz * nx), F32))
    out = pl.pallas_call(
        functools.partial(_proj_kernel, tm=tm, ctx_len=ctx_len, has_extra=has_extra),
        grid=(bsz, tall // tm, n // tn),
        in_specs=in_specs,
        out_specs=out_specs,
        out_shape=out_shape,
        scratch_shapes=[pltpu.VMEM((tm, d), BF16)],
        compiler_params=_cparams(3),
        name="norm_mod_proj",
    )(*args)
    return out if has_extra else out[0]


def _conv_kernel(main_ref, prev_ref, next_ref, w_ref, b_ref, o_ref, *, tt, ctx_len, n_lat):
    i = pl.program_id(1)
    p = i * tt + lax.broadcasted_iota(jnp.int32, (tt, 1), 0)
    is_ctx = p < ctx_len
    q = p - ctx_len
    col = jnp.bitwise_and(p, GRID_W - 1)
    m_up = q >= GRID_W
    m_dn = jnp.where(is_ctx, n_lat, q) < n_lat - GRID_W
    m_l = jnp.where(is_ctx, p, col) > 0
    m_r = jnp.where(is_ctx, p - (ctx_len - 1), col - (GRID_W - 1)) < 0

    main = main_ref[0].astype(F32)
    rows = {
        -1: jnp.concatenate([prev_ref[0].astype(F32), main[:tt - GRID_W]], axis=0),
        0: main,
        1: jnp.concatenate([main[GRID_W:], next_ref[0].astype(F32)], axis=0),
    }
    w = w_ref[...]
    acc = jnp.zeros_like(main) + b_ref[...]
    for dy in (-1, 0, 1):
        s = rows[dy]
        k0 = 3 * (dy + 1)
        t = (s * w[k0 + 1:k0 + 2]
             + jnp.where(m_l, pltpu.roll(s, 1, 0), 0.0) * w[k0:k0 + 1]
             + jnp.where(m_r, pltpu.roll(s, tt - 1, 0), 0.0) * w[k0 + 2:k0 + 3])
        if dy == -1:
            t = jnp.where(m_up, t, 0.0)
        elif dy == 1:
            t = jnp.where(m_dn, t, 0.0)
        acc = acc + t
    o_ref[0] = _silu(acc).astype(o_ref.dtype)


def _conv_silu(proj, conv_w, conv_b, ctx_len, n_ch):
    bsz, tall, _ = proj.shape
    n_rows = tall // GRID_W
    tt = _largest_divisor(tall, GRID_W, 768)
    assert ctx_len <= tt and ctx_len % GRID_W == 0
    r = tt // GRID_W
    tc = 512
    return pl.pallas_call(
        functools.partial(_conv_kernel, tt=tt, ctx_len=ctx_len, n_lat=tall - ctx_len),
        grid=(bsz, tall // tt, n_ch // tc),
        in_specs=[pl.BlockSpec((1, tt, tc), lambda b, i, c: (b, i, c)),
                  pl.BlockSpec((1, GRID_W, tc), lambda b, i, c: (b, jnp.maximum(i * r - 1, 0), c)),
                  pl.BlockSpec((1, GRID_W, tc), lambda b, i, c: (b, jnp.minimum((i + 1) * r, n_rows - 1), c)),
                  pl.BlockSpec((9, tc), lambda b, i, c: (0, c)),
                  pl.BlockSpec((1, tc), lambda b, i, c: (0, c))],
        out_specs=pl.BlockSpec((1, tt, tc), lambda b, i, c: (b, i, c)),
        out_shape=jax.ShapeDtypeStruct((bsz, tall, n_ch), BF16),
        compiler_params=_cparams(3),
        name="dwconv_silu",
    )(proj, proj, proj, conv_w.reshape(9, n_ch), conv_b.reshape(1, n_ch))


def _scan_block(s, n_ctx_blocks, n_blocks, reverse):
    if not reverse:
        return s
    return jnp.where(s < n_ctx_blocks, n_ctx_blocks - 1 - s, n_blocks - 1 - s + n_ctx_blocks)


def _tri_mask(c, reverse):
    ri = lax.broadcasted_iota(jnp.int32, (c, c), 0)
    ci = lax.broadcasted_iota(jnp.int32, (c, c), 1)
    return (ci >= ri) if reverse else (ci <= ri)


def _ssd_expand_matrix():
    eye = jnp.eye(SSD_HEADS, dtype=F32)
    e_head = jnp.repeat(eye, SSD_HEADDIM, axis=1)
    e_seg = jnp.repeat(eye, SSD_CHUNK, axis=1)
    zh = jnp.zeros_like(e_head)
    zs = jnp.zeros_like(e_seg)
    blk = jnp.concatenate([
        jnp.concatenate([e_head, zh, zh, zs], axis=1),
        jnp.concatenate([zh, e_head, zh, zs], axis=1),
        jnp.concatenate([zh, zh, e_head, zs], axis=1),
        jnp.concatenate([zh, zh, zh, e_seg], axis=1)], axis=0)
    return jnp.concatenate([blk, blk, blk], axis=0).astype(BF16)


def _ssd_kernel(*refs, direction, finish):
    if finish:
        (x_ref, bm_ref, cm_ref, dtlr_ref, dtb_ref, nega_ref, exp_ref, z_ref, yf_ref, dsk_ref, nw_ref,
         o_ref, st_ref) = refs
    else:
        (x_ref, bm_ref, cm_ref, dtlr_ref, dtb_ref, nega_ref, exp_ref, o_ref, st_ref) = refs
    reverse = direction == 1
    c = SSD_CHUNK
    p = SSD_HEADDIM
    gw = SSD_INNER // SSD_GROUPS
    hpg = SSD_HEADS // SSD_GROUPS

    @pl.when(pl.program_id(1) == 0)
    def _():
        st_ref[...] = jnp.zeros_like(st_ref)

    mask = _tri_mask(c, reverse)
    tri3 = _tri3(mask)
    last = 0 if reverse else c - 1
    n_chunks = x_ref.shape[1] // c
    order = range(n_chunks - 1, -1, -1) if reverse else range(n_chunks)
    for ck in order:
        rs = slice(ck * c, (ck + 1) * c)
        x = x_ref[0, rs, :].astype(F32)
        bm = bm_ref[0, rs, :]
        cm = cm_ref[0, rs, :]
        dt_raw = dtlr_ref[0, rs, :][:, direction * SSD_HEADS:(direction + 1) * SSD_HEADS].astype(F32)
        dt = _softplus(dt_raw + dtb_ref[...])
        la3 = jnp.concatenate(_split3(dt * nega_ref[...]), axis=0)
        acum = _dot(tri3, la3)
        acum_t = _dot(la3, tri3, _TT)
        a_last = acum[last:last + 1]
        narrow = jnp.concatenate([dt, dt * jnp.exp(a_last - acum), jnp.exp(acum), acum], axis=1)
        wide = _dot(jnp.concatenate(_split3(narrow), axis=1), exp_ref[...])
        dt_w = wide[:, :SSD_INNER]
        dec_w = wide[:, SSD_INNER:2 * SSD_INNER]
        ea_w = wide[:, 2 * SSD_INNER:3 * SSD_INNER]
        xdt = (x * dt_w).astype(BF16)
        xw = (x * dec_w).astype(BF16)
        e_last = ea_w[last:last + 1]
        y_groups = []
        for g in range(SSD_GROUPS):
            gs = slice(g * gw, (g + 1) * gw)
            cg = cm[:, g * SSD_STATE:(g + 1) * SSD_STATE]
            bg = bm[:, g * SSD_STATE:(g + 1) * SSD_STATE]
            cb = _dot(cg, bg, _NT)
            st = st_ref[:, gs]
            y_state = _dot(cg, st.astype(BF16)) * ea_w[:, gs]
            ys = []
            for r in range(hpg):
                h = g * hpg + r
                a_i = wide[:, 3 * SSD_INNER + h * c:3 * SSD_INNER + (h + 1) * c]
                decay = jnp.exp(jnp.where(mask, a_i - acum_t[h:h + 1, :], -jnp.inf))
                scores = (cb * decay).astype(BF16)
                ys.append(_dot(scores, xdt[:, h * p:(h + 1) * p]))
            st_ref[:, gs] = st * e_last[:, gs] + _dot(bg, xw[:, gs], _TN)
            y_groups.append(jnp.concatenate(ys, axis=1) + y_state)
        y = jnp.concatenate(y_groups, axis=1)
        if finish:
            z = z_ref[0, rs, :].astype(F32)
            y = (y + yf_ref[0, rs, :] + dsk_ref[...] * x) * _silu(z)
            outs = []
            for g in range(SSD_GROUPS):
                sl = slice(g * gw, (g + 1) * gw)
                outs.append(_rms(y[:, sl], nw_ref[:, sl]))
            o_ref[0, rs, :] = jnp.concatenate(outs, axis=1).astype(o_ref.dtype)
        else:
            o_ref[0, rs, :] = y


def _ssd_scan(xbc, proj, dt_bias, neg_a, z_col, dtlr_col, ctx_len, direction, finish_args=None):
    bsz, tall, _ = xbc.shape
    tb = SCAN_BLOCK
    nb, ncb = tall // tb, ctx_len // tb
    reverse = direction == 1
    finish = finish_args is not None
    expand = _ssd_expand_matrix()

    def tok(col):
        return lambda b, s: (b, _scan_block(s, ncb, nb, reverse), col)

    in_specs = [pl.BlockSpec((1, tb, SSD_INNER), tok(0)),
                pl.BlockSpec((1, tb, SSD_BC), tok(SSD_INNER // SSD_BC)),
                pl.BlockSpec((1, tb, SSD_BC), tok(SSD_INNER // SSD_BC + 1)),
                pl.BlockSpec((1, tb, LANE), tok(dtlr_col)),
                pl.BlockSpec((1, SSD_HEADS), lambda b, s: (0, 0)),
                pl.BlockSpec((1, SSD_HEADS), lambda b, s: (0, 0)),
                pl.BlockSpec(expand.shape, lambda b, s: (0, 0))]
    args = [xbc, xbc, xbc, proj, dt_bias[direction:direction + 1], neg_a[direction:direction + 1], expand]
    if finish:
        y_f, d_skip_wide, norm_w = finish_args
        in_specs += [pl.BlockSpec((1, tb, SSD_INNER), tok(z_col)),
                     pl.BlockSpec((1, tb, SSD_INNER), tok(0)),
                     pl.BlockSpec((1, SSD_INNER), lambda b, s: (0, 0)),
                     pl.BlockSpec((1, SSD_INNER), lambda b, s: (0, 0))]
        args += [proj, y_f, d_skip_wide, norm_w]
    return pl.pallas_call(
        functools.partial(_ssd_kernel, direction=direction, finish=finish),
        grid=(bsz, nb),
        in_specs=in_specs,
        out_specs=pl.BlockSpec((1, tb, SSD_INNER), tok(0)),
        out_shape=jax.ShapeDtypeStruct((bsz, tall, SSD_INNER), BF16 if finish else F32),
        scratch_shapes=[pltpu.VMEM((SSD_STATE, SSD_INNER), F32)],
        compiler_params=_cparams(2),
        name="ssd_scan_bwd" if reverse else "ssd_scan_fwd",
    )(*args)


def _lin_kernel(*refs, mode, direction, finish, heads, dk, dv):
    refs = list(refs)
    if mode == "gla":
        q_ref, k_ref, v_ref, aux_ref, p1_ref, p2_ref = refs[:6]
        rest = refs[6:]
    else:
        q_ref, v_ref, aux_ref, p1_ref, p2_ref = refs[:5]
        k_ref = None
        rest = refs[5:]
    if finish:
        of_ref, gate_ref, nw_ref, o_ref, st_ref = rest
    else:
        o_ref, st_ref = rest
    reverse = direction == 1
    c = LIN_CHUNK

    @pl.when(pl.program_id(1) == 0)
    def _():
        st_ref[...] = jnp.zeros_like(st_ref)

    mask = _tri_mask(c, reverse)
    tri3 = _tri3(mask)
    last = 0 if reverse else c - 1
    n_chunks = q_ref.shape[1] // c
    order = range(n_chunks - 1, -1, -1) if reverse else range(n_chunks)
    for ck in order:
        rs = slice(ck * c, (ck + 1) * c)
        v = v_ref[0, rs, :]
        if mode == "gla":
            q = q_ref[0, rs, :].astype(F32) * (dk ** -0.5)
            k = k_ref[0, rs, :].astype(F32)
            off = 2 * SSD_HEADS + direction * GLA_GATE_RANK
            lr = aux_ref[0, rs, :][:, off:off + GLA_GATE_RANK]
            gk = _dot(lr, p1_ref[...]) + p2_ref[...]
            lg = _log_sigmoid(gk) * (1.0 / GLA_GATE_NORM)
        else:
            q = _silu(q_ref[0, rs, :].astype(F32))
            f_raw = aux_ref[0, rs, :].astype(F32)
            e = jnp.exp(-jnp.abs(f_raw))
            r = 1.0 / (1.0 + e)
            er = e * r
            pos = f_raw >= 0.0
            lg = jnp.log(p1_ref[...] + p2_ref[...] * jnp.where(pos, r, er))
            k = p2_ref[...] * jnp.where(pos, er, r)
        gcum = _cumsum_rows(tri3, lg)
        e_last = jnp.exp(gcum[last:last + 1])
        q_dec = (q * jnp.exp(gcum)).astype(BF16)
        k_invf = k * jnp.exp(-gcum)
        k_inv = k_invf.astype(BF16)
        k_end = (k_invf * e_last).astype(BF16)
        outs = []
        for h in range(heads):
            ks = slice(h * dk, (h + 1) * dk)
            vs = slice(h * dv, (h + 1) * dv)
            att = jnp.where(mask, _dot(q_dec[:, ks], k_inv[:, ks], _NT), 0.0).astype(BF16)
            st = st_ref[h]
            o_h = _dot(att, v[:, vs]) + _dot(q_dec[:, ks], st.astype(BF16), _NT)
            st_ref[h] = st * e_last[:, ks] + _dot(v[:, vs], k_end[:, ks], _TN)
            if finish:
                o_h = _rms(o_h + of_ref[0, rs, vs], nw_ref[...])
            outs.append(o_h)
        o = jnp.concatenate(outs, axis=1)
        if finish:
            o_ref[0, rs, :] = (o * _silu(gate_ref[0, rs, :].astype(F32))).astype(o_ref.dtype)
        else:
            o_ref[0, rs, :] = o


def _lin_scan(mode, proj, cols, params, ctx_len, direction, finish_args=None):
    bsz, tall, _ = proj.shape
    if mode == "gla":
        heads, dk, dv = GLA_HEADS, GLA_DK, GLA_DV
    else:
        heads, dk, dv = HGRN_HEADS, HGRN_DK, HGRN_DV
    kw, vw = heads * dk, heads * dv
    tb = SCAN_BLOCK
    nb, ncb = tall // tb, ctx_len // tb
    reverse = direction == 1
    finish = finish_args is not None

    def tok(col):
        return lambda b, s: (b, _scan_block(s, ncb, nb, reverse), col)

    def const2(shape):
        return pl.BlockSpec(shape, lambda b, s: (0, 0))

    p1, p2 = params
    if mode == "gla":
        in_specs = [pl.BlockSpec((1, tb, kw), tok(cols["q"])),
                    pl.BlockSpec((1, tb, kw), tok(cols["k"])),
                    pl.BlockSpec((1, tb, vw), tok(cols["v"])),
                    pl.BlockSpec((1, tb, LANE), tok(cols["aux"])),
                    const2(p1.shape), const2(p2.shape)]
        args = [proj, proj, proj, proj, p1, p2]
    else:
        in_specs = [pl.BlockSpec((1, tb, kw), tok(cols["q"])),
                    pl.BlockSpec((1, tb, vw), tok(cols["v"])),
                    pl.BlockSpec((1, tb, kw), tok(cols["aux"] + direction)),
                    const2(p1.shape), const2(p2.shape)]
        args = [proj, proj, proj, p1, p2]
    if finish:
        o_f, norm_w = finish_args
        in_specs += [pl.BlockSpec((1, tb, vw), tok(0)),
                     pl.BlockSpec((1, tb, vw), tok(cols["gate"])),
                     const2(norm_w.shape)]
        args += [o_f, proj, norm_w]
    return pl.pallas_call(
        functools.partial(_lin_kernel, mode=mode, direction=direction, finish=finish,
                          heads=heads, dk=dk, dv=dv),
        grid=(bsz, nb),
        in_specs=in_specs,
        out_specs=pl.BlockSpec((1, tb, vw), tok(0)),
        out_shape=jax.ShapeDtypeStruct((bsz, tall, vw), BF16 if finish else F32),
        scratch_shapes=[pltpu.VMEM((heads, dv, dk), F32)],
        compiler_params=_cparams(2),
        name=f"{mode}_scan_{'bwd' if reverse else 'fwd'}",
    )(*args)


def _s5_kernel(u_ref, bmat_ref, lre_ref, lim_ref, cmat_ref, o_ref, h_ref, st_ref, *, bsz):
    d = pl.program_id(0)
    steps = S5_CHUNK
    n_slabs = bmat_ref.shape[1]
    sw = bmat_ref.shape[3] // 2

    @pl.when(pl.program_id(1) == 0)
    def _():
        st_ref[...] = jnp.zeros_like(st_ref)

    u = u_ref[...].astype(BF16)
    outs = []
    for s in range(n_slabs):
        re_cols = slice(2 * s * sw, (2 * s + 1) * sw)
        im_cols = slice((2 * s + 1) * sw, (2 * s + 2) * sw)
        both = slice(2 * s * sw, (2 * s + 2) * sw)
        h_ref[:, both] = _dot(u[:, s * LANE:(s + 1) * LANE], bmat_ref[0, s])
        lam_re = jnp.broadcast_to(lre_ref[0, :, s * sw:(s + 1) * sw], (bsz, sw))
        lam_im = jnp.broadcast_to(lim_ref[0, :, s * sw:(s + 1) * sw], (bsz, sw))

        def body(tt, carry, re_cols=re_cols, im_cols=im_cols, lam_re=lam_re, lam_im=lam_im):
            hr, hi = carry
            t = jnp.where(d == 0, tt, steps - 1 - tt)
            rows = pl.ds(pl.multiple_of(t * bsz, bsz), bsz)
            nr = lam_re * hr - lam_im * hi + h_ref[rows, re_cols]
            ni = lam_re * hi + lam_im * hr + h_ref[rows, im_cols]
            h_ref[rows, re_cols] = nr
            h_ref[rows, im_cols] = ni
            return nr, ni

        hr, hi = lax.fori_loop(0, steps, body, (st_ref[:, re_cols], st_ref[:, im_cols]), unroll=8)
        st_ref[:, re_cols] = hr
        st_ref[:, im_cols] = hi
        outs.append(_dot(h_ref[:, both].astype(BF16), cmat_ref[s]))
    o_ref[0] = jnp.concatenate(outs, axis=1)


def _s5_scan(u_t, bmat, lam_re, lam_im, cmat, bsz, ctx_len):
    rows_total, width = u_t.shape
    tall = rows_total // bsz
    nch, ncc = tall // S5_CHUNK, ctx_len // S5_CHUNK
    n_state = lam_re.shape[-1]
    rows = S5_CHUNK * bsz

    def chunk(d, s):
        return jnp.where(d == 0, s, _scan_block(s, ncc, nch, True))

    return pl.pallas_call(
        functools.partial(_s5_kernel, bsz=bsz),
        grid=(2, nch),
        in_specs=[pl.BlockSpec((rows, width), lambda d, s: (chunk(d, s), 0)),
                  pl.BlockSpec((1,) + bmat.shape[1:], lambda d, s: (d, 0, 0, 0)),
                  pl.BlockSpec((1, 1, n_state), lambda d, s: (d, 0, 0)),
                  pl.BlockSpec((1, 1, n_state), lambda d, s: (d, 0, 0)),
                  pl.BlockSpec(cmat.shape, lambda d, s: (0, 0, 0))],
        out_specs=pl.BlockSpec((1, rows, width), lambda d, s: (d, chunk(d, s), 0)),
        out_shape=jax.ShapeDtypeStruct((2, rows_total, width), F32),
        scratch_shapes=[pltpu.VMEM((rows, 2 * n_state), F32),
                        pltpu.VMEM((bsz, 2 * n_state), F32)],
        compiler_params=_cparams(2),
        name="s5_scan",
    )(u_t, bmat, lam_re, lam_im, cmat)


def _out0_kernel(x_ref, a_ref, b_ref, wa_ref, wb_ref, gl_ref, gc_ref, o_ref, *, tm, ctx_len):
    i = pl.program_id(1)
    o = _dot(a_ref[0], wa_ref[...]) + _dot(b_ref[0], wb_ref[...])
    gate = _row_select(i, tm, ctx_len, gc_ref[...], gl_ref[0])
    o_ref[0] = x_ref[0] + gate * o


def _out_proj0(x_all, mix_a, mix_b, w_a, w_b, mod_l, mod_c, ctx_len, tm):
    bsz, tall, d = x_all.shape
    return pl.pallas_call(
        functools.partial(_out0_kernel, tm=tm, ctx_len=ctx_len),
        grid=(bsz, tall // tm),
        in_specs=[pl.BlockSpec((1, tm, d), lambda b, i: (b, i, 0)),
                  pl.BlockSpec((1, tm, mix_a.shape[2]), lambda b, i: (b, i, 0)),
                  pl.BlockSpec((1, tm, mix_b.shape[2]), lambda b, i: (b, i, 0)),
                  pl.BlockSpec(w_a.shape, lambda b, i: (0, 0)),
                  pl.BlockSpec(w_b.shape, lambda b, i: (0, 0))] + _mod_specs(d, (2,), 2),
        out_specs=pl.BlockSpec((1, tm, d), lambda b, i: (b, i, 0)),
        out_shape=jax.ShapeDtypeStruct((bsz, tall, d), F32),
        compiler_params=_cparams(2),
        name="out_proj_even",
    )(x_all, mix_a, mix_b, w_a, w_b, mod_l, mod_c)


def _gelu_tanh(x):
    return 0.5 * x * (1.0 + jnp.tanh(math.sqrt(2.0 / math.pi) * (x + 0.044715 * (x * x * x))))


def _out1_kernel(x_ref, a_ref, yf_ref, yb_ref, u_ref, dsk_ref, gw_ref, gb_ref, wa_ref, wb_ref,
                 gl_ref, gc_ref, o_ref, *, tm, ctx_len):
    i = pl.program_id(1)
    y = _gelu_tanh(yf_ref[0] + yb_ref[0] + dsk_ref[...] * u_ref[...])
    glu = _dot(y.astype(BF16), gw_ref[...]) + gb_ref[...]
    y = y * (1.0 / (1.0 + jnp.exp(-glu)))
    o = _dot(a_ref[0], wa_ref[...]) + _dot(y.astype(BF16), wb_ref[...])
    gate = _row_select(i, tm, ctx_len, gc_ref[...], gl_ref[0])
    o_ref[0] = x_ref[0] + gate * o


def _out_proj1(x_all, mix_a, y_dirs, u_t, d_skip, glu_w, glu_b, w_a, w_b, mod_l, mod_c, ctx_len, tm):
    bsz, tall, d = x_all.shape
    width = d_skip.shape[1]
    y3 = y_dirs.reshape(2, tall, bsz * width)
    u2 = u_t.reshape(tall, bsz * width)
    return pl.pallas_call(
        functools.partial(_out1_kernel, tm=tm, ctx_len=ctx_len),
        grid=(bsz, tall // tm),
        in_specs=[pl.BlockSpec((1, tm, d), lambda b, i: (b, i, 0)),
                  pl.BlockSpec((1, tm, mix_a.shape[2]), lambda b, i: (b, i, 0)),
                  pl.BlockSpec((1, tm, width), lambda b, i: (0, i, b)),
                  pl.BlockSpec((1, tm, width), lambda b, i: (1, i, b)),
                  pl.BlockSpec((tm, width), lambda b, i: (i, b)),
                  pl.BlockSpec((1, width), lambda b, i: (0, 0)),
                  pl.BlockSpec(glu_w.shape, lambda b, i: (0, 0)),
                  pl.BlockSpec((1, width), lambda b, i: (0, 0)),
                  pl.BlockSpec(w_a.shape, lambda b, i: (0, 0)),
                  pl.BlockSpec(w_b.shape, lambda b, i: (0, 0))] + _mod_specs(d, (2,), 2),
        out_specs=pl.BlockSpec((1, tm, d), lambda b, i: (b, i, 0)),
        out_shape=jax.ShapeDtypeStruct((bsz, tall, d), F32),
        compiler_params=_cparams(2),
        name="out_proj_odd",
    )(x_all, mix_a, y3, y3, u2, d_skip, glu_w, glu_b, w_a, w_b, mod_l, mod_c)


def _mlp_kernel(*refs, tm, ctx_len, final):
    if final:
        (x_ref, nw_ref, shl_ref, shc_ref, scl_ref, scc_ref, gl_ref, gc_ref, w1_ref, w2_ref, fw_ref,
         o_ref, h_scr, acc_scr) = refs
    else:
        (x_ref, nw_ref, shl_ref, shc_ref, scl_ref, scc_ref, gl_ref, gc_ref, w1_ref, w2_ref,
         o_ref, h_scr, acc_scr) = refs
    i = pl.program_id(1)
    j = pl.program_id(2)

    @pl.when(j == 0)
    def _():
        y = _rms(x_ref[0], nw_ref[...])
        shift = _row_select(i, tm, ctx_len, shc_ref[...], shl_ref[0])
        scale = _row_select(i, tm, ctx_len, scc_ref[...], scl_ref[0])
        h_scr[...] = (y * (1.0 + scale) + shift).astype(BF16)
        acc_scr[...] = jnp.zeros_like(acc_scr)

    a = jnp.maximum(_dot(h_scr[...], w1_ref[...]), 0.0)
    acc_scr[...] += _dot((a * a).astype(BF16), w2_ref[...])

    @pl.when(j == pl.num_programs(2) - 1)
    def _():
        gate = _row_select(i, tm, ctx_len, gc_ref[...], gl_ref[0])
        out = x_ref[0] + gate * acc_scr[...]
        if final:
            out = _rms(out, fw_ref[...])
        o_ref[0] = out


def _mlp(x_all, norm_w, mod_l, mod_c, w1, w2, final_w, ctx_len, tm, tf):
    bsz, tall, d = x_all.shape
    ff = w1.shape[1]
    final = final_w is not None
    in_specs = [pl.BlockSpec((1, tm, d), lambda b, i, j: (b, i, 0)),
                pl.BlockSpec((1, d), lambda b, i, j: (0, 0))]
    in_specs += _mod_specs(d, (3, 4, 5), 3)
    in_specs += [pl.BlockSpec((d, tf), lambda b, i, j: (0, j)),
                 pl.BlockSpec((tf, d), lambda b, i, j: (j, 0))]
    args = [x_all, norm_w.reshape(1, d)] + [mod_l, mod_c] * 3 + [w1, w2]
    if final:
        in_specs.append(pl.BlockSpec((1, d), lambda b, i, j: (0, 0)))
        args.append(final_w.reshape(1, d))
    return pl.pallas_call(
        functools.partial(_mlp_kernel, tm=tm, ctx_len=ctx_len, final=final),
        grid=(bsz, tall // tm, ff // tf),
        in_specs=in_specs,
        out_specs=pl.BlockSpec((1, tm, d), lambda b, i, j: (b, i, 0)),
        out_shape=jax.ShapeDtypeStruct((bsz, tall, d), F32),
        scratch_shapes=[pltpu.VMEM((tm, d), BF16), pltpu.VMEM((tm, d), F32)],
        compiler_params=_cparams(3),
        name="sq_relu_mlp",
    )(*args)


def _even_in_weight(w_in):
    sizes = (SSD_INNER, SSD_INNER + 2 * SSD_BC, 2 * SSD_HEADS, GLA_KEY, GLA_KEY, GLA_VAL,
             2 * GLA_GATE_RANK, GLA_VAL)
    offs = [0]
    for s in sizes:
        offs.append(offs[-1] + s)
    z, xbc, dt, q, k, v, lr, r = (w_in[:, offs[n]:offs[n + 1]] for n in range(8))
    pad = jnp.zeros((w_in.shape[0], LANE - dt.shape[1] - lr.shape[1]), w_in.dtype)
    return jnp.concatenate([xbc, z, v, r, q, k, dt, lr, pad], axis=1).astype(BF16)


def _s5_params(a_re, a_im, log_dt, b_re, b_im, c_re, c_im):
    delta = jnp.exp(log_dt.astype(F32))[..., None]
    mag = jnp.exp(a_re * delta)
    lbar_re, lbar_im = mag * jnp.cos(a_im * delta), mag * jnp.sin(a_im * delta)
    den = a_re * a_re + a_im * a_im
    zr = ((lbar_re - 1.0) * a_re + lbar_im * a_im) / den
    zi = (lbar_im * a_re - (lbar_re - 1.0) * a_im) / den
    bb_re = zr[..., None] * b_re - zi[..., None] * b_im
    bb_im = zr[..., None] * b_im + zi[..., None] * b_re
    n_slabs = S5_GROUPS // S5_SLAB
    eye = jnp.eye(S5_SLAB, dtype=F32)
    sw = S5_SLAB * S5_STATE

    def block_in(bb):
        bb = bb.reshape(2, n_slabs, S5_SLAB, S5_STATE, S5_GROUP)
        return jnp.einsum("dsgpc,gh->dsgchp", bb, eye).reshape(2, n_slabs, LANE, sw)

    def block_out(cc):
        cc = cc.reshape(n_slabs, S5_SLAB, S5_GROUP, S5_STATE)
        return jnp.einsum("sgcp,gh->sgphc", cc, eye).reshape(n_slabs, sw, LANE)

    bmat = jnp.concatenate([block_in(bb_re), block_in(bb_im)], axis=3).astype(BF16)
    cmat = jnp.concatenate([block_out(c_re), -block_out(c_im)], axis=1).astype(BF16)
    return (bmat, lbar_re.reshape(2, 1, S5_NSTATE), lbar_im.reshape(2, 1, S5_NSTATE), cmat)


def _layer_even(x_all, mod_l, mod_c, ctx_len, tm, norm1_w, w_in, conv_w, conv_b, dt_bias, a_log,
                d_skip, ssd_norm_w, gate_w, gate_b, gla_norm_w, w_out):
    w = _even_in_weight(w_in)
    n = w.shape[1]
    proj = _project(x_all, norm1_w, mod_l, mod_c, w, None, ctx_len, tm, _largest_divisor(n, LANE, 1024))
    n_xbc = SSD_INNER + 2 * SSD_BC
    c_z, c_v, c_r = n_xbc // SSD_INNER, n_xbc // GLA_VAL + 1, n_xbc // GLA_VAL + 2
    c_q = (n_xbc + 3 * SSD_INNER) // GLA_KEY
    c_aux = (n_xbc + 3 * SSD_INNER + 2 * GLA_KEY) // LANE
    xbc = _conv_silu(proj, conv_w, conv_b, ctx_len, n_xbc)

    neg_a = -jnp.exp(a_log.astype(F32))
    dt_bias = dt_bias.astype(F32)
    d_wide = jnp.repeat(d_skip.astype(F32), SSD_HEADDIM).reshape(1, SSD_INNER)
    y_f = _ssd_scan(xbc, proj, dt_bias, neg_a, c_z, c_aux, ctx_len, 0)
    y_mix = _ssd_scan(xbc, proj, dt_bias, neg_a, c_z, c_aux, ctx_len, 1,
                      (y_f, d_wide, ssd_norm_w.reshape(1, SSD_INNER)))

    cols = {"q": c_q, "k": c_q + 1, "v": c_v, "aux": c_aux, "gate": c_r}
    gparams = [(gate_w[d].astype(BF16), gate_b[d].reshape(1, GLA_KEY).astype(F32)) for d in range(2)]
    o_f = _lin_scan("gla", proj, cols, gparams[0], ctx_len, 0)
    o_mix = _lin_scan("gla", proj, cols, gparams[1], ctx_len, 1, (o_f, gla_norm_w.reshape(1, GLA_DV)))

    w_out = w_out.astype(BF16)
    return _out_proj0(x_all, y_mix, o_mix, w_out[:SSD_INNER], w_out[SSD_INNER:], mod_l, mod_c, ctx_len, tm)


def _layer_odd(x_all, mod_l, mod_c, ctx_len, tm, norm1_w, w_in, lb, hgrn_norm_w, a_re, a_im, log_dt,
               b_re, b_im, c_re, c_im, d_skip, glu_w, glu_b, w_out):
    bsz = x_all.shape[0]
    n_main = 5 * HGRN_WIDTH
    w_main = w_in[:, :n_main].astype(BF16)
    w_u = w_in[:, n_main:].astype(BF16)
    proj, u_t = _project(x_all, norm1_w, mod_l, mod_c, w_main, w_u, ctx_len, tm, HGRN_WIDTH)

    lb = lb.astype(F32).reshape(2, 1, HGRN_WIDTH)
    cols = {"q": 0, "v": 1, "aux": 2, "gate": 4}
    o_f = _lin_scan("hgrn", proj, cols, (lb[0], 1.0 - lb[0]), ctx_len, 0)
    o_mix = _lin_scan("hgrn", proj, cols, (lb[1], 1.0 - lb[1]), ctx_len, 1,
                      (o_f, hgrn_norm_w.reshape(1, HGRN_DV)))

    bmat, lam_re, lam_im, cmat = _s5_params(a_re.astype(F32), a_im.astype(F32), log_dt, b_re.astype(F32),
                                            b_im.astype(F32), c_re.astype(F32), c_im.astype(F32))
    y_dirs = _s5_scan(u_t.reshape(-1, S5_WIDTH), bmat, lam_re, lam_im, cmat, bsz, ctx_len)

    w_out = w_out.astype(BF16)
    return _out_proj1(x_all, o_mix, y_dirs, u_t, d_skip.astype(F32).reshape(1, S5_WIDTH),
                      glu_w.astype(BF16), glu_b.astype(F32).reshape(1, S5_WIDTH),
                      w_out[:HGRN_WIDTH], w_out[HGRN_WIDTH:], mod_l, mod_c, ctx_len, tm)


def kernel(x, c, ctx, c_ctx, ada_w, ada_b, norm1_w, norm2_w, ssd_gla_w_in, ssd_conv_w, ssd_conv_b, ssd_dt_bias, ssd_a_log, ssd_d, ssd_norm_w, gla_gate_w, gla_gate_b, gla_norm_w, ssd_gla_w_out, hgrn_s5_w_in, hgrn_lb_logits, hgrn_norm_w, s5_a_re, s5_a_im, s5_log_dt, s5_b_re, s5_b_im, s5_c_re, s5_c_im, s5_d, s5_glu_w, s5_glu_b, hgrn_s5_w_out, mlp_w1, mlp_w2, final_norm_w):
    bsz, seq, d = x.shape
    ctx_len = ctx.shape[1]
    depth = ada_w.shape[0]
    tall = ctx_len + seq
    assert bsz % 8 == 0 and ctx_len % SCAN_BLOCK == 0 and seq % SCAN_BLOCK == 0 and seq % GRID_W == 0
    tm = _largest_divisor(tall, 16, 1056)
    tf = 1024

    n_rows = -(-(bsz + 1) // 8) * 8
    cvec = jnp.concatenate([c, c_ctx[None, :], jnp.zeros((n_rows - bsz - 1, d), c.dtype)], axis=0)
    mod = _modulation(cvec.astype(F32), ada_w, ada_b)

    p_lb = jax.nn.softmax(hgrn_lb_logits.astype(F32), axis=0)
    lb_all = jnp.cumsum(p_lb, axis=0) - p_lb[0]

    x_all = jnp.concatenate([ctx, x], axis=1).astype(F32)
    for layer in range(depth):
        j = layer // 2
        mod_l = mod[layer, :bsz].reshape(bsz, 1, N_MOD * d)
        mod_c = mod[layer, bsz:bsz + 1]
        if layer % 2 == 0:
            x_all = _layer_even(x_all, mod_l, mod_c, ctx_len, tm, norm1_w[layer], ssd_gla_w_in[j],
                                ssd_conv_w[j], ssd_conv_b[j], ssd_dt_bias[j], ssd_a_log[j], ssd_d[j],
                                ssd_norm_w[j], gla_gate_w[j], gla_gate_b[j], gla_norm_w[j], ssd_gla_w_out[j])
        else:
            x_all = _layer_odd(x_all, mod_l, mod_c, ctx_len, tm, norm1_w[layer], hgrn_s5_w_in[j],
                               lb_all[layer], hgrn_norm_w[j], s5_a_re[j], s5_a_im[j], s5_log_dt[j],
                               s5_b_re[j], s5_b_im[j], s5_c_re[j], s5_c_im[j], s5_d[j], s5_glu_w[j],
                               s5_glu_b[j], hgrn_s5_w_out[j])
        last = layer == depth - 1
        x_all = _mlp(x_all, norm2_w[layer], mod_l, mod_c, mlp_w1[layer].astype(BF16),
                     mlp_w2[layer].astype(BF16), final_norm_w if last else None, ctx_len, tm, tf)
    return x_all[:, ctx_len:].astype(x.dtype)
```

```python
import functools
import math

import jax
import jax.numpy as jnp
from jax import lax
from jax.experimental import pallas as pl
from jax.experimental.pallas import tpu as pltpu

F32 = jnp.float32
BF16 = jnp.bfloat16

GRID_W = 64
NORM_EPS = 1e-6
N_MOD = 6
SSD_HEADDIM = 64
SSD_HEADS = 16
SSD_GROUPS = 4
SSD_STATE = 128
SSD_CHUNK = 128
GLA_HEADS = 8
GLA_DK = 64
GLA_DV = 128
GLA_GATE_RANK = 16
GLA_GATE_NORM = 16.0
HGRN_HEADS = 8
HGRN_DK = 128
HGRN_DV = 128
S5_GROUP = 16
S5_GROUPS = 24
S5_STATE = 64
LIN_CHUNK = 64

SSD_INNER = SSD_HEADS * SSD_HEADDIM
SSD_BC = SSD_GROUPS * SSD_STATE
GLA_KEY = GLA_HEADS * GLA_DK
GLA_VAL = GLA_HEADS * GLA_DV
HGRN_WIDTH = HGRN_HEADS * HGRN_DV
S5_WIDTH = S5_GROUPS * S5_GROUP
S5_NSTATE = S5_GROUPS * S5_STATE

VMEM_LIMIT_BYTES = 56 * 1024 * 1024
LANE = 128
SCAN_BLOCK = 256
S5_CHUNK = 128
S5_SLAB = LANE // S5_GROUP


def _cparams(n_axes):
    return pltpu.CompilerParams(dimension_semantics=("arbitrary",) * n_axes,
                                vmem_limit_bytes=VMEM_LIMIT_BYTES)


def _largest_divisor(n, multiple, cap):
    best = None
    for d in range(multiple, min(n, cap) + 1, multiple):
        if n % d == 0:
            best = d
    assert best is not None, (n, multiple, cap)
    return best


_NEG_LOG2E = -1.4426950408889634


def _sigmoid(x):
    return 1.0 / (1.0 + jnp.exp2(x * _NEG_LOG2E))


def _silu(x):
    return x * _sigmoid(x)


def _softplus(x):
    return jnp.maximum(x, 0.0) + jnp.log1p(jnp.exp(-jnp.abs(x)))


def _log_sigmoid(x):
    return -_softplus(-x)


def _rms(x, w):
    return x * lax.rsqrt(jnp.mean(x * x, axis=-1, keepdims=True) + NORM_EPS) * w


def _dot(a, b, dims=(((1,), (0,)), ((), ())), precision=None):
    return lax.dot_general(a, b, dims, precision=precision, preferred_element_type=F32)


def _split3(v):
    hi = v.astype(BF16)
    r1 = v - hi.astype(F32)
    mid = r1.astype(BF16)
    lo = (r1 - mid.astype(F32)).astype(BF16)
    return hi, mid, lo


def _tri3(mask):
    tri = mask.astype(BF16)
    return jnp.concatenate([tri, tri, tri], axis=1)


def _cumsum_rows(tri2, v):
    hi = v.astype(BF16)
    lo = (v - hi.astype(F32)).astype(BF16)
    return _dot(tri2, jnp.concatenate([hi, lo], axis=0))


_NT = (((1,), (1,)), ((), ()))
_TN = (((0,), (0,)), ((), ()))
_TT = (((0,), (1,)), ((), ()))


def _mod_kernel(c_ref, w_ref, b_ref, o_ref):
    a = _silu(c_ref[...]).astype(BF16)
    o_ref[0] = _dot(a, w_ref[0].astype(BF16)) + b_ref[0]


def _modulation(cvec, ada_w, ada_b):
    depth, d, n = ada_w.shape
    rows = cvec.shape[0]
    tn = _largest_divisor(n, LANE, 1024)
    return pl.pallas_call(
        _mod_kernel,
        grid=(depth, n // tn),
        in_specs=[pl.BlockSpec((rows, d), lambda l, j: (0, 0)),
                  pl.BlockSpec((1, d, tn), lambda l, j: (l, 0, j)),
                  pl.BlockSpec((1, 1, tn), lambda l, j: (l, 0, j))],
        out_specs=pl.BlockSpec((1, rows, tn), lambda l, j: (l, 0, j)),
        out_shape=jax.ShapeDtypeStruct((depth, rows, n), F32),
        compiler_params=_cparams(2),
        name="adaln_mod",
    )(cvec, ada_w, ada_b.reshape(depth, 1, n))


def _row_select(i, tm, ctx_len, ctx_val, lat_val):
    row = i * tm + lax.broadcasted_iota(jnp.int32, (tm, 1), 0)
    return jnp.where(row < ctx_len, ctx_val, lat_val)


def _mod_specs(d, cols, n_grid_axes):
    specs = []
    for k in cols:
        if n_grid_axes == 2:
            specs.append(pl.BlockSpec((1, 1, d), lambda b, i, k=k: (b, 0, k)))
            specs.append(pl.BlockSpec((1, d), lambda b, i, k=k: (0, k)))
        else:
            specs.append(pl.BlockSpec((1, 1, d), lambda b, i, j, k=k: (b, 0, k)))
            specs.append(pl.BlockSpec((1, d), lambda b, i, j, k=k: (0, k)))
    return specs


def _proj_kernel(*refs, tm, ctx_len, has_extra):
    if has_extra:
        (x_ref, nw_ref, shl_ref, shc_ref, scl_ref, scc_ref, w_ref, wx_ref, o_ref, ox_ref, h_scr) = refs
    else:
        (x_ref, nw_ref, shl_ref, shc_ref, scl_ref, scc_ref, w_ref, o_ref, h_scr) = refs
    i = pl.program_id(1)
    j = pl.program_id(2)

    @pl.when(j == 0)
    def _():
        y = _rms(x_ref[0], nw_ref[...])
        shift = _row_select(i, tm, ctx_len, shc_ref[...], shl_ref[0])
        scale = _row_select(i, tm, ctx_len, scc_ref[...], scl_ref[0])
        h = (y * (1.0 + scale) + shift).astype(BF16)
        h_scr[...] = h
        if has_extra:
            ox_ref[...] = _dot(h, wx_ref[...])

    o_ref[0] = _dot(h_scr[...], w_ref[...]).astype(o_ref.dtype)


def _project(x_all, norm_w, mod_l, mod_c, w, w_extra, ctx_len, tm, tn):
    bsz, tall, d = x_all.shape
    n = w.shape[1]
    has_extra = w_extra is not None
    in_specs = [pl.BlockSpec((1, tm, d), lambda b, i, j: (b, i, 0)),
                pl.BlockSpec((1, d), lambda b, i, j: (0, 0))]
    in_specs += _mod_specs(d, (0, 1), 3)
    in_specs.append(pl.BlockSpec((d, tn), lambda b, i, j: (0, j)))
    args = [x_all, norm_w.reshape(1, d), mod_l, mod_c, mod_l, mod_c, w]
    out_specs = [pl.BlockSpec((1, tm, tn), lambda b, i, j: (b, i, j))]
    out_shape = [jax.ShapeDtypeStruct((bsz, tall, n), BF16)]
    if has_extra:
        nx = w_extra.shape[1]
        in_specs.append(pl.BlockSpec((d, nx), lambda b, i, j: (0, 0)))
        args.append(w_extra)
        out_specs.append(pl.BlockSpec((tm, nx), lambda b, i, j: (i, b)))
        out_shape.append(jax.ShapeDtypeStruct((tall, bsz * nx), F32))
    out = pl.pallas_call(
        functools.partial(_proj_kernel, tm=tm, ctx_len=ctx_len, has_extra=has_extra),
        grid=(bsz, tall // tm, n // tn),
        in_specs=in_specs,
        out_specs=out_specs,
        out_shape=out_shape,
        scratch_shapes=[pltpu.VMEM((tm, d), BF16)],
        compiler_params=_cparams(3),
        name="norm_mod_proj",
    )(*args)
    return out if has_extra else out[0]


def _conv_kernel(main_ref, prev_ref, next_ref, w_ref, b_ref, o_ref, *, tt, ctx_len, n_lat):
    i = pl.program_id(1)
    p = i * tt + lax.broadcasted_iota(jnp.int32, (tt, 1), 0)
    is_ctx = p < ctx_len
    q = p - ctx_len
    col = jnp.bitwise_and(p, GRID_W - 1)
    m_up = q >= GRID_W
    m_dn = jnp.where(is_ctx, n_lat, q) < n_lat - GRID_W
    m_l = jnp.where(is_ctx, p, col) > 0
    m_r = jnp.where(is_ctx, p - (ctx_len - 1), col - (GRID_W - 1)) < 0

    main = main_ref[0].astype(F32)
    rows = {
        -1: jnp.concatenate([prev_ref[0].astype(F32), main[:tt - GRID_W]], axis=0),
        0: main,
        1: jnp.concatenate([main[GRID_W:], next_ref[0].astype(F32)], axis=0),
    }
    w = w_ref[...]
    acc = jnp.zeros_like(main) + b_ref[...]
    for dy in (-1, 0, 1):
        s = rows[dy]
        k0 = 3 * (dy + 1)
        t = (s * w[k0 + 1:k0 + 2]
             + jnp.where(m_l, pltpu.roll(s, 1, 0), 0.0) * w[k0:k0 + 1]
             + jnp.where(m_r, pltpu.roll(s, tt - 1, 0), 0.0) * w[k0 + 2:k0 + 3])
        if dy == -1:
            t = jnp.where(m_up, t, 0.0)
        elif dy == 1:
            t = jnp.where(m_dn, t, 0.0)
        acc = acc + t
    o_ref[0] = _silu(acc).astype(o_ref.dtype)


def _conv_silu(proj, conv_w, conv_b, ctx_len, n_ch):
    bsz, tall, _ = proj.shape
    n_rows = tall // GRID_W
    tt = _largest_divisor(tall, GRID_W, 768)
    assert ctx_len <= tt and ctx_len % GRID_W == 0
    r = tt // GRID_W
    tc = 512
    return pl.pallas_call(
        functools.partial(_conv_kernel, tt=tt, ctx_len=ctx_len, n_lat=tall - ctx_len),
        grid=(bsz, tall // tt, n_ch // tc),
        in_specs=[pl.BlockSpec((1, tt, tc), lambda b, i, c: (b, i, c)),
                  pl.BlockSpec((1, GRID_W, tc), lambda b, i, c: (b, jnp.maximum(i * r - 1, 0), c)),
                  pl.BlockSpec((1, GRID_W, tc), lambda b, i, c: (b, jnp.minimum((i + 1) * r, n_rows - 1), c)),
                  pl.BlockSpec((9, tc), lambda b, i, c: (0, c)),
                  pl.BlockSpec((1, tc), lambda b, i, c: (0, c))],
        out_specs=pl.BlockSpec((1, tt, tc), lambda b, i, c: (b, i, c)),
        out_shape=jax.ShapeDtypeStruct((bsz, tall, n_ch), BF16),
        compiler_params=_cparams(3),
        name="dwconv_silu",
    )(proj, proj, proj, conv_w.reshape(9, n_ch), conv_b.reshape(1, n_ch))


def _scan_block(s, n_ctx_blocks, n_blocks, reverse):
    if not reverse:
        return s
    return jnp.where(s < n_ctx_blocks, n_ctx_blocks - 1 - s, n_blocks - 1 - s + n_ctx_blocks)


def _tri_mask(c, reverse):
    ri = lax.broadcasted_iota(jnp.int32, (c, c), 0)
    ci = lax.broadcasted_iota(jnp.int32, (c, c), 1)
    return (ci >= ri) if reverse else (ci <= ri)


def _ssd_expand_matrix():
    eye = jnp.eye(SSD_HEADS, dtype=F32)
    e_head = jnp.repeat(eye, SSD_HEADDIM, axis=1)
    e_seg = jnp.repeat(eye, SSD_CHUNK, axis=1)
    zh = jnp.zeros_like(e_head)
    zs = jnp.zeros_like(e_seg)
    blk = jnp.concatenate([
        jnp.concatenate([e_head, zh, zh, zs], axis=1),
        jnp.concatenate([zh, e_head, zh, zs], axis=1),
        jnp.concatenate([zh, zh, e_head, zs], axis=1),
        jnp.concatenate([zh, zh, zh, e_seg], axis=1)], axis=0)
    return jnp.concatenate([blk, blk, blk], axis=0).astype(BF16)


def _ssd_kernel(*refs, direction, finish):
    if finish:
        (x_ref, bm_ref, cm_ref, dtlr_ref, dtb_ref, nega_ref, exp_ref, z_ref, yf_ref, dsk_ref, nw_ref,
         o_ref, st_ref) = refs
    else:
        (x_ref, bm_ref, cm_ref, dtlr_ref, dtb_ref, nega_ref, exp_ref, o_ref, st_ref) = refs
    reverse = direction == 1
    c = SSD_CHUNK
    p = SSD_HEADDIM
    gw = SSD_INNER // SSD_GROUPS
    hpg = SSD_HEADS // SSD_GROUPS

    @pl.when(pl.program_id(1) == 0)
    def _():
        st_ref[...] = jnp.zeros_like(st_ref)

    mask = _tri_mask(c, reverse)
    tri3 = _tri3(mask)
    last = 0 if reverse else c - 1
    n_chunks = x_ref.shape[1] // c
    order = range(n_chunks - 1, -1, -1) if reverse else range(n_chunks)
    groups = range(SSD_GROUPS)
    g_cols = [slice(g * gw, (g + 1) * gw) for g in groups]
    n_cols = [slice(g * SSD_STATE, (g + 1) * SSD_STATE) for g in groups]
    prep = {}
    for ck in order:
        rs = slice(ck * c, (ck + 1) * c)
        x = x_ref[0, rs, :].astype(F32)
        dt_raw = dtlr_ref[0, rs, :][:, direction * SSD_HEADS:(direction + 1) * SSD_HEADS].astype(F32)
        dt = _softplus(dt_raw + dtb_ref[...])
        la3 = jnp.concatenate(_split3(dt * nega_ref[...]), axis=0)
        acum = _dot(tri3, la3)
        acum_t = _dot(la3, tri3, _TT)
        a_last = acum[last:last + 1]
        narrow = jnp.concatenate([dt, dt * jnp.exp(a_last - acum), jnp.exp(acum), acum], axis=1)
        wide = _dot(jnp.concatenate(_split3(narrow), axis=1), exp_ref[...])
        ea_w = wide[:, 2 * SSD_INNER:3 * SSD_INNER]
        prep[ck] = dict(
            x=x, bm=bm_ref[0, rs, :], cm=cm_ref[0, rs, :], wide=wide, acum_t=acum_t, ea_w=ea_w,
            xdt=(x * wide[:, :SSD_INNER]).astype(BF16),
            xw=(x * wide[:, SSD_INNER:2 * SSD_INNER]).astype(BF16),
            e_last=ea_w[last:last + 1])
    cb = {(ck, g): _dot(prep[ck]["cm"][:, n_cols[g]], prep[ck]["bm"][:, n_cols[g]], _NT)
          for ck in order for g in groups}
    kv = {(ck, g): _dot(prep[ck]["bm"][:, n_cols[g]], prep[ck]["xw"][:, g_cols[g]], _TN)
          for ck in order for g in groups}
    scores, st_used = {}, {}
    for g in groups:
        st = st_ref[:, g_cols[g]]
        for ck in order:
            st_used[ck, g] = st.astype(BF16)
            st = st * prep[ck]["e_last"][:, g_cols[g]] + kv[ck, g]
        st_ref[:, g_cols[g]] = st
    for ck in order:
        for h in range(SSD_HEADS):
            a_i = prep[ck]["wide"][:, 3 * SSD_INNER + h * c:3 * SSD_INNER + (h + 1) * c]
            decay = jnp.exp(jnp.where(mask, a_i - prep[ck]["acum_t"][h:h + 1, :], -jnp.inf))
            scores[ck, h] = (cb[ck, h // hpg] * decay).astype(BF16)
    for ck in order:
        rs = slice(ck * c, (ck + 1) * c)
        x = prep[ck]["x"]
        y_groups = []
        for g in groups:
            ys = [_dot(scores[ck, h], prep[ck]["xdt"][:, h * p:(h + 1) * p])
                  for h in range(g * hpg, (g + 1) * hpg)]
            y_state = _dot(prep[ck]["cm"][:, n_cols[g]], st_used[ck, g]) * prep[ck]["ea_w"][:, g_cols[g]]
            y_groups.append(jnp.concatenate(ys, axis=1) + y_state)
        y = jnp.concatenate(y_groups, axis=1)
        if finish:
            z = z_ref[0, rs, :].astype(F32)
            y = (y + yf_ref[0, rs, :] + dsk_ref[...] * x) * _silu(z)
            outs = []
            for g in range(SSD_GROUPS):
                sl = slice(g * gw, (g + 1) * gw)
                outs.append(_rms(y[:, sl], nw_ref[:, sl]))
            o_ref[0, rs, :] = jnp.concatenate(outs, axis=1).astype(o_ref.dtype)
        else:
            o_ref[0, rs, :] = y


def _ssd_scan(xbc, proj, dt_bias, neg_a, z_col, dtlr_col, ctx_len, direction, finish_args=None):
    bsz, tall, _ = xbc.shape
    tb = SCAN_BLOCK
    nb, ncb = tall // tb, ctx_len // tb
    reverse = direction == 1
    finish = finish_args is not None
    expand = _ssd_expand_matrix()

    def tok(col):
        return lambda b, s: (b, _scan_block(s, ncb, nb, reverse), col)

    in_specs = [pl.BlockSpec((1, tb, SSD_INNER), tok(0)),
                pl.BlockSpec((1, tb, SSD_BC), tok(SSD_INNER // SSD_BC)),
                pl.BlockSpec((1, tb, SSD_BC), tok(SSD_INNER // SSD_BC + 1)),
                pl.BlockSpec((1, tb, LANE), tok(dtlr_col)),
                pl.BlockSpec((1, SSD_HEADS), lambda b, s: (0, 0)),
                pl.BlockSpec((1, SSD_HEADS), lambda b, s: (0, 0)),
                pl.BlockSpec(expand.shape, lambda b, s: (0, 0))]
    args = [xbc, xbc, xbc, proj, dt_bias[direction:direction + 1], neg_a[direction:direction + 1], expand]
    if finish:
        y_f, d_skip_wide, norm_w = finish_args
        in_specs += [pl.BlockSpec((1, tb, SSD_INNER), tok(z_col)),
                     pl.BlockSpec((1, tb, SSD_INNER), tok(0)),
                     pl.BlockSpec((1, SSD_INNER), lambda b, s: (0, 0)),
                     pl.BlockSpec((1, SSD_INNER), lambda b, s: (0, 0))]
        args += [proj, y_f, d_skip_wide, norm_w]
    return pl.pallas_call(
        functools.partial(_ssd_kernel, direction=direction, finish=finish),
        grid=(bsz, nb),
        in_specs=in_specs,
        out_specs=pl.BlockSpec((1, tb, SSD_INNER), tok(0)),
        out_shape=jax.ShapeDtypeStruct((bsz, tall, SSD_INNER), BF16 if finish else F32),
        scratch_shapes=[pltpu.VMEM((SSD_STATE, SSD_INNER), F32)],
        compiler_params=_cparams(2),
        name="ssd_scan_bwd" if reverse else "ssd_scan_fwd",
    )(*args)


def _lin_kernel(*refs, mode, direction, finish, heads, dk, dv):
    refs = list(refs)
    if mode == "gla":
        q_ref, k_ref, v_ref, aux_ref, p1_ref, p2_ref = refs[:6]
        rest = refs[6:]
    else:
        q_ref, v_ref, aux_ref, p1_ref, p2_ref = refs[:5]
        k_ref = None
        rest = refs[5:]
    if finish:
        of_ref, gate_ref, nw_ref, o_ref, st_ref = rest
    else:
        o_ref, st_ref = rest
    reverse = direction == 1
    c = LIN_CHUNK

    @pl.when(pl.program_id(1) == 0)
    def _():
        st_ref[...] = jnp.zeros_like(st_ref)

    tb = q_ref.shape[1]
    nc = tb // c
    hpt = LANE // dk
    chunks = range(nc - 1, -1, -1) if reverse else range(nc)
    last = 0 if reverse else c - 1

    ri = lax.broadcasted_iota(jnp.int32, (tb, tb), 0)
    ci = lax.broadcasted_iota(jnp.int32, (tb, tb), 1)
    c_shift = c.bit_length() - 1
    same_chunk = jnp.right_shift(ri, c_shift) == jnp.right_shift(ci, c_shift)
    if reverse:
        bd_mask = jnp.where(same_chunk, ci - ri, -1) >= 0
    else:
        bd_mask = jnp.where(same_chunk, ci - ri, 1) <= 0
    tri = _tri_mask(c, reverse).astype(BF16)
    tri2 = jnp.concatenate([tri, tri], axis=1)

    v = v_ref[0]
    if mode == "gla":
        q = q_ref[0].astype(F32) * (dk ** -0.5)
        k = k_ref[0].astype(F32)
        off = 2 * SSD_HEADS + direction * GLA_GATE_RANK
        lr = aux_ref[0][:, off:off + GLA_GATE_RANK]
        gk = _dot(lr, p1_ref[...]) + p2_ref[...]
        lg = _log_sigmoid(gk) * (1.0 / GLA_GATE_NORM)
    else:
        q = _silu(q_ref[0].astype(F32))
        f_raw = aux_ref[0].astype(F32)
        e = jnp.exp2(jnp.abs(f_raw) * _NEG_LOG2E)
        r = 1.0 / (1.0 + e)
        forget = p1_ref[...] + p2_ref[...] * jnp.where(f_raw >= 0.0, r, e * r)
        lg = jnp.log(forget)
        k = 1.0 - forget

    gcum = jnp.concatenate([_cumsum_rows(tri2, lg[cc * c:(cc + 1) * c]) for cc in range(nc)], axis=0)
    e_last = [jnp.exp(gcum[cc * c + last:cc * c + last + 1]) for cc in range(nc)]
    e_rows = jnp.concatenate([jnp.broadcast_to(e, (c, e.shape[1])) for e in e_last], axis=0)
    e_gcum = jnp.exp(gcum)
    q_decf = q * e_gcum
    if hpt > 1:
        head_in_tile = jnp.bitwise_and(jnp.right_shift(
            lax.broadcasted_iota(jnp.int32, (1, q_decf.shape[1]), 1), dk.bit_length() - 1), hpt - 1)
        q_dec = [jnp.where(head_in_tile == r, q_decf, 0.0).astype(BF16) for r in range(hpt)]
    else:
        q_dec = [q_decf.astype(BF16)]
    k_invf = k * (1.0 / e_gcum)
    k_inv = k_invf.astype(BF16)
    k_end = (k_invf * e_rows).astype(BF16)
    zeros = jnp.zeros((c, LANE), BF16)

    def chunk_blocks(a):
        cols = []
        for b in range(nc):
            cols.append(jnp.concatenate(
                [a[cc * c:(cc + 1) * c] if cc == b else zeros for cc in range(nc)], axis=0))
        return jnp.concatenate(cols, axis=1)

    lane_tiles = [slice((h // hpt) * LANE, (h // hpt + 1) * LANE) for h in range(heads)]
    v_cols = [slice(h * dv, (h + 1) * dv) for h in range(heads)]
    q_heads = [q_dec[h % hpt][:, lane_tiles[h]] for h in range(heads)]
    scores = [_dot(q_heads[h], k_inv[:, lane_tiles[h]], _NT) for h in range(heads)]
    kv_t = [_dot(v[:, v_cols[h]], chunk_blocks(k_end[:, lane_tiles[h]]), _TN) for h in range(heads)]
    atts, st_cats = [], []
    for h in range(heads):
        atts.append(jnp.where(bd_mask, scores[h], 0.0).astype(BF16))
        st = st_ref[h]
        used = [None] * nc
        for cc in chunks:
            used[cc] = st.astype(BF16)
            st = st * e_last[cc][:, lane_tiles[h]] + kv_t[h][:, cc * LANE:(cc + 1) * LANE]
        st_ref[h] = st
        st_cats.append(used)
    outs = []
    for h in range(heads):
        o_state = jnp.concatenate(
            [_dot(q_heads[h][cc * c:(cc + 1) * c], st_cats[h][cc], _NT) for cc in range(nc)], axis=0)
        o_h = _dot(atts[h], v[:, v_cols[h]]) + o_state
        if finish:
            o_h = _rms(o_h + of_ref[0, :, v_cols[h]], nw_ref[...])
        outs.append(o_h)
    o = jnp.concatenate(outs, axis=1)
    if finish:
        o_ref[0] = (o * _silu(gate_ref[0].astype(F32))).astype(o_ref.dtype)
    else:
        o_ref[0] = o


def _lin_scan(mode, proj, cols, params, ctx_len, direction, finish_args=None):
    bsz, tall, _ = proj.shape
    if mode == "gla":
        heads, dk, dv = GLA_HEADS, GLA_DK, GLA_DV
    else:
        heads, dk, dv = HGRN_HEADS, HGRN_DK, HGRN_DV
    kw, vw = heads * dk, heads * dv
    tb = SCAN_BLOCK
    nb, ncb = tall // tb, ctx_len // tb
    reverse = direction == 1
    finish = finish_args is not None

    def tok(col):
        return lambda b, s: (b, _scan_block(s, ncb, nb, reverse), col)

    def const2(shape):
        return pl.BlockSpec(shape, lambda b, s: (0, 0))

    p1, p2 = params
    if mode == "gla":
        in_specs = [pl.BlockSpec((1, tb, kw), tok(cols["q"])),
                    pl.BlockSpec((1, tb, kw), tok(cols["k"])),
                    pl.BlockSpec((1, tb, vw), tok(cols["v"])),
                    pl.BlockSpec((1, tb, LANE), tok(cols["aux"])),
                    const2(p1.shape), const2(p2.shape)]
        args = [proj, proj, proj, proj, p1, p2]
    else:
        in_specs = [pl.BlockSpec((1, tb, kw), tok(cols["q"])),
                    pl.BlockSpec((1, tb, vw), tok(cols["v"])),
                    pl.BlockSpec((1, tb, kw), tok(cols["aux"] + direction)),
                    const2(p1.shape), const2(p2.shape)]
        args = [proj, proj, proj, p1, p2]
    if finish:
        o_f, norm_w = finish_args
        in_specs += [pl.BlockSpec((1, tb, vw), tok(0)),
                     pl.BlockSpec((1, tb, vw), tok(cols["gate"])),
                     const2(norm_w.shape)]
        args += [o_f, proj, norm_w]
    return pl.pallas_call(
        functools.partial(_lin_kernel, mode=mode, direction=direction, finish=finish,
                          heads=heads, dk=dk, dv=dv),
        grid=(bsz, nb),
        in_specs=in_specs,
        out_specs=pl.BlockSpec((1, tb, vw), tok(0)),
        out_shape=jax.ShapeDtypeStruct((bsz, tall, vw), BF16 if finish else F32),
        scratch_shapes=[pltpu.VMEM((heads, dv, LANE), F32)],
        compiler_params=_cparams(2),
        name=f"{mode}_scan_{'bwd' if reverse else 'fwd'}",
    )(*args)


def _s5_kernel(u_ref, bmat_ref, lre_ref, lim_ref, cmat_ref, o_ref, h_ref, st_ref, *, bsz):
    d = pl.program_id(0)
    steps = S5_CHUNK
    n_slabs = bmat_ref.shape[1]
    sw = bmat_ref.shape[3] // 2

    @pl.when(pl.program_id(1) == 0)
    def _():
        st_ref[...] = jnp.zeros_like(st_ref)

    u = u_ref[...].astype(BF16)
    outs = []
    for s in range(n_slabs):
        re_cols = slice(2 * s * sw, (2 * s + 1) * sw)
        im_cols = slice((2 * s + 1) * sw, (2 * s + 2) * sw)
        both = slice(2 * s * sw, (2 * s + 2) * sw)
        h_ref[:, both] = _dot(u[:, s * LANE:(s + 1) * LANE], bmat_ref[0, s])
        lam_re = jnp.broadcast_to(lre_ref[0, :, s * sw:(s + 1) * sw], (bsz, sw))
        lam_im = jnp.broadcast_to(lim_ref[0, :, s * sw:(s + 1) * sw], (bsz, sw))

        def body(tt, carry, re_cols=re_cols, im_cols=im_cols, lam_re=lam_re, lam_im=lam_im):
            hr, hi = carry
            t = jnp.where(d == 0, tt, steps - 1 - tt)
            rows = pl.ds(pl.multiple_of(t * bsz, bsz), bsz)
            nr = lam_re * hr - lam_im * hi + h_ref[rows, re_cols]
            ni = lam_re * hi + lam_im * hr + h_ref[rows, im_cols]
            h_ref[rows, re_cols] = nr
            h_ref[rows, im_cols] = ni
            return nr, ni

        hr, hi = lax.fori_loop(0, steps, body, (st_ref[:, re_cols], st_ref[:, im_cols]), unroll=8)
        st_ref[:, re_cols] = hr
        st_ref[:, im_cols] = hi
        outs.append(_dot(h_ref[:, both].astype(BF16), cmat_ref[s]))
    o_ref[0] = jnp.concatenate(outs, axis=1)


def _s5_scan(u_t, bmat, lam_re, lam_im, cmat, bsz, ctx_len):
    rows_total, width = u_t.shape
    tall = rows_total // bsz
    nch, ncc = tall // S5_CHUNK, ctx_len // S5_CHUNK
    n_state = lam_re.shape[-1]
    rows = S5_CHUNK * bsz

    def chunk(d, s):
        return jnp.where(d == 0, s, _scan_block(s, ncc, nch, True))

    return pl.pallas_call(
        functools.partial(_s5_kernel, bsz=bsz),
        grid=(2, nch),
        in_specs=[pl.BlockSpec((rows, width), lambda d, s: (chunk(d, s), 0)),
                  pl.BlockSpec((1,) + bmat.shape[1:], lambda d, s: (d, 0, 0, 0)),
                  pl.BlockSpec((1, 1, n_state), lambda d, s: (d, 0, 0)),
                  pl.BlockSpec((1, 1, n_state), lambda d, s: (d, 0, 0)),
                  pl.BlockSpec(cmat.shape, lambda d, s: (0, 0, 0))],
        out_specs=pl.BlockSpec((1, rows, width), lambda d, s: (d, chunk(d, s), 0)),
        out_shape=jax.ShapeDtypeStruct((2, rows_total, width), F32),
        scratch_shapes=[pltpu.VMEM((rows, 2 * n_state), F32),
                        pltpu.VMEM((bsz, 2 * n_state), F32)],
        compiler_params=_cparams(2),
        name="s5_scan",
    )(u_t, bmat, lam_re, lam_im, cmat)


def _out0_kernel(x_ref, a_ref, b_ref, wa_ref, wb_ref, gl_ref, gc_ref, o_ref, *, tm, ctx_len):
    i = pl.program_id(1)
    o = _dot(a_ref[0], wa_ref[...]) + _dot(b_ref[0], wb_ref[...])
    gate = _row_select(i, tm, ctx_len, gc_ref[...], gl_ref[0])
    o_ref[0] = x_ref[0] + gate * o


def _out_proj0(x_all, mix_a, mix_b, w_a, w_b, mod_l, mod_c, ctx_len, tm):
    bsz, tall, d = x_all.shape
    return pl.pallas_call(
        functools.partial(_out0_kernel, tm=tm, ctx_len=ctx_len),
        grid=(bsz, tall // tm),
        in_specs=[pl.BlockSpec((1, tm, d), lambda b, i: (b, i, 0)),
                  pl.BlockSpec((1, tm, mix_a.shape[2]), lambda b, i: (b, i, 0)),
                  pl.BlockSpec((1, tm, mix_b.shape[2]), lambda b, i: (b, i, 0)),
                  pl.BlockSpec(w_a.shape, lambda b, i: (0, 0)),
                  pl.BlockSpec(w_b.shape, lambda b, i: (0, 0))] + _mod_specs(d, (2,), 2),
        out_specs=pl.BlockSpec((1, tm, d), lambda b, i: (b, i, 0)),
        out_shape=jax.ShapeDtypeStruct((bsz, tall, d), F32),
        compiler_params=_cparams(2),
        name="out_proj_even",
    )(x_all, mix_a, mix_b, w_a, w_b, mod_l, mod_c)


def _gelu_tanh(x):
    return 0.5 * x * (1.0 + jnp.tanh(math.sqrt(2.0 / math.pi) * (x + 0.044715 * (x * x * x))))


def _out1_kernel(x_ref, a_ref, yf_ref, yb_ref, u_ref, dsk_ref, gw_ref, gb_ref, wa_ref, wb_ref,
                 gl_ref, gc_ref, o_ref, *, tm, ctx_len):
    i = pl.program_id(1)
    y = _gelu_tanh(yf_ref[0] + yb_ref[0] + dsk_ref[...] * u_ref[...])
    glu = _dot(y.astype(BF16), gw_ref[...]) + gb_ref[...]
    y = y * (1.0 / (1.0 + jnp.exp(-glu)))
    o = _dot(a_ref[0], wa_ref[...]) + _dot(y.astype(BF16), wb_ref[...])
    gate = _row_select(i, tm, ctx_len, gc_ref[...], gl_ref[0])
    o_ref[0] = x_ref[0] + gate * o


def _out_proj1(x_all, mix_a, y_dirs, u_t, d_skip, glu_w, glu_b, w_a, w_b, mod_l, mod_c, ctx_len, tm):
    bsz, tall, d = x_all.shape
    width = d_skip.shape[1]
    y3 = y_dirs.reshape(2, tall, bsz * width)
    u2 = u_t.reshape(tall, bsz * width)
    return pl.pallas_call(
        functools.partial(_out1_kernel, tm=tm, ctx_len=ctx_len),
        grid=(bsz, tall // tm),
        in_specs=[pl.BlockSpec((1, tm, d), lambda b, i: (b, i, 0)),
                  pl.BlockSpec((1, tm, mix_a.shape[2]), lambda b, i: (b, i, 0)),
                  pl.BlockSpec((1, tm, width), lambda b, i: (0, i, b)),
                  pl.BlockSpec((1, tm, width), lambda b, i: (1, i, b)),
                  pl.BlockSpec((tm, width), lambda b, i: (i, b)),
                  pl.BlockSpec((1, width), lambda b, i: (0, 0)),
                  pl.BlockSpec(glu_w.shape, lambda b, i: (0, 0)),
                  pl.BlockSpec((1, width), lambda b, i: (0, 0)),
                  pl.BlockSpec(w_a.shape, lambda b, i: (0, 0)),
                  pl.BlockSpec(w_b.shape, lambda b, i: (0, 0))] + _mod_specs(d, (2,), 2),
        out_specs=pl.BlockSpec((1, tm, d), lambda b, i: (b, i, 0)),
        out_shape=jax.ShapeDtypeStruct((bsz, tall, d), F32),
        compiler_params=_cparams(2),
        name="out_proj_odd",
    )(x_all, mix_a, y3, y3, u2, d_skip, glu_w, glu_b, w_a, w_b, mod_l, mod_c)


def _mlp_kernel(*refs, tm, ctx_len, final):
    if final:
        (x_ref, nw_ref, shl_ref, shc_ref, scl_ref, scc_ref, gl_ref, gc_ref, w1_ref, w2_ref, fw_ref,
         o_ref, h_scr, acc_scr) = refs
    else:
        (x_ref, nw_ref, shl_ref, shc_ref, scl_ref, scc_ref, gl_ref, gc_ref, w1_ref, w2_ref,
         o_ref, h_scr, acc_scr) = refs
    i = pl.program_id(1)
    j = pl.program_id(2)

    @pl.when(j == 0)
    def _():
        y = _rms(x_ref[0], nw_ref[...])
        shift = _row_select(i, tm, ctx_len, shc_ref[...], shl_ref[0])
        scale = _row_select(i, tm, ctx_len, scc_ref[...], scl_ref[0])
        h_scr[...] = (y * (1.0 + scale) + shift).astype(BF16)
        acc_scr[...] = jnp.zeros_like(acc_scr)

    a = jnp.maximum(_dot(h_scr[...], w1_ref[...]), 0.0)
    acc_scr[...] += _dot((a * a).astype(BF16), w2_ref[...])

    @pl.when(j == pl.num_programs(2) - 1)
    def _():
        gate = _row_select(i, tm, ctx_len, gc_ref[...], gl_ref[0])
        out = x_ref[0] + gate * acc_scr[...]
        if final:
            out = _rms(out, fw_ref[...])
        o_ref[0] = out


def _mlp(x_all, norm_w, mod_l, mod_c, w1, w2, final_w, ctx_len, tm, tf):
    bsz, tall, d = x_all.shape
    ff = w1.shape[1]
    final = final_w is not None
    in_specs = [pl.BlockSpec((1, tm, d), lambda b, i, j: (b, i, 0)),
                pl.BlockSpec((1, d), lambda b, i, j: (0, 0))]
    in_specs += _mod_specs(d, (3, 4, 5), 3)
    in_specs += [pl.BlockSpec((d, tf), lambda b, i, j: (0, j)),
                 pl.BlockSpec((tf, d), lambda b, i, j: (j, 0))]
    args = [x_all, norm_w.reshape(1, d)] + [mod_l, mod_c] * 3 + [w1, w2]
    if final:
        in_specs.append(pl.BlockSpec((1, d), lambda b, i, j: (0, 0)))
        args.append(final_w.reshape(1, d))
    return pl.pallas_call(
        functools.partial(_mlp_kernel, tm=tm, ctx_len=ctx_len, final=final),
        grid=(bsz, tall // tm, ff // tf),
        in_specs=in_specs,
        out_specs=pl.BlockSpec((1, tm, d), lambda b, i, j: (b, i, 0)),
        out_shape=jax.ShapeDtypeStruct((bsz, tall, d), F32),
        scratch_shapes=[pltpu.VMEM((tm, d), BF16), pltpu.VMEM((tm, d), F32)],
        compiler_params=_cparams(3),
        name="sq_relu_mlp",
    )(*args)


def _even_in_weight(w_in):
    sizes = (SSD_INNER, SSD_INNER + 2 * SSD_BC, 2 * SSD_HEADS, GLA_KEY, GLA_KEY, GLA_VAL,
             2 * GLA_GATE_RANK, GLA_VAL)
    offs = [0]
    for s in sizes:
        offs.append(offs[-1] + s)
    z, xbc, dt, q, k, v, lr, r = (w_in[:, offs[n]:offs[n + 1]] for n in range(8))
    pad = jnp.zeros((w_in.shape[0], LANE - dt.shape[1] - lr.shape[1]), w_in.dtype)
    return jnp.concatenate([xbc, z, v, r, q, k, dt, lr, pad], axis=1).astype(BF16)


def _s5_params(a_re, a_im, log_dt, b_re, b_im, c_re, c_im):
    delta = jnp.exp(log_dt.astype(F32))[..., None]
    mag = jnp.exp(a_re * delta)
    lbar_re, lbar_im = mag * jnp.cos(a_im * delta), mag * jnp.sin(a_im * delta)
    den = a_re * a_re + a_im * a_im
    zr = ((lbar_re - 1.0) * a_re + lbar_im * a_im) / den
    zi = (lbar_im * a_re - (lbar_re - 1.0) * a_im) / den
    bb_re = zr[..., None] * b_re - zi[..., None] * b_im
    bb_im = zr[..., None] * b_im + zi[..., None] * b_re
    n_slabs = S5_GROUPS // S5_SLAB
    eye = jnp.eye(S5_SLAB, dtype=F32)
    sw = S5_SLAB * S5_STATE

    def block_in(bb):
        bb = bb.reshape(2, n_slabs, S5_SLAB, S5_STATE, S5_GROUP)
        return jnp.einsum("dsgpc,gh->dsgchp", bb, eye).reshape(2, n_slabs, LANE, sw)

    def block_out(cc):
        cc = cc.reshape(n_slabs, S5_SLAB, S5_GROUP, S5_STATE)
        return jnp.einsum("sgcp,gh->sgphc", cc, eye).reshape(n_slabs, sw, LANE)

    bmat = jnp.concatenate([block_in(bb_re), block_in(bb_im)], axis=3).astype(BF16)
    cmat = jnp.concatenate([block_out(c_re), -block_out(c_im)], axis=1).astype(BF16)
    return (bmat, lbar_re.reshape(2, 1, S5_NSTATE), lbar_im.reshape(2, 1, S5_NSTATE), cmat)


def _layer_even(x_all, mod_l, mod_c, ctx_len, tm, norm1_w, w_in, conv_w, conv_b, dt_bias, a_log,
                d_skip, ssd_norm_w, gate_w, gate_b, gla_norm_w, w_out):
    w = _even_in_weight(w_in)
    n = w.shape[1]
    proj = _project(x_all, norm1_w, mod_l, mod_c, w, None, ctx_len, tm, _largest_divisor(n, LANE, 1024))
    n_xbc = SSD_INNER + 2 * SSD_BC
    c_z, c_v, c_r = n_xbc // SSD_INNER, n_xbc // GLA_VAL + 1, n_xbc // GLA_VAL + 2
    c_q = (n_xbc + 3 * SSD_INNER) // GLA_KEY
    c_aux = (n_xbc + 3 * SSD_INNER + 2 * GLA_KEY) // LANE
    xbc = _conv_silu(proj, conv_w, conv_b, ctx_len, n_xbc)

    neg_a = -jnp.exp(a_log.astype(F32))
    dt_bias = dt_bias.astype(F32)
    d_wide = jnp.repeat(d_skip.astype(F32), SSD_HEADDIM).reshape(1, SSD_INNER)
    y_f = _ssd_scan(xbc, proj, dt_bias, neg_a, c_z, c_aux, ctx_len, 0)
    y_mix = _ssd_scan(xbc, proj, dt_bias, neg_a, c_z, c_aux, ctx_len, 1,
                      (y_f, d_wide, ssd_norm_w.reshape(1, SSD_INNER)))

    cols = {"q": c_q, "k": c_q + 1, "v": c_v, "aux": c_aux, "gate": c_r}
    gparams = [(gate_w[d].astype(BF16), gate_b[d].reshape(1, GLA_KEY).astype(F32)) for d in range(2)]
    o_f = _lin_scan("gla", proj, cols, gparams[0], ctx_len, 0)
    o_mix = _lin_scan("gla", proj, cols, gparams[1], ctx_len, 1, (o_f, gla_norm_w.reshape(1, GLA_DV)))

    w_out = w_out.astype(BF16)
    return _out_proj0(x_all, y_mix, o_mix, w_out[:SSD_INNER], w_out[SSD_INNER:], mod_l, mod_c, ctx_len, tm)


def _layer_odd(x_all, mod_l, mod_c, ctx_len, tm, norm1_w, w_in, lb, hgrn_norm_w, a_re, a_im, log_dt,
               b_re, b_im, c_re, c_im, d_skip, glu_w, glu_b, w_out):
    bsz = x_all.shape[0]
    n_main = 5 * HGRN_WIDTH
    w_main = w_in[:, :n_main].astype(BF16)
    w_u = w_in[:, n_main:].astype(BF16)
    proj, u_t = _project(x_all, norm1_w, mod_l, mod_c, w_main, w_u, ctx_len, tm, HGRN_WIDTH)

    lb = lb.astype(F32).reshape(2, 1, HGRN_WIDTH)
    cols = {"q": 0, "v": 1, "aux": 2, "gate": 4}
    o_f = _lin_scan("hgrn", proj, cols, (lb[0], 1.0 - lb[0]), ctx_len, 0)
    o_mix = _lin_scan("hgrn", proj, cols, (lb[1], 1.0 - lb[1]), ctx_len, 1,
                      (o_f, hgrn_norm_w.reshape(1, HGRN_DV)))

    bmat, lam_re, lam_im, cmat = _s5_params(a_re.astype(F32), a_im.astype(F32), log_dt, b_re.astype(F32),
                                            b_im.astype(F32), c_re.astype(F32), c_im.astype(F32))
    y_dirs = _s5_scan(u_t.reshape(-1, S5_WIDTH), bmat, lam_re, lam_im, cmat, bsz, ctx_len)

    w_out = w_out.astype(BF16)
    return _out_proj1(x_all, o_mix, y_dirs, u_t, d_skip.astype(F32).reshape(1, S5_WIDTH),
                      glu_w.astype(BF16), glu_b.astype(F32).reshape(1, S5_WIDTH),
                      w_out[:HGRN_WIDTH], w_out[HGRN_WIDTH:], mod_l, mod_c, ctx_len, tm)


def kernel(x, c, ctx, c_ctx, ada_w, ada_b, norm1_w, norm2_w, ssd_gla_w_in, ssd_conv_w, ssd_conv_b, ssd_dt_bias, ssd_a_log, ssd_d, ssd_norm_w, gla_gate_w, gla_gate_b, gla_norm_w, ssd_gla_w_out, hgrn_s5_w_in, hgrn_lb_logits, hgrn_norm_w, s5_a_re, s5_a_im, s5_log_dt, s5_b_re, s5_b_im, s5_c_re, s5_c_im, s5_d, s5_glu_w, s5_glu_b, hgrn_s5_w_out, mlp_w1, mlp_w2, final_norm_w):
    bsz, seq, d = x.shape
    ctx_len = ctx.shape[1]
    depth = ada_w.shape[0]
    tall = ctx_len + seq
    assert bsz % 8 == 0 and ctx_len % SCAN_BLOCK == 0 and seq % SCAN_BLOCK == 0 and seq % GRID_W == 0
    tm = _largest_divisor(tall, 16, 1056)
    tf = 1024

    n_rows = -(-(bsz + 1) // 8) * 8
    cvec = jnp.concatenate([c, c_ctx[None, :], jnp.zeros((n_rows - bsz - 1, d), c.dtype)], axis=0)
    mod = _modulation(cvec.astype(F32), ada_w, ada_b)

    p_lb = jax.nn.softmax(hgrn_lb_logits.astype(F32), axis=0)
    lb_all = jnp.cumsum(p_lb, axis=0) - p_lb[0]

    x_all = jnp.concatenate([ctx, x], axis=1).astype(F32)
    for layer in range(depth):
        j = layer // 2
        mod_l = mod[layer, :bsz].reshape(bsz, 1, N_MOD * d)
        mod_c = mod[layer, bsz:bsz + 1]
        if layer % 2 == 0:
            x_all = _layer_even(x_all, mod_l, mod_c, ctx_len, tm, norm1_w[layer], ssd_gla_w_in[j],
                                ssd_conv_w[j], ssd_conv_b[j], ssd_dt_bias[j], ssd_a_log[j], ssd_d[j],
                                ssd_norm_w[j], gla_gate_w[j], gla_gate_b[j], gla_norm_w[j], ssd_gla_w_out[j])
        else:
            x_all = _layer_odd(x_all, mod_l, mod_c, ctx_len, tm, norm1_w[layer], hgrn_s5_w_in[j],
                               lb_all[layer], hgrn_norm_w[j], s5_a_re[j], s5_a_im[j], s5_log_dt[j],
                               s5_b_re[j], s5_b_im[j], s5_c_re[j], s5_c_im[j], s5_d[j], s5_glu_w[j],
                               s5_glu_b[j], hgrn_s5_w_out[j])
        last = layer == depth - 1
        x_all = _mlp(x_all, norm2_w[layer], mod_l, mod_c, mlp_w1[layer].astype(BF16),
                     mlp_w2[layer].astype(BF16), final_norm_w if last else None, ctx_len, tm, tf)
    return x_all[:, ctx_len:].astype(x.dtype)
```

```python
import functools
import math

import jax
import jax.numpy as jnp
from jax import lax
from jax.experimental import pallas as pl
from jax.experimental.pallas import tpu as pltpu

F32 = jnp.float32
BF16 = jnp.bfloat16

GRID_W = 64
NORM_EPS = 1e-6
N_MOD = 6
SSD_HEADDIM = 64
SSD_HEADS = 16
SSD_GROUPS = 4
SSD_STATE = 128
SSD_CHUNK = 128
GLA_HEADS = 8
GLA_DK = 64
GLA_DV = 128
GLA_GATE_RANK = 16
GLA_GATE_NORM = 16.0
HGRN_HEADS = 8
HGRN_DK = 128
HGRN_DV = 128
S5_GROUP = 16
S5_GROUPS = 24
S5_STATE = 64
LIN_CHUNK = 64

SSD_INNER = SSD_HEADS * SSD_HEADDIM
SSD_BC = SSD_GROUPS * SSD_STATE
GLA_KEY = GLA_HEADS * GLA_DK
GLA_VAL = GLA_HEADS * GLA_DV
HGRN_WIDTH = HGRN_HEADS * HGRN_DV
S5_WIDTH = S5_GROUPS * S5_GROUP
S5_NSTATE = S5_GROUPS * S5_STATE

VMEM_LIMIT_BYTES = 56 * 1024 * 1024
LANE = 128
SCAN_BLOCK = 256
S5_CHUNK = 128
S5_SLAB = LANE // S5_GROUP


def _cparams(n_axes):
    return pltpu.CompilerParams(dimension_semantics=("arbitrary",) * n_axes,
                                vmem_limit_bytes=VMEM_LIMIT_BYTES)


def _largest_divisor(n, multiple, cap):
    best = None
    for d in range(multiple, min(n, cap) + 1, multiple):
        if n % d == 0:
            best = d
    assert best is not None, (n, multiple, cap)
    return best


_NEG_LOG2E = -1.4426950408889634


def _sigmoid(x):
    return 1.0 / (1.0 + jnp.exp2(x * _NEG_LOG2E))


def _silu(x):
    return x * _sigmoid(x)


def _softplus(x):
    return jnp.maximum(x, 0.0) + jnp.log1p(jnp.exp(-jnp.abs(x)))


def _log_sigmoid(x):
    return -_softplus(-x)


def _rms(x, w):
    return x * lax.rsqrt(jnp.mean(x * x, axis=-1, keepdims=True) + NORM_EPS) * w


def _dot(a, b, dims=(((1,), (0,)), ((), ())), precision=None):
    return lax.dot_general(a, b, dims, precision=precision, preferred_element_type=F32)


def _split3(v):
    hi = v.astype(BF16)
    r1 = v - hi.astype(F32)
    mid = r1.astype(BF16)
    lo = (r1 - mid.astype(F32)).astype(BF16)
    return hi, mid, lo


def _tri3(mask):
    tri = mask.astype(BF16)
    return jnp.concatenate([tri, tri, tri], axis=1)


def _cumsum_rows(tri2, v):
    hi = v.astype(BF16)
    lo = (v - hi.astype(F32)).astype(BF16)
    return _dot(tri2, jnp.concatenate([hi, lo], axis=0))


_NT = (((1,), (1,)), ((), ()))
_TN = (((0,), (0,)), ((), ()))
_TT = (((0,), (1,)), ((), ()))


def _mod_kernel(c_ref, w_ref, b_ref, o_ref):
    a = _silu(c_ref[...]).astype(BF16)
    o_ref[0] = _dot(a, w_ref[0].astype(BF16)) + b_ref[0]


def _modulation(cvec, ada_w, ada_b):
    depth, d, n = ada_w.shape
    rows = cvec.shape[0]
    tn = _largest_divisor(n, LANE, 1024)
    return pl.pallas_call(
        _mod_kernel,
        grid=(depth, n // tn),
        in_specs=[pl.BlockSpec((rows, d), lambda l, j: (0, 0)),
                  pl.BlockSpec((1, d, tn), lambda l, j: (l, 0, j)),
                  pl.BlockSpec((1, 1, tn), lambda l, j: (l, 0, j))],
        out_specs=pl.BlockSpec((1, rows, tn), lambda l, j: (l, 0, j)),
        out_shape=jax.ShapeDtypeStruct((depth, rows, n), F32),
        compiler_params=_cparams(2),
        name="adaln_mod",
    )(cvec, ada_w, ada_b.reshape(depth, 1, n))


def _row_select(i, tm, n_lat, ctx_val, lat_val):
    row = i * tm + lax.broadcasted_iota(jnp.int32, (tm, 1), 0)
    return jnp.where(row >= n_lat, ctx_val, lat_val)


def _mod_specs(d, cols, n_grid_axes):
    specs = []
    for k in cols:
        if n_grid_axes == 2:
            specs.append(pl.BlockSpec((1, 1, d), lambda b, i, k=k: (b, 0, k)))
            specs.append(pl.BlockSpec((1, d), lambda b, i, k=k: (0, k)))
        else:
            specs.append(pl.BlockSpec((1, 1, d), lambda b, i, j, k=k: (b, 0, k)))
            specs.append(pl.BlockSpec((1, d), lambda b, i, j, k=k: (0, k)))
    return specs


def _proj_kernel(*refs, tm, n_lat, has_extra):
    if has_extra:
        (x_ref, nw_ref, shl_ref, shc_ref, scl_ref, scc_ref, w_ref, wx_ref, o_ref, ox_ref, h_scr) = refs
    else:
        (x_ref, nw_ref, shl_ref, shc_ref, scl_ref, scc_ref, w_ref, o_ref, h_scr) = refs
    i = pl.program_id(1)
    j = pl.program_id(2)

    @pl.when(j == 0)
    def _():
        y = _rms(x_ref[0], nw_ref[...])
        shift = _row_select(i, tm, n_lat, shc_ref[...], shl_ref[0])
        scale = _row_select(i, tm, n_lat, scc_ref[...], scl_ref[0])
        h = (y * (1.0 + scale) + shift).astype(BF16)
        h_scr[...] = h
        if has_extra:
            ox_ref[...] = _dot(h, wx_ref[...])

    o_ref[0] = _dot(h_scr[...], w_ref[...]).astype(o_ref.dtype)


def _project(x_all, norm_w, mod_l, mod_c, w, w_extra, n_lat, tm, tn):
    bsz, tall, d = x_all.shape
    n = w.shape[1]
    has_extra = w_extra is not None
    in_specs = [pl.BlockSpec((1, tm, d), lambda b, i, j: (b, i, 0)),
                pl.BlockSpec((1, d), lambda b, i, j: (0, 0))]
    in_specs += _mod_specs(d, (0, 1), 3)
    in_specs.append(pl.BlockSpec((d, tn), lambda b, i, j: (0, j)))
    args = [x_all, norm_w.reshape(1, d), mod_l, mod_c, mod_l, mod_c, w]
    out_specs = [pl.BlockSpec((1, tm, tn), lambda b, i, j: (b, i, j))]
    out_shape = [jax.ShapeDtypeStruct((bsz, tall, n), BF16)]
    if has_extra:
        nx = w_extra.shape[1]
        in_specs.append(pl.BlockSpec((d, nx), lambda b, i, j: (0, 0)))
        args.append(w_extra)
        out_specs.append(pl.BlockSpec((tm, nx), lambda b, i, j: (i, b)))
        out_shape.append(jax.ShapeDtypeStruct((tall, bsz * nx), F32))
    out = pl.pallas_call(
        functools.partial(_proj_kernel, tm=tm, n_lat=n_lat, has_extra=has_extra),
        grid=(bsz, tall // tm, n // tn),
        in_specs=in_specs,
        out_specs=out_specs,
        out_shape=out_shape,
        scratch_shapes=[pltpu.VMEM((tm, d), BF16)],
        compiler_params=_cparams(3),
        name="norm_mod_proj",
    )(*args)
    return out if has_extra else out[0]


def _conv_kernel(main_ref, prev_ref, next_ref, w_ref, b_ref, o_ref, *, tt, n_lat, tall):
    i = pl.program_id(1)
    p = i * tt + lax.broadcasted_iota(jnp.int32, (tt, 1), 0)
    is_ctx = p >= n_lat
    col = jnp.bitwise_and(p, GRID_W - 1)
    m_up = jnp.where(is_ctx, 0, p) >= GRID_W
    m_dn = jnp.where(is_ctx, n_lat, p) < n_lat - GRID_W
    m_l = jnp.where(is_ctx, p - n_lat, col) > 0
    m_r = jnp.where(is_ctx, p - (tall - 1), col - (GRID_W - 1)) < 0

    main = main_ref[0].astype(F32)
    rows = {
        -1: jnp.concatenate([prev_ref[0].astype(F32), main[:tt - GRID_W]], axis=0),
        0: main,
        1: jnp.concatenate([main[GRID_W:], next_ref[0].astype(F32)], axis=0),
    }
    w = w_ref[...]
    acc = jnp.zeros_like(main) + b_ref[...]
    for dy in (-1, 0, 1):
        s = rows[dy]
        k0 = 3 * (dy + 1)
        t = (s * w[k0 + 1:k0 + 2]
             + jnp.where(m_l, pltpu.roll(s, 1, 0), 0.0) * w[k0:k0 + 1]
             + jnp.where(m_r, pltpu.roll(s, tt - 1, 0), 0.0) * w[k0 + 2:k0 + 3])
        if dy == -1:
            t = jnp.where(m_up, t, 0.0)
        elif dy == 1:
            t = jnp.where(m_dn, t, 0.0)
        acc = acc + t
    o_ref[0] = _silu(acc).astype(o_ref.dtype)


def _conv_silu(proj, conv_w, conv_b, n_lat, n_ch):
    bsz, tall, _ = proj.shape
    n_rows = tall // GRID_W
    tt = _largest_divisor(tall, GRID_W, 768)
    assert n_lat % GRID_W == 0 and n_lat // tt == (tall - 1) // tt
    r = tt // GRID_W
    tc = 512
    return pl.pallas_call(
        functools.partial(_conv_kernel, tt=tt, n_lat=n_lat, tall=tall),
        grid=(bsz, tall // tt, n_ch // tc),
        in_specs=[pl.BlockSpec((1, tt, tc), lambda b, i, c: (b, i, c)),
                  pl.BlockSpec((1, GRID_W, tc), lambda b, i, c: (b, jnp.maximum(i * r - 1, 0), c)),
                  pl.BlockSpec((1, GRID_W, tc), lambda b, i, c: (b, jnp.minimum((i + 1) * r, n_rows - 1), c)),
                  pl.BlockSpec((9, tc), lambda b, i, c: (0, c)),
                  pl.BlockSpec((1, tc), lambda b, i, c: (0, c))],
        out_specs=pl.BlockSpec((1, tt, tc), lambda b, i, c: (b, i, c)),
        out_shape=jax.ShapeDtypeStruct((bsz, tall, n_ch), BF16),
        compiler_params=_cparams(3),
        name="dwconv_silu",
    )(proj, proj, proj, conv_w.reshape(9, n_ch), conv_b.reshape(1, n_ch))


def _scan_block(s, n_lat_blocks, n_blocks, reverse):
    n_ctx_blocks = n_blocks - n_lat_blocks
    if not reverse:
        return jnp.where(s < n_ctx_blocks, n_lat_blocks + s, s - n_ctx_blocks)
    return n_blocks - 1 - s


def _tri_mask(c, reverse):
    ri = lax.broadcasted_iota(jnp.int32, (c, c), 0)
    ci = lax.broadcasted_iota(jnp.int32, (c, c), 1)
    return (ci >= ri) if reverse else (ci <= ri)


def _ssd_expand_matrix():
    eye = jnp.eye(SSD_HEADS, dtype=F32)
    e_head = jnp.repeat(eye, SSD_HEADDIM, axis=1)
    e_seg = jnp.repeat(eye, SSD_CHUNK, axis=1)
    zh = jnp.zeros_like(e_head)
    zs = jnp.zeros_like(e_seg)
    blk = jnp.concatenate([
        jnp.concatenate([e_head, zh, zh, zs], axis=1),
        jnp.concatenate([zh, e_head, zh, zs], axis=1),
        jnp.concatenate([zh, zh, e_head, zs], axis=1),
        jnp.concatenate([zh, zh, zh, e_seg], axis=1)], axis=0)
    return jnp.concatenate([blk, blk, blk], axis=0).astype(BF16)


def _ssd_kernel(*refs, direction, finish):
    if finish:
        (x_ref, bm_ref, cm_ref, dtlr_ref, dtb_ref, nega_ref, exp_ref, z_ref, yf_ref, dsk_ref, nw_ref,
         o_ref, st_ref) = refs
    else:
        (x_ref, bm_ref, cm_ref, dtlr_ref, dtb_ref, nega_ref, exp_ref, o_ref, st_ref) = refs
    reverse = direction == 1
    c = SSD_CHUNK
    p = SSD_HEADDIM
    gw = SSD_INNER // SSD_GROUPS
    hpg = SSD_HEADS // SSD_GROUPS

    @pl.when(pl.program_id(1) == 0)
    def _():
        st_ref[...] = jnp.zeros_like(st_ref)

    mask = _tri_mask(c, reverse)
    tri3 = _tri3(mask)
    last = 0 if reverse else c - 1
    n_chunks = x_ref.shape[1] // c
    order = range(n_chunks - 1, -1, -1) if reverse else range(n_chunks)
    groups = range(SSD_GROUPS)
    g_cols = [slice(g * gw, (g + 1) * gw) for g in groups]
    n_cols = [slice(g * SSD_STATE, (g + 1) * SSD_STATE) for g in groups]
    prep = {}
    for ck in order:
        rs = slice(ck * c, (ck + 1) * c)
        x = x_ref[0, rs, :].astype(F32)
        dt_raw = dtlr_ref[0, rs, :][:, direction * SSD_HEADS:(direction + 1) * SSD_HEADS].astype(F32)
        dt = _softplus(dt_raw + dtb_ref[...])
        la3 = jnp.concatenate(_split3(dt * nega_ref[...]), axis=0)
        acum = _dot(tri3, la3)
        acum_t = _dot(la3, tri3, _TT)
        a_last = acum[last:last + 1]
        narrow = jnp.concatenate([dt, dt * jnp.exp(a_last - acum), jnp.exp(acum), acum], axis=1)
        wide = _dot(jnp.concatenate(_split3(narrow), axis=1), exp_ref[...])
        ea_w = wide[:, 2 * SSD_INNER:3 * SSD_INNER]
        prep[ck] = dict(
            x=x, bm=bm_ref[0, rs, :], cm=cm_ref[0, rs, :], wide=wide, acum_t=acum_t, ea_w=ea_w,
            xdt=(x * wide[:, :SSD_INNER]).astype(BF16),
            xw=(x * wide[:, SSD_INNER:2 * SSD_INNER]).astype(BF16),
            e_last=ea_w[last:last + 1])
    cb = {(ck, g): _dot(prep[ck]["cm"][:, n_cols[g]], prep[ck]["bm"][:, n_cols[g]], _NT)
          for ck in order for g in groups}
    kv = {(ck, g): _dot(prep[ck]["bm"][:, n_cols[g]], prep[ck]["xw"][:, g_cols[g]], _TN)
          for ck in order for g in groups}
    scores, st_used = {}, {}
    for g in groups:
        st = st_ref[:, g_cols[g]]
        for ck in order:
            st_used[ck, g] = st.astype(BF16)
            st = st * prep[ck]["e_last"][:, g_cols[g]] + kv[ck, g]
        st_ref[:, g_cols[g]] = st
    for ck in order:
        for h in range(SSD_HEADS):
            a_i = prep[ck]["wide"][:, 3 * SSD_INNER + h * c:3 * SSD_INNER + (h + 1) * c]
            decay = jnp.exp(jnp.where(mask, a_i - prep[ck]["acum_t"][h:h + 1, :], -jnp.inf))
            scores[ck, h] = (cb[ck, h // hpg] * decay).astype(BF16)
    for ck in order:
        rs = slice(ck * c, (ck + 1) * c)
        x = prep[ck]["x"]
        y_groups = []
        for g in groups:
            ys = [_dot(scores[ck, h], prep[ck]["xdt"][:, h * p:(h + 1) * p])
                  for h in range(g * hpg, (g + 1) * hpg)]
            y_state = _dot(prep[ck]["cm"][:, n_cols[g]], st_used[ck, g]) * prep[ck]["ea_w"][:, g_cols[g]]
            y_groups.append(jnp.concatenate(ys, axis=1) + y_state)
        y = jnp.concatenate(y_groups, axis=1)
        if finish:
            z = z_ref[0, rs, :].astype(F32)
            y = (y + yf_ref[0, rs, :] + dsk_ref[...] * x) * _silu(z)
            outs = []
            for g in range(SSD_GROUPS):
                sl = slice(g * gw, (g + 1) * gw)
                outs.append(_rms(y[:, sl], nw_ref[:, sl]))
            o_ref[0, rs, :] = jnp.concatenate(outs, axis=1).astype(o_ref.dtype)
        else:
            o_ref[0, rs, :] = y


def _ssd_scan(xbc, proj, dt_bias, neg_a, z_col, dtlr_col, n_lat, direction, finish_args=None):
    bsz, tall, _ = xbc.shape
    tb = SCAN_BLOCK
    nb, n_lat_blocks = tall // tb, n_lat // tb
    reverse = direction == 1
    finish = finish_args is not None
    expand = _ssd_expand_matrix()

    def tok(col):
        return lambda b, s: (b, _scan_block(s, n_lat_blocks, nb, reverse), col)

    in_specs = [pl.BlockSpec((1, tb, SSD_INNER), tok(0)),
                pl.BlockSpec((1, tb, SSD_BC), tok(SSD_INNER // SSD_BC)),
                pl.BlockSpec((1, tb, SSD_BC), tok(SSD_INNER // SSD_BC + 1)),
                pl.BlockSpec((1, tb, LANE), tok(dtlr_col)),
                pl.BlockSpec((1, SSD_HEADS), lambda b, s: (0, 0)),
                pl.BlockSpec((1, SSD_HEADS), lambda b, s: (0, 0)),
                pl.BlockSpec(expand.shape, lambda b, s: (0, 0))]
    args = [xbc, xbc, xbc, proj, dt_bias[direction:direction + 1], neg_a[direction:direction + 1], expand]
    if finish:
        y_f, d_skip_wide, norm_w = finish_args
        in_specs += [pl.BlockSpec((1, tb, SSD_INNER), tok(z_col)),
                     pl.BlockSpec((1, tb, SSD_INNER), tok(0)),
                     pl.BlockSpec((1, SSD_INNER), lambda b, s: (0, 0)),
                     pl.BlockSpec((1, SSD_INNER), lambda b, s: (0, 0))]
        args += [proj, y_f, d_skip_wide, norm_w]
    return pl.pallas_call(
        functools.partial(_ssd_kernel, direction=direction, finish=finish),
        grid=(bsz, nb),
        in_specs=in_specs,
        out_specs=pl.BlockSpec((1, tb, SSD_INNER), tok(0)),
        out_shape=jax.ShapeDtypeStruct((bsz, tall, SSD_INNER), BF16 if finish else F32),
        scratch_shapes=[pltpu.VMEM((SSD_STATE, SSD_INNER), F32)],
        compiler_params=_cparams(2),
        name="ssd_scan_bwd" if reverse else "ssd_scan_fwd",
    )(*args)


def _lin_kernel(*refs, mode, direction, finish, heads, dk, dv):
    refs = list(refs)
    if mode == "gla":
        q_ref, k_ref, v_ref, aux_ref, p1_ref, p2_ref = refs[:6]
        rest = refs[6:]
    else:
        q_ref, v_ref, aux_ref, p1_ref, p2_ref = refs[:5]
        k_ref = None
        rest = refs[5:]
    if finish:
        of_ref, gate_ref, nw_ref, o_ref, st_ref = rest
    else:
        o_ref, st_ref = rest
    reverse = direction == 1
    c = LIN_CHUNK

    @pl.when(pl.program_id(1) == 0)
    def _():
        st_ref[...] = jnp.zeros_like(st_ref)

    tb = q_ref.shape[1]
    nc = tb // c
    hpt = LANE // dk
    chunks = range(nc - 1, -1, -1) if reverse else range(nc)
    last = 0 if reverse else c - 1

    ri = lax.broadcasted_iota(jnp.int32, (tb, tb), 0)
    ci = lax.broadcasted_iota(jnp.int32, (tb, tb), 1)
    c_shift = c.bit_length() - 1
    same_chunk = jnp.right_shift(ri, c_shift) == jnp.right_shift(ci, c_shift)
    if reverse:
        bd_mask = jnp.where(same_chunk, ci - ri, -1) >= 0
    else:
        bd_mask = jnp.where(same_chunk, ci - ri, 1) <= 0
    tri = _tri_mask(c, reverse).astype(BF16)
    tri2 = jnp.concatenate([tri, tri], axis=1)

    v = v_ref[0]
    if mode == "gla":
        q = q_ref[0].astype(F32) * (dk ** -0.5)
        k = k_ref[0].astype(F32)
        off = 2 * SSD_HEADS + direction * GLA_GATE_RANK
        lr = aux_ref[0][:, off:off + GLA_GATE_RANK]
        gk = _dot(lr, p1_ref[...]) + p2_ref[...]
        lg = _log_sigmoid(gk) * (1.0 / GLA_GATE_NORM)
    else:
        q = _silu(q_ref[0].astype(F32))
        f_raw = aux_ref[0].astype(F32)
        e = jnp.exp2(jnp.abs(f_raw) * _NEG_LOG2E)
        r = 1.0 / (1.0 + e)
        forget = p1_ref[...] + p2_ref[...] * jnp.where(f_raw >= 0.0, r, e * r)
        lg = jnp.log(forget)
        k = 1.0 - forget

    gcum = jnp.concatenate([_cumsum_rows(tri2, lg[cc * c:(cc + 1) * c]) for cc in range(nc)], axis=0)
    e_last = [jnp.exp(gcum[cc * c + last:cc * c + last + 1]) for cc in range(nc)]
    e_rows = jnp.concatenate([jnp.broadcast_to(e, (c, e.shape[1])) for e in e_last], axis=0)
    e_gcum = jnp.exp(gcum)
    q_decf = q * e_gcum
    if hpt > 1:
        head_in_tile = jnp.bitwise_and(jnp.right_shift(
            lax.broadcasted_iota(jnp.int32, (1, q_decf.shape[1]), 1), dk.bit_length() - 1), hpt - 1)
        q_dec = [jnp.where(head_in_tile == r, q_decf, 0.0).astype(BF16) for r in range(hpt)]
    else:
        q_dec = [q_decf.astype(BF16)]
    k_invf = k * (1.0 / e_gcum)
    k_inv = k_invf.astype(BF16)
    k_end = (k_invf * e_rows).astype(BF16)
    zeros = jnp.zeros((c, LANE), BF16)

    def chunk_blocks(a):
        cols = []
        for b in range(nc):
            cols.append(jnp.concatenate(
                [a[cc * c:(cc + 1) * c] if cc == b else zeros for cc in range(nc)], axis=0))
        return jnp.concatenate(cols, axis=1)

    lane_tiles = [slice((h // hpt) * LANE, (h // hpt + 1) * LANE) for h in range(heads)]
    v_cols = [slice(h * dv, (h + 1) * dv) for h in range(heads)]
    q_heads = [q_dec[h % hpt][:, lane_tiles[h]] for h in range(heads)]
    scores = [_dot(q_heads[h], k_inv[:, lane_tiles[h]], _NT) for h in range(heads)]
    kv_t = [_dot(v[:, v_cols[h]], chunk_blocks(k_end[:, lane_tiles[h]]), _TN) for h in range(heads)]
    atts, st_cats = [], []
    for h in range(heads):
        atts.append(jnp.where(bd_mask, scores[h], 0.0).astype(BF16))
        st = st_ref[h]
        used = [None] * nc
        for cc in chunks:
            used[cc] = st.astype(BF16)
            st = st * e_last[cc][:, lane_tiles[h]] + kv_t[h][:, cc * LANE:(cc + 1) * LANE]
        st_ref[h] = st
        st_cats.append(used)
    outs = []
    for h in range(heads):
        o_state = jnp.concatenate(
            [_dot(q_heads[h][cc * c:(cc + 1) * c], st_cats[h][cc], _NT) for cc in range(nc)], axis=0)
        o_h = _dot(atts[h], v[:, v_cols[h]]) + o_state
        if finish:
            o_h = _rms(o_h + of_ref[0, :, v_cols[h]], nw_ref[...])
        outs.append(o_h)
    o = jnp.concatenate(outs, axis=1)
    if finish:
        o_ref[0] = (o * _silu(gate_ref[0].astype(F32))).astype(o_ref.dtype)
    else:
        o_ref[0] = o


def _lin_scan(mode, proj, cols, params, n_lat, direction, finish_args=None):
    bsz, tall, _ = proj.shape
    if mode == "gla":
        heads, dk, dv = GLA_HEADS, GLA_DK, GLA_DV
    else:
        heads, dk, dv = HGRN_HEADS, HGRN_DK, HGRN_DV
    kw, vw = heads * dk, heads * dv
    tb = SCAN_BLOCK
    nb, n_lat_blocks = tall // tb, n_lat // tb
    reverse = direction == 1
    finish = finish_args is not None

    def tok(col):
        return lambda b, s: (b, _scan_block(s, n_lat_blocks, nb, reverse), col)

    def const2(shape):
        return pl.BlockSpec(shape, lambda b, s: (0, 0))

    p1, p2 = params
    if mode == "gla":
        in_specs = [pl.BlockSpec((1, tb, kw), tok(cols["q"])),
                    pl.BlockSpec((1, tb, kw), tok(cols["k"])),
                    pl.BlockSpec((1, tb, vw), tok(cols["v"])),
                    pl.BlockSpec((1, tb, LANE), tok(cols["aux"])),
                    const2(p1.shape), const2(p2.shape)]
        args = [proj, proj, proj, proj, p1, p2]
    else:
        in_specs = [pl.BlockSpec((1, tb, kw), tok(cols["q"])),
                    pl.BlockSpec((1, tb, vw), tok(cols["v"])),
                    pl.BlockSpec((1, tb, kw), tok(cols["aux"] + direction)),
                    const2(p1.shape), const2(p2.shape)]
        args = [proj, proj, proj, p1, p2]
    if finish:
        o_f, norm_w = finish_args
        in_specs += [pl.BlockSpec((1, tb, vw), tok(0)),
                     pl.BlockSpec((1, tb, vw), tok(cols["gate"])),
                     const2(norm_w.shape)]
        args += [o_f, proj, norm_w]
    return pl.pallas_call(
        functools.partial(_lin_kernel, mode=mode, direction=direction, finish=finish,
                          heads=heads, dk=dk, dv=dv),
        grid=(bsz, nb),
        in_specs=in_specs,
        out_specs=pl.BlockSpec((1, tb, vw), tok(0)),
        out_shape=jax.ShapeDtypeStruct((bsz, tall, vw), BF16 if finish else F32),
        scratch_shapes=[pltpu.VMEM((heads, dv, LANE), F32)],
        compiler_params=_cparams(2),
        name=f"{mode}_scan_{'bwd' if reverse else 'fwd'}",
    )(*args)


def _s5_kernel(u_ref, bmat_ref, lre_ref, lim_ref, cmat_ref, o_ref, h_ref, ut_ref, yt_ref, st_ref, *, bsz):
    d = pl.program_id(0)
    steps = S5_CHUNK
    n_slabs = bmat_ref.shape[1]
    sw = bmat_ref.shape[3] // 2
    width = n_slabs * LANE
    re_cols = [slice(2 * s * sw, (2 * s + 1) * sw) for s in range(n_slabs)]
    im_cols = [slice((2 * s + 1) * sw, (2 * s + 2) * sw) for s in range(n_slabs)]
    both = [slice(2 * s * sw, (2 * s + 2) * sw) for s in range(n_slabs)]

    @pl.when(pl.program_id(1) == 0)
    def _():
        st_ref[...] = jnp.zeros_like(st_ref)

    for b in range(bsz):
        for s in range(n_slabs):
            ut_ref[s, pl.ds(b, steps, stride=bsz), :] = u_ref[:, b * width + s * LANE:b * width + (s + 1) * LANE]
    for s in range(n_slabs):
        h_ref[:, both[s]] = _dot(ut_ref[s].astype(BF16), bmat_ref[0, s])

    lam_re = [jnp.broadcast_to(lre_ref[0, :, s * sw:(s + 1) * sw], (bsz, sw)) for s in range(n_slabs)]
    lam_im = [jnp.broadcast_to(lim_ref[0, :, s * sw:(s + 1) * sw], (bsz, sw)) for s in range(n_slabs)]

    def body(tt, carry):
        t = jnp.where(d == 0, tt, steps - 1 - tt)
        rows = pl.ds(pl.multiple_of(t * bsz, bsz), bsz)
        new = []
        for s in range(n_slabs):
            hr, hi = carry[2 * s], carry[2 * s + 1]
            nr = lam_re[s] * hr - lam_im[s] * hi + h_ref[rows, re_cols[s]]
            ni = lam_re[s] * hi + lam_im[s] * hr + h_ref[rows, im_cols[s]]
            h_ref[rows, re_cols[s]] = nr
            h_ref[rows, im_cols[s]] = ni
            new += [nr, ni]
        return tuple(new)

    init = []
    for s in range(n_slabs):
        init += [st_ref[:, re_cols[s]], st_ref[:, im_cols[s]]]
    final = lax.fori_loop(0, steps, body, tuple(init), unroll=4)
    for s in range(n_slabs):
        st_ref[:, re_cols[s]] = final[2 * s]
        st_ref[:, im_cols[s]] = final[2 * s + 1]
        yt_ref[s] = _dot(h_ref[:, both[s]].astype(BF16), cmat_ref[s])
    for b in range(bsz):
        for s in range(n_slabs):
            o_ref[0, :, b * width + s * LANE:b * width + (s + 1) * LANE] = yt_ref[s, pl.ds(b, steps, stride=bsz), :]


def _s5_scan(u_t, bmat, lam_re, lam_im, cmat, bsz, n_lat):
    tall, bw = u_t.shape
    width = bw // bsz
    nch, n_lat_chunks = tall // S5_CHUNK, n_lat // S5_CHUNK
    n_state = lam_re.shape[-1]
    rows = S5_CHUNK * bsz

    def chunk(d, s):
        return jnp.where(d == 0, _scan_block(s, n_lat_chunks, nch, False),
                         _scan_block(s, n_lat_chunks, nch, True))

    return pl.pallas_call(
        functools.partial(_s5_kernel, bsz=bsz),
        grid=(2, nch),
        in_specs=[pl.BlockSpec((S5_CHUNK, bw), lambda d, s: (chunk(d, s), 0)),
                  pl.BlockSpec((1,) + bmat.shape[1:], lambda d, s: (d, 0, 0, 0)),
                  pl.BlockSpec((1, 1, n_state), lambda d, s: (d, 0, 0)),
                  pl.BlockSpec((1, 1, n_state), lambda d, s: (d, 0, 0)),
                  pl.BlockSpec(cmat.shape, lambda d, s: (0, 0, 0))],
        out_specs=pl.BlockSpec((1, S5_CHUNK, bw), lambda d, s: (d, chunk(d, s), 0)),
        out_shape=jax.ShapeDtypeStruct((2, tall, bw), F32),
        scratch_shapes=[pltpu.VMEM((rows, 2 * n_state), F32),
                        pltpu.VMEM((width // LANE, rows, LANE), F32),
                        pltpu.VMEM((width // LANE, rows, LANE), F32),
                        pltpu.VMEM((bsz, 2 * n_state), F32)],
        compiler_params=_cparams(2),
        name="s5_scan",
    )(u_t, bmat, lam_re, lam_im, cmat)


def _out0_kernel(x_ref, a_ref, b_ref, wa_ref, wb_ref, gl_ref, gc_ref, o_ref, *, tm, n_lat):
    i = pl.program_id(1)
    o = _dot(a_ref[0], wa_ref[...]) + _dot(b_ref[0], wb_ref[...])
    gate = _row_select(i, tm, n_lat, gc_ref[...], gl_ref[0])
    o_ref[0] = x_ref[0] + gate * o


def _out_proj0(x_all, mix_a, mix_b, w_a, w_b, mod_l, mod_c, n_lat, n_rows, tm):
    bsz, _, d = x_all.shape
    tall = n_rows
    return pl.pallas_call(
        functools.partial(_out0_kernel, tm=tm, n_lat=n_lat),
        grid=(bsz, tall // tm),
        in_specs=[pl.BlockSpec((1, tm, d), lambda b, i: (b, i, 0)),
                  pl.BlockSpec((1, tm, mix_a.shape[2]), lambda b, i: (b, i, 0)),
                  pl.BlockSpec((1, tm, mix_b.shape[2]), lambda b, i: (b, i, 0)),
                  pl.BlockSpec(w_a.shape, lambda b, i: (0, 0)),
                  pl.BlockSpec(w_b.shape, lambda b, i: (0, 0))] + _mod_specs(d, (2,), 2),
        out_specs=pl.BlockSpec((1, tm, d), lambda b, i: (b, i, 0)),
        out_shape=jax.ShapeDtypeStruct((bsz, tall, d), F32),
        compiler_params=_cparams(2),
        name="out_proj_even",
    )(x_all, mix_a, mix_b, w_a, w_b, mod_l, mod_c)


def _gelu_tanh(x):
    return 0.5 * x * (1.0 + jnp.tanh(math.sqrt(2.0 / math.pi) * (x + 0.044715 * (x * x * x))))


def _out1_kernel(x_ref, a_ref, yf_ref, yb_ref, u_ref, dsk_ref, gw_ref, gb_ref, wa_ref, wb_ref,
                 gl_ref, gc_ref, o_ref, *, tm, n_lat):
    i = pl.program_id(1)
    y = _gelu_tanh(yf_ref[0] + yb_ref[0] + dsk_ref[...] * u_ref[...])
    glu = _dot(y.astype(BF16), gw_ref[...]) + gb_ref[...]
    y = y * (1.0 / (1.0 + jnp.exp(-glu)))
    o = _dot(a_ref[0], wa_ref[...]) + _dot(y.astype(BF16), wb_ref[...])
    gate = _row_select(i, tm, n_lat, gc_ref[...], gl_ref[0])
    o_ref[0] = x_ref[0] + gate * o


def _out_proj1(x_all, mix_a, y_dirs, u_t, d_skip, glu_w, glu_b, w_a, w_b, mod_l, mod_c, n_lat, n_rows, tm):
    bsz, _, d = x_all.shape
    tall = n_rows
    width = d_skip.shape[1]
    y3, u2 = y_dirs, u_t
    return pl.pallas_call(
        functools.partial(_out1_kernel, tm=tm, n_lat=n_lat),
        grid=(bsz, tall // tm),
        in_specs=[pl.BlockSpec((1, tm, d), lambda b, i: (b, i, 0)),
                  pl.BlockSpec((1, tm, mix_a.shape[2]), lambda b, i: (b, i, 0)),
                  pl.BlockSpec((1, tm, width), lambda b, i: (0, i, b)),
                  pl.BlockSpec((1, tm, width), lambda b, i: (1, i, b)),
                  pl.BlockSpec((tm, width), lambda b, i: (i, b)),
                  pl.BlockSpec((1, width), lambda b, i: (0, 0)),
                  pl.BlockSpec(glu_w.shape, lambda b, i: (0, 0)),
                  pl.BlockSpec((1, width), lambda b, i: (0, 0)),
                  pl.BlockSpec(w_a.shape, lambda b, i: (0, 0)),
                  pl.BlockSpec(w_b.shape, lambda b, i: (0, 0))] + _mod_specs(d, (2,), 2),
        out_specs=pl.BlockSpec((1, tm, d), lambda b, i: (b, i, 0)),
        out_shape=jax.ShapeDtypeStruct((bsz, tall, d), F32),
        compiler_params=_cparams(2),
        name="out_proj_odd",
    )(x_all, mix_a, y3, y3, u2, d_skip, glu_w, glu_b, w_a, w_b, mod_l, mod_c)


def _mlp_kernel(*refs, tm, n_lat, final):
    if final:
        (x_ref, nw_ref, shl_ref, shc_ref, scl_ref, scc_ref, gl_ref, gc_ref, w1_ref, w2_ref, fw_ref,
         o_ref, h_scr, acc_scr) = refs
    else:
        (x_ref, nw_ref, shl_ref, shc_ref, scl_ref, scc_ref, gl_ref, gc_ref, w1_ref, w2_ref,
         o_ref, h_scr, acc_scr) = refs
    i = pl.program_id(1)
    j = pl.program_id(2)

    @pl.when(j == 0)
    def _():
        y = _rms(x_ref[0], nw_ref[...])
        shift = _row_select(i, tm, n_lat, shc_ref[...], shl_ref[0])
        scale = _row_select(i, tm, n_lat, scc_ref[...], scl_ref[0])
        h_scr[...] = (y * (1.0 + scale) + shift).astype(BF16)
        acc_scr[...] = jnp.zeros_like(acc_scr)

    a = jnp.maximum(_dot(h_scr[...], w1_ref[...]), 0.0)
    acc_scr[...] += _dot((a * a).astype(BF16), w2_ref[...])

    @pl.when(j == pl.num_programs(2) - 1)
    def _():
        gate = _row_select(i, tm, n_lat, gc_ref[...], gl_ref[0])
        out = x_ref[0] + gate * acc_scr[...]
        if final:
            out = _rms(out, fw_ref[...])
        o_ref[0] = out


def _mlp(x_all, norm_w, mod_l, mod_c, w1, w2, final_w, n_lat, tm, tf):
    bsz, tall, d = x_all.shape
    ff = w1.shape[1]
    final = final_w is not None
    in_specs = [pl.BlockSpec((1, tm, d), lambda b, i, j: (b, i, 0)),
                pl.BlockSpec((1, d), lambda b, i, j: (0, 0))]
    in_specs += _mod_specs(d, (3, 4, 5), 3)
    in_specs += [pl.BlockSpec((d, tf), lambda b, i, j: (0, j)),
                 pl.BlockSpec((tf, d), lambda b, i, j: (j, 0))]
    args = [x_all, norm_w.reshape(1, d)] + [mod_l, mod_c] * 3 + [w1, w2]
    if final:
        in_specs.append(pl.BlockSpec((1, d), lambda b, i, j: (0, 0)))
        args.append(final_w.reshape(1, d))
    return pl.pallas_call(
        functools.partial(_mlp_kernel, tm=tm, n_lat=n_lat, final=final),
        grid=(bsz, tall // tm, ff // tf),
        in_specs=in_specs,
        out_specs=pl.BlockSpec((1, tm, d), lambda b, i, j: (b, i, 0)),
        out_shape=jax.ShapeDtypeStruct((bsz, tall, d), F32),
        scratch_shapes=[pltpu.VMEM((tm, d), BF16), pltpu.VMEM((tm, d), F32)],
        compiler_params=_cparams(3),
        name="sq_relu_mlp",
    )(*args)


def _even_in_weight(w_in):
    sizes = (SSD_INNER, SSD_INNER + 2 * SSD_BC, 2 * SSD_HEADS, GLA_KEY, GLA_KEY, GLA_VAL,
             2 * GLA_GATE_RANK, GLA_VAL)
    offs = [0]
    for s in sizes:
        offs.append(offs[-1] + s)
    z, xbc, dt, q, k, v, lr, r = (w_in[:, offs[n]:offs[n + 1]] for n in range(8))
    pad = jnp.zeros((w_in.shape[0], LANE - dt.shape[1] - lr.shape[1]), w_in.dtype)
    return jnp.concatenate([xbc, z, v, r, q, k, dt, lr, pad], axis=1).astype(BF16)


def _s5_params(a_re, a_im, log_dt, b_re, b_im, c_re, c_im):
    delta = jnp.exp(log_dt.astype(F32))[..., None]
    mag = jnp.exp(a_re * delta)
    lbar_re, lbar_im = mag * jnp.cos(a_im * delta), mag * jnp.sin(a_im * delta)
    den = a_re * a_re + a_im * a_im
    zr = ((lbar_re - 1.0) * a_re + lbar_im * a_im) / den
    zi = (lbar_im * a_re - (lbar_re - 1.0) * a_im) / den
    bb_re = zr[..., None] * b_re - zi[..., None] * b_im
    bb_im = zr[..., None] * b_im + zi[..., None] * b_re
    n_slabs = S5_GROUPS // S5_SLAB
    eye = jnp.eye(S5_SLAB, dtype=F32)
    sw = S5_SLAB * S5_STATE

    def block_in(bb):
        bb = bb.reshape(2, n_slabs, S5_SLAB, S5_STATE, S5_GROUP)
        return jnp.einsum("dsgpc,gh->dsgchp", bb, eye).reshape(2, n_slabs, LANE, sw)

    def block_out(cc):
        cc = cc.reshape(n_slabs, S5_SLAB, S5_GROUP, S5_STATE)
        return jnp.einsum("sgcp,gh->sgphc", cc, eye).reshape(n_slabs, sw, LANE)

    bmat = jnp.concatenate([block_in(bb_re), block_in(bb_im)], axis=3).astype(BF16)
    cmat = jnp.concatenate([block_out(c_re), -block_out(c_im)], axis=1).astype(BF16)
    return (bmat, lbar_re.reshape(2, 1, S5_NSTATE), lbar_im.reshape(2, 1, S5_NSTATE), cmat)


def _layer_even(x_all, mod_l, mod_c, n_lat, tm, n_rows_out, tm_out, norm1_w, w_in, conv_w, conv_b, dt_bias,
                a_log, d_skip, ssd_norm_w, gate_w, gate_b, gla_norm_w, w_out):
    w = _even_in_weight(w_in)
    n = w.shape[1]
    proj = _project(x_all, norm1_w, mod_l, mod_c, w, None, n_lat, tm, _largest_divisor(n, LANE, 1024))
    n_xbc = SSD_INNER + 2 * SSD_BC
    c_z, c_v, c_r = n_xbc // SSD_INNER, n_xbc // GLA_VAL + 1, n_xbc // GLA_VAL + 2
    c_q = (n_xbc + 3 * SSD_INNER) // GLA_KEY
    c_aux = (n_xbc + 3 * SSD_INNER + 2 * GLA_KEY) // LANE
    xbc = _conv_silu(proj, conv_w, conv_b, n_lat, n_xbc)

    neg_a = -jnp.exp(a_log.astype(F32))
    dt_bias = dt_bias.astype(F32)
    d_wide = jnp.repeat(d_skip.astype(F32), SSD_HEADDIM).reshape(1, SSD_INNER)
    y_f = _ssd_scan(xbc, proj, dt_bias, neg_a, c_z, c_aux, n_lat, 0)
    y_mix = _ssd_scan(xbc, proj, dt_bias, neg_a, c_z, c_aux, n_lat, 1,
                      (y_f, d_wide, ssd_norm_w.reshape(1, SSD_INNER)))

    cols = {"q": c_q, "k": c_q + 1, "v": c_v, "aux": c_aux, "gate": c_r}
    gparams = [(gate_w[d].astype(BF16), gate_b[d].reshape(1, GLA_KEY).astype(F32)) for d in range(2)]
    o_f = _lin_scan("gla", proj, cols, gparams[0], n_lat, 0)
    o_mix = _lin_scan("gla", proj, cols, gparams[1], n_lat, 1, (o_f, gla_norm_w.reshape(1, GLA_DV)))

    w_out = w_out.astype(BF16)
    return _out_proj0(x_all, y_mix, o_mix, w_out[:SSD_INNER], w_out[SSD_INNER:], mod_l, mod_c, n_lat,
                      n_rows_out, tm_out)


def _layer_odd(x_all, mod_l, mod_c, n_lat, tm, n_rows_out, tm_out, norm1_w, w_in, lb, hgrn_norm_w, a_re, a_im,
               log_dt, b_re, b_im, c_re, c_im, d_skip, glu_w, glu_b, w_out):
    bsz = x_all.shape[0]
    n_main = 5 * HGRN_WIDTH
    w_main = w_in[:, :n_main].astype(BF16)
    w_u = w_in[:, n_main:].astype(BF16)
    proj, u_t = _project(x_all, norm1_w, mod_l, mod_c, w_main, w_u, n_lat, tm, HGRN_WIDTH)

    lb = lb.astype(F32).reshape(2, 1, HGRN_WIDTH)
    cols = {"q": 0, "v": 1, "aux": 2, "gate": 4}
    o_f = _lin_scan("hgrn", proj, cols, (lb[0], 1.0 - lb[0]), n_lat, 0)
    o_mix = _lin_scan("hgrn", proj, cols, (lb[1], 1.0 - lb[1]), n_lat, 1,
                      (o_f, hgrn_norm_w.reshape(1, HGRN_DV)))

    bmat, lam_re, lam_im, cmat = _s5_params(a_re.astype(F32), a_im.astype(F32), log_dt, b_re.astype(F32),
                                            b_im.astype(F32), c_re.astype(F32), c_im.astype(F32))
    y_dirs = _s5_scan(u_t, bmat, lam_re, lam_im, cmat, bsz, n_lat)

    w_out = w_out.astype(BF16)
    return _out_proj1(x_all, o_mix, y_dirs, u_t, d_skip.astype(F32).reshape(1, S5_WIDTH),
                      glu_w.astype(BF16), glu_b.astype(F32).reshape(1, S5_WIDTH),
                      w_out[:HGRN_WIDTH], w_out[HGRN_WIDTH:], mod_l, mod_c, n_lat, n_rows_out, tm_out)


def kernel(x, c, ctx, c_ctx, ada_w, ada_b, norm1_w, norm2_w, ssd_gla_w_in, ssd_conv_w, ssd_conv_b, ssd_dt_bias, ssd_a_log, ssd_d, ssd_norm_w, gla_gate_w, gla_gate_b, gla_norm_w, ssd_gla_w_out, hgrn_s5_w_in, hgrn_lb_logits, hgrn_norm_w, s5_a_re, s5_a_im, s5_log_dt, s5_b_re, s5_b_im, s5_c_re, s5_c_im, s5_d, s5_glu_w, s5_glu_b, hgrn_s5_w_out, mlp_w1, mlp_w2, final_norm_w):
    bsz, n_lat, d = x.shape
    ctx_len = ctx.shape[1]
    depth = ada_w.shape[0]
    tall = n_lat + ctx_len
    assert bsz % 8 == 0 and ctx_len % SCAN_BLOCK == 0 and n_lat % SCAN_BLOCK == 0 and n_lat % GRID_W == 0
    tf = 1024

    n_rows = -(-(bsz + 1) // 8) * 8
    cvec = jnp.concatenate([c, c_ctx[None, :], jnp.zeros((n_rows - bsz - 1, d), c.dtype)], axis=0)
    mod = _modulation(cvec.astype(F32), ada_w, ada_b)

    p_lb = jax.nn.softmax(hgrn_lb_logits.astype(F32), axis=0)
    lb_all = jnp.cumsum(p_lb, axis=0) - p_lb[0]

    x_all = jnp.concatenate([x, ctx], axis=1).astype(F32)
    tm_all = _largest_divisor(tall, 16, 1056)
    for layer in range(depth):
        j = layer // 2
        last = layer == depth - 1
        n_rows = n_lat if last else tall
        tm_out = _largest_divisor(n_rows, 16, 1056)
        mod_l = mod[layer, :bsz].reshape(bsz, 1, N_MOD * d)
        mod_c = mod[layer, bsz:bsz + 1]
        if layer % 2 == 0:
            x_all = _layer_even(x_all, mod_l, mod_c, n_lat, tm_all, n_rows, tm_out, norm1_w[layer],
                                ssd_gla_w_in[j], ssd_conv_w[j], ssd_conv_b[j], ssd_dt_bias[j], ssd_a_log[j],
                                ssd_d[j], ssd_norm_w[j], gla_gate_w[j], gla_gate_b[j], gla_norm_w[j],
                                ssd_gla_w_out[j])
        else:
            x_all = _layer_odd(x_all, mod_l, mod_c, n_lat, tm_all, n_rows, tm_out, norm1_w[layer],
                               hgrn_s5_w_in[j], lb_all[layer], hgrn_norm_w[j], s5_a_re[j], s5_a_im[j],
                               s5_log_dt[j], s5_b_re[j], s5_b_im[j], s5_c_re[j], s5_c_im[j], s5_d[j],
                               s5_glu_w[j], s5_glu_b[j], hgrn_s5_w_out[j])
        x_all = _mlp(x_all, norm2_w[layer], mod_l, mod_c, mlp_w1[layer].astype(BF16),
                     mlp_w2[layer].astype(BF16), final_norm_w if last else None, n_lat, tm_out, tf)
    return x_all.astype(x.dtype)
```

```python
import functools
import math

import jax
import jax.numpy as jnp
from jax import lax
from jax.experimental import pallas as pl
from jax.experimental.pallas import tpu as pltpu

F32 = jnp.float32
BF16 = jnp.bfloat16

GRID_W = 64
NORM_EPS = 1e-6
N_MOD = 6
SSD_HEADDIM = 64
SSD_HEADS = 16
SSD_GROUPS = 4
SSD_STATE = 128
SSD_CHUNK = 128
GLA_HEADS = 8
GLA_DK = 64
GLA_DV = 128
GLA_GATE_RANK = 16
GLA_GATE_NORM = 16.0
HGRN_HEADS = 8
HGRN_DK = 128
HGRN_DV = 128
S5_GROUP = 16
S5_GROUPS = 24
S5_STATE = 64
LIN_CHUNK = 64

SSD_INNER = SSD_HEADS * SSD_HEADDIM
SSD_BC = SSD_GROUPS * SSD_STATE
GLA_KEY = GLA_HEADS * GLA_DK
GLA_VAL = GLA_HEADS * GLA_DV
HGRN_WIDTH = HGRN_HEADS * HGRN_DV
S5_WIDTH = S5_GROUPS * S5_GROUP
S5_NSTATE = S5_GROUPS * S5_STATE

VMEM_LIMIT_BYTES = 56 * 1024 * 1024
LANE = 128
SCAN_BLOCK = 256
S5_CHUNK = 128
S5_SLAB = LANE // S5_GROUP


def _cparams(n_axes):
    return pltpu.CompilerParams(dimension_semantics=("arbitrary",) * n_axes,
                                vmem_limit_bytes=VMEM_LIMIT_BYTES)


def _largest_divisor(n, multiple, cap):
    best = None
    for d in range(multiple, min(n, cap) + 1, multiple):
        if n % d == 0:
            best = d
    assert best is not None, (n, multiple, cap)
    return best


_NEG_LOG2E = -1.4426950408889634


def _sigmoid(x):
    return 1.0 / (1.0 + jnp.exp2(x * _NEG_LOG2E))


def _silu(x):
    return x * _sigmoid(x)


def _softplus(x):
    return jnp.maximum(x, 0.0) + jnp.log1p(jnp.exp(-jnp.abs(x)))


def _log_sigmoid(x):
    return -_softplus(-x)


def _rms(x, w):
    return x * lax.rsqrt(jnp.mean(x * x, axis=-1, keepdims=True) + NORM_EPS) * w


def _dot(a, b, dims=(((1,), (0,)), ((), ())), precision=None):
    return lax.dot_general(a, b, dims, precision=precision, preferred_element_type=F32)


def _split3(v):
    hi = v.astype(BF16)
    r1 = v - hi.astype(F32)
    mid = r1.astype(BF16)
    lo = (r1 - mid.astype(F32)).astype(BF16)
    return hi, mid, lo


def _tri3(mask):
    tri = mask.astype(BF16)
    return jnp.concatenate([tri, tri, tri], axis=1)


def _cumsum_rows(tri2, v):
    hi = v.astype(BF16)
    lo = (v - hi.astype(F32)).astype(BF16)
    return _dot(tri2, jnp.concatenate([hi, lo], axis=0))


_NT = (((1,), (1,)), ((), ()))
_TN = (((0,), (0,)), ((), ()))
_TT = (((0,), (1,)), ((), ()))


def _mod_kernel(c_ref, w_ref, b_ref, o_ref):
    a = _silu(c_ref[...]).astype(BF16)
    o_ref[0] = _dot(a, w_ref[0].astype(BF16)) + b_ref[0]


def _modulation(cvec, ada_w, ada_b):
    depth, d, n = ada_w.shape
    rows = cvec.shape[0]
    tn = _largest_divisor(n, LANE, 1024)
    return pl.pallas_call(
        _mod_kernel,
        grid=(depth, n // tn),
        in_specs=[pl.BlockSpec((rows, d), lambda l, j: (0, 0)),
                  pl.BlockSpec((1, d, tn), lambda l, j: (l, 0, j)),
                  pl.BlockSpec((1, 1, tn), lambda l, j: (l, 0, j))],
        out_specs=pl.BlockSpec((1, rows, tn), lambda l, j: (l, 0, j)),
        out_shape=jax.ShapeDtypeStruct((depth, rows, n), F32),
        compiler_params=_cparams(2),
        name="adaln_mod",
    )(cvec, ada_w, ada_b.reshape(depth, 1, n))


def _row_select(i, tm, n_lat, ctx_val, lat_val):
    row = i * tm + lax.broadcasted_iota(jnp.int32, (tm, 1), 0)
    return jnp.where(row >= n_lat, ctx_val, lat_val)


def _store_norm_modulated(h_ref, x, nw, shift_l, shift_c, scale_l, scale_c, i, tm, n_lat):
    xn = x * lax.rsqrt(jnp.mean(x * x, axis=-1, keepdims=True) + NORM_EPS)
    gain_l = nw * (1.0 + scale_l)
    has_ctx = (i + 1) * tm > n_lat

    @pl.when(has_ctx)
    def _():
        gain = _row_select(i, tm, n_lat, nw * (1.0 + scale_c), gain_l)
        h_ref[...] = (xn * gain + _row_select(i, tm, n_lat, shift_c, shift_l)).astype(h_ref.dtype)

    @pl.when(jnp.logical_not(has_ctx))
    def _():
        h_ref[...] = (xn * gain_l + shift_l).astype(h_ref.dtype)


def _gated_residual(o_ref, x, gate_l, gate_c, upd, i, tm, n_lat, post=None):
    post = post or (lambda v: v)
    has_ctx = (i + 1) * tm > n_lat

    @pl.when(has_ctx)
    def _():
        o_ref[0] = post(x + _row_select(i, tm, n_lat, gate_c, gate_l) * upd)

    @pl.when(jnp.logical_not(has_ctx))
    def _():
        o_ref[0] = post(x + gate_l * upd)


def _mod_specs(d, cols, n_grid_axes):
    specs = []
    for k in cols:
        if n_grid_axes == 2:
            specs.append(pl.BlockSpec((1, 1, d), lambda b, i, k=k: (b, 0, k)))
            specs.append(pl.BlockSpec((1, d), lambda b, i, k=k: (0, k)))
        else:
            specs.append(pl.BlockSpec((1, 1, d), lambda b, i, j, k=k: (b, 0, k)))
            specs.append(pl.BlockSpec((1, d), lambda b, i, j, k=k: (0, k)))
    return specs


def _proj_kernel(*refs, tm, n_lat, has_extra):
    if has_extra:
        (x_ref, nw_ref, shl_ref, shc_ref, scl_ref, scc_ref, w_ref, wx_ref, o_ref, ox_ref, h_scr) = refs
    else:
        (x_ref, nw_ref, shl_ref, shc_ref, scl_ref, scc_ref, w_ref, o_ref, h_scr) = refs
    i = pl.program_id(1)
    j = pl.program_id(2)

    @pl.when(j == 0)
    def _():
        _store_norm_modulated(h_scr, x_ref[0], nw_ref[...], shl_ref[0], shc_ref[...], scl_ref[0], scc_ref[...],
                              i, tm, n_lat)
        if has_extra:
            ox_ref[...] = _dot(h_scr[...], wx_ref[...]).reshape(ox_ref.shape)

    o_ref[0] = _dot(h_scr[...], w_ref[...]).astype(o_ref.dtype)


def _project(x_all, norm_w, mod_l, mod_c, w, w_extra, n_lat, tm, tn, extra_token_major=True):
    bsz, tall, d = x_all.shape
    n = w.shape[1]
    has_extra = w_extra is not None
    in_specs = [pl.BlockSpec((1, tm, d), lambda b, i, j: (b, i, 0)),
                pl.BlockSpec((1, d), lambda b, i, j: (0, 0))]
    in_specs += _mod_specs(d, (0, 1), 3)
    in_specs.append(pl.BlockSpec((d, tn), lambda b, i, j: (0, j)))
    args = [x_all, norm_w.reshape(1, d), mod_l, mod_c, mod_l, mod_c, w]
    out_specs = [pl.BlockSpec((1, tm, tn), lambda b, i, j: (b, i, j))]
    out_shape = [jax.ShapeDtypeStruct((bsz, tall, n), BF16)]
    if has_extra:
        nx = w_extra.shape[1]
        in_specs.append(pl.BlockSpec((d, nx), lambda b, i, j: (0, 0)))
        args.append(w_extra)
        if extra_token_major:
            out_specs.append(pl.BlockSpec((tm, nx), lambda b, i, j: (i, b)))
            out_shape.append(jax.ShapeDtypeStruct((tall, bsz * nx), F32))
        else:
            out_specs.append(pl.BlockSpec((1, tm, nx), lambda b, i, j: (b, i, 0)))
            out_shape.append(jax.ShapeDtypeStruct((bsz, tall, nx), F32))
    out = pl.pallas_call(
        functools.partial(_proj_kernel, tm=tm, n_lat=n_lat, has_extra=has_extra),
        grid=(bsz, tall // tm, n // tn),
        in_specs=in_specs,
        out_specs=out_specs,
        out_shape=out_shape,
        scratch_shapes=[pltpu.VMEM((tm, d), BF16)],
        compiler_params=_cparams(3),
        name="norm_mod_proj",
    )(*args)
    return out if has_extra else out[0]


def _conv_kernel(main_ref, prev_ref, next_ref, w_ref, b_ref, o_ref, *, tt, n_lat, tall):
    i = pl.program_id(1)
    p = i * tt + lax.broadcasted_iota(jnp.int32, (tt, 1), 0)
    is_ctx = p >= n_lat
    col = jnp.bitwise_and(p, GRID_W - 1)
    m_up = jnp.where(is_ctx, 0, p) >= GRID_W
    m_dn = jnp.where(is_ctx, n_lat, p) < n_lat - GRID_W
    m_l = jnp.where(is_ctx, p - n_lat, col) > 0
    m_r = jnp.where(is_ctx, p - (tall - 1), col - (GRID_W - 1)) < 0

    main = main_ref[0].astype(F32)
    rows = {
        -1: jnp.concatenate([prev_ref[0].astype(F32), main[:tt - GRID_W]], axis=0),
        0: main,
        1: jnp.concatenate([main[GRID_W:], next_ref[0].astype(F32)], axis=0),
    }
    w = w_ref[...]
    acc = jnp.zeros_like(main) + b_ref[...]
    for dy in (-1, 0, 1):
        s = rows[dy]
        k0 = 3 * (dy + 1)
        t = (s * w[k0 + 1:k0 + 2]
             + jnp.where(m_l, pltpu.roll(s, 1, 0), 0.0) * w[k0:k0 + 1]
             + jnp.where(m_r, pltpu.roll(s, tt - 1, 0), 0.0) * w[k0 + 2:k0 + 3])
        if dy == -1:
            t = jnp.where(m_up, t, 0.0)
        elif dy == 1:
            t = jnp.where(m_dn, t, 0.0)
        acc = acc + t
    o_ref[0] = _silu(acc).astype(o_ref.dtype)


def _conv_silu(proj, conv_w, conv_b, n_lat, n_ch):
    bsz, tall, _ = proj.shape
    n_rows = tall // GRID_W
    tt = _largest_divisor(tall, GRID_W, 768)
    assert n_lat % GRID_W == 0 and n_lat // tt == (tall - 1) // tt
    r = tt // GRID_W
    tc = 512
    return pl.pallas_call(
        functools.partial(_conv_kernel, tt=tt, n_lat=n_lat, tall=tall),
        grid=(bsz, tall // tt, n_ch // tc),
        in_specs=[pl.BlockSpec((1, tt, tc), lambda b, i, c: (b, i, c)),
                  pl.BlockSpec((1, GRID_W, tc), lambda b, i, c: (b, jnp.maximum(i * r - 1, 0), c)),
                  pl.BlockSpec((1, GRID_W, tc), lambda b, i, c: (b, jnp.minimum((i + 1) * r, n_rows - 1), c)),
                  pl.BlockSpec((9, tc), lambda b, i, c: (0, c)),
                  pl.BlockSpec((1, tc), lambda b, i, c: (0, c))],
        out_specs=pl.BlockSpec((1, tt, tc), lambda b, i, c: (b, i, c)),
        out_shape=jax.ShapeDtypeStruct((bsz, tall, n_ch), BF16),
        compiler_params=_cparams(3),
        name="dwconv_silu",
    )(proj, proj, proj, conv_w.reshape(9, n_ch), conv_b.reshape(1, n_ch))


def _scan_block(s, n_lat_blocks, n_blocks, reverse):
    n_ctx_blocks = n_blocks - n_lat_blocks
    if not reverse:
        return jnp.where(s < n_ctx_blocks, n_lat_blocks + s, s - n_ctx_blocks)
    return n_blocks - 1 - s


def _tri_mask(c, reverse):
    ri = lax.broadcasted_iota(jnp.int32, (c, c), 0)
    ci = lax.broadcasted_iota(jnp.int32, (c, c), 1)
    return (ci >= ri) if reverse else (ci <= ri)


def _ssd_expand_matrix():
    eye = jnp.eye(SSD_HEADS, dtype=F32)
    e_head = jnp.repeat(eye, SSD_HEADDIM, axis=1)
    e_seg = jnp.repeat(eye, SSD_CHUNK, axis=1)
    zh = jnp.zeros_like(e_head)
    zs = jnp.zeros_like(e_seg)
    blk = jnp.concatenate([
        jnp.concatenate([e_head, zh, zh, zs], axis=1),
        jnp.concatenate([zh, e_head, zh, zs], axis=1),
        jnp.concatenate([zh, zh, e_head, zs], axis=1),
        jnp.concatenate([zh, zh, zh, e_seg], axis=1)], axis=0)
    return jnp.concatenate([blk, blk, blk], axis=0).astype(BF16)


def _ssd_kernel(*refs, direction, finish):
    if finish:
        (x_ref, bm_ref, cm_ref, dtlr_ref, dtb_ref, nega_ref, exp_ref, z_ref, yf_ref, dsk_ref, nw_ref,
         o_ref, st_ref) = refs
    else:
        (x_ref, bm_ref, cm_ref, dtlr_ref, dtb_ref, nega_ref, exp_ref, o_ref, st_ref) = refs
    reverse = direction == 1
    c = SSD_CHUNK
    p = SSD_HEADDIM
    gw = SSD_INNER // SSD_GROUPS
    hpg = SSD_HEADS // SSD_GROUPS

    @pl.when(pl.program_id(1) == 0)
    def _():
        st_ref[...] = jnp.zeros_like(st_ref)

    mask = _tri_mask(c, reverse)
    tri3 = _tri3(mask)
    last = 0 if reverse else c - 1
    n_chunks = x_ref.shape[1] // c
    order = range(n_chunks - 1, -1, -1) if reverse else range(n_chunks)
    groups = range(SSD_GROUPS)
    g_cols = [slice(g * gw, (g + 1) * gw) for g in groups]
    n_cols = [slice(g * SSD_STATE, (g + 1) * SSD_STATE) for g in groups]
    cb = {(ck, g): _dot(cm_ref[0, ck * c:(ck + 1) * c, n_cols[g]], bm_ref[0, ck * c:(ck + 1) * c, n_cols[g]], _NT)
          for ck in order for g in groups}
    prep = {}
    for ck in order:
        rs = slice(ck * c, (ck + 1) * c)
        x = x_ref[0, rs, :].astype(F32)
        dt_raw = dtlr_ref[0, rs, :][:, direction * SSD_HEADS:(direction + 1) * SSD_HEADS].astype(F32)
        dt = _softplus(dt_raw + dtb_ref[...])
        la3 = jnp.concatenate(_split3(dt * nega_ref[...]), axis=0)
        acum = _dot(tri3, la3)
        acum_t = _dot(la3, tri3, _TT)
        a_last = acum[last:last + 1]
        narrow = jnp.concatenate([dt, dt * jnp.exp(a_last - acum), jnp.exp(acum), acum], axis=1)
        wide = _dot(jnp.concatenate(_split3(narrow), axis=1), exp_ref[...])
        ea_w = wide[:, 2 * SSD_INNER:3 * SSD_INNER]
        prep[ck] = dict(
            x=x, bm=bm_ref[0, rs, :], cm=cm_ref[0, rs, :], wide=wide, acum_t=acum_t, ea_w=ea_w,
            xdt=(x * wide[:, :SSD_INNER]).astype(BF16),
            xw=(x * wide[:, SSD_INNER:2 * SSD_INNER]).astype(BF16),
            e_last=ea_w[last:last + 1])
    kv = {(ck, g): _dot(prep[ck]["bm"][:, n_cols[g]], prep[ck]["xw"][:, g_cols[g]], _TN)
          for ck in order for g in groups}
    scores, st_used = {}, {}
    for g in groups:
        st = st_ref[:, g_cols[g]]
        for ck in order:
            st_used[ck, g] = st.astype(BF16)
            st = st * prep[ck]["e_last"][:, g_cols[g]] + kv[ck, g]
        st_ref[:, g_cols[g]] = st
    for ck in order:
        for h in range(SSD_HEADS):
            a_i = prep[ck]["wide"][:, 3 * SSD_INNER + h * c:3 * SSD_INNER + (h + 1) * c]
            decay = jnp.exp(jnp.where(mask, a_i - prep[ck]["acum_t"][h:h + 1, :], -jnp.inf))
            scores[ck, h] = (cb[ck, h // hpg] * decay).astype(BF16)
    for ck in order:
        rs = slice(ck * c, (ck + 1) * c)
        x = prep[ck]["x"]
        y_groups = []
        for g in groups:
            ys = [_dot(scores[ck, h], prep[ck]["xdt"][:, h * p:(h + 1) * p])
                  for h in range(g * hpg, (g + 1) * hpg)]
            y_state = _dot(prep[ck]["cm"][:, n_cols[g]], st_used[ck, g]) * prep[ck]["ea_w"][:, g_cols[g]]
            y_groups.append(jnp.concatenate(ys, axis=1) + y_state)
        y = jnp.concatenate(y_groups, axis=1)
        if finish:
            z = z_ref[0, rs, :].astype(F32)
            y = (y + yf_ref[0, rs, :] + dsk_ref[...] * x) * _silu(z)
            outs = []
            for g in range(SSD_GROUPS):
                sl = slice(g * gw, (g + 1) * gw)
                outs.append(_rms(y[:, sl], nw_ref[:, sl]))
            o_ref[0, rs, :] = jnp.concatenate(outs, axis=1).astype(o_ref.dtype)
        else:
            o_ref[0, rs, :] = y


def _ssd_scan(xbc, proj, aux, dt_bias, neg_a, z_col, n_lat, direction, finish_args=None):
    bsz, tall, _ = xbc.shape
    tb = SCAN_BLOCK
    nb, n_lat_blocks = tall // tb, n_lat // tb
    reverse = direction == 1
    finish = finish_args is not None
    expand = _ssd_expand_matrix()

    def tok(col):
        return lambda b, s: (b, _scan_block(s, n_lat_blocks, nb, reverse), col)

    in_specs = [pl.BlockSpec((1, tb, SSD_INNER), tok(0)),
                pl.BlockSpec((1, tb, SSD_BC), tok(SSD_INNER // SSD_BC)),
                pl.BlockSpec((1, tb, SSD_BC), tok(SSD_INNER // SSD_BC + 1)),
                pl.BlockSpec((1, tb, LANE), tok(0)),
                pl.BlockSpec((1, SSD_HEADS), lambda b, s: (0, 0)),
                pl.BlockSpec((1, SSD_HEADS), lambda b, s: (0, 0)),
                pl.BlockSpec(expand.shape, lambda b, s: (0, 0))]
    args = [xbc, xbc, xbc, aux, dt_bias[direction:direction + 1], neg_a[direction:direction + 1], expand]
    if finish:
        y_f, d_skip_wide, norm_w = finish_args
        in_specs += [pl.BlockSpec((1, tb, SSD_INNER), tok(z_col)),
                     pl.BlockSpec((1, tb, SSD_INNER), tok(0)),
                     pl.BlockSpec((1, SSD_INNER), lambda b, s: (0, 0)),
                     pl.BlockSpec((1, SSD_INNER), lambda b, s: (0, 0))]
        args += [proj, y_f, d_skip_wide, norm_w]
    return pl.pallas_call(
        functools.partial(_ssd_kernel, direction=direction, finish=finish),
        grid=(bsz, nb),
        in_specs=in_specs,
        out_specs=pl.BlockSpec((1, tb, SSD_INNER), tok(0)),
        out_shape=jax.ShapeDtypeStruct((bsz, tall, SSD_INNER), BF16 if finish else F32),
        scratch_shapes=[pltpu.VMEM((SSD_STATE, SSD_INNER), F32)],
        compiler_params=_cparams(2),
        name="ssd_scan_bwd" if reverse else "ssd_scan_fwd",
    )(*args)


def _lin_kernel(*refs, mode, direction, finish, heads, dk, dv):
    refs = list(refs)
    if mode == "gla":
        q_ref, k_ref, v_ref, aux_ref, p1_ref, p2_ref = refs[:6]
        rest = refs[6:]
    else:
        q_ref, v_ref, aux_ref, p1_ref, p2_ref = refs[:5]
        k_ref = None
        rest = refs[5:]
    if finish:
        of_ref, gate_ref, nw_ref, o_ref, st_ref = rest
    else:
        o_ref, st_ref = rest
    reverse = direction == 1
    c = LIN_CHUNK

    @pl.when(pl.program_id(1) == 0)
    def _():
        st_ref[...] = jnp.zeros_like(st_ref)

    tb = q_ref.shape[1]
    nc = tb // c
    hpt = LANE // dk
    chunks = range(nc - 1, -1, -1) if reverse else range(nc)
    last = 0 if reverse else c - 1

    ri = lax.broadcasted_iota(jnp.int32, (tb, tb), 0)
    ci = lax.broadcasted_iota(jnp.int32, (tb, tb), 1)
    c_shift = c.bit_length() - 1
    same_chunk = jnp.right_shift(ri, c_shift) == jnp.right_shift(ci, c_shift)
    if reverse:
        bd_mask = jnp.where(same_chunk, ci - ri, -1) >= 0
    else:
        bd_mask = jnp.where(same_chunk, ci - ri, 1) <= 0
    tri = _tri_mask(c, reverse).astype(BF16)
    tri2 = jnp.concatenate([tri, tri], axis=1)

    zeros = jnp.zeros((c, LANE), BF16)

    def chunk_blocks(a):
        cols = []
        for b in range(nc):
            cols.append(jnp.concatenate(
                [a[cc * c:(cc + 1) * c] if cc == b else zeros for cc in range(nc)], axis=0))
        return jnp.concatenate(cols, axis=1)

    n_tiles = heads // hpt
    span = n_tiles if mode == "gla" else 1
    gw = span * LANE

    def prep(g):
        ls = slice(g * gw, (g + 1) * gw)
        if mode == "gla":
            q = q_ref[0, :, ls].astype(F32) * (dk ** -0.5)
            k = k_ref[0, :, ls].astype(F32)
            off = 2 * SSD_HEADS + direction * GLA_GATE_RANK
            lr = aux_ref[0][:, off:off + GLA_GATE_RANK].astype(BF16)
            lg = _log_sigmoid(_dot(lr, p1_ref[:, ls]) + p2_ref[:, ls]) * (1.0 / GLA_GATE_NORM)
        else:
            q = _silu(q_ref[0, :, ls].astype(F32))
            f_raw = aux_ref[0, :, ls].astype(F32)
            e = jnp.exp2(jnp.abs(f_raw) * _NEG_LOG2E)
            r = 1.0 / (1.0 + e)
            forget = p1_ref[:, ls] + p2_ref[:, ls] * jnp.where(f_raw >= 0.0, r, e * r)
            lg = jnp.log(forget)
            k = 1.0 - forget
        gcum = jnp.concatenate([_cumsum_rows(tri2, lg[cc * c:(cc + 1) * c]) for cc in range(nc)], axis=0)
        e_last = [jnp.exp(gcum[cc * c + last:cc * c + last + 1]) for cc in range(nc)]
        e_rows = jnp.concatenate([jnp.broadcast_to(e, (c, gw)) for e in e_last], axis=0)
        e_gcum = jnp.exp(gcum)
        q_decf = q * e_gcum
        if hpt > 1:
            head_of_lane = jnp.bitwise_and(jnp.right_shift(
                lax.broadcasted_iota(jnp.int32, (1, gw), 1), dk.bit_length() - 1), hpt - 1)
            q_dec = [jnp.where(head_of_lane == r, q_decf, 0.0).astype(BF16) for r in range(hpt)]
        else:
            q_dec = [q_decf.astype(BF16)]
        k_invf = k * (1.0 / e_gcum)
        return dict(q_dec=q_dec, k_inv=k_invf.astype(BF16), k_end=(k_invf * e_rows).astype(BF16),
                    e_last=e_last)

    groups, work = {}, {}

    def tile_cols(h):
        t = (h // hpt) % span
        return slice(t * LANE, (t + 1) * LANE)

    def products(h):
        gp, ts = groups[h // hpt // span], tile_cols(h)
        qh = gp["q_dec"][h % hpt][:, ts]
        vh = v_ref[0, :, h * dv:(h + 1) * dv]
        work[h] = dict(
            qh=qh, vh=vh, scores=_dot(qh, gp["k_inv"][:, ts], _NT),
            kv_t=_dot(vh, chunk_blocks(gp["k_end"][:, ts]), _TN))

    def mask_and_chain(h):
        w = work[h]
        e_last = groups[h // hpt // span]["e_last"]
        w["att"] = jnp.where(bd_mask, w.pop("scores"), 0.0).astype(BF16)
        st = st_ref[h]
        used = [None] * nc
        for cc in chunks:
            used[cc] = st.astype(BF16)
            st = st * e_last[cc][:, tile_cols(h)] + w["kv_t"][:, cc * LANE:(cc + 1) * LANE]
        st_ref[h] = st
        w["used"] = used
        del w["kv_t"]

    def outputs(h):
        w = work[h]
        o_state = jnp.concatenate(
            [_dot(w["qh"][cc * c:(cc + 1) * c], w["used"][cc], _NT) for cc in range(nc)], axis=0)
        w["o"] = _dot(w["att"], w["vh"]) + o_state

    def emit(h):
        vs = slice(h * dv, (h + 1) * dv)
        o_h = work.pop(h)["o"]
        if finish:
            o_h = _rms(o_h + of_ref[0, :, vs], nw_ref[...]) * _silu(gate_ref[0, :, vs].astype(F32))
        o_ref[0, :, vs] = o_h.astype(o_ref.dtype)

    n_groups = n_tiles // span
    hpg = hpt * span
    lag = 2
    for it in range(n_groups + 4 * lag):
        def heads_of(g):
            return range(g * hpg, (g + 1) * hpg) if 0 <= g < n_groups else ()
        for h in heads_of(it - lag):
            products(h)
        for h in heads_of(it - 3 * lag):
            outputs(h)
        if it < n_groups:
            groups[it] = prep(it)
        for h in heads_of(it - 2 * lag):
            mask_and_chain(h)
        for h in heads_of(it - 4 * lag):
            emit(h)


def _lin_scan(mode, proj, cols, params, n_lat, direction, finish_args=None, aux=None):
    bsz, tall, _ = proj.shape
    if mode == "gla":
        heads, dk, dv = GLA_HEADS, GLA_DK, GLA_DV
    else:
        heads, dk, dv = HGRN_HEADS, HGRN_DK, HGRN_DV
    kw, vw = heads * dk, heads * dv
    tb = SCAN_BLOCK
    nb, n_lat_blocks = tall // tb, n_lat // tb
    reverse = direction == 1
    finish = finish_args is not None

    def tok(col):
        return lambda b, s: (b, _scan_block(s, n_lat_blocks, nb, reverse), col)

    def const2(shape):
        return pl.BlockSpec(shape, lambda b, s: (0, 0))

    p1, p2 = params
    if mode == "gla":
        in_specs = [pl.BlockSpec((1, tb, kw), tok(cols["q"])),
                    pl.BlockSpec((1, tb, kw), tok(cols["k"])),
                    pl.BlockSpec((1, tb, vw), tok(cols["v"])),
                    pl.BlockSpec((1, tb, LANE), tok(0)),
                    const2(p1.shape), const2(p2.shape)]
        args = [proj, proj, proj, aux, p1, p2]
    else:
        in_specs = [pl.BlockSpec((1, tb, kw), tok(cols["q"])),
                    pl.BlockSpec((1, tb, vw), tok(cols["v"])),
                    pl.BlockSpec((1, tb, kw), tok(cols["aux"] + direction)),
                    const2(p1.shape), const2(p2.shape)]
        args = [proj, proj, proj, p1, p2]
    if finish:
        o_f, norm_w = finish_args
        in_specs += [pl.BlockSpec((1, tb, vw), tok(0)),
                     pl.BlockSpec((1, tb, vw), tok(cols["gate"])),
                     const2(norm_w.shape)]
        args += [o_f, proj, norm_w]
    return pl.pallas_call(
        functools.partial(_lin_kernel, mode=mode, direction=direction, finish=finish,
                          heads=heads, dk=dk, dv=dv),
        grid=(bsz, nb),
        in_specs=in_specs,
        out_specs=pl.BlockSpec((1, tb, vw), tok(0)),
        out_shape=jax.ShapeDtypeStruct((bsz, tall, vw), BF16 if finish else F32),
        scratch_shapes=[pltpu.VMEM((heads, dv, LANE), F32)],
        compiler_params=_cparams(2),
        name=f"{mode}_scan_{'bwd' if reverse else 'fwd'}",
    )(*args)


def _s5_kernel(u_ref, bmat_ref, lre_ref, lim_ref, cmat_ref, o_ref, h_ref, ut_ref, yt_ref, st_ref, *, bsz):
    d = pl.program_id(0)
    steps = S5_CHUNK
    n_slabs = bmat_ref.shape[1]
    sw = bmat_ref.shape[3] // 2
    width = n_slabs * LANE
    re_cols = [slice(2 * s * sw, (2 * s + 1) * sw) for s in range(n_slabs)]
    im_cols = [slice((2 * s + 1) * sw, (2 * s + 2) * sw) for s in range(n_slabs)]
    both = [slice(2 * s * sw, (2 * s + 2) * sw) for s in range(n_slabs)]

    @pl.when(pl.program_id(1) == 0)
    def _():
        st_ref[...] = jnp.zeros_like(st_ref)

    for b in range(bsz):
        for s in range(n_slabs):
            ut_ref[s, pl.ds(b, steps, stride=bsz), :] = u_ref[:, b * width + s * LANE:b * width + (s + 1) * LANE]
    for s in range(n_slabs):
        h_ref[:, both[s]] = _dot(ut_ref[s].astype(BF16), bmat_ref[0, s])

    lam_re = [jnp.broadcast_to(lre_ref[0, :, s * sw:(s + 1) * sw], (bsz, sw)) for s in range(n_slabs)]
    lam_im = [jnp.broadcast_to(lim_ref[0, :, s * sw:(s + 1) * sw], (bsz, sw)) for s in range(n_slabs)]

    def body(tt, carry):
        t = jnp.where(d == 0, tt, steps - 1 - tt)
        rows = pl.ds(pl.multiple_of(t * bsz, bsz), bsz)
        new = []
        for s in range(n_slabs):
            hr, hi = carry[2 * s], carry[2 * s + 1]
            nr = lam_re[s] * hr - lam_im[s] * hi + h_ref[rows, re_cols[s]]
            ni = lam_re[s] * hi + lam_im[s] * hr + h_ref[rows, im_cols[s]]
            h_ref[rows, re_cols[s]] = nr
            h_ref[rows, im_cols[s]] = ni
            new += [nr, ni]
        return tuple(new)

    init = []
    for s in range(n_slabs):
        init += [st_ref[:, re_cols[s]], st_ref[:, im_cols[s]]]
    final = lax.fori_loop(0, steps, body, tuple(init), unroll=4)
    for s in range(n_slabs):
        st_ref[:, re_cols[s]] = final[2 * s]
        st_ref[:, im_cols[s]] = final[2 * s + 1]
        yt_ref[s] = _dot(h_ref[:, both[s]].astype(BF16), cmat_ref[s])
    for b in range(bsz):
        for s in range(n_slabs):
            o_ref[0, :, b * width + s * LANE:b * width + (s + 1) * LANE] = yt_ref[s, pl.ds(b, steps, stride=bsz), :]


def _s5_scan(u_t, bmat, lam_re, lam_im, cmat, bsz, n_lat):
    tall, bw = u_t.shape
    width = bw // bsz
    nch, n_lat_chunks = tall // S5_CHUNK, n_lat // S5_CHUNK
    n_state = lam_re.shape[-1]
    rows = S5_CHUNK * bsz

    def chunk(d, s):
        return jnp.where(d == 0, _scan_block(s, n_lat_chunks, nch, False),
                         _scan_block(s, n_lat_chunks, nch, True))

    return pl.pallas_call(
        functools.partial(_s5_kernel, bsz=bsz),
        grid=(2, nch),
        in_specs=[pl.BlockSpec((S5_CHUNK, bw), lambda d, s: (chunk(d, s), 0)),
                  pl.BlockSpec((1,) + bmat.shape[1:], lambda d, s: (d, 0, 0, 0)),
                  pl.BlockSpec((1, 1, n_state), lambda d, s: (d, 0, 0)),
                  pl.BlockSpec((1, 1, n_state), lambda d, s: (d, 0, 0)),
                  pl.BlockSpec(cmat.shape, lambda d, s: (0, 0, 0))],
        out_specs=pl.BlockSpec((1, S5_CHUNK, bw), lambda d, s: (d, chunk(d, s), 0)),
        out_shape=jax.ShapeDtypeStruct((2, tall, bw), F32),
        scratch_shapes=[pltpu.VMEM((rows, 2 * n_state), F32),
                        pltpu.VMEM((width // LANE, rows, LANE), F32),
                        pltpu.VMEM((width // LANE, rows, LANE), F32),
                        pltpu.VMEM((bsz, 2 * n_state), F32)],
        compiler_params=_cparams(2),
        name="s5_scan",
    )(u_t, bmat, lam_re, lam_im, cmat)


def _out0_kernel(x_ref, a_ref, b_ref, wa_ref, wb_ref, gl_ref, gc_ref, o_ref, *, tm, n_lat):
    i = pl.program_id(1)
    o = _dot(a_ref[0], wa_ref[...]) + _dot(b_ref[0], wb_ref[...])
    o_ref[0] = x_ref[0] + _row_select(i, tm, n_lat, gc_ref[...], gl_ref[0]) * o


def _out_proj0(x_all, mix_a, mix_b, w_a, w_b, mod_l, mod_c, n_lat, n_rows, tm):
    bsz, _, d = x_all.shape
    tall = n_rows
    return pl.pallas_call(
        functools.partial(_out0_kernel, tm=tm, n_lat=n_lat),
        grid=(bsz, tall // tm),
        in_specs=[pl.BlockSpec((1, tm, d), lambda b, i: (b, i, 0)),
                  pl.BlockSpec((1, tm, mix_a.shape[2]), lambda b, i: (b, i, 0)),
                  pl.BlockSpec((1, tm, mix_b.shape[2]), lambda b, i: (b, i, 0)),
                  pl.BlockSpec(w_a.shape, lambda b, i: (0, 0)),
                  pl.BlockSpec(w_b.shape, lambda b, i: (0, 0))] + _mod_specs(d, (2,), 2),
        out_specs=pl.BlockSpec((1, tm, d), lambda b, i: (b, i, 0)),
        out_shape=jax.ShapeDtypeStruct((bsz, tall, d), F32),
        compiler_params=_cparams(2),
        name="out_proj_even",
    )(x_all, mix_a, mix_b, w_a, w_b, mod_l, mod_c)


def _gelu_tanh(x):
    return 0.5 * x * (1.0 + jnp.tanh(math.sqrt(2.0 / math.pi) * (x + 0.044715 * (x * x * x))))


def _out1_kernel(x_ref, a_ref, yf_ref, yb_ref, u_ref, dsk_ref, gw_ref, gb_ref, wa_ref, wb_ref,
                 gl_ref, gc_ref, o_ref, *, tm, n_lat):
    i = pl.program_id(1)
    y = _gelu_tanh(yf_ref[0] + yb_ref[0] + dsk_ref[...] * u_ref[...])
    glu = _dot(y.astype(BF16), gw_ref[...]) + gb_ref[...]
    y = y * _sigmoid(glu)
    o = _dot(a_ref[0], wa_ref[...]) + _dot(y.astype(BF16), wb_ref[...])
    o_ref[0] = x_ref[0] + _row_select(i, tm, n_lat, gc_ref[...], gl_ref[0]) * o


def _out_proj1(x_all, mix_a, y_dirs, u_t, d_skip, glu_w, glu_b, w_a, w_b, mod_l, mod_c, n_lat, n_rows, tm):
    bsz, _, d = x_all.shape
    tall = n_rows
    width = d_skip.shape[1]
    y3, u2 = y_dirs, u_t
    return pl.pallas_call(
        functools.partial(_out1_kernel, tm=tm, n_lat=n_lat),
        grid=(bsz, tall // tm),
        in_specs=[pl.BlockSpec((1, tm, d), lambda b, i: (b, i, 0)),
                  pl.BlockSpec((1, tm, mix_a.shape[2]), lambda b, i: (b, i, 0)),
                  pl.BlockSpec((1, tm, width), lambda b, i: (0, i, b)),
                  pl.BlockSpec((1, tm, width), lambda b, i: (1, i, b)),
                  pl.BlockSpec((tm, width), lambda b, i: (i, b)),
                  pl.BlockSpec((1, width), lambda b, i: (0, 0)),
                  pl.BlockSpec(glu_w.shape, lambda b, i: (0, 0)),
                  pl.BlockSpec((1, width), lambda b, i: (0, 0)),
                  pl.BlockSpec(w_a.shape, lambda b, i: (0, 0)),
                  pl.BlockSpec(w_b.shape, lambda b, i: (0, 0))] + _mod_specs(d, (2,), 2),
        out_specs=pl.BlockSpec((1, tm, d), lambda b, i: (b, i, 0)),
        out_shape=jax.ShapeDtypeStruct((bsz, tall, d), F32),
        compiler_params=_cparams(2),
        name="out_proj_odd",
    )(x_all, mix_a, y3, y3, u2, d_skip, glu_w, glu_b, w_a, w_b, mod_l, mod_c)


def _mlp_kernel(*refs, tm, n_lat, final):
    if final:
        (x_ref, nw_ref, shl_ref, shc_ref, scl_ref, scc_ref, gl_ref, gc_ref, w1_ref, w2_ref, fw_ref,
         o_ref, h_scr, acc_scr) = refs
    else:
        (x_ref, nw_ref, shl_ref, shc_ref, scl_ref, scc_ref, gl_ref, gc_ref, w1_ref, w2_ref,
         o_ref, h_scr, acc_scr) = refs
    i = pl.program_id(1)
    j = pl.program_id(2)

    last_j = pl.num_programs(2) - 1

    @pl.when(j == 0)
    def _():
        _store_norm_modulated(h_scr, x_ref[0], nw_ref[...], shl_ref[0], shc_ref[...], scl_ref[0], scc_ref[...],
                              i, tm, n_lat)

    a = jnp.maximum(_dot(h_scr[...], w1_ref[...]), 0.0)
    part = _dot((a * a).astype(BF16), w2_ref[...])

    @pl.when(j == 0)
    def _():
        acc_scr[...] = part

    @pl.when(jnp.logical_and(j > 0, j < last_j))
    def _():
        acc_scr[...] += part

    @pl.when(j == last_j)
    def _():
        post = (lambda v: _rms(v, fw_ref[...])) if final else None
        _gated_residual(o_ref, x_ref[0], gl_ref[0], gc_ref[...], acc_scr[...] + part, i, tm, n_lat, post)


def _mlp(x_all, norm_w, mod_l, mod_c, w1, w2, final_w, n_lat, tm, tf):
    bsz, tall, d = x_all.shape
    ff = w1.shape[1]
    assert ff // tf >= 2
    final = final_w is not None
    in_specs = [pl.BlockSpec((1, tm, d), lambda b, i, j: (b, i, 0)),
                pl.BlockSpec((1, d), lambda b, i, j: (0, 0))]
    in_specs += _mod_specs(d, (3, 4, 5), 3)
    in_specs += [pl.BlockSpec((d, tf), lambda b, i, j: (0, j)),
                 pl.BlockSpec((tf, d), lambda b, i, j: (j, 0))]
    args = [x_all, norm_w.reshape(1, d)] + [mod_l, mod_c] * 3 + [w1, w2]
    if final:
        in_specs.append(pl.BlockSpec((1, d), lambda b, i, j: (0, 0)))
        args.append(final_w.reshape(1, d))
    return pl.pallas_call(
        functools.partial(_mlp_kernel, tm=tm, n_lat=n_lat, final=final),
        grid=(bsz, tall // tm, ff // tf),
        in_specs=in_specs,
        out_specs=pl.BlockSpec((1, tm, d), lambda b, i, j: (b, i, 0)),
        out_shape=jax.ShapeDtypeStruct((bsz, tall, d), F32),
        scratch_shapes=[pltpu.VMEM((tm, d), BF16), pltpu.VMEM((tm, d), F32)],
        compiler_params=_cparams(3),
        name="sq_relu_mlp",
    )(*args)


def _even_in_weight(w_in):
    sizes = (SSD_INNER, SSD_INNER + 2 * SSD_BC, 2 * SSD_HEADS, GLA_KEY, GLA_KEY, GLA_VAL,
             2 * GLA_GATE_RANK, GLA_VAL)
    offs = [0]
    for s in sizes:
        offs.append(offs[-1] + s)
    z, xbc, dt, q, k, v, lr, r = (w_in[:, offs[n]:offs[n + 1]] for n in range(8))
    pad = jnp.zeros((w_in.shape[0], LANE - dt.shape[1] - lr.shape[1]), w_in.dtype)
    main = jnp.concatenate([xbc, z, v, r, q, k], axis=1).astype(BF16)
    aux = jnp.concatenate([dt, lr, pad], axis=1).astype(BF16)
    return main, aux


def _s5_params(a_re, a_im, log_dt, b_re, b_im, c_re, c_im):
    delta = jnp.exp(log_dt.astype(F32))[..., None]
    mag = jnp.exp(a_re * delta)
    lbar_re, lbar_im = mag * jnp.cos(a_im * delta), mag * jnp.sin(a_im * delta)
    den = a_re * a_re + a_im * a_im
    zr = ((lbar_re - 1.0) * a_re + lbar_im * a_im) / den
    zi = (lbar_im * a_re - (lbar_re - 1.0) * a_im) / den
    bb_re = zr[..., None] * b_re - zi[..., None] * b_im
    bb_im = zr[..., None] * b_im + zi[..., None] * b_re
    n_slabs = S5_GROUPS // S5_SLAB
    eye = jnp.eye(S5_SLAB, dtype=F32)
    sw = S5_SLAB * S5_STATE

    def block_in(bb):
        bb = bb.reshape(2, n_slabs, S5_SLAB, S5_STATE, S5_GROUP)
        return jnp.einsum("dsgpc,gh->dsgchp", bb, eye).reshape(2, n_slabs, LANE, sw)

    def block_out(cc):
        cc = cc.reshape(n_slabs, S5_SLAB, S5_GROUP, S5_STATE)
        return jnp.einsum("sgcp,gh->sgphc", cc, eye).reshape(n_slabs, sw, LANE)

    bmat = jnp.concatenate([block_in(bb_re), block_in(bb_im)], axis=3).astype(BF16)
    cmat = jnp.concatenate([block_out(c_re), -block_out(c_im)], axis=1).astype(BF16)
    return (bmat, lbar_re.reshape(2, 1, S5_NSTATE), lbar_im.reshape(2, 1, S5_NSTATE), cmat)


def _layer_even(x_all, mod_l, mod_c, n_lat, tm, n_rows_out, tm_out, norm1_w, w_in, conv_w, conv_b, dt_bias,
                a_log, d_skip, ssd_norm_w, gate_w, gate_b, gla_norm_w, w_out):
    w, w_aux = _even_in_weight(w_in)
    n = w.shape[1]
    proj, aux = _project(x_all, norm1_w, mod_l, mod_c, w, w_aux, n_lat, tm, _largest_divisor(n, 2 * LANE, 1024),
                         extra_token_major=False)
    n_xbc = SSD_INNER + 2 * SSD_BC
    c_z, c_v, c_r = n_xbc // SSD_INNER, n_xbc // GLA_VAL + 1, n_xbc // GLA_VAL + 2
    c_q = (n_xbc + 3 * SSD_INNER) // GLA_KEY
    xbc = _conv_silu(proj, conv_w, conv_b, n_lat, n_xbc)

    neg_a = -jnp.exp(a_log.astype(F32))
    dt_bias = dt_bias.astype(F32)
    d_wide = jnp.repeat(d_skip.astype(F32), SSD_HEADDIM).reshape(1, SSD_INNER)
    y_f = _ssd_scan(xbc, proj, aux, dt_bias, neg_a, c_z, n_lat, 0)
    y_mix = _ssd_scan(xbc, proj, aux, dt_bias, neg_a, c_z, n_lat, 1,
                      (y_f, d_wide, ssd_norm_w.reshape(1, SSD_INNER)))

    cols = {"q": c_q, "k": c_q + 1, "v": c_v, "gate": c_r}
    gparams = [(gate_w[d].astype(BF16), gate_b[d].reshape(1, GLA_KEY).astype(F32)) for d in range(2)]
    o_f = _lin_scan("gla", proj, cols, gparams[0], n_lat, 0, aux=aux)
    o_mix = _lin_scan("gla", proj, cols, gparams[1], n_lat, 1, (o_f, gla_norm_w.reshape(1, GLA_DV)), aux=aux)

    w_out = w_out.astype(BF16)
    return _out_proj0(x_all, y_mix, o_mix, w_out[:SSD_INNER], w_out[SSD_INNER:], mod_l, mod_c, n_lat,
                      n_rows_out, tm_out)


def _layer_odd(x_all, mod_l, mod_c, n_lat, tm, n_rows_out, tm_out, norm1_w, w_in, lb, hgrn_norm_w, a_re, a_im,
               log_dt, b_re, b_im, c_re, c_im, d_skip, glu_w, glu_b, w_out):
    bsz = x_all.shape[0]
    n_main = 5 * HGRN_WIDTH
    w_main = w_in[:, :n_main].astype(BF16)
    w_u = w_in[:, n_main:].astype(BF16)
    proj, u_t = _project(x_all, norm1_w, mod_l, mod_c, w_main, w_u, n_lat, tm, HGRN_WIDTH)

    lb = lb.astype(F32).reshape(2, 1, HGRN_WIDTH)
    cols = {"q": 0, "v": 1, "aux": 2, "gate": 4}
    o_f = _lin_scan("hgrn", proj, cols, (lb[0], 1.0 - lb[0]), n_lat, 0)
    o_mix = _lin_scan("hgrn", proj, cols, (lb[1], 1.0 - lb[1]), n_lat, 1,
                      (o_f, hgrn_norm_w.reshape(1, HGRN_DV)))

    bmat, lam_re, lam_im, cmat = _s5_params(a_re.astype(F32), a_im.astype(F32), log_dt, b_re.astype(F32),
                                            b_im.astype(F32), c_re.astype(F32), c_im.astype(F32))
    y_dirs = _s5_scan(u_t, bmat, lam_re, lam_im, cmat, bsz, n_lat)

    w_out = w_out.astype(BF16)
    return _out_proj1(x_all, o_mix, y_dirs, u_t, d_skip.astype(F32).reshape(1, S5_WIDTH),
                      glu_w.astype(BF16), glu_b.astype(F32).reshape(1, S5_WIDTH),
                      w_out[:HGRN_WIDTH], w_out[HGRN_WIDTH:], mod_l, mod_c, n_lat, n_rows_out, tm_out)


def kernel(x, c, ctx, c_ctx, ada_w, ada_b, norm1_w, norm2_w, ssd_gla_w_in, ssd_conv_w, ssd_conv_b, ssd_dt_bias, ssd_a_log, ssd_d, ssd_norm_w, gla_gate_w, gla_gate_b, gla_norm_w, ssd_gla_w_out, hgrn_s5_w_in, hgrn_lb_logits, hgrn_norm_w, s5_a_re, s5_a_im, s5_log_dt, s5_b_re, s5_b_im, s5_c_re, s5_c_im, s5_d, s5_glu_w, s5_glu_b, hgrn_s5_w_out, mlp_w1, mlp_w2, final_norm_w):
    bsz, n_lat, d = x.shape
    ctx_len = ctx.shape[1]
    depth = ada_w.shape[0]
    tall = n_lat + ctx_len
    assert bsz % 8 == 0 and ctx_len % SCAN_BLOCK == 0 and n_lat % SCAN_BLOCK == 0 and n_lat % GRID_W == 0
    tf = 1024

    n_rows = -(-(bsz + 1) // 8) * 8
    cvec = jnp.concatenate([c, c_ctx[None, :], jnp.zeros((n_rows - bsz - 1, d), c.dtype)], axis=0)
    mod = _modulation(cvec.astype(F32), ada_w, ada_b)

    p_lb = jax.nn.softmax(hgrn_lb_logits.astype(F32), axis=0)
    lb_all = jnp.cumsum(p_lb, axis=0) - p_lb[0]

    x_all = jnp.concatenate([x, ctx], axis=1).astype(F32)
    tm_all = _largest_divisor(tall, 16, 1056)
    for layer in range(depth):
        j = layer // 2
        last = layer == depth - 1
        n_rows = n_lat if last else tall
        tm_out = _largest_divisor(n_rows, 16, 1056)
        mod_l = mod[layer, :bsz].reshape(bsz, 1, N_MOD * d)
        mod_c = mod[layer, bsz:bsz + 1]
        if layer % 2 == 0:
            x_all = _layer_even(x_all, mod_l, mod_c, n_lat, tm_all, n_rows, tm_out, norm1_w[layer],
                                ssd_gla_w_in[j], ssd_conv_w[j], ssd_conv_b[j], ssd_dt_bias[j], ssd_a_log[j],
                                ssd_d[j], ssd_norm_w[j], gla_gate_w[j], gla_gate_b[j], gla_norm_w[j],
                                ssd_gla_w_out[j])
        else:
            x_all = _layer_odd(x_all, mod_l, mod_c, n_lat, tm_all, n_rows, tm_out, norm1_w[layer],
                               hgrn_s5_w_in[j], lb_all[layer], hgrn_norm_w[j], s5_a_re[j], s5_a_im[j],
                               s5_log_dt[j], s5_b_re[j], s5_b_im[j], s5_c_re[j], s5_c_im[j], s5_d[j],
                               s5_glu_w[j], s5_glu_b[j], hgrn_s5_w_out[j])
        x_all = _mlp(x_all, norm2_w[layer], mod_l, mod_c, mlp_w1[layer].astype(BF16),
                     mlp_w2[layer].astype(BF16), final_norm_w if last else None, n_lat, tm_out, tf)
    return x_all.astype(x.dtype)
```

```python
import functools
import math

import jax
import jax.numpy as jnp
from jax import lax
from jax.experimental import pallas as pl
from jax.experimental.pallas import tpu as pltpu

F32 = jnp.float32
BF16 = jnp.bfloat16

GRID_W = 64
NORM_EPS = 1e-6
N_MOD = 6
SSD_HEADDIM = 64
SSD_HEADS = 16
SSD_GROUPS = 4
SSD_STATE = 128
SSD_CHUNK = 128
GLA_HEADS = 8
GLA_DK = 64
GLA_DV = 128
GLA_GATE_RANK = 16
GLA_GATE_NORM = 16.0
HGRN_HEADS = 8
HGRN_DK = 128
HGRN_DV = 128
S5_GROUP = 16
S5_GROUPS = 24
S5_STATE = 64
LIN_CHUNK = 64

SSD_INNER = SSD_HEADS * SSD_HEADDIM
SSD_BC = SSD_GROUPS * SSD_STATE
GLA_KEY = GLA_HEADS * GLA_DK
GLA_VAL = GLA_HEADS * GLA_DV
HGRN_WIDTH = HGRN_HEADS * HGRN_DV
S5_WIDTH = S5_GROUPS * S5_GROUP
S5_NSTATE = S5_GROUPS * S5_STATE

VMEM_LIMIT_BYTES = 56 * 1024 * 1024
LANE = 128
SCAN_BLOCK = 256
S5_CHUNK = 128
S5_SLAB = LANE // S5_GROUP


def _cparams(n_axes):
    return pltpu.CompilerParams(dimension_semantics=("arbitrary",) * n_axes,
                                vmem_limit_bytes=VMEM_LIMIT_BYTES)


def _largest_divisor(n, multiple, cap):
    best = None
    for d in range(multiple, min(n, cap) + 1, multiple):
        if n % d == 0:
            best = d
    assert best is not None, (n, multiple, cap)
    return best


_NEG_LOG2E = -1.4426950408889634


def _sigmoid(x):
    return 1.0 / (1.0 + jnp.exp2(x * _NEG_LOG2E))


def _silu(x):
    return x * _sigmoid(x)


def _softplus(x):
    return jnp.maximum(x, 0.0) + jnp.log1p(jnp.exp(-jnp.abs(x)))


def _log_sigmoid(x):
    return -_softplus(-x)


def _rms(x, w):
    return x * lax.rsqrt(jnp.mean(x * x, axis=-1, keepdims=True) + NORM_EPS) * w


def _dot(a, b, dims=(((1,), (0,)), ((), ())), precision=None):
    return lax.dot_general(a, b, dims, precision=precision, preferred_element_type=F32)


def _split3(v):
    hi = v.astype(BF16)
    r1 = v - hi.astype(F32)
    mid = r1.astype(BF16)
    lo = (r1 - mid.astype(F32)).astype(BF16)
    return hi, mid, lo


def _tri3(mask):
    tri = mask.astype(BF16)
    return jnp.concatenate([tri, tri, tri], axis=1)


def _cumsum_rows(tri2, v):
    hi = v.astype(BF16)
    lo = (v - hi.astype(F32)).astype(BF16)
    return _dot(tri2, jnp.concatenate([hi, lo], axis=0))


_NT = (((1,), (1,)), ((), ()))
_TN = (((0,), (0,)), ((), ()))
_TT = (((0,), (1,)), ((), ()))


def _mod_kernel(c_ref, w_ref, b_ref, o_ref):
    a = _silu(c_ref[...]).astype(BF16)
    o_ref[0] = _dot(a, w_ref[0].astype(BF16)) + b_ref[0]


def _modulation(cvec, ada_w, ada_b):
    depth, d, n = ada_w.shape
    rows = cvec.shape[0]
    tn = _largest_divisor(n, LANE, 1024)
    return pl.pallas_call(
        _mod_kernel,
        grid=(depth, n // tn),
        in_specs=[pl.BlockSpec((rows, d), lambda l, j: (0, 0)),
                  pl.BlockSpec((1, d, tn), lambda l, j: (l, 0, j)),
                  pl.BlockSpec((1, 1, tn), lambda l, j: (l, 0, j))],
        out_specs=pl.BlockSpec((1, rows, tn), lambda l, j: (l, 0, j)),
        out_shape=jax.ShapeDtypeStruct((depth, rows, n), F32),
        compiler_params=_cparams(2),
        name="adaln_mod",
    )(cvec, ada_w, ada_b.reshape(depth, 1, n))


def _row_select(i, tm, n_lat, ctx_val, lat_val):
    row = i * tm + lax.broadcasted_iota(jnp.int32, (tm, 1), 0)
    return jnp.where(row >= n_lat, ctx_val, lat_val)


def _store_norm_modulated(h_ref, x, nw, shift_l, shift_c, scale_l, scale_c, i, tm, n_lat):
    xn = x * lax.rsqrt(jnp.mean(x * x, axis=-1, keepdims=True) + NORM_EPS)
    gain = _row_select(i, tm, n_lat, nw * (1.0 + scale_c), nw * (1.0 + scale_l))
    h_ref[...] = (xn * gain + _row_select(i, tm, n_lat, shift_c, shift_l)).astype(h_ref.dtype)


def _mod_specs(d, cols, n_grid_axes):
    specs = []
    for k in cols:
        if n_grid_axes == 2:
            specs.append(pl.BlockSpec((1, 1, d), lambda b, i, k=k: (b, 0, k)))
            specs.append(pl.BlockSpec((1, d), lambda b, i, k=k: (0, k)))
        else:
            specs.append(pl.BlockSpec((1, 1, d), lambda b, i, j, k=k: (b, 0, k)))
            specs.append(pl.BlockSpec((1, d), lambda b, i, j, k=k: (0, k)))
    return specs


def _proj_kernel(*refs, tm, n_lat, has_extra):
    if has_extra:
        (x_ref, nw_ref, shl_ref, shc_ref, scl_ref, scc_ref, w_ref, wx_ref, o_ref, ox_ref, h_scr) = refs
    else:
        (x_ref, nw_ref, shl_ref, shc_ref, scl_ref, scc_ref, w_ref, o_ref, h_scr) = refs
    i = pl.program_id(1)
    j = pl.program_id(2)

    @pl.when(j == 0)
    def _():
        _store_norm_modulated(h_scr, x_ref[0], nw_ref[...], shl_ref[0], shc_ref[...], scl_ref[0], scc_ref[...],
                              i, tm, n_lat)
        if has_extra:
            ox_ref[...] = _dot(h_scr[...], wx_ref[...]).reshape(ox_ref.shape)

    o_ref[0] = _dot(h_scr[...], w_ref[...]).astype(o_ref.dtype)


def _project(x_all, norm_w, mod_l, mod_c, w, w_extra, n_lat, tm, tn, extra_token_major=True):
    bsz, tall, d = x_all.shape
    n = w.shape[1]
    has_extra = w_extra is not None
    in_specs = [pl.BlockSpec((1, tm, d), lambda b, i, j: (b, i, 0)),
                pl.BlockSpec((1, d), lambda b, i, j: (0, 0))]
    in_specs += _mod_specs(d, (0, 1), 3)
    in_specs.append(pl.BlockSpec((d, tn), lambda b, i, j: (0, j)))
    args = [x_all, norm_w.reshape(1, d), mod_l, mod_c, mod_l, mod_c, w]
    out_specs = [pl.BlockSpec((1, tm, tn), lambda b, i, j: (b, i, j))]
    out_shape = [jax.ShapeDtypeStruct((bsz, tall, n), BF16)]
    if has_extra:
        nx = w_extra.shape[1]
        in_specs.append(pl.BlockSpec((d, nx), lambda b, i, j: (0, 0)))
        args.append(w_extra)
        if extra_token_major:
            out_specs.append(pl.BlockSpec((tm, nx), lambda b, i, j: (i, b)))
            out_shape.append(jax.ShapeDtypeStruct((tall, bsz * nx), F32))
        else:
            out_specs.append(pl.BlockSpec((1, tm, nx), lambda b, i, j: (b, i, 0)))
            out_shape.append(jax.ShapeDtypeStruct((bsz, tall, nx), F32))
    out = pl.pallas_call(
        functools.partial(_proj_kernel, tm=tm, n_lat=n_lat, has_extra=has_extra),
        grid=(bsz, tall // tm, n // tn),
        in_specs=in_specs,
        out_specs=out_specs,
        out_shape=out_shape,
        scratch_shapes=[pltpu.VMEM((tm, d), BF16)],
        compiler_params=_cparams(3),
        name="norm_mod_proj",
    )(*args)
    return out if has_extra else out[0]


def _conv_kernel(main_ref, prev_ref, next_ref, w_ref, b_ref, o_ref, *, tt, n_lat, tall):
    i = pl.program_id(1)
    p = i * tt + lax.broadcasted_iota(jnp.int32, (tt, 1), 0)
    is_ctx = p >= n_lat
    col = jnp.bitwise_and(p, GRID_W - 1)
    m_up = jnp.where(is_ctx, 0, p) >= GRID_W
    m_dn = jnp.where(is_ctx, n_lat, p) < n_lat - GRID_W
    m_l = jnp.where(is_ctx, p - n_lat, col) > 0
    m_r = jnp.where(is_ctx, p - (tall - 1), col - (GRID_W - 1)) < 0

    main = main_ref[0].astype(F32)
    rows = {
        -1: jnp.concatenate([prev_ref[0].astype(F32), main[:tt - GRID_W]], axis=0),
        0: main,
        1: jnp.concatenate([main[GRID_W:], next_ref[0].astype(F32)], axis=0),
    }
    w = w_ref[...]
    acc = jnp.zeros_like(main) + b_ref[...]
    for dy in (-1, 0, 1):
        s = rows[dy]
        k0 = 3 * (dy + 1)
        t = (s * w[k0 + 1:k0 + 2]
             + jnp.where(m_l, pltpu.roll(s, 1, 0), 0.0) * w[k0:k0 + 1]
             + jnp.where(m_r, pltpu.roll(s, tt - 1, 0), 0.0) * w[k0 + 2:k0 + 3])
        if dy == -1:
            t = jnp.where(m_up, t, 0.0)
        elif dy == 1:
            t = jnp.where(m_dn, t, 0.0)
        acc = acc + t
    o_ref[0] = _silu(acc).astype(o_ref.dtype)


def _conv_silu(proj, conv_w, conv_b, n_lat, n_ch):
    bsz, tall, _ = proj.shape
    n_rows = tall // GRID_W
    tt = _largest_divisor(tall, GRID_W, 768)
    assert n_lat % GRID_W == 0 and n_lat // tt == (tall - 1) // tt
    r = tt // GRID_W
    tc = 512
    return pl.pallas_call(
        functools.partial(_conv_kernel, tt=tt, n_lat=n_lat, tall=tall),
        grid=(bsz, tall // tt, n_ch // tc),
        in_specs=[pl.BlockSpec((1, tt, tc), lambda b, i, c: (b, i, c)),
                  pl.BlockSpec((1, GRID_W, tc), lambda b, i, c: (b, jnp.maximum(i * r - 1, 0), c)),
                  pl.BlockSpec((1, GRID_W, tc), lambda b, i, c: (b, jnp.minimum((i + 1) * r, n_rows - 1), c)),
                  pl.BlockSpec((9, tc), lambda b, i, c: (0, c)),
                  pl.BlockSpec((1, tc), lambda b, i, c: (0, c))],
        out_specs=pl.BlockSpec((1, tt, tc), lambda b, i, c: (b, i, c)),
        out_shape=jax.ShapeDtypeStruct((bsz, tall, n_ch), BF16),
        compiler_params=_cparams(3),
        name="dwconv_silu",
    )(proj, proj, proj, conv_w.reshape(9, n_ch), conv_b.reshape(1, n_ch))


def _scan_block(s, n_lat_blocks, n_blocks, reverse):
    n_ctx_blocks = n_blocks - n_lat_blocks
    if not reverse:
        return jnp.where(s < n_ctx_blocks, n_lat_blocks + s, s - n_ctx_blocks)
    return n_blocks - 1 - s


def _tri_mask(c, reverse):
    ri = lax.broadcasted_iota(jnp.int32, (c, c), 0)
    ci = lax.broadcasted_iota(jnp.int32, (c, c), 1)
    return (ci >= ri) if reverse else (ci <= ri)


def _ssd_expand_matrix():
    eye = jnp.eye(SSD_HEADS, dtype=F32)
    e_head = jnp.repeat(eye, SSD_HEADDIM, axis=1)
    e_seg = jnp.repeat(eye, SSD_CHUNK, axis=1)
    zh = jnp.zeros_like(e_head)
    zs = jnp.zeros_like(e_seg)
    blk = jnp.concatenate([
        jnp.concatenate([e_head, zh, zh, zs], axis=1),
        jnp.concatenate([zh, e_head, zh, zs], axis=1),
        jnp.concatenate([zh, zh, e_head, zs], axis=1),
        jnp.concatenate([zh, zh, zh, e_seg], axis=1)], axis=0)
    return jnp.concatenate([blk, blk, blk], axis=0).astype(BF16)


def _ssd_kernel(*refs, direction, finish):
    if finish:
        (x_ref, bm_ref, cm_ref, dtlr_ref, dtb_ref, nega_ref, exp_ref, z_ref, yf_ref, dsk_ref, nw_ref,
         o_ref, st_ref) = refs
    else:
        (x_ref, bm_ref, cm_ref, dtlr_ref, dtb_ref, nega_ref, exp_ref, o_ref, st_ref) = refs
    reverse = direction == 1
    c = SSD_CHUNK
    p = SSD_HEADDIM
    gw = SSD_INNER // SSD_GROUPS
    hpg = SSD_HEADS // SSD_GROUPS

    @pl.when(pl.program_id(1) == 0)
    def _():
        st_ref[...] = jnp.zeros_like(st_ref)

    mask = _tri_mask(c, reverse)
    tri3 = _tri3(mask)
    last = 0 if reverse else c - 1
    n_chunks = x_ref.shape[1] // c
    order = range(n_chunks - 1, -1, -1) if reverse else range(n_chunks)
    groups = range(SSD_GROUPS)
    g_cols = [slice(g * gw, (g + 1) * gw) for g in groups]
    n_cols = [slice(g * SSD_STATE, (g + 1) * SSD_STATE) for g in groups]
    cb = {(ck, g): _dot(cm_ref[0, ck * c:(ck + 1) * c, n_cols[g]], bm_ref[0, ck * c:(ck + 1) * c, n_cols[g]], _NT)
          for ck in order for g in groups}
    prep = {}
    for ck in order:
        rs = slice(ck * c, (ck + 1) * c)
        x = x_ref[0, rs, :].astype(F32)
        dt_raw = dtlr_ref[0, rs, :][:, direction * SSD_HEADS:(direction + 1) * SSD_HEADS].astype(F32)
        dt = _softplus(dt_raw + dtb_ref[...])
        la3 = jnp.concatenate(_split3(dt * nega_ref[...]), axis=0)
        acum = _dot(tri3, la3)
        acum_t = _dot(la3, tri3, _TT)
        a_last = acum[last:last + 1]
        narrow = jnp.concatenate([dt, dt * jnp.exp(a_last - acum), jnp.exp(acum), acum], axis=1)
        wide = _dot(jnp.concatenate(_split3(narrow), axis=1), exp_ref[...])
        ea_w = wide[:, 2 * SSD_INNER:3 * SSD_INNER]
        prep[ck] = dict(
            x=x, bm=bm_ref[0, rs, :], cm=cm_ref[0, rs, :], wide=wide, acum_t=acum_t, ea_w=ea_w,
            xdt=(x * wide[:, :SSD_INNER]).astype(BF16),
            xw=(x * wide[:, SSD_INNER:2 * SSD_INNER]).astype(BF16),
            e_last=ea_w[last:last + 1])
    kv = {(ck, g): _dot(prep[ck]["bm"][:, n_cols[g]], prep[ck]["xw"][:, g_cols[g]], _TN)
          for ck in order for g in groups}
    scores, st_used = {}, {}
    for g in groups:
        st = st_ref[:, g_cols[g]]
        for ck in order:
            st_used[ck, g] = st.astype(BF16)
            st = st * prep[ck]["e_last"][:, g_cols[g]] + kv[ck, g]
        st_ref[:, g_cols[g]] = st
    for ck in order:
        for h in range(SSD_HEADS):
            a_i = prep[ck]["wide"][:, 3 * SSD_INNER + h * c:3 * SSD_INNER + (h + 1) * c]
            decay = jnp.exp(jnp.where(mask, a_i - prep[ck]["acum_t"][h:h + 1, :], -jnp.inf))
            scores[ck, h] = (cb[ck, h // hpg] * decay).astype(BF16)
    for ck in order:
        rs = slice(ck * c, (ck + 1) * c)
        x = prep[ck]["x"]
        y_groups = []
        for g in groups:
            ys = [_dot(scores[ck, h], prep[ck]["xdt"][:, h * p:(h + 1) * p])
                  for h in range(g * hpg, (g + 1) * hpg)]
            y_state = _dot(prep[ck]["cm"][:, n_cols[g]], st_used[ck, g]) * prep[ck]["ea_w"][:, g_cols[g]]
            y_groups.append(jnp.concatenate(ys, axis=1) + y_state)
        y = jnp.concatenate(y_groups, axis=1)
        if finish:
            z = z_ref[0, rs, :].astype(F32)
            y = (y + yf_ref[0, rs, :] + dsk_ref[...] * x) * _silu(z)
            outs = []
            for g in range(SSD_GROUPS):
                sl = slice(g * gw, (g + 1) * gw)
                outs.append(_rms(y[:, sl], nw_ref[:, sl]))
            o_ref[0, rs, :] = jnp.concatenate(outs, axis=1).astype(o_ref.dtype)
        else:
            o_ref[0, rs, :] = y


def _ssd_scan(xbc, proj, aux, dt_bias, neg_a, z_col, n_lat, direction, finish_args=None):
    bsz, tall, _ = xbc.shape
    tb = SCAN_BLOCK
    nb, n_lat_blocks = tall // tb, n_lat // tb
    reverse = direction == 1
    finish = finish_args is not None
    expand = _ssd_expand_matrix()

    def tok(col):
        return lambda b, s: (b, _scan_block(s, n_lat_blocks, nb, reverse), col)

    in_specs = [pl.BlockSpec((1, tb, SSD_INNER), tok(0)),
                pl.BlockSpec((1, tb, SSD_BC), tok(SSD_INNER // SSD_BC)),
                pl.BlockSpec((1, tb, SSD_BC), tok(SSD_INNER // SSD_BC + 1)),
                pl.BlockSpec((1, tb, LANE), tok(0)),
                pl.BlockSpec((1, SSD_HEADS), lambda b, s: (0, 0)),
                pl.BlockSpec((1, SSD_HEADS), lambda b, s: (0, 0)),
                pl.BlockSpec(expand.shape, lambda b, s: (0, 0))]
    args = [xbc, xbc, xbc, aux, dt_bias[direction:direction + 1], neg_a[direction:direction + 1], expand]
    if finish:
        y_f, d_skip_wide, norm_w = finish_args
        in_specs += [pl.BlockSpec((1, tb, SSD_INNER), tok(z_col)),
                     pl.BlockSpec((1, tb, SSD_INNER), tok(0)),
                     pl.BlockSpec((1, SSD_INNER), lambda b, s: (0, 0)),
                     pl.BlockSpec((1, SSD_INNER), lambda b, s: (0, 0))]
        args += [proj, y_f, d_skip_wide, norm_w]
    return pl.pallas_call(
        functools.partial(_ssd_kernel, direction=direction, finish=finish),
        grid=(bsz, nb),
        in_specs=in_specs,
        out_specs=pl.BlockSpec((1, tb, SSD_INNER), tok(0)),
        out_shape=jax.ShapeDtypeStruct((bsz, tall, SSD_INNER), BF16 if finish else F32),
        scratch_shapes=[pltpu.VMEM((SSD_STATE, SSD_INNER), F32)],
        compiler_params=_cparams(2),
        name="ssd_scan_bwd" if reverse else "ssd_scan_fwd",
    )(*args)


def _lin_kernel(*refs, mode, direction, finish, heads, dk, dv):
    refs = list(refs)
    if mode == "gla":
        q_ref, k_ref, v_ref, aux_ref, p1_ref, p2_ref = refs[:6]
        rest = refs[6:]
    else:
        q_ref, v_ref, aux_ref, p1_ref, p2_ref = refs[:5]
        k_ref = None
        rest = refs[5:]
    if finish:
        of_ref, gate_ref, nw_ref, o_ref, st_ref = rest
    else:
        o_ref, st_ref = rest
    reverse = direction == 1
    c = LIN_CHUNK

    @pl.when(pl.program_id(1) == 0)
    def _():
        st_ref[...] = jnp.zeros_like(st_ref)

    tb = q_ref.shape[1]
    nc = tb // c
    hpt = LANE // dk
    chunks = range(nc - 1, -1, -1) if reverse else range(nc)
    last = 0 if reverse else c - 1

    ri = lax.broadcasted_iota(jnp.int32, (tb, tb), 0)
    ci = lax.broadcasted_iota(jnp.int32, (tb, tb), 1)
    c_shift = c.bit_length() - 1
    same_chunk = jnp.right_shift(ri, c_shift) == jnp.right_shift(ci, c_shift)
    if reverse:
        bd_mask = jnp.where(same_chunk, ci - ri, -1) >= 0
    else:
        bd_mask = jnp.where(same_chunk, ci - ri, 1) <= 0
    tri = _tri_mask(c, reverse).astype(BF16)
    tri2 = jnp.concatenate([tri, tri], axis=1)

    zeros = jnp.zeros((c, LANE), BF16)

    def chunk_blocks(a):
        cols = []
        for b in range(nc):
            cols.append(jnp.concatenate(
                [a[cc * c:(cc + 1) * c] if cc == b else zeros for cc in range(nc)], axis=0))
        return jnp.concatenate(cols, axis=1)

    n_tiles = heads // hpt
    span = n_tiles if mode == "gla" else 1
    gw = span * LANE

    def prep(g):
        ls = slice(g * gw, (g + 1) * gw)
        if mode == "gla":
            q = q_ref[0, :, ls].astype(F32) * (dk ** -0.5)
            k = k_ref[0, :, ls].astype(F32)
            off = 2 * SSD_HEADS + direction * GLA_GATE_RANK
            lr = aux_ref[0][:, off:off + GLA_GATE_RANK].astype(BF16)
            lg = _log_sigmoid(_dot(lr, p1_ref[:, ls]) + p2_ref[:, ls]) * (1.0 / GLA_GATE_NORM)
        else:
            q = _silu(q_ref[0, :, ls].astype(F32))
            f_raw = aux_ref[0, :, ls].astype(F32)
            e = jnp.exp2(jnp.abs(f_raw) * _NEG_LOG2E)
            r = 1.0 / (1.0 + e)
            forget = p1_ref[:, ls] + p2_ref[:, ls] * jnp.where(f_raw >= 0.0, r, e * r)
            lg = jnp.log(forget)
            k = 1.0 - forget
        gcum = jnp.concatenate([_cumsum_rows(tri2, lg[cc * c:(cc + 1) * c]) for cc in range(nc)], axis=0)
        e_last = [jnp.exp(gcum[cc * c + last:cc * c + last + 1]) for cc in range(nc)]
        e_rows = jnp.concatenate([jnp.broadcast_to(e, (c, gw)) for e in e_last], axis=0)
        e_gcum = jnp.exp(gcum)
        q_decf = q * e_gcum
        if hpt > 1:
            head_of_lane = jnp.bitwise_and(jnp.right_shift(
                lax.broadcasted_iota(jnp.int32, (1, gw), 1), dk.bit_length() - 1), hpt - 1)
            q_dec = [jnp.where(head_of_lane == r, q_decf, 0.0).astype(BF16) for r in range(hpt)]
        else:
            q_dec = [q_decf.astype(BF16)]
        k_invf = k * (1.0 / e_gcum)
        return dict(q_dec=q_dec, k_inv=k_invf.astype(BF16), k_end=(k_invf * e_rows).astype(BF16),
                    e_last=e_last)

    groups, work = {}, {}

    def tile_cols(h):
        t = (h // hpt) % span
        return slice(t * LANE, (t + 1) * LANE)

    def products(h):
        gp, ts = groups[h // hpt // span], tile_cols(h)
        qh = gp["q_dec"][h % hpt][:, ts]
        vh = v_ref[0, :, h * dv:(h + 1) * dv]
        work[h] = dict(
            qh=qh, vh=vh, scores=_dot(qh, gp["k_inv"][:, ts], _NT),
            kv_t=_dot(vh, chunk_blocks(gp["k_end"][:, ts]), _TN))

    def mask_and_chain(h):
        w = work[h]
        e_last = groups[h // hpt // span]["e_last"]
        w["att"] = jnp.where(bd_mask, w.pop("scores"), 0.0).astype(BF16)
        st = st_ref[h]
        used = [None] * nc
        for cc in chunks:
            used[cc] = st.astype(BF16)
            st = st * e_last[cc][:, tile_cols(h)] + w["kv_t"][:, cc * LANE:(cc + 1) * LANE]
        st_ref[h] = st
        w["used"] = used
        del w["kv_t"]

    def outputs(h):
        w = work[h]
        o_state = jnp.concatenate(
            [_dot(w["qh"][cc * c:(cc + 1) * c], w["used"][cc], _NT) for cc in range(nc)], axis=0)
        w["o"] = _dot(w["att"], w["vh"]) + o_state

    def emit(h):
        vs = slice(h * dv, (h + 1) * dv)
        o_h = work.pop(h)["o"]
        if finish:
            o_h = _rms(o_h + of_ref[0, :, vs], nw_ref[...]) * _silu(gate_ref[0, :, vs].astype(F32))
        o_ref[0, :, vs] = o_h.astype(o_ref.dtype)

    n_groups = n_tiles // span
    hpg = hpt * span
    lag = 2
    for it in range(n_groups + 4 * lag):
        def heads_of(g):
            return range(g * hpg, (g + 1) * hpg) if 0 <= g < n_groups else ()
        for h in heads_of(it - lag):
            products(h)
        for h in heads_of(it - 3 * lag):
            outputs(h)
        if it < n_groups:
            groups[it] = prep(it)
        for h in heads_of(it - 2 * lag):
            mask_and_chain(h)
        for h in heads_of(it - 4 * lag):
            emit(h)


def _lin_scan(mode, proj, cols, params, n_lat, direction, finish_args=None, aux=None):
    bsz, tall, _ = proj.shape
    if mode == "gla":
        heads, dk, dv = GLA_HEADS, GLA_DK, GLA_DV
    else:
        heads, dk, dv = HGRN_HEADS, HGRN_DK, HGRN_DV
    kw, vw = heads * dk, heads * dv
    tb = SCAN_BLOCK
    nb, n_lat_blocks = tall // tb, n_lat // tb
    reverse = direction == 1
    finish = finish_args is not None

    def tok(col):
        return lambda b, s: (b, _scan_block(s, n_lat_blocks, nb, reverse), col)

    def const2(shape):
        return pl.BlockSpec(shape, lambda b, s: (0, 0))

    p1, p2 = params
    if mode == "gla":
        in_specs = [pl.BlockSpec((1, tb, kw), tok(cols["q"])),
                    pl.BlockSpec((1, tb, kw), tok(cols["k"])),
                    pl.BlockSpec((1, tb, vw), tok(cols["v"])),
                    pl.BlockSpec((1, tb, LANE), tok(0)),
                    const2(p1.shape), const2(p2.shape)]
        args = [proj, proj, proj, aux, p1, p2]
    else:
        in_specs = [pl.BlockSpec((1, tb, kw), tok(cols["q"])),
                    pl.BlockSpec((1, tb, vw), tok(cols["v"])),
                    pl.BlockSpec((1, tb, kw), tok(cols["aux"] + direction)),
                    const2(p1.shape), const2(p2.shape)]
        args = [proj, proj, proj, p1, p2]
    if finish:
        o_f, norm_w = finish_args
        in_specs += [pl.BlockSpec((1, tb, vw), tok(0)),
                     pl.BlockSpec((1, tb, vw), tok(cols["gate"])),
                     const2(norm_w.shape)]
        args += [o_f, proj, norm_w]
    return pl.pallas_call(
        functools.partial(_lin_kernel, mode=mode, direction=direction, finish=finish,
                          heads=heads, dk=dk, dv=dv),
        grid=(bsz, nb),
        in_specs=in_specs,
        out_specs=pl.BlockSpec((1, tb, vw), tok(0)),
        out_shape=jax.ShapeDtypeStruct((bsz, tall, vw), BF16 if finish else F32),
        scratch_shapes=[pltpu.VMEM((heads, dv, LANE), F32)],
        compiler_params=_cparams(2),
        name=f"{mode}_scan_{'bwd' if reverse else 'fwd'}",
    )(*args)


def _s5_kernel(uf_ref, ub_ref, bmat_ref, lre_ref, lim_ref, cmat_ref, of_ref, ob_ref, h_ref, ut_ref, yt_ref, st_ref,
               *, bsz):
    steps = S5_CHUNK
    n_slabs = bmat_ref.shape[1]
    sw = bmat_ref.shape[3] // 2
    width = n_slabs * LANE
    re_cols = [slice(2 * s * sw, (2 * s + 1) * sw) for s in range(n_slabs)]
    im_cols = [slice((2 * s + 1) * sw, (2 * s + 2) * sw) for s in range(n_slabs)]
    both = [slice(2 * s * sw, (2 * s + 2) * sw) for s in range(n_slabs)]

    @pl.when(pl.program_id(0) == 0)
    def _():
        st_ref[...] = jnp.zeros_like(st_ref)

    def inputs(d, u_ref):
        for b in range(bsz):
            for s in range(n_slabs):
                ut_ref[d, s, pl.ds(b, steps, stride=bsz), :] = (
                    u_ref[:, b * width + s * LANE:b * width + (s + 1) * LANE])
        for s in range(n_slabs):
            h_ref[d, :, both[s]] = _dot(ut_ref[d, s].astype(BF16), bmat_ref[d, s])

    def scan(d):
        lam_re = [jnp.broadcast_to(lre_ref[d:d + 1, s * sw:(s + 1) * sw], (bsz, sw)) for s in range(n_slabs)]
        lam_im = [jnp.broadcast_to(lim_ref[d:d + 1, s * sw:(s + 1) * sw], (bsz, sw)) for s in range(n_slabs)]
        hr = [st_ref[d, :, re_cols[s]] for s in range(n_slabs)]
        hi = [st_ref[d, :, im_cols[s]] for s in range(n_slabs)]
        for tt in range(steps):
            t = tt if d == 0 else steps - 1 - tt
            rows = slice(t * bsz, (t + 1) * bsz)
            for s in range(n_slabs):
                nr = lam_re[s] * hr[s] - lam_im[s] * hi[s] + h_ref[d, rows, re_cols[s]]
                ni = lam_re[s] * hi[s] + lam_im[s] * hr[s] + h_ref[d, rows, im_cols[s]]
                h_ref[d, rows, re_cols[s]] = nr
                h_ref[d, rows, im_cols[s]] = ni
                hr[s], hi[s] = nr, ni
        for s in range(n_slabs):
            st_ref[d, :, re_cols[s]] = hr[s]
            st_ref[d, :, im_cols[s]] = hi[s]

    def outputs(d):
        for s in range(n_slabs):
            yt_ref[d, s] = _dot(h_ref[d, :, both[s]].astype(BF16), cmat_ref[s])

    def emit(d, o_ref):
        for b in range(bsz):
            for s in range(n_slabs):
                o_ref[:, b * width + s * LANE:b * width + (s + 1) * LANE] = (
                    yt_ref[d, s, pl.ds(b, steps, stride=bsz), :])

    inputs(0, uf_ref)
    inputs(1, ub_ref)
    scan(0)
    outputs(0)
    scan(1)
    outputs(1)
    emit(0, of_ref)
    emit(1, ob_ref)


def _s5_scan(u_t, bmat, lam_re, lam_im, cmat, bsz, n_lat):
    tall, bw = u_t.shape
    width = bw // bsz
    nch, n_lat_chunks = tall // S5_CHUNK, n_lat // S5_CHUNK
    n_state = lam_re.shape[-1]
    rows = S5_CHUNK * bsz
    fwd = lambda s: (_scan_block(s, n_lat_chunks, nch, False), 0)
    bwd = lambda s: (_scan_block(s, n_lat_chunks, nch, True), 0)
    whole = lambda a: pl.BlockSpec(a.shape, lambda s: (0,) * a.ndim)
    lam_re, lam_im = lam_re.reshape(2, n_state), lam_im.reshape(2, n_state)
    return pl.pallas_call(
        functools.partial(_s5_kernel, bsz=bsz),
        grid=(nch,),
        in_specs=[pl.BlockSpec((S5_CHUNK, bw), fwd), pl.BlockSpec((S5_CHUNK, bw), bwd),
                  whole(bmat), whole(lam_re), whole(lam_im), whole(cmat)],
        out_specs=[pl.BlockSpec((S5_CHUNK, bw), fwd), pl.BlockSpec((S5_CHUNK, bw), bwd)],
        out_shape=[jax.ShapeDtypeStruct((tall, bw), F32)] * 2,
        scratch_shapes=[pltpu.VMEM((2, rows, 2 * n_state), F32),
                        pltpu.VMEM((2, width // LANE, rows, LANE), F32),
                        pltpu.VMEM((2, width // LANE, rows, LANE), F32),
                        pltpu.VMEM((2, bsz, 2 * n_state), F32)],
        compiler_params=_cparams(1),
        name="s5_scan",
    )(u_t, u_t, bmat, lam_re, lam_im, cmat)


def _out0_kernel(x_ref, a_ref, b_ref, wa_ref, wb_ref, gl_ref, gc_ref, o_ref, *, tm, n_lat):
    i = pl.program_id(1)
    o = _dot(a_ref[0], wa_ref[...]) + _dot(b_ref[0], wb_ref[...])
    o_ref[0] = x_ref[0] + _row_select(i, tm, n_lat, gc_ref[...], gl_ref[0]) * o


def _out_proj0(x_all, mix_a, mix_b, w_a, w_b, mod_l, mod_c, n_lat, n_rows, tm):
    bsz, _, d = x_all.shape
    tall = n_rows
    return pl.pallas_call(
        functools.partial(_out0_kernel, tm=tm, n_lat=n_lat),
        grid=(bsz, tall // tm),
        in_specs=[pl.BlockSpec((1, tm, d), lambda b, i: (b, i, 0)),
                  pl.BlockSpec((1, tm, mix_a.shape[2]), lambda b, i: (b, i, 0)),
                  pl.BlockSpec((1, tm, mix_b.shape[2]), lambda b, i: (b, i, 0)),
                  pl.BlockSpec(w_a.shape, lambda b, i: (0, 0)),
                  pl.BlockSpec(w_b.shape, lambda b, i: (0, 0))] + _mod_specs(d, (2,), 2),
        out_specs=pl.BlockSpec((1, tm, d), lambda b, i: (b, i, 0)),
        out_shape=jax.ShapeDtypeStruct((bsz, tall, d), F32),
        compiler_params=_cparams(2),
        name="out_proj_even",
    )(x_all, mix_a, mix_b, w_a, w_b, mod_l, mod_c)


def _gelu_tanh(x):
    return 0.5 * x * (1.0 + jnp.tanh(math.sqrt(2.0 / math.pi) * (x + 0.044715 * (x * x * x))))


def _out1_kernel(x_ref, a_ref, yf_ref, yb_ref, u_ref, dsk_ref, gw_ref, gb_ref, wa_ref, wb_ref,
                 gl_ref, gc_ref, o_ref, *, tm, n_lat):
    i = pl.program_id(1)
    y = _gelu_tanh(yf_ref[...] + yb_ref[...] + dsk_ref[...] * u_ref[...])
    glu = _dot(y.astype(BF16), gw_ref[...]) + gb_ref[...]
    y = y * _sigmoid(glu)
    o = _dot(a_ref[0], wa_ref[...]) + _dot(y.astype(BF16), wb_ref[...])
    o_ref[0] = x_ref[0] + _row_select(i, tm, n_lat, gc_ref[...], gl_ref[0]) * o


def _out_proj1(x_all, mix_a, y_dirs, u_t, d_skip, glu_w, glu_b, w_a, w_b, mod_l, mod_c, n_lat, n_rows, tm):
    bsz, _, d = x_all.shape
    tall = n_rows
    width = d_skip.shape[1]
    (y_f, y_b), u2 = y_dirs, u_t
    return pl.pallas_call(
        functools.partial(_out1_kernel, tm=tm, n_lat=n_lat),
        grid=(bsz, tall // tm),
        in_specs=[pl.BlockSpec((1, tm, d), lambda b, i: (b, i, 0)),
                  pl.BlockSpec((1, tm, mix_a.shape[2]), lambda b, i: (b, i, 0)),
                  pl.BlockSpec((tm, width), lambda b, i: (i, b)),
                  pl.BlockSpec((tm, width), lambda b, i: (i, b)),
                  pl.BlockSpec((tm, width), lambda b, i: (i, b)),
                  pl.BlockSpec((1, width), lambda b, i: (0, 0)),
                  pl.BlockSpec(glu_w.shape, lambda b, i: (0, 0)),
                  pl.BlockSpec((1, width), lambda b, i: (0, 0)),
                  pl.BlockSpec(w_a.shape, lambda b, i: (0, 0)),
                  pl.BlockSpec(w_b.shape, lambda b, i: (0, 0))] + _mod_specs(d, (2,), 2),
        out_specs=pl.BlockSpec((1, tm, d), lambda b, i: (b, i, 0)),
        out_shape=jax.ShapeDtypeStruct((bsz, tall, d), F32),
        compiler_params=_cparams(2),
        name="out_proj_odd",
    )(x_all, mix_a, y_f, y_b, u2, d_skip, glu_w, glu_b, w_a, w_b, mod_l, mod_c)


def _mlp_kernel(*refs, tm, n_lat, final):
    if final:
        (x_ref, nw_ref, shl_ref, shc_ref, scl_ref, scc_ref, gl_ref, gc_ref, w1_ref, w2_ref, fw_ref,
         o_ref, h_scr, acc_scr) = refs
    else:
        (x_ref, nw_ref, shl_ref, shc_ref, scl_ref, scc_ref, gl_ref, gc_ref, w1_ref, w2_ref,
         o_ref, h_scr, acc_scr) = refs
    i = pl.program_id(1)
    j = pl.program_id(2)

    @pl.when(j == 0)
    def _():
        _store_norm_modulated(h_scr, x_ref[0], nw_ref[...], shl_ref[0], shc_ref[...], scl_ref[0], scc_ref[...],
                              i, tm, n_lat)
        acc_scr[...] = jnp.zeros_like(acc_scr)

    a = jnp.maximum(_dot(h_scr[...], w1_ref[...]), 0.0)
    acc_scr[...] += _dot((a * a).astype(BF16), w2_ref[...])

    @pl.when(j == pl.num_programs(2) - 1)
    def _():
        out = x_ref[0] + _row_select(i, tm, n_lat, gc_ref[...], gl_ref[0]) * acc_scr[...]
        if final:
            out = _rms(out, fw_ref[...])
        o_ref[0] = out


def _mlp(x_all, norm_w, mod_l, mod_c, w1, w2, final_w, n_lat, tm, tf):
    bsz, tall, d = x_all.shape
    ff = w1.shape[1]
    final = final_w is not None
    in_specs = [pl.BlockSpec((1, tm, d), lambda b, i, j: (b, i, 0)),
                pl.BlockSpec((1, d), lambda b, i, j: (0, 0))]
    in_specs += _mod_specs(d, (3, 4, 5), 3)
    in_specs += [pl.BlockSpec((d, tf), lambda b, i, j: (0, j)),
                 pl.BlockSpec((tf, d), lambda b, i, j: (j, 0))]
    args = [x_all, norm_w.reshape(1, d)] + [mod_l, mod_c] * 3 + [w1, w2]
    if final:
        in_specs.append(pl.BlockSpec((1, d), lambda b, i, j: (0, 0)))
        args.append(final_w.reshape(1, d))
    return pl.pallas_call(
        functools.partial(_mlp_kernel, tm=tm, n_lat=n_lat, final=final),
        grid=(bsz, tall // tm, ff // tf),
        in_specs=in_specs,
        out_specs=pl.BlockSpec((1, tm, d), lambda b, i, j: (b, i, 0)),
        out_shape=jax.ShapeDtypeStruct((bsz, tall, d), F32),
        scratch_shapes=[pltpu.VMEM((tm, d), BF16), pltpu.VMEM((tm, d), F32)],
        compiler_params=_cparams(3),
        name="sq_relu_mlp",
    )(*args)


def _even_in_weight(w_in):
    sizes = (SSD_INNER, SSD_INNER + 2 * SSD_BC, 2 * SSD_HEADS, GLA_KEY, GLA_KEY, GLA_VAL,
             2 * GLA_GATE_RANK, GLA_VAL)
    offs = [0]
    for s in sizes:
        offs.append(offs[-1] + s)
    z, xbc, dt, q, k, v, lr, r = (w_in[:, offs[n]:offs[n + 1]] for n in range(8))
    pad = jnp.zeros((w_in.shape[0], LANE - dt.shape[1] - lr.shape[1]), w_in.dtype)
    main = jnp.concatenate([xbc, z, v, r, q, k], axis=1).astype(BF16)
    aux = jnp.concatenate([dt, lr, pad], axis=1).astype(BF16)
    return main, aux


def _s5_params(a_re, a_im, log_dt, b_re, b_im, c_re, c_im):
    delta = jnp.exp(log_dt.astype(F32))[..., None]
    mag = jnp.exp(a_re * delta)
    lbar_re, lbar_im = mag * jnp.cos(a_im * delta), mag * jnp.sin(a_im * delta)
    den = a_re * a_re + a_im * a_im
    zr = ((lbar_re - 1.0) * a_re + lbar_im * a_im) / den
    zi = (lbar_im * a_re - (lbar_re - 1.0) * a_im) / den
    bb_re = zr[..., None] * b_re - zi[..., None] * b_im
    bb_im = zr[..., None] * b_im + zi[..., None] * b_re
    n_slabs = S5_GROUPS // S5_SLAB
    eye = jnp.eye(S5_SLAB, dtype=F32)
    sw = S5_SLAB * S5_STATE

    def block_in(bb):
        bb = bb.reshape(2, n_slabs, S5_SLAB, S5_STATE, S5_GROUP)
        return jnp.einsum("dsgpc,gh->dsgchp", bb, eye).reshape(2, n_slabs, LANE, sw)

    def block_out(cc):
        cc = cc.reshape(n_slabs, S5_SLAB, S5_GROUP, S5_STATE)
        return jnp.einsum("sgcp,gh->sgphc", cc, eye).reshape(n_slabs, sw, LANE)

    bmat = jnp.concatenate([block_in(bb_re), block_in(bb_im)], axis=3).astype(BF16)
    cmat = jnp.concatenate([block_out(c_re), -block_out(c_im)], axis=1).astype(BF16)
    return (bmat, lbar_re.reshape(2, 1, S5_NSTATE), lbar_im.reshape(2, 1, S5_NSTATE), cmat)


def _layer_even(x_all, mod_l, mod_c, n_lat, tm, n_rows_out, tm_out, norm1_w, w_in, conv_w, conv_b, dt_bias,
                a_log, d_skip, ssd_norm_w, gate_w, gate_b, gla_norm_w, w_out):
    w, w_aux = _even_in_weight(w_in)
    n = w.shape[1]
    proj, aux = _project(x_all, norm1_w, mod_l, mod_c, w, w_aux, n_lat, tm, _largest_divisor(n, 2 * LANE, 1024),
                         extra_token_major=False)
    n_xbc = SSD_INNER + 2 * SSD_BC
    c_z, c_v, c_r = n_xbc // SSD_INNER, n_xbc // GLA_VAL + 1, n_xbc // GLA_VAL + 2
    c_q = (n_xbc + 3 * SSD_INNER) // GLA_KEY
    xbc = _conv_silu(proj, conv_w, conv_b, n_lat, n_xbc)

    neg_a = -jnp.exp(a_log.astype(F32))
    dt_bias = dt_bias.astype(F32)
    d_wide = jnp.repeat(d_skip.astype(F32), SSD_HEADDIM).reshape(1, SSD_INNER)
    y_f = _ssd_scan(xbc, proj, aux, dt_bias, neg_a, c_z, n_lat, 0)
    y_mix = _ssd_scan(xbc, proj, aux, dt_bias, neg_a, c_z, n_lat, 1,
                      (y_f, d_wide, ssd_norm_w.reshape(1, SSD_INNER)))

    cols = {"q": c_q, "k": c_q + 1, "v": c_v, "gate": c_r}
    gparams = [(gate_w[d].astype(BF16), gate_b[d].reshape(1, GLA_KEY).astype(F32)) for d in range(2)]
    o_f = _lin_scan("gla", proj, cols, gparams[0], n_lat, 0, aux=aux)
    o_mix = _lin_scan("gla", proj, cols, gparams[1], n_lat, 1, (o_f, gla_norm_w.reshape(1, GLA_DV)), aux=aux)

    w_out = w_out.astype(BF16)
    return _out_proj0(x_all, y_mix, o_mix, w_out[:SSD_INNER], w_out[SSD_INNER:], mod_l, mod_c, n_lat,
                      n_rows_out, tm_out)


def _layer_odd(x_all, mod_l, mod_c, n_lat, tm, n_rows_out, tm_out, norm1_w, w_in, lb, hgrn_norm_w, a_re, a_im,
               log_dt, b_re, b_im, c_re, c_im, d_skip, glu_w, glu_b, w_out):
    bsz = x_all.shape[0]
    n_main = 5 * HGRN_WIDTH
    w_main = w_in[:, :n_main].astype(BF16)
    w_u = w_in[:, n_main:].astype(BF16)
    proj, u_t = _project(x_all, norm1_w, mod_l, mod_c, w_main, w_u, n_lat, tm, HGRN_WIDTH)

    lb = lb.astype(F32).reshape(2, 1, HGRN_WIDTH)
    cols = {"q": 0, "v": 1, "aux": 2, "gate": 4}
    o_f = _lin_scan("hgrn", proj, cols, (lb[0], 1.0 - lb[0]), n_lat, 0)
    o_mix = _lin_scan("hgrn", proj, cols, (lb[1], 1.0 - lb[1]), n_lat, 1,
                      (o_f, hgrn_norm_w.reshape(1, HGRN_DV)))

    bmat, lam_re, lam_im, cmat = _s5_params(a_re.astype(F32), a_im.astype(F32), log_dt, b_re.astype(F32),
                                            b_im.astype(F32), c_re.astype(F32), c_im.astype(F32))
    y_dirs = _s5_scan(u_t, bmat, lam_re, lam_im, cmat, bsz, n_lat)

    w_out = w_out.astype(BF16)
    return _out_proj1(x_all, o_mix, y_dirs, u_t, d_skip.astype(F32).reshape(1, S5_WIDTH),
                      glu_w.astype(BF16), glu_b.astype(F32).reshape(1, S5_WIDTH),
                      w_out[:HGRN_WIDTH], w_out[HGRN_WIDTH:], mod_l, mod_c, n_lat, n_rows_out, tm_out)


def kernel(x, c, ctx, c_ctx, ada_w, ada_b, norm1_w, norm2_w, ssd_gla_w_in, ssd_conv_w, ssd_conv_b, ssd_dt_bias, ssd_a_log, ssd_d, ssd_norm_w, gla_gate_w, gla_gate_b, gla_norm_w, ssd_gla_w_out, hgrn_s5_w_in, hgrn_lb_logits, hgrn_norm_w, s5_a_re, s5_a_im, s5_log_dt, s5_b_re, s5_b_im, s5_c_re, s5_c_im, s5_d, s5_glu_w, s5_glu_b, hgrn_s5_w_out, mlp_w1, mlp_w2, final_norm_w):
    bsz, n_lat, d = x.shape
    ctx_len = ctx.shape[1]
    depth = ada_w.shape[0]
    tall = n_lat + ctx_len
    assert bsz % 8 == 0 and ctx_len % SCAN_BLOCK == 0 and n_lat % SCAN_BLOCK == 0 and n_lat % GRID_W == 0
    tf = 1024

    n_rows = -(-(bsz + 1) // 8) * 8
    cvec = jnp.concatenate([c, c_ctx[None, :], jnp.zeros((n_rows - bsz - 1, d), c.dtype)], axis=0)
    mod = _modulation(cvec.astype(F32), ada_w, ada_b)

    p_lb = jax.nn.softmax(hgrn_lb_logits.astype(F32), axis=0)
    lb_all = jnp.cumsum(p_lb, axis=0) - p_lb[0]

    x_all = jnp.concatenate([x, ctx], axis=1).astype(F32)
    tm_all = _largest_divisor(tall, 16, 1056)
    for layer in range(depth):
        j = layer // 2
        last = layer == depth - 1
        n_rows = n_lat if last else tall
        tm_out = _largest_divisor(n_rows, 16, 1056)
        mod_l = mod[layer, :bsz].reshape(bsz, 1, N_MOD * d)
        mod_c = mod[layer, bsz:bsz + 1]
        if layer % 2 == 0:
            x_all = _layer_even(x_all, mod_l, mod_c, n_lat, tm_all, n_rows, tm_out, norm1_w[layer],
                                ssd_gla_w_in[j], ssd_conv_w[j], ssd_conv_b[j], ssd_dt_bias[j], ssd_a_log[j],
                                ssd_d[j], ssd_norm_w[j], gla_gate_w[j], gla_gate_b[j], gla_norm_w[j],
                                ssd_gla_w_out[j])
        else:
            x_all = _layer_odd(x_all, mod_l, mod_c, n_lat, tm_all, n_rows, tm_out, norm1_w[layer],
                               hgrn_s5_w_in[j], lb_all[layer], hgrn_norm_w[j], s5_a_re[j], s5_a_im[j],
                               s5_log_dt[j], s5_b_re[j], s5_b_im[j], s5_c_re[j], s5_c_im[j], s5_d[j],
                               s5_glu_w[j], s5_glu_b[j], hgrn_s5_w_out[j])
        x_all = _mlp(x_all, norm2_w[layer], mod_l, mod_c, mlp_w1[layer].astype(BF16),
                     mlp_w2[layer].astype(BF16), final_norm_w if last else None, n_lat, tm_out, tf)
    return x_all.astype(x.dtype)
```

```python
import functools
import math

import jax
import jax.numpy as jnp
from jax import lax
from jax.experimental import pallas as pl
from jax.experimental.pallas import tpu as pltpu

F32 = jnp.float32
BF16 = jnp.bfloat16

GRID_W = 64
NORM_EPS = 1e-6
N_MOD = 6
SSD_HEADDIM = 64
SSD_HEADS = 16
SSD_GROUPS = 4
SSD_STATE = 128
SSD_CHUNK = 128
GLA_HEADS = 8
GLA_DK = 64
GLA_DV = 128
GLA_GATE_RANK = 16
GLA_GATE_NORM = 16.0
HGRN_HEADS = 8
HGRN_DK = 128
HGRN_DV = 128
S5_GROUP = 16
S5_GROUPS = 24
S5_STATE = 64
LIN_CHUNK = 64

SSD_INNER = SSD_HEADS * SSD_HEADDIM
SSD_BC = SSD_GROUPS * SSD_STATE
GLA_KEY = GLA_HEADS * GLA_DK
GLA_VAL = GLA_HEADS * GLA_DV
HGRN_WIDTH = HGRN_HEADS * HGRN_DV
S5_WIDTH = S5_GROUPS * S5_GROUP
S5_NSTATE = S5_GROUPS * S5_STATE

VMEM_LIMIT_BYTES = 56 * 1024 * 1024
LANE = 128
PROJ_TN = 2560
SCAN_BLOCK = 256
S5_CHUNK = 128
S5_SLAB = LANE // S5_GROUP


def _cparams(n_axes):
    return pltpu.CompilerParams(dimension_semantics=("arbitrary",) * n_axes,
                                vmem_limit_bytes=VMEM_LIMIT_BYTES)


def _largest_divisor(n, multiple, cap):
    best = None
    for d in range(multiple, min(n, cap) + 1, multiple):
        if n % d == 0:
            best = d
    assert best is not None, (n, multiple, cap)
    return best


_NEG_LOG2E = -1.4426950408889634


def _sigmoid(x):
    return 1.0 / (1.0 + jnp.exp2(x * _NEG_LOG2E))


def _silu(x):
    return x * _sigmoid(x)


def _softplus(x):
    return jnp.maximum(x, 0.0) + jnp.log1p(jnp.exp(-jnp.abs(x)))


def _log_sigmoid(x):
    return -_softplus(-x)


def _rms(x, w):
    return x * lax.rsqrt(jnp.mean(x * x, axis=-1, keepdims=True) + NORM_EPS) * w


def _dot(a, b, dims=(((1,), (0,)), ((), ())), precision=None):
    return lax.dot_general(a, b, dims, precision=precision, preferred_element_type=F32)


def _split3(v):
    hi = v.astype(BF16)
    r1 = v - hi.astype(F32)
    mid = r1.astype(BF16)
    lo = (r1 - mid.astype(F32)).astype(BF16)
    return hi, mid, lo


def _tri3(mask):
    tri = mask.astype(BF16)
    return jnp.concatenate([tri, tri, tri], axis=1)


def _cumsum_rows(tri2, v):
    hi = v.astype(BF16)
    lo = (v - hi.astype(F32)).astype(BF16)
    return _dot(tri2, jnp.concatenate([hi, lo], axis=0))


_NT = (((1,), (1,)), ((), ()))
_TN = (((0,), (0,)), ((), ()))
_TT = (((0,), (1,)), ((), ()))


def _mod_kernel(c_ref, w_ref, b_ref, o_ref):
    a = _silu(c_ref[...]).astype(BF16)
    o_ref[0] = _dot(a, w_ref[0].astype(BF16)) + b_ref[0]


def _modulation(cvec, ada_w, ada_b):
    depth, d, n = ada_w.shape
    rows = cvec.shape[0]
    tn = _largest_divisor(n, LANE, 1024)
    return pl.pallas_call(
        _mod_kernel,
        grid=(depth, n // tn),
        in_specs=[pl.BlockSpec((rows, d), lambda l, j: (0, 0)),
                  pl.BlockSpec((1, d, tn), lambda l, j: (l, 0, j)),
                  pl.BlockSpec((1, 1, tn), lambda l, j: (l, 0, j))],
        out_specs=pl.BlockSpec((1, rows, tn), lambda l, j: (l, 0, j)),
        out_shape=jax.ShapeDtypeStruct((depth, rows, n), F32),
        compiler_params=_cparams(2),
        name="adaln_mod",
    )(cvec, ada_w, ada_b.reshape(depth, 1, n))


def _row_select(i, tm, n_lat, ctx_val, lat_val):
    row = i * tm + lax.broadcasted_iota(jnp.int32, (tm, 1), 0)
    return jnp.where(row >= n_lat, ctx_val, lat_val)


def _store_norm_modulated(h_ref, x, nw, shift_l, shift_c, scale_l, scale_c, i, tm, n_lat):
    xn = x * lax.rsqrt(jnp.mean(x * x, axis=-1, keepdims=True) + NORM_EPS)
    gain = _row_select(i, tm, n_lat, nw * (1.0 + scale_c), nw * (1.0 + scale_l))
    h_ref[...] = (xn * gain + _row_select(i, tm, n_lat, shift_c, shift_l)).astype(h_ref.dtype)


def _mod_specs(d, cols, n_grid_axes):
    specs = []
    for k in cols:
        if n_grid_axes == 2:
            specs.append(pl.BlockSpec((1, 1, d), lambda b, i, k=k: (b, 0, k)))
            specs.append(pl.BlockSpec((1, d), lambda b, i, k=k: (0, k)))
        else:
            specs.append(pl.BlockSpec((1, 1, d), lambda b, i, j, k=k: (b, 0, k)))
            specs.append(pl.BlockSpec((1, d), lambda b, i, j, k=k: (0, k)))
    return specs


def _proj_kernel(*refs, tm, n_lat, has_extra):
    if has_extra:
        (x_ref, nw_ref, shl_ref, shc_ref, scl_ref, scc_ref, w_ref, wx_ref, o_ref, ox_ref, h_scr) = refs
    else:
        (x_ref, nw_ref, shl_ref, shc_ref, scl_ref, scc_ref, w_ref, o_ref, h_scr) = refs
    i = pl.program_id(1)
    j = pl.program_id(2)

    @pl.when(j == 0)
    def _():
        _store_norm_modulated(h_scr, x_ref[0], nw_ref[...], shl_ref[0], shc_ref[...], scl_ref[0], scc_ref[...],
                              i, tm, n_lat)
        if has_extra:
            ox_ref[...] = _dot(h_scr[...], wx_ref[...]).reshape(ox_ref.shape)

    o_ref[0] = _dot(h_scr[...], w_ref[...]).astype(o_ref.dtype)


def _project(x_all, norm_w, mod_l, mod_c, w, w_extra, n_lat, tm, tn, extra_token_major=True):
    bsz, tall, d = x_all.shape
    n = w.shape[1]
    has_extra = w_extra is not None
    in_specs = [pl.BlockSpec((1, tm, d), lambda b, i, j: (b, i, 0)),
                pl.BlockSpec((1, d), lambda b, i, j: (0, 0))]
    in_specs += _mod_specs(d, (0, 1), 3)
    in_specs.append(pl.BlockSpec((d, tn), lambda b, i, j: (0, j)))
    args = [x_all, norm_w.reshape(1, d), mod_l, mod_c, mod_l, mod_c, w]
    out_specs = [pl.BlockSpec((1, tm, tn), lambda b, i, j: (b, i, j))]
    out_shape = [jax.ShapeDtypeStruct((bsz, tall, n), BF16)]
    if has_extra:
        nx = w_extra.shape[1]
        in_specs.append(pl.BlockSpec((d, nx), lambda b, i, j: (0, 0)))
        args.append(w_extra)
        if extra_token_major:
            out_specs.append(pl.BlockSpec((tm, nx), lambda b, i, j: (i, b)))
            out_shape.append(jax.ShapeDtypeStruct((tall, bsz * nx), F32))
        else:
            out_specs.append(pl.BlockSpec((1, tm, nx), lambda b, i, j: (b, i, 0)))
            out_shape.append(jax.ShapeDtypeStruct((bsz, tall, nx), F32))
    out = pl.pallas_call(
        functools.partial(_proj_kernel, tm=tm, n_lat=n_lat, has_extra=has_extra),
        grid=(bsz, tall // tm, n // tn),
        in_specs=in_specs,
        out_specs=out_specs,
        out_shape=out_shape,
        scratch_shapes=[pltpu.VMEM((tm, d), BF16)],
        compiler_params=_cparams(3),
        name="norm_mod_proj",
    )(*args)
    return out if has_extra else out[0]


def _conv_kernel(main_ref, prev_ref, next_ref, w_ref, b_ref, o_ref, *, tt, n_lat, tall):
    i = pl.program_id(1)
    te = tt + 2 * GRID_W
    p = i * tt - GRID_W + lax.broadcasted_iota(jnp.int32, (te, 1), 0)
    is_ctx = p >= n_lat
    col = jnp.bitwise_and(p, GRID_W - 1)
    has_left = jnp.where(is_ctx, p - n_lat, col) > 0
    has_right = jnp.where(is_ctx, p - (tall - 1), col - (GRID_W - 1)) < 0
    w = w_ref[...]

    def conv(interior):
        ext = jnp.concatenate([prev_ref[0], main_ref[0], next_ref[0]], axis=0).astype(F32)
        own = slice(GRID_W, GRID_W + tt)
        if interior:
            as_left = jnp.where(has_right, ext, 0.0)
            as_right = jnp.where(has_left, ext, 0.0)
        acc = jnp.zeros((tt, w.shape[1]), F32) + b_ref[...]
        for dy in (-1, 0, 1):
            rs = slice(GRID_W + GRID_W * dy, GRID_W + GRID_W * dy + tt)
            k0 = 3 * (dy + 1)
            if interior:
                left, right = pltpu.roll(as_left[rs], 1, 0), pltpu.roll(as_right[rs], tt - 1, 0)
            else:
                left = jnp.where(has_left[own], pltpu.roll(ext[rs], 1, 0), 0.0)
                right = jnp.where(has_right[own], pltpu.roll(ext[rs], tt - 1, 0), 0.0)
            t = ext[rs] * w[k0 + 1:k0 + 2] + left * w[k0:k0 + 1] + right * w[k0 + 2:k0 + 3]
            if not interior and dy != 0:
                q = p[own]
                if dy == -1:
                    ok = jnp.where(q >= n_lat, 0, q) >= GRID_W
                else:
                    ok = jnp.where(q >= n_lat, n_lat, q) < n_lat - GRID_W
                t = jnp.where(ok, t, 0.0)
            acc = acc + t
        o_ref[0] = _silu(acc).astype(o_ref.dtype)

    interior = jnp.logical_and(i * tt >= GRID_W, (i + 1) * tt <= n_lat - GRID_W)
    pl.when(interior)(lambda: conv(True))
    pl.when(jnp.logical_not(interior))(lambda: conv(False))


def _conv_silu(proj, conv_w, conv_b, n_lat, n_ch):
    bsz, tall, _ = proj.shape
    n_rows = tall // GRID_W
    tt = _largest_divisor(tall, GRID_W, 768)
    assert n_lat % GRID_W == 0 and n_lat // tt == (tall - 1) // tt
    r = tt // GRID_W
    tc = 512
    return pl.pallas_call(
        functools.partial(_conv_kernel, tt=tt, n_lat=n_lat, tall=tall),
        grid=(bsz, tall // tt, n_ch // tc),
        in_specs=[pl.BlockSpec((1, tt, tc), lambda b, i, c: (b, i, c)),
                  pl.BlockSpec((1, GRID_W, tc), lambda b, i, c: (b, jnp.maximum(i * r - 1, 0), c)),
                  pl.BlockSpec((1, GRID_W, tc), lambda b, i, c: (b, jnp.minimum((i + 1) * r, n_rows - 1), c)),
                  pl.BlockSpec((9, tc), lambda b, i, c: (0, c)),
                  pl.BlockSpec((1, tc), lambda b, i, c: (0, c))],
        out_specs=pl.BlockSpec((1, tt, tc), lambda b, i, c: (b, i, c)),
        out_shape=jax.ShapeDtypeStruct((bsz, tall, n_ch), BF16),
        compiler_params=_cparams(3),
        name="dwconv_silu",
    )(proj, proj, proj, conv_w.reshape(9, n_ch), conv_b.reshape(1, n_ch))


def _scan_block(s, n_lat_blocks, n_blocks, reverse):
    n_ctx_blocks = n_blocks - n_lat_blocks
    if not reverse:
        return jnp.where(s < n_ctx_blocks, n_lat_blocks + s, s - n_ctx_blocks)
    return n_blocks - 1 - s


def _tri_mask(c, reverse):
    ri = lax.broadcasted_iota(jnp.int32, (c, c), 0)
    ci = lax.broadcasted_iota(jnp.int32, (c, c), 1)
    return (ci >= ri) if reverse else (ci <= ri)


def _ssd_expand_matrix():
    eye = jnp.eye(SSD_HEADS, dtype=F32)
    e_head = jnp.repeat(eye, SSD_HEADDIM, axis=1)
    e_seg = jnp.repeat(eye, SSD_CHUNK, axis=1)
    zh = jnp.zeros_like(e_head)
    zs = jnp.zeros_like(e_seg)
    blk = jnp.concatenate([
        jnp.concatenate([e_head, zh, zh, zs], axis=1),
        jnp.concatenate([zh, e_head, zh, zs], axis=1),
        jnp.concatenate([zh, zh, e_head, zs], axis=1),
        jnp.concatenate([zh, zh, zh, e_seg], axis=1)], axis=0)
    return jnp.concatenate([blk, blk, blk], axis=0).astype(BF16)


def _ssd_kernel(*refs, direction, finish):
    if finish:
        (x_ref, bm_ref, cm_ref, dtlr_ref, dtb_ref, nega_ref, exp_ref, z_ref, yf_ref, dsk_ref, nw_ref,
         o_ref, st_ref) = refs
    else:
        (x_ref, bm_ref, cm_ref, dtlr_ref, dtb_ref, nega_ref, exp_ref, o_ref, st_ref) = refs
    reverse = direction == 1
    c = SSD_CHUNK
    p = SSD_HEADDIM
    gw = SSD_INNER // SSD_GROUPS
    hpg = SSD_HEADS // SSD_GROUPS

    @pl.when(pl.program_id(1) == 0)
    def _():
        st_ref[...] = jnp.zeros_like(st_ref)

    mask = _tri_mask(c, reverse)
    tri3 = _tri3(mask)
    last = 0 if reverse else c - 1
    n_chunks = x_ref.shape[1] // c
    order = range(n_chunks - 1, -1, -1) if reverse else range(n_chunks)
    groups = range(SSD_GROUPS)
    g_cols = [slice(g * gw, (g + 1) * gw) for g in groups]
    n_cols = [slice(g * SSD_STATE, (g + 1) * SSD_STATE) for g in groups]
    cb = {(ck, g): _dot(cm_ref[0, ck * c:(ck + 1) * c, n_cols[g]], bm_ref[0, ck * c:(ck + 1) * c, n_cols[g]], _NT)
          for ck in order for g in groups}
    prep = {}
    for ck in order:
        rs = slice(ck * c, (ck + 1) * c)
        x = x_ref[0, rs, :].astype(F32)
        dt_raw = dtlr_ref[0, rs, :][:, direction * SSD_HEADS:(direction + 1) * SSD_HEADS].astype(F32)
        dt = _softplus(dt_raw + dtb_ref[...])
        la3 = jnp.concatenate(_split3(dt * nega_ref[...]), axis=0)
        acum = _dot(tri3, la3)
        acum_t = _dot(la3, tri3, _TT)
        a_last = acum[last:last + 1]
        narrow = jnp.concatenate([dt, dt * jnp.exp(a_last - acum), jnp.exp(acum), acum], axis=1)
        wide = _dot(jnp.concatenate(_split3(narrow), axis=1), exp_ref[...])
        ea_w = wide[:, 2 * SSD_INNER:3 * SSD_INNER]
        prep[ck] = dict(
            x=x, bm=bm_ref[0, rs, :], cm=cm_ref[0, rs, :], wide=wide, acum_t=acum_t, ea_w=ea_w,
            xdt=(x * wide[:, :SSD_INNER]).astype(BF16),
            xw=(x * wide[:, SSD_INNER:2 * SSD_INNER]).astype(BF16),
            e_last=ea_w[last:last + 1])
    kv = {(ck, g): _dot(prep[ck]["bm"][:, n_cols[g]], prep[ck]["xw"][:, g_cols[g]], _TN)
          for ck in order for g in groups}
    scores, st_used = {}, {}
    for g in groups:
        st = st_ref[:, g_cols[g]]
        for ck in order:
            st_used[ck, g] = st.astype(BF16)
            st = st * prep[ck]["e_last"][:, g_cols[g]] + kv[ck, g]
        st_ref[:, g_cols[g]] = st
    for ck in order:
        for h in range(SSD_HEADS):
            a_i = prep[ck]["wide"][:, 3 * SSD_INNER + h * c:3 * SSD_INNER + (h + 1) * c]
            decay = jnp.exp(jnp.where(mask, a_i - prep[ck]["acum_t"][h:h + 1, :], -jnp.inf))
            scores[ck, h] = (cb[ck, h // hpg] * decay).astype(BF16)
    for ck in order:
        rs = slice(ck * c, (ck + 1) * c)
        x = prep[ck]["x"]
        y_groups = []
        for g in groups:
            ys = [_dot(scores[ck, h], prep[ck]["xdt"][:, h * p:(h + 1) * p])
                  for h in range(g * hpg, (g + 1) * hpg)]
            y_state = _dot(prep[ck]["cm"][:, n_cols[g]], st_used[ck, g]) * prep[ck]["ea_w"][:, g_cols[g]]
            y_groups.append(jnp.concatenate(ys, axis=1) + y_state)
        y = jnp.concatenate(y_groups, axis=1)
        if finish:
            z = z_ref[0, rs, :].astype(F32)
            y = (y + yf_ref[0, rs, :] + dsk_ref[...] * x) * _silu(z)
            outs = []
            for g in range(SSD_GROUPS):
                sl = slice(g * gw, (g + 1) * gw)
                outs.append(_rms(y[:, sl], nw_ref[:, sl]))
            o_ref[0, rs, :] = jnp.concatenate(outs, axis=1).astype(o_ref.dtype)
        else:
            o_ref[0, rs, :] = y


def _ssd_scan(xbc, proj, aux, dt_bias, neg_a, z_col, n_lat, direction, finish_args=None):
    bsz, tall, _ = xbc.shape
    tb = SCAN_BLOCK
    nb, n_lat_blocks = tall // tb, n_lat // tb
    reverse = direction == 1
    finish = finish_args is not None
    expand = _ssd_expand_matrix()

    def tok(col):
        return lambda b, s: (b, _scan_block(s, n_lat_blocks, nb, reverse), col)

    in_specs = [pl.BlockSpec((1, tb, SSD_INNER), tok(0)),
                pl.BlockSpec((1, tb, SSD_BC), tok(SSD_INNER // SSD_BC)),
                pl.BlockSpec((1, tb, SSD_BC), tok(SSD_INNER // SSD_BC + 1)),
                pl.BlockSpec((1, tb, LANE), tok(0)),
                pl.BlockSpec((1, SSD_HEADS), lambda b, s: (0, 0)),
                pl.BlockSpec((1, SSD_HEADS), lambda b, s: (0, 0)),
                pl.BlockSpec(expand.shape, lambda b, s: (0, 0))]
    args = [xbc, xbc, xbc, aux, dt_bias[direction:direction + 1], neg_a[direction:direction + 1], expand]
    if finish:
        y_f, d_skip_wide, norm_w = finish_args
        in_specs += [pl.BlockSpec((1, tb, SSD_INNER), tok(z_col)),
                     pl.BlockSpec((1, tb, SSD_INNER), tok(0)),
                     pl.BlockSpec((1, SSD_INNER), lambda b, s: (0, 0)),
                     pl.BlockSpec((1, SSD_INNER), lambda b, s: (0, 0))]
        args += [proj, y_f, d_skip_wide, norm_w]
    return pl.pallas_call(
        functools.partial(_ssd_kernel, direction=direction, finish=finish),
        grid=(bsz, nb),
        in_specs=in_specs,
        out_specs=pl.BlockSpec((1, tb, SSD_INNER), tok(0)),
        out_shape=jax.ShapeDtypeStruct((bsz, tall, SSD_INNER), BF16 if finish else F32),
        scratch_shapes=[pltpu.VMEM((SSD_STATE, SSD_INNER), F32)],
        compiler_params=_cparams(2),
        name="ssd_scan_bwd" if reverse else "ssd_scan_fwd",
    )(*args)


def _lin_kernel(*refs, mode, direction, finish, heads, dk, dv):
    refs = list(refs)
    if mode == "gla":
        q_ref, k_ref, v_ref, aux_ref, p1_ref, p2_ref = refs[:6]
        rest = refs[6:]
    else:
        q_ref, v_ref, aux_ref, p1_ref, p2_ref = refs[:5]
        k_ref = None
        rest = refs[5:]
    if finish:
        of_ref, gate_ref, nw_ref, o_ref, st_ref = rest
    else:
        o_ref, st_ref = rest
    reverse = direction == 1
    c = LIN_CHUNK

    @pl.when(pl.program_id(1) == 0)
    def _():
        st_ref[...] = jnp.zeros_like(st_ref)

    tb = q_ref.shape[1]
    nc = tb // c
    hpt = LANE // dk
    chunks = range(nc - 1, -1, -1) if reverse else range(nc)
    last = 0 if reverse else c - 1

    ri = lax.broadcasted_iota(jnp.int32, (tb, tb), 0)
    ci = lax.broadcasted_iota(jnp.int32, (tb, tb), 1)
    c_shift = c.bit_length() - 1
    same_chunk = jnp.right_shift(ri, c_shift) == jnp.right_shift(ci, c_shift)
    if reverse:
        bd_mask = jnp.where(same_chunk, ci - ri, -1) >= 0
    else:
        bd_mask = jnp.where(same_chunk, ci - ri, 1) <= 0
    tri = _tri_mask(c, reverse).astype(BF16)
    tri2 = jnp.concatenate([tri, tri], axis=1)

    zeros = jnp.zeros((c, LANE), BF16)

    def chunk_blocks(a):
        cols = []
        for b in range(nc):
            cols.append(jnp.concatenate(
                [a[cc * c:(cc + 1) * c] if cc == b else zeros for cc in range(nc)], axis=0))
        return jnp.concatenate(cols, axis=1)

    n_tiles = heads // hpt
    span = n_tiles if mode == "gla" else 1
    gw = span * LANE

    def prep(g):
        ls = slice(g * gw, (g + 1) * gw)
        if mode == "gla":
            q = q_ref[0, :, ls].astype(F32) * (dk ** -0.5)
            k = k_ref[0, :, ls].astype(F32)
            off = 2 * SSD_HEADS + direction * GLA_GATE_RANK
            lr = aux_ref[0][:, off:off + GLA_GATE_RANK].astype(BF16)
            lg = _log_sigmoid(_dot(lr, p1_ref[:, ls]) + p2_ref[:, ls]) * (1.0 / GLA_GATE_NORM)
        else:
            q = _silu(q_ref[0, :, ls].astype(F32))
            f_raw = aux_ref[0, :, ls].astype(F32)
            e = jnp.exp2(jnp.abs(f_raw) * _NEG_LOG2E)
            r = 1.0 / (1.0 + e)
            forget = p1_ref[:, ls] + p2_ref[:, ls] * jnp.where(f_raw >= 0.0, r, e * r)
            lg = jnp.log(forget)
            k = 1.0 - forget
        gcum = jnp.concatenate([_cumsum_rows(tri2, lg[cc * c:(cc + 1) * c]) for cc in range(nc)], axis=0)
        e_last = [jnp.exp(gcum[cc * c + last:cc * c + last + 1]) for cc in range(nc)]
        e_rows = jnp.concatenate([jnp.broadcast_to(e, (c, gw)) for e in e_last], axis=0)
        e_gcum = jnp.exp(gcum)
        q_decf = q * e_gcum
        if hpt > 1:
            head_of_lane = jnp.bitwise_and(jnp.right_shift(
                lax.broadcasted_iota(jnp.int32, (1, gw), 1), dk.bit_length() - 1), hpt - 1)
            q_dec = [jnp.where(head_of_lane == r, q_decf, 0.0).astype(BF16) for r in range(hpt)]
        else:
            q_dec = [q_decf.astype(BF16)]
        k_invf = k * (1.0 / e_gcum)
        return dict(q_dec=q_dec, k_inv=k_invf.astype(BF16), k_end=(k_invf * e_rows).astype(BF16),
                    e_last=e_last)

    groups, work = {}, {}

    def tile_cols(h):
        t = (h // hpt) % span
        return slice(t * LANE, (t + 1) * LANE)

    def products(h):
        gp, ts = groups[h // hpt // span], tile_cols(h)
        qh = gp["q_dec"][h % hpt][:, ts]
        vh = v_ref[0, :, h * dv:(h + 1) * dv]
        work[h] = dict(
            qh=qh, vh=vh, scores=_dot(qh, gp["k_inv"][:, ts], _NT),
            kv_t=_dot(vh, chunk_blocks(gp["k_end"][:, ts]), _TN))

    def mask_and_chain(h):
        w = work[h]
        e_last = groups[h // hpt // span]["e_last"]
        w["att"] = jnp.where(bd_mask, w.pop("scores"), 0.0).astype(BF16)
        st = st_ref[h]
        used = [None] * nc
        for cc in chunks:
            used[cc] = st.astype(BF16)
            st = st * e_last[cc][:, tile_cols(h)] + w["kv_t"][:, cc * LANE:(cc + 1) * LANE]
        st_ref[h] = st
        w["used"] = used
        del w["kv_t"]

    def outputs(h):
        w = work[h]
        o_state = jnp.concatenate(
            [_dot(w["qh"][cc * c:(cc + 1) * c], w["used"][cc], _NT) for cc in range(nc)], axis=0)
        w["o"] = _dot(w["att"], w["vh"]) + o_state

    def emit(h):
        vs = slice(h * dv, (h + 1) * dv)
        o_h = work.pop(h)["o"]
        if finish:
            o_h = _rms(o_h + of_ref[0, :, vs], nw_ref[...]) * _silu(gate_ref[0, :, vs].astype(F32))
        o_ref[0, :, vs] = o_h.astype(o_ref.dtype)

    n_groups = n_tiles // span
    hpg = hpt * span
    lag = 2
    for it in range(n_groups + 4 * lag):
        def heads_of(g):
            return range(g * hpg, (g + 1) * hpg) if 0 <= g < n_groups else ()
        for h in heads_of(it - lag):
            products(h)
        for h in heads_of(it - 3 * lag):
            outputs(h)
        if it < n_groups:
            groups[it] = prep(it)
        for h in heads_of(it - 2 * lag):
            mask_and_chain(h)
        for h in heads_of(it - 4 * lag):
            emit(h)


def _lin_scan(mode, proj, cols, params, n_lat, direction, finish_args=None, aux=None):
    bsz, tall, _ = proj.shape
    if mode == "gla":
        heads, dk, dv = GLA_HEADS, GLA_DK, GLA_DV
    else:
        heads, dk, dv = HGRN_HEADS, HGRN_DK, HGRN_DV
    kw, vw = heads * dk, heads * dv
    tb = SCAN_BLOCK
    nb, n_lat_blocks = tall // tb, n_lat // tb
    reverse = direction == 1
    finish = finish_args is not None

    def tok(col):
        return lambda b, s: (b, _scan_block(s, n_lat_blocks, nb, reverse), col)

    def const2(shape):
        return pl.BlockSpec(shape, lambda b, s: (0, 0))

    p1, p2 = params
    if mode == "gla":
        in_specs = [pl.BlockSpec((1, tb, kw), tok(cols["q"])),
                    pl.BlockSpec((1, tb, kw), tok(cols["k"])),
                    pl.BlockSpec((1, tb, vw), tok(cols["v"])),
                    pl.BlockSpec((1, tb, LANE), tok(0)),
                    const2(p1.shape), const2(p2.shape)]
        args = [proj, proj, proj, aux, p1, p2]
    else:
        in_specs = [pl.BlockSpec((1, tb, kw), tok(cols["q"])),
                    pl.BlockSpec((1, tb, vw), tok(cols["v"])),
                    pl.BlockSpec((1, tb, kw), tok(cols["aux"] + direction)),
                    const2(p1.shape), const2(p2.shape)]
        args = [proj, proj, proj, p1, p2]
    if finish:
        o_f, norm_w = finish_args
        in_specs += [pl.BlockSpec((1, tb, vw), tok(0)),
                     pl.BlockSpec((1, tb, vw), tok(cols["gate"])),
                     const2(norm_w.shape)]
        args += [o_f, proj, norm_w]
    return pl.pallas_call(
        functools.partial(_lin_kernel, mode=mode, direction=direction, finish=finish,
                          heads=heads, dk=dk, dv=dv),
        grid=(bsz, nb),
        in_specs=in_specs,
        out_specs=pl.BlockSpec((1, tb, vw), tok(0)),
        out_shape=jax.ShapeDtypeStruct((bsz, tall, vw), BF16 if finish else F32),
        scratch_shapes=[pltpu.VMEM((heads, dv, LANE), F32)],
        compiler_params=_cparams(2),
        name=f"{mode}_scan_{'bwd' if reverse else 'fwd'}",
    )(*args)


def _s5_kernel(uf_ref, ub_ref, bmat_ref, lre_ref, lim_ref, cmat_ref, of_ref, ob_ref, h_ref, ut_ref, yt_ref, st_ref,
               *, bsz):
    steps = S5_CHUNK
    n_slabs = bmat_ref.shape[1]
    sw = bmat_ref.shape[3] // 2
    width = n_slabs * LANE
    re_cols = [slice(2 * s * sw, (2 * s + 1) * sw) for s in range(n_slabs)]
    im_cols = [slice((2 * s + 1) * sw, (2 * s + 2) * sw) for s in range(n_slabs)]
    both = [slice(2 * s * sw, (2 * s + 2) * sw) for s in range(n_slabs)]

    @pl.when(pl.program_id(0) == 0)
    def _():
        st_ref[...] = jnp.zeros_like(st_ref)

    def inputs(d, u_ref):
        for b in range(bsz):
            for s in range(n_slabs):
                ut_ref[d, s, pl.ds(b, steps, stride=bsz), :] = (
                    u_ref[:, b * width + s * LANE:b * width + (s + 1) * LANE])
        for s in range(n_slabs):
            h_ref[d, :, both[s]] = _dot(ut_ref[d, s].astype(BF16), bmat_ref[d, s])

    def scan(d):
        lam_re = [jnp.broadcast_to(lre_ref[d:d + 1, s * sw:(s + 1) * sw], (bsz, sw)) for s in range(n_slabs)]
        lam_im = [jnp.broadcast_to(lim_ref[d:d + 1, s * sw:(s + 1) * sw], (bsz, sw)) for s in range(n_slabs)]
        hr = [st_ref[d, :, re_cols[s]] for s in range(n_slabs)]
        hi = [st_ref[d, :, im_cols[s]] for s in range(n_slabs)]
        for tt in range(steps):
            t = tt if d == 0 else steps - 1 - tt
            rows = slice(t * bsz, (t + 1) * bsz)
            for s in range(n_slabs):
                nr = lam_re[s] * hr[s] - lam_im[s] * hi[s] + h_ref[d, rows, re_cols[s]]
                ni = lam_re[s] * hi[s] + lam_im[s] * hr[s] + h_ref[d, rows, im_cols[s]]
                h_ref[d, rows, re_cols[s]] = nr
                h_ref[d, rows, im_cols[s]] = ni
                hr[s], hi[s] = nr, ni
        for s in range(n_slabs):
            st_ref[d, :, re_cols[s]] = hr[s]
            st_ref[d, :, im_cols[s]] = hi[s]

    def outputs(d):
        for s in range(n_slabs):
            yt_ref[d, s] = _dot(h_ref[d, :, both[s]].astype(BF16), cmat_ref[s])

    def emit(d, o_ref):
        for b in range(bsz):
            for s in range(n_slabs):
                o_ref[:, b * width + s * LANE:b * width + (s + 1) * LANE] = (
                    yt_ref[d, s, pl.ds(b, steps, stride=bsz), :])

    inputs(0, uf_ref)
    inputs(1, ub_ref)
    scan(0)
    outputs(0)
    scan(1)
    outputs(1)
    emit(0, of_ref)
    emit(1, ob_ref)


def _s5_scan(u_t, bmat, lam_re, lam_im, cmat, bsz, n_lat):
    tall, bw = u_t.shape
    width = bw // bsz
    nch, n_lat_chunks = tall // S5_CHUNK, n_lat // S5_CHUNK
    n_state = lam_re.shape[-1]
    rows = S5_CHUNK * bsz
    fwd = lambda s: (_scan_block(s, n_lat_chunks, nch, False), 0)
    bwd = lambda s: (_scan_block(s, n_lat_chunks, nch, True), 0)
    whole = lambda a: pl.BlockSpec(a.shape, lambda s: (0,) * a.ndim)
    lam_re, lam_im = lam_re.reshape(2, n_state), lam_im.reshape(2, n_state)
    return pl.pallas_call(
        functools.partial(_s5_kernel, bsz=bsz),
        grid=(nch,),
        in_specs=[pl.BlockSpec((S5_CHUNK, bw), fwd), pl.BlockSpec((S5_CHUNK, bw), bwd),
                  whole(bmat), whole(lam_re), whole(lam_im), whole(cmat)],
        out_specs=[pl.BlockSpec((S5_CHUNK, bw), fwd), pl.BlockSpec((S5_CHUNK, bw), bwd)],
        out_shape=[jax.ShapeDtypeStruct((tall, bw), F32)] * 2,
        scratch_shapes=[pltpu.VMEM((2, rows, 2 * n_state), F32),
                        pltpu.VMEM((2, width // LANE, rows, LANE), F32),
                        pltpu.VMEM((2, width // LANE, rows, LANE), F32),
                        pltpu.VMEM((2, bsz, 2 * n_state), F32)],
        compiler_params=_cparams(1),
        name="s5_scan",
    )(u_t, u_t, bmat, lam_re, lam_im, cmat)


def _out0_kernel(x_ref, a_ref, b_ref, wa_ref, wb_ref, gl_ref, gc_ref, o_ref, *, tm, n_lat):
    i = pl.program_id(1)
    o = _dot(a_ref[0], wa_ref[...]) + _dot(b_ref[0], wb_ref[...])
    o_ref[0] = x_ref[0] + _row_select(i, tm, n_lat, gc_ref[...], gl_ref[0]) * o


def _out_proj0(x_all, mix_a, mix_b, w_a, w_b, mod_l, mod_c, n_lat, n_rows, tm):
    bsz, _, d = x_all.shape
    tall = n_rows
    return pl.pallas_call(
        functools.partial(_out0_kernel, tm=tm, n_lat=n_lat),
        grid=(bsz, tall // tm),
        in_specs=[pl.BlockSpec((1, tm, d), lambda b, i: (b, i, 0)),
                  pl.BlockSpec((1, tm, mix_a.shape[2]), lambda b, i: (b, i, 0)),
                  pl.BlockSpec((1, tm, mix_b.shape[2]), lambda b, i: (b, i, 0)),
                  pl.BlockSpec(w_a.shape, lambda b, i: (0, 0)),
                  pl.BlockSpec(w_b.shape, lambda b, i: (0, 0))] + _mod_specs(d, (2,), 2),
        out_specs=pl.BlockSpec((1, tm, d), lambda b, i: (b, i, 0)),
        out_shape=jax.ShapeDtypeStruct((bsz, tall, d), F32),
        compiler_params=_cparams(2),
        name="out_proj_even",
    )(x_all, mix_a, mix_b, w_a, w_b, mod_l, mod_c)


def _gelu_tanh(x):
    return 0.5 * x * (1.0 + jnp.tanh(math.sqrt(2.0 / math.pi) * (x + 0.044715 * (x * x * x))))


def _out1_kernel(x_ref, a_ref, yf_ref, yb_ref, u_ref, dsk_ref, gw_ref, gb_ref, wa_ref, wb_ref,
                 gl_ref, gc_ref, o_ref, *, tm, n_lat):
    i = pl.program_id(1)
    y = _gelu_tanh(yf_ref[...] + yb_ref[...] + dsk_ref[...] * u_ref[...])
    glu = _dot(y.astype(BF16), gw_ref[...]) + gb_ref[...]
    y = y * _sigmoid(glu)
    o = _dot(a_ref[0], wa_ref[...]) + _dot(y.astype(BF16), wb_ref[...])
    o_ref[0] = x_ref[0] + _row_select(i, tm, n_lat, gc_ref[...], gl_ref[0]) * o


def _out_proj1(x_all, mix_a, y_dirs, u_t, d_skip, glu_w, glu_b, w_a, w_b, mod_l, mod_c, n_lat, n_rows, tm):
    bsz, _, d = x_all.shape
    tall = n_rows
    width = d_skip.shape[1]
    (y_f, y_b), u2 = y_dirs, u_t
    return pl.pallas_call(
        functools.partial(_out1_kernel, tm=tm, n_lat=n_lat),
        grid=(bsz, tall // tm),
        in_specs=[pl.BlockSpec((1, tm, d), lambda b, i: (b, i, 0)),
                  pl.BlockSpec((1, tm, mix_a.shape[2]), lambda b, i: (b, i, 0)),
                  pl.BlockSpec((tm, width), lambda b, i: (i, b)),
                  pl.BlockSpec((tm, width), lambda b, i: (i, b)),
                  pl.BlockSpec((tm, width), lambda b, i: (i, b)),
                  pl.BlockSpec((1, width), lambda b, i: (0, 0)),
                  pl.BlockSpec(glu_w.shape, lambda b, i: (0, 0)),
                  pl.BlockSpec((1, width), lambda b, i: (0, 0)),
                  pl.BlockSpec(w_a.shape, lambda b, i: (0, 0)),
                  pl.BlockSpec(w_b.shape, lambda b, i: (0, 0))] + _mod_specs(d, (2,), 2),
        out_specs=pl.BlockSpec((1, tm, d), lambda b, i: (b, i, 0)),
        out_shape=jax.ShapeDtypeStruct((bsz, tall, d), F32),
        compiler_params=_cparams(2),
        name="out_proj_odd",
    )(x_all, mix_a, y_f, y_b, u2, d_skip, glu_w, glu_b, w_a, w_b, mod_l, mod_c)


def _mlp_kernel(*refs, tm, n_lat, final):
    if final:
        (x_ref, nw_ref, shl_ref, shc_ref, scl_ref, scc_ref, gl_ref, gc_ref, w1_ref, w2_ref, fw_ref,
         o_ref, h_scr, acc_scr) = refs
    else:
        (x_ref, nw_ref, shl_ref, shc_ref, scl_ref, scc_ref, gl_ref, gc_ref, w1_ref, w2_ref,
         o_ref, h_scr, acc_scr) = refs
    i = pl.program_id(1)
    j = pl.program_id(2)

    @pl.when(j == 0)
    def _():
        _store_norm_modulated(h_scr, x_ref[0], nw_ref[...], shl_ref[0], shc_ref[...], scl_ref[0], scc_ref[...],
                              i, tm, n_lat)
        acc_scr[...] = jnp.zeros_like(acc_scr)

    def partial_product():
        a = jnp.maximum(_dot(h_scr[...], w1_ref[...]), 0.0)
        return _dot((a * a).astype(BF16), w2_ref[...])

    last_j = pl.num_programs(2) - 1

    @pl.when(j < last_j)
    def _():
        acc_scr[...] += partial_product()

    @pl.when(j == last_j)
    def _():
        out = x_ref[0] + _row_select(i, tm, n_lat, gc_ref[...], gl_ref[0]) * (acc_scr[...] + partial_product())
        if final:
            out = _rms(out, fw_ref[...])
        o_ref[0] = out


def _mlp(x_all, norm_w, mod_l, mod_c, w1, w2, final_w, n_lat, tm, tf):
    bsz, tall, d = x_all.shape
    ff = w1.shape[1]
    final = final_w is not None
    in_specs = [pl.BlockSpec((1, tm, d), lambda b, i, j: (b, i, 0)),
                pl.BlockSpec((1, d), lambda b, i, j: (0, 0))]
    in_specs += _mod_specs(d, (3, 4, 5), 3)
    in_specs += [pl.BlockSpec((d, tf), lambda b, i, j: (0, j)),
                 pl.BlockSpec((tf, d), lambda b, i, j: (j, 0))]
    args = [x_all, norm_w.reshape(1, d)] + [mod_l, mod_c] * 3 + [w1, w2]
    if final:
        in_specs.append(pl.BlockSpec((1, d), lambda b, i, j: (0, 0)))
        args.append(final_w.reshape(1, d))
    return pl.pallas_call(
        functools.partial(_mlp_kernel, tm=tm, n_lat=n_lat, final=final),
        grid=(bsz, tall // tm, ff // tf),
        in_specs=in_specs,
        out_specs=pl.BlockSpec((1, tm, d), lambda b, i, j: (b, i, 0)),
        out_shape=jax.ShapeDtypeStruct((bsz, tall, d), F32),
        scratch_shapes=[pltpu.VMEM((tm, d), BF16), pltpu.VMEM((tm, d), F32)],
        compiler_params=_cparams(3),
        name="sq_relu_mlp",
    )(*args)


def _even_in_weight(w_in):
    sizes = (SSD_INNER, SSD_INNER + 2 * SSD_BC, 2 * SSD_HEADS, GLA_KEY, GLA_KEY, GLA_VAL,
             2 * GLA_GATE_RANK, GLA_VAL)
    offs = [0]
    for s in sizes:
        offs.append(offs[-1] + s)
    z, xbc, dt, q, k, v, lr, r = (w_in[:, offs[n]:offs[n + 1]] for n in range(8))
    pad = jnp.zeros((w_in.shape[0], LANE - dt.shape[1] - lr.shape[1]), w_in.dtype)
    main = jnp.concatenate([xbc, z, v, r, q, k], axis=1).astype(BF16)
    aux = jnp.concatenate([dt, lr, pad], axis=1).astype(BF16)
    return main, aux


def _s5_params(a_re, a_im, log_dt, b_re, b_im, c_re, c_im):
    delta = jnp.exp(log_dt.astype(F32))[..., None]
    mag = jnp.exp(a_re * delta)
    lbar_re, lbar_im = mag * jnp.cos(a_im * delta), mag * jnp.sin(a_im * delta)
    den = a_re * a_re + a_im * a_im
    zr = ((lbar_re - 1.0) * a_re + lbar_im * a_im) / den
    zi = (lbar_im * a_re - (lbar_re - 1.0) * a_im) / den
    bb_re = zr[..., None] * b_re - zi[..., None] * b_im
    bb_im = zr[..., None] * b_im + zi[..., None] * b_re
    n_slabs = S5_GROUPS // S5_SLAB
    eye = jnp.eye(S5_SLAB, dtype=F32)
    sw = S5_SLAB * S5_STATE

    def block_in(bb):
        bb = bb.reshape(2, n_slabs, S5_SLAB, S5_STATE, S5_GROUP)
        return jnp.einsum("dsgpc,gh->dsgchp", bb, eye).reshape(2, n_slabs, LANE, sw)

    def block_out(cc):
        cc = cc.reshape(n_slabs, S5_SLAB, S5_GROUP, S5_STATE)
        return jnp.einsum("sgcp,gh->sgphc", cc, eye).reshape(n_slabs, sw, LANE)

    bmat = jnp.concatenate([block_in(bb_re), block_in(bb_im)], axis=3).astype(BF16)
    cmat = jnp.concatenate([block_out(c_re), -block_out(c_im)], axis=1).astype(BF16)
    return (bmat, lbar_re.reshape(2, 1, S5_NSTATE), lbar_im.reshape(2, 1, S5_NSTATE), cmat)


def _layer_even(x_all, mod_l, mod_c, n_lat, tm, n_rows_out, tm_out, norm1_w, w_in, conv_w, conv_b, dt_bias,
                a_log, d_skip, ssd_norm_w, gate_w, gate_b, gla_norm_w, w_out):
    w, w_aux = _even_in_weight(w_in)
    n = w.shape[1]
    proj, aux = _project(x_all, norm1_w, mod_l, mod_c, w, w_aux, n_lat, tm, _largest_divisor(n, 2 * LANE, PROJ_TN),
                         extra_token_major=False)
    n_xbc = SSD_INNER + 2 * SSD_BC
    c_z, c_v, c_r = n_xbc // SSD_INNER, n_xbc // GLA_VAL + 1, n_xbc // GLA_VAL + 2
    c_q = (n_xbc + 3 * SSD_INNER) // GLA_KEY
    xbc = _conv_silu(proj, conv_w, conv_b, n_lat, n_xbc)

    neg_a = -jnp.exp(a_log.astype(F32))
    dt_bias = dt_bias.astype(F32)
    d_wide = jnp.repeat(d_skip.astype(F32), SSD_HEADDIM).reshape(1, SSD_INNER)
    y_f = _ssd_scan(xbc, proj, aux, dt_bias, neg_a, c_z, n_lat, 0)
    y_mix = _ssd_scan(xbc, proj, aux, dt_bias, neg_a, c_z, n_lat, 1,
                      (y_f, d_wide, ssd_norm_w.reshape(1, SSD_INNER)))

    cols = {"q": c_q, "k": c_q + 1, "v": c_v, "gate": c_r}
    gparams = [(gate_w[d].astype(BF16), gate_b[d].reshape(1, GLA_KEY).astype(F32)) for d in range(2)]
    o_f = _lin_scan("gla", proj, cols, gparams[0], n_lat, 0, aux=aux)
    o_mix = _lin_scan("gla", proj, cols, gparams[1], n_lat, 1, (o_f, gla_norm_w.reshape(1, GLA_DV)), aux=aux)

    w_out = w_out.astype(BF16)
    return _out_proj0(x_all, y_mix, o_mix, w_out[:SSD_INNER], w_out[SSD_INNER:], mod_l, mod_c, n_lat,
                      n_rows_out, tm_out)


def _layer_odd(x_all, mod_l, mod_c, n_lat, tm, n_rows_out, tm_out, norm1_w, w_in, lb, hgrn_norm_w, a_re, a_im,
               log_dt, b_re, b_im, c_re, c_im, d_skip, glu_w, glu_b, w_out):
    bsz = x_all.shape[0]
    n_main = 5 * HGRN_WIDTH
    w_main = w_in[:, :n_main].astype(BF16)
    w_u = w_in[:, n_main:].astype(BF16)
    proj, u_t = _project(x_all, norm1_w, mod_l, mod_c, w_main, w_u, n_lat, tm,
                         _largest_divisor(n_main, 2 * LANE, PROJ_TN))

    lb = lb.astype(F32).reshape(2, 1, HGRN_WIDTH)
    cols = {"q": 0, "v": 1, "aux": 2, "gate": 4}
    o_f = _lin_scan("hgrn", proj, cols, (lb[0], 1.0 - lb[0]), n_lat, 0)
    o_mix = _lin_scan("hgrn", proj, cols, (lb[1], 1.0 - lb[1]), n_lat, 1,
                      (o_f, hgrn_norm_w.reshape(1, HGRN_DV)))

    bmat, lam_re, lam_im, cmat = _s5_params(a_re.astype(F32), a_im.astype(F32), log_dt, b_re.astype(F32),
                                            b_im.astype(F32), c_re.astype(F32), c_im.astype(F32))
    y_dirs = _s5_scan(u_t, bmat, lam_re, lam_im, cmat, bsz, n_lat)

    w_out = w_out.astype(BF16)
    return _out_proj1(x_all, o_mix, y_dirs, u_t, d_skip.astype(F32).reshape(1, S5_WIDTH),
                      glu_w.astype(BF16), glu_b.astype(F32).reshape(1, S5_WIDTH),
                      w_out[:HGRN_WIDTH], w_out[HGRN_WIDTH:], mod_l, mod_c, n_lat, n_rows_out, tm_out)


def kernel(x, c, ctx, c_ctx, ada_w, ada_b, norm1_w, norm2_w, ssd_gla_w_in, ssd_conv_w, ssd_conv_b, ssd_dt_bias, ssd_a_log, ssd_d, ssd_norm_w, gla_gate_w, gla_gate_b, gla_norm_w, ssd_gla_w_out, hgrn_s5_w_in, hgrn_lb_logits, hgrn_norm_w, s5_a_re, s5_a_im, s5_log_dt, s5_b_re, s5_b_im, s5_c_re, s5_c_im, s5_d, s5_glu_w, s5_glu_b, hgrn_s5_w_out, mlp_w1, mlp_w2, final_norm_w):
    bsz, n_lat, d = x.shape
    ctx_len = ctx.shape[1]
    depth = ada_w.shape[0]
    tall = n_lat + ctx_len
    assert bsz % 8 == 0 and ctx_len % SCAN_BLOCK == 0 and n_lat % SCAN_BLOCK == 0 and n_lat % GRID_W == 0
    tf = 1024

    n_rows = -(-(bsz + 1) // 8) * 8
    cvec = jnp.concatenate([c, c_ctx[None, :], jnp.zeros((n_rows - bsz - 1, d), c.dtype)], axis=0)
    mod = _modulation(cvec.astype(F32), ada_w, ada_b)

    p_lb = jax.nn.softmax(hgrn_lb_logits.astype(F32), axis=0)
    lb_all = jnp.cumsum(p_lb, axis=0) - p_lb[0]

    x_all = jnp.concatenate([x, ctx], axis=1).astype(F32)
    tm_all = _largest_divisor(tall, 16, 1056)
    for layer in range(depth):
        j = layer // 2
        last = layer == depth - 1
        n_rows = n_lat if last else tall
        tm_out = _largest_divisor(n_rows, 16, 1056)
        mod_l = mod[layer, :bsz].reshape(bsz, 1, N_MOD * d)
        mod_c = mod[layer, bsz:bsz + 1]
        if layer % 2 == 0:
            x_all = _layer_even(x_all, mod_l, mod_c, n_lat, tm_all, n_rows, tm_out, norm1_w[layer],
                                ssd_gla_w_in[j], ssd_conv_w[j], ssd_conv_b[j], ssd_dt_bias[j], ssd_a_log[j],
                                ssd_d[j], ssd_norm_w[j], gla_gate_w[j], gla_gate_b[j], gla_norm_w[j],
                                ssd_gla_w_out[j])
        else:
            x_all = _layer_odd(x_all, mod_l, mod_c, n_lat, tm_all, n_rows, tm_out, norm1_w[layer],
                               hgrn_s5_w_in[j], lb_all[layer], hgrn_norm_w[j], s5_a_re[j], s5_a_im[j],
                               s5_log_dt[j], s5_b_re[j], s5_b_im[j], s5_c_re[j], s5_c_im[j], s5_d[j],
                               s5_glu_w[j], s5_glu_b[j], hgrn_s5_w_out[j])
        x_all = _mlp(x_all, norm2_w[layer], mod_l, mod_c, mlp_w1[layer].astype(BF16),
                     mlp_w2[layer].astype(BF16), final_norm_w if last else None, n_lat, tm_out, tf)
    return x_all.astype(x.dtype)
```

```python
import functools
import math

import jax
import jax.numpy as jnp
from jax import lax
from jax.experimental import pallas as pl
from jax.experimental.pallas import tpu as pltpu

F32 = jnp.float32
BF16 = jnp.bfloat16

GRID_W = 64
NORM_EPS = 1e-6
N_MOD = 6
SSD_HEADDIM = 64
SSD_HEADS = 16
SSD_GROUPS = 4
SSD_STATE = 128
SSD_CHUNK = 128
GLA_HEADS = 8
GLA_DK = 64
GLA_DV = 128
GLA_GATE_RANK = 16
GLA_GATE_NORM = 16.0
HGRN_HEADS = 8
HGRN_DK = 128
HGRN_DV = 128
S5_GROUP = 16
S5_GROUPS = 24
S5_STATE = 64
LIN_CHUNK = 64

SSD_INNER = SSD_HEADS * SSD_HEADDIM
SSD_BC = SSD_GROUPS * SSD_STATE
GLA_KEY = GLA_HEADS * GLA_DK
GLA_VAL = GLA_HEADS * GLA_DV
HGRN_WIDTH = HGRN_HEADS * HGRN_DV
S5_WIDTH = S5_GROUPS * S5_GROUP
S5_NSTATE = S5_GROUPS * S5_STATE

VMEM_LIMIT_BYTES = 56 * 1024 * 1024
LANE = 128
PROJ_TN = 2560
SCAN_BLOCK = 256
S5_CHUNK = 128
S5_SLAB = LANE // S5_GROUP


def _cparams(n_axes):
    return pltpu.CompilerParams(dimension_semantics=("arbitrary",) * n_axes,
                                vmem_limit_bytes=VMEM_LIMIT_BYTES)


def _largest_divisor(n, multiple, cap):
    best = None
    for d in range(multiple, min(n, cap) + 1, multiple):
        if n % d == 0:
            best = d
    assert best is not None, (n, multiple, cap)
    return best


_NEG_LOG2E = -1.4426950408889634


def _sigmoid(x):
    return 1.0 / (1.0 + jnp.exp2(x * _NEG_LOG2E))


def _silu(x):
    return x * _sigmoid(x)


def _softplus(x):
    return jnp.maximum(x, 0.0) + jnp.log1p(jnp.exp(-jnp.abs(x)))


def _log_sigmoid(x):
    return -_softplus(-x)


def _rms(x, w):
    return x * lax.rsqrt(jnp.mean(x * x, axis=-1, keepdims=True) + NORM_EPS) * w


def _dot(a, b, dims=(((1,), (0,)), ((), ())), precision=None):
    return lax.dot_general(a, b, dims, precision=precision, preferred_element_type=F32)


def _split3(v):
    hi = v.astype(BF16)
    r1 = v - hi.astype(F32)
    mid = r1.astype(BF16)
    lo = (r1 - mid.astype(F32)).astype(BF16)
    return hi, mid, lo


def _tri3(mask):
    tri = mask.astype(BF16)
    return jnp.concatenate([tri, tri, tri], axis=1)


def _cumsum_rows(tri2, v):
    hi = v.astype(BF16)
    lo = (v - hi.astype(F32)).astype(BF16)
    return _dot(tri2, jnp.concatenate([hi, lo], axis=0))


_NT = (((1,), (1,)), ((), ()))
_TN = (((0,), (0,)), ((), ()))
_TT = (((0,), (1,)), ((), ()))


def _mod_kernel(c_ref, w_ref, b_ref, o_ref):
    a = _silu(c_ref[...]).astype(BF16)
    o_ref[0] = _dot(a, w_ref[0].astype(BF16)) + b_ref[0]


def _modulation(cvec, ada_w, ada_b):
    depth, d, n = ada_w.shape
    rows = cvec.shape[0]
    tn = _largest_divisor(n, LANE, 1024)
    return pl.pallas_call(
        _mod_kernel,
        grid=(depth, n // tn),
        in_specs=[pl.BlockSpec((rows, d), lambda l, j: (0, 0)),
                  pl.BlockSpec((1, d, tn), lambda l, j: (l, 0, j)),
                  pl.BlockSpec((1, 1, tn), lambda l, j: (l, 0, j))],
        out_specs=pl.BlockSpec((1, rows, tn), lambda l, j: (l, 0, j)),
        out_shape=jax.ShapeDtypeStruct((depth, rows, n), F32),
        compiler_params=_cparams(2),
        name="adaln_mod",
    )(cvec, ada_w, ada_b.reshape(depth, 1, n))


def _row_select(i, tm, n_lat, ctx_val, lat_val):
    row = i * tm + lax.broadcasted_iota(jnp.int32, (tm, 1), 0)
    return jnp.where(row >= n_lat, ctx_val, lat_val)


def _store_norm_modulated(h_ref, x, nw, shift_l, shift_c, scale_l, scale_c, i, tm, n_lat):
    xn = x * lax.rsqrt(jnp.mean(x * x, axis=-1, keepdims=True) + NORM_EPS)
    gain = _row_select(i, tm, n_lat, nw * (1.0 + scale_c), nw * (1.0 + scale_l))
    h_ref[...] = (xn * gain + _row_select(i, tm, n_lat, shift_c, shift_l)).astype(h_ref.dtype)


def _mod_specs(d, cols, n_grid_axes):
    specs = []
    for k in cols:
        if n_grid_axes == 2:
            specs.append(pl.BlockSpec((1, 1, d), lambda b, i, k=k: (b, 0, k)))
            specs.append(pl.BlockSpec((1, d), lambda b, i, k=k: (0, k)))
        else:
            specs.append(pl.BlockSpec((1, 1, d), lambda b, i, j, k=k: (b, 0, k)))
            specs.append(pl.BlockSpec((1, d), lambda b, i, j, k=k: (0, k)))
    return specs


def _proj_kernel(*refs, tm, n_lat, has_extra):
    if has_extra:
        (x_ref, nw_ref, shl_ref, shc_ref, scl_ref, scc_ref, w_ref, wx_ref, o_ref, ox_ref, h_scr) = refs
    else:
        (x_ref, nw_ref, shl_ref, shc_ref, scl_ref, scc_ref, w_ref, o_ref, h_scr) = refs
    i = pl.program_id(1)
    j = pl.program_id(2)

    @pl.when(j == 0)
    def _():
        _store_norm_modulated(h_scr, x_ref[0], nw_ref[...], shl_ref[0], shc_ref[...], scl_ref[0], scc_ref[...],
                              i, tm, n_lat)
        if has_extra:
            ox_ref[...] = _dot(h_scr[...], wx_ref[...]).reshape(ox_ref.shape)

    o_ref[0] = _dot(h_scr[...], w_ref[...]).astype(o_ref.dtype)


def _project(x_all, norm_w, mod_l, mod_c, w, w_extra, n_lat, tm, tn, extra_token_major=True):
    bsz, tall, d = x_all.shape
    n = w.shape[1]
    has_extra = w_extra is not None
    in_specs = [pl.BlockSpec((1, tm, d), lambda b, i, j: (b, i, 0)),
                pl.BlockSpec((1, d), lambda b, i, j: (0, 0))]
    in_specs += _mod_specs(d, (0, 1), 3)
    in_specs.append(pl.BlockSpec((d, tn), lambda b, i, j: (0, j)))
    args = [x_all, norm_w.reshape(1, d), mod_l, mod_c, mod_l, mod_c, w]
    out_specs = [pl.BlockSpec((1, tm, tn), lambda b, i, j: (b, i, j))]
    out_shape = [jax.ShapeDtypeStruct((bsz, tall, n), BF16)]
    if has_extra:
        nx = w_extra.shape[1]
        in_specs.append(pl.BlockSpec((d, nx), lambda b, i, j: (0, 0)))
        args.append(w_extra)
        if extra_token_major:
            out_specs.append(pl.BlockSpec((tm, nx), lambda b, i, j: (i, b)))
            out_shape.append(jax.ShapeDtypeStruct((tall, bsz * nx), F32))
        else:
            out_specs.append(pl.BlockSpec((1, tm, nx), lambda b, i, j: (b, i, 0)))
            out_shape.append(jax.ShapeDtypeStruct((bsz, tall, nx), F32))
    out = pl.pallas_call(
        functools.partial(_proj_kernel, tm=tm, n_lat=n_lat, has_extra=has_extra),
        grid=(bsz, tall // tm, n // tn),
        in_specs=in_specs,
        out_specs=out_specs,
        out_shape=out_shape,
        scratch_shapes=[pltpu.VMEM((tm, d), BF16)],
        compiler_params=_cparams(3),
        name="norm_mod_proj",
    )(*args)
    return out if has_extra else out[0]


def _conv_kernel(main_ref, prev_ref, next_ref, w_ref, b_ref, o_ref, *, tt, n_lat, tall):
    i = pl.program_id(1)
    te = tt + 2 * GRID_W
    p = i * tt - GRID_W + lax.broadcasted_iota(jnp.int32, (te, 1), 0)
    is_ctx = p >= n_lat
    col = jnp.bitwise_and(p, GRID_W - 1)
    has_left = jnp.where(is_ctx, p - n_lat, col) > 0
    has_right = jnp.where(is_ctx, p - (tall - 1), col - (GRID_W - 1)) < 0
    w = w_ref[...]

    def conv(interior):
        ext = jnp.concatenate([prev_ref[0], main_ref[0], next_ref[0]], axis=0).astype(F32)
        own = slice(GRID_W, GRID_W + tt)
        if interior:
            as_left = jnp.where(has_right, ext, 0.0)
            as_right = jnp.where(has_left, ext, 0.0)
        acc = jnp.zeros((tt, w.shape[1]), F32) + b_ref[...]
        for dy in (-1, 0, 1):
            rs = slice(GRID_W + GRID_W * dy, GRID_W + GRID_W * dy + tt)
            k0 = 3 * (dy + 1)
            if interior:
                left, right = pltpu.roll(as_left[rs], 1, 0), pltpu.roll(as_right[rs], tt - 1, 0)
            else:
                left = jnp.where(has_left[own], pltpu.roll(ext[rs], 1, 0), 0.0)
                right = jnp.where(has_right[own], pltpu.roll(ext[rs], tt - 1, 0), 0.0)
            t = ext[rs] * w[k0 + 1:k0 + 2] + left * w[k0:k0 + 1] + right * w[k0 + 2:k0 + 3]
            if not interior and dy != 0:
                q = p[own]
                if dy == -1:
                    ok = jnp.where(q >= n_lat, 0, q) >= GRID_W
                else:
                    ok = jnp.where(q >= n_lat, n_lat, q) < n_lat - GRID_W
                t = jnp.where(ok, t, 0.0)
            acc = acc + t
        o_ref[0] = _silu(acc).astype(o_ref.dtype)

    interior = jnp.logical_and(i * tt >= GRID_W, (i + 1) * tt <= n_lat - GRID_W)
    pl.when(interior)(lambda: conv(True))
    pl.when(jnp.logical_not(interior))(lambda: conv(False))


def _conv_silu(proj, conv_w, conv_b, n_lat, n_ch):
    bsz, tall, _ = proj.shape
    n_rows = tall // GRID_W
    tt = _largest_divisor(tall, GRID_W, 768)
    assert n_lat % GRID_W == 0 and n_lat // tt == (tall - 1) // tt
    r = tt // GRID_W
    tc = 512
    return pl.pallas_call(
        functools.partial(_conv_kernel, tt=tt, n_lat=n_lat, tall=tall),
        grid=(bsz, tall // tt, n_ch // tc),
        in_specs=[pl.BlockSpec((1, tt, tc), lambda b, i, c: (b, i, c)),
                  pl.BlockSpec((1, GRID_W, tc), lambda b, i, c: (b, jnp.maximum(i * r - 1, 0), c)),
                  pl.BlockSpec((1, GRID_W, tc), lambda b, i, c: (b, jnp.minimum((i + 1) * r, n_rows - 1), c)),
                  pl.BlockSpec((9, tc), lambda b, i, c: (0, c)),
                  pl.BlockSpec((1, tc), lambda b, i, c: (0, c))],
        out_specs=pl.BlockSpec((1, tt, tc), lambda b, i, c: (b, i, c)),
        out_shape=jax.ShapeDtypeStruct((bsz, tall, n_ch), BF16),
        compiler_params=_cparams(3),
        name="dwconv_silu",
    )(proj, proj, proj, conv_w.reshape(9, n_ch), conv_b.reshape(1, n_ch))


def _scan_block(s, n_lat_blocks, n_blocks, reverse):
    n_ctx_blocks = n_blocks - n_lat_blocks
    if not reverse:
        return jnp.where(s < n_ctx_blocks, n_lat_blocks + s, s - n_ctx_blocks)
    return n_blocks - 1 - s


def _tri_mask(c, reverse):
    ri = lax.broadcasted_iota(jnp.int32, (c, c), 0)
    ci = lax.broadcasted_iota(jnp.int32, (c, c), 1)
    return (ci >= ri) if reverse else (ci <= ri)


def _ssd_expand_matrix():
    eye = jnp.eye(SSD_HEADS, dtype=F32)
    e_head = jnp.repeat(eye, SSD_HEADDIM, axis=1)
    e_seg = jnp.repeat(eye, SSD_CHUNK, axis=1)
    zh = jnp.zeros_like(e_head)
    zs = jnp.zeros_like(e_seg)
    blk = jnp.concatenate([
        jnp.concatenate([e_head, zh, zh, zs], axis=1),
        jnp.concatenate([zh, e_head, zh, zs], axis=1),
        jnp.concatenate([zh, zh, e_head, zs], axis=1),
        jnp.concatenate([zh, zh, zh, e_seg], axis=1)], axis=0)
    return jnp.concatenate([blk, blk, blk], axis=0).astype(BF16)


def _ssd_kernel(*refs, direction, finish):
    if finish:
        (x_ref, bm_ref, cm_ref, dtlr_ref, dtb_ref, nega_ref, exp_ref, z_ref, yf_ref, dsk_ref, nw_ref,
         o_ref, st_ref) = refs
    else:
        (x_ref, bm_ref, cm_ref, dtlr_ref, dtb_ref, nega_ref, exp_ref, o_ref, st_ref) = refs
    reverse = direction == 1
    c = SSD_CHUNK
    p = SSD_HEADDIM
    gw = SSD_INNER // SSD_GROUPS
    hpg = SSD_HEADS // SSD_GROUPS

    @pl.when(pl.program_id(1) == 0)
    def _():
        st_ref[...] = jnp.zeros_like(st_ref)

    mask = _tri_mask(c, reverse)
    tri3 = _tri3(mask)
    last = 0 if reverse else c - 1
    n_chunks = x_ref.shape[1] // c
    order = range(n_chunks - 1, -1, -1) if reverse else range(n_chunks)
    groups = range(SSD_GROUPS)
    g_cols = [slice(g * gw, (g + 1) * gw) for g in groups]
    n_cols = [slice(g * SSD_STATE, (g + 1) * SSD_STATE) for g in groups]
    cb = {(ck, g): _dot(cm_ref[0, ck * c:(ck + 1) * c, n_cols[g]], bm_ref[0, ck * c:(ck + 1) * c, n_cols[g]], _NT)
          for ck in order for g in groups}
    prep = {}
    for ck in order:
        rs = slice(ck * c, (ck + 1) * c)
        x = x_ref[0, rs, :].astype(F32)
        dt_raw = dtlr_ref[0, rs, :][:, direction * SSD_HEADS:(direction + 1) * SSD_HEADS].astype(F32)
        dt = _softplus(dt_raw + dtb_ref[...])
        la3 = jnp.concatenate(_split3(dt * nega_ref[...]), axis=0)
        acum = _dot(tri3, la3)
        acum_t = _dot(la3, tri3, _TT)
        a_last = acum[last:last + 1]
        narrow = jnp.concatenate([dt, dt * jnp.exp(a_last - acum), jnp.exp(acum), acum], axis=1)
        wide = _dot(jnp.concatenate(_split3(narrow), axis=1), exp_ref[...])
        ea_w = wide[:, 2 * SSD_INNER:3 * SSD_INNER]
        prep[ck] = dict(
            x=x, bm=bm_ref[0, rs, :], cm=cm_ref[0, rs, :], wide=wide, acum_t=acum_t, ea_w=ea_w,
            xdt=(x * wide[:, :SSD_INNER]).astype(BF16),
            xw=(x * wide[:, SSD_INNER:2 * SSD_INNER]).astype(BF16),
            e_last=ea_w[last:last + 1])
    kv = {(ck, g): _dot(prep[ck]["bm"][:, n_cols[g]], prep[ck]["xw"][:, g_cols[g]], _TN)
          for ck in order for g in groups}
    scores, st_used = {}, {}
    for g in groups:
        st = st_ref[:, g_cols[g]]
        for ck in order:
            st_used[ck, g] = st.astype(BF16)
            st = st * prep[ck]["e_last"][:, g_cols[g]] + kv[ck, g]
        st_ref[:, g_cols[g]] = st
    for ck in order:
        for h in range(SSD_HEADS):
            a_i = prep[ck]["wide"][:, 3 * SSD_INNER + h * c:3 * SSD_INNER + (h + 1) * c]
            decay = jnp.exp(jnp.where(mask, a_i - prep[ck]["acum_t"][h:h + 1, :], -jnp.inf))
            scores[ck, h] = (cb[ck, h // hpg] * decay).astype(BF16)
    for ck in order:
        rs = slice(ck * c, (ck + 1) * c)
        x = prep[ck]["x"]
        y_groups = []
        for g in groups:
            ys = [_dot(scores[ck, h], prep[ck]["xdt"][:, h * p:(h + 1) * p])
                  for h in range(g * hpg, (g + 1) * hpg)]
            y_state = _dot(prep[ck]["cm"][:, n_cols[g]], st_used[ck, g]) * prep[ck]["ea_w"][:, g_cols[g]]
            y_groups.append(jnp.concatenate(ys, axis=1) + y_state)
        y = jnp.concatenate(y_groups, axis=1)
        if finish:
            z = z_ref[0, rs, :].astype(F32)
            y = (y + yf_ref[0, rs, :] + dsk_ref[...] * x) * _silu(z)
            outs = []
            for g in range(SSD_GROUPS):
                sl = slice(g * gw, (g + 1) * gw)
                outs.append(_rms(y[:, sl], nw_ref[:, sl]))
            o_ref[0, rs, :] = jnp.concatenate(outs, axis=1).astype(o_ref.dtype)
        else:
            o_ref[0, rs, :] = y


def _ssd_scan(xbc, proj, aux, dt_bias, neg_a, z_col, n_lat, direction, finish_args=None):
    bsz, tall, _ = xbc.shape
    tb = SCAN_BLOCK
    nb, n_lat_blocks = tall // tb, n_lat // tb
    reverse = direction == 1
    finish = finish_args is not None
    expand = _ssd_expand_matrix()

    def tok(col):
        return lambda b, s: (b, _scan_block(s, n_lat_blocks, nb, reverse), col)

    in_specs = [pl.BlockSpec((1, tb, SSD_INNER), tok(0)),
                pl.BlockSpec((1, tb, SSD_BC), tok(SSD_INNER // SSD_BC)),
                pl.BlockSpec((1, tb, SSD_BC), tok(SSD_INNER // SSD_BC + 1)),
                pl.BlockSpec((1, tb, LANE), tok(0)),
                pl.BlockSpec((1, SSD_HEADS), lambda b, s: (0, 0)),
                pl.BlockSpec((1, SSD_HEADS), lambda b, s: (0, 0)),
                pl.BlockSpec(expand.shape, lambda b, s: (0, 0))]
    args = [xbc, xbc, xbc, aux, dt_bias[direction:direction + 1], neg_a[direction:direction + 1], expand]
    if finish:
        y_f, d_skip_wide, norm_w = finish_args
        in_specs += [pl.BlockSpec((1, tb, SSD_INNER), tok(z_col)),
                     pl.BlockSpec((1, tb, SSD_INNER), tok(0)),
                     pl.BlockSpec((1, SSD_INNER), lambda b, s: (0, 0)),
                     pl.BlockSpec((1, SSD_INNER), lambda b, s: (0, 0))]
        args += [proj, y_f, d_skip_wide, norm_w]
    return pl.pallas_call(
        functools.partial(_ssd_kernel, direction=direction, finish=finish),
        grid=(bsz, nb),
        in_specs=in_specs,
        out_specs=pl.BlockSpec((1, tb, SSD_INNER), tok(0)),
        out_shape=jax.ShapeDtypeStruct((bsz, tall, SSD_INNER), BF16 if finish else F32),
        scratch_shapes=[pltpu.VMEM((SSD_STATE, SSD_INNER), F32)],
        compiler_params=_cparams(2),
        name="ssd_scan_bwd" if reverse else "ssd_scan_fwd",
    )(*args)


def _lin_kernel(*refs, mode, heads, dk, dv):
    n_in = 6 if mode == "gla" else 5
    ins = [refs[:n_in], refs[n_in:2 * n_in]]
    o_refs = refs[2 * n_in:2 * n_in + 2]
    st_ref = refs[2 * n_in + 2]
    c = LIN_CHUNK

    @pl.when(pl.program_id(1) == 0)
    def _():
        st_ref[...] = jnp.zeros_like(st_ref)

    tb = o_refs[0].shape[1]
    nc = tb // c
    hpt = LANE // dk
    n_tiles = heads // hpt
    span = n_tiles if mode == "gla" else 1
    gw = span * LANE
    n_groups = n_tiles // span
    hpg = hpt * span
    zeros = jnp.zeros((c, LANE), BF16)
    ri = lax.broadcasted_iota(jnp.int32, (tb, tb), 0)
    ci = lax.broadcasted_iota(jnp.int32, (tb, tb), 1)
    c_shift = c.bit_length() - 1
    same_chunk = jnp.right_shift(ri, c_shift) == jnp.right_shift(ci, c_shift)

    def chunk_blocks(a):
        cols = []
        for b in range(nc):
            cols.append(jnp.concatenate(
                [a[cc * c:(cc + 1) * c] if cc == b else zeros for cc in range(nc)], axis=0))
        return jnp.concatenate(cols, axis=1)

    def tile_cols(h):
        t = (h // hpt) % span
        return slice(t * LANE, (t + 1) * LANE)

    def make_stream(direction):
        if mode == "gla":
            q_ref, k_ref, v_ref, aux_ref, p1_ref, p2_ref = ins[direction]
        else:
            q_ref, v_ref, aux_ref, p1_ref, p2_ref = ins[direction]
            k_ref = None
        o_ref = o_refs[direction]
        reverse = direction == 1
        chunks = range(nc - 1, -1, -1) if reverse else range(nc)
        last = 0 if reverse else c - 1
        if reverse:
            bd_mask = jnp.where(same_chunk, ci - ri, -1) >= 0
        else:
            bd_mask = jnp.where(same_chunk, ci - ri, 1) <= 0
        tri = _tri_mask(c, reverse).astype(BF16)
        tri2 = jnp.concatenate([tri, tri], axis=1)
        groups, work = {}, {}

        def prep(g):
            ls = slice(g * gw, (g + 1) * gw)
            if mode == "gla":
                q = q_ref[0, :, ls].astype(F32) * (dk ** -0.5)
                k = k_ref[0, :, ls].astype(F32)
                off = 2 * SSD_HEADS + direction * GLA_GATE_RANK
                lr = aux_ref[0][:, off:off + GLA_GATE_RANK].astype(BF16)
                lg = _log_sigmoid(_dot(lr, p1_ref[:, ls]) + p2_ref[:, ls]) * (1.0 / GLA_GATE_NORM)
            else:
                q = _silu(q_ref[0, :, ls].astype(F32))
                f_raw = aux_ref[0, :, ls].astype(F32)
                e = jnp.exp2(jnp.abs(f_raw) * _NEG_LOG2E)
                r = 1.0 / (1.0 + e)
                forget = p1_ref[:, ls] + p2_ref[:, ls] * jnp.where(f_raw >= 0.0, r, e * r)
                lg = jnp.log(forget)
                k = 1.0 - forget
            gcum = jnp.concatenate([_cumsum_rows(tri2, lg[cc * c:(cc + 1) * c]) for cc in range(nc)], axis=0)
            e_last = [jnp.exp(gcum[cc * c + last:cc * c + last + 1]) for cc in range(nc)]
            e_rows = jnp.concatenate([jnp.broadcast_to(e, (c, gw)) for e in e_last], axis=0)
            e_gcum = jnp.exp(gcum)
            q_decf = q * e_gcum
            if hpt > 1:
                head_of_lane = jnp.bitwise_and(jnp.right_shift(
                    lax.broadcasted_iota(jnp.int32, (1, gw), 1), dk.bit_length() - 1), hpt - 1)
                q_dec = [jnp.where(head_of_lane == r, q_decf, 0.0).astype(BF16) for r in range(hpt)]
            else:
                q_dec = [q_decf.astype(BF16)]
            k_invf = k * (1.0 / e_gcum)
            groups[g] = dict(q_dec=q_dec, k_inv=k_invf.astype(BF16), k_end=(k_invf * e_rows).astype(BF16),
                             e_last=e_last)

        def products(h):
            gp, ts = groups[h // hpg], tile_cols(h)
            qh = gp["q_dec"][h % hpt][:, ts]
            vh = v_ref[0, :, h * dv:(h + 1) * dv]
            work[h] = dict(
                qh=qh, vh=vh, scores=_dot(qh, gp["k_inv"][:, ts], _NT),
                kv_t=_dot(vh, chunk_blocks(gp["k_end"][:, ts]), _TN))

        def mask_and_chain(h):
            w = work[h]
            e_last = groups[h // hpg]["e_last"]
            w["att"] = jnp.where(bd_mask, w.pop("scores"), 0.0).astype(BF16)
            st = st_ref[direction, h]
            used = [None] * nc
            for cc in chunks:
                used[cc] = st.astype(BF16)
                st = st * e_last[cc][:, tile_cols(h)] + w["kv_t"][:, cc * LANE:(cc + 1) * LANE]
            st_ref[direction, h] = st
            w["used"] = used
            del w["kv_t"]

        def outputs(h):
            w = work[h]
            o_state = jnp.concatenate(
                [_dot(w["qh"][cc * c:(cc + 1) * c], w["used"][cc], _NT) for cc in range(nc)], axis=0)
            w["o"] = _dot(w["att"], w["vh"]) + o_state

        def emit(h):
            o_ref[0, :, h * dv:(h + 1) * dv] = work.pop(h)["o"]

        return dict(prep=prep, products=products, mask_and_chain=mask_and_chain, outputs=outputs, emit=emit)

    streams = [make_stream(0), make_stream(1)]

    def heads_of(g):
        return range(g * hpg, (g + 1) * hpg) if 0 <= g < n_groups else ()

    lag = 2 if n_groups > 1 else 1
    for it in range(n_groups + 4 * lag):
        for stage, delay in (("products", lag), ("outputs", 3 * lag)):
            for st in streams:
                for h in heads_of(it - delay):
                    st[stage](h)
        if it < n_groups:
            for st in streams:
                st["prep"](it)
        for stage, delay in (("mask_and_chain", 2 * lag), ("emit", 4 * lag)):
            for st in streams:
                for h in heads_of(it - delay):
                    st[stage](h)


def _lin_scan(mode, proj, cols, params, n_lat, aux=None):
    bsz, tall, _ = proj.shape
    if mode == "gla":
        heads, dk, dv = GLA_HEADS, GLA_DK, GLA_DV
    else:
        heads, dk, dv = HGRN_HEADS, HGRN_DK, HGRN_DV
    kw, vw = heads * dk, heads * dv
    tb = SCAN_BLOCK
    nb, n_lat_blocks = tall // tb, n_lat // tb

    def const2(shape):
        return pl.BlockSpec(shape, lambda b, s: (0, 0))

    in_specs, args = [], []
    for direction in (0, 1):
        def tok(col, reverse=direction == 1):
            return lambda b, s: (b, _scan_block(s, n_lat_blocks, nb, reverse), col)
        p1, p2 = params[direction]
        if mode == "gla":
            in_specs += [pl.BlockSpec((1, tb, kw), tok(cols["q"])),
                         pl.BlockSpec((1, tb, kw), tok(cols["k"])),
                         pl.BlockSpec((1, tb, vw), tok(cols["v"])),
                         pl.BlockSpec((1, tb, LANE), tok(0)),
                         const2(p1.shape), const2(p2.shape)]
            args += [proj, proj, proj, aux, p1, p2]
        else:
            in_specs += [pl.BlockSpec((1, tb, kw), tok(cols["q"])),
                         pl.BlockSpec((1, tb, vw), tok(cols["v"])),
                         pl.BlockSpec((1, tb, kw), tok(cols["aux"] + direction)),
                         const2(p1.shape), const2(p2.shape)]
            args += [proj, proj, proj, p1, p2]
    out_specs = [pl.BlockSpec((1, tb, vw), lambda b, s: (b, _scan_block(s, n_lat_blocks, nb, False), 0)),
                 pl.BlockSpec((1, tb, vw), lambda b, s: (b, _scan_block(s, n_lat_blocks, nb, True), 0))]
    return pl.pallas_call(
        functools.partial(_lin_kernel, mode=mode, heads=heads, dk=dk, dv=dv),
        grid=(bsz, nb),
        in_specs=in_specs,
        out_specs=out_specs,
        out_shape=[jax.ShapeDtypeStruct((bsz, tall, vw), F32)] * 2,
        scratch_shapes=[pltpu.VMEM((2, heads, dv, LANE), F32)],
        compiler_params=_cparams(2),
        name=f"{mode}_scan",
    )(*args)


def _s5_kernel(uf_ref, ub_ref, bmat_ref, lre_ref, lim_ref, cmat_ref, of_ref, ob_ref, h_ref, ut_ref, yt_ref, st_ref,
               *, bsz):
    steps = S5_CHUNK
    n_slabs = bmat_ref.shape[1]
    sw = bmat_ref.shape[3] // 2
    width = n_slabs * LANE
    re_cols = [slice(2 * s * sw, (2 * s + 1) * sw) for s in range(n_slabs)]
    im_cols = [slice((2 * s + 1) * sw, (2 * s + 2) * sw) for s in range(n_slabs)]
    both = [slice(2 * s * sw, (2 * s + 2) * sw) for s in range(n_slabs)]

    @pl.when(pl.program_id(0) == 0)
    def _():
        st_ref[...] = jnp.zeros_like(st_ref)

    def inputs(d, u_ref):
        for b in range(bsz):
            for s in range(n_slabs):
                ut_ref[d, s, pl.ds(b, steps, stride=bsz), :] = (
                    u_ref[:, b * width + s * LANE:b * width + (s + 1) * LANE])
        for s in range(n_slabs):
            h_ref[d, :, both[s]] = _dot(ut_ref[d, s].astype(BF16), bmat_ref[d, s])

    def scan(d):
        lam_re = [jnp.broadcast_to(lre_ref[d:d + 1, s * sw:(s + 1) * sw], (bsz, sw)) for s in range(n_slabs)]
        lam_im = [jnp.broadcast_to(lim_ref[d:d + 1, s * sw:(s + 1) * sw], (bsz, sw)) for s in range(n_slabs)]
        hr = [st_ref[d, :, re_cols[s]] for s in range(n_slabs)]
        hi = [st_ref[d, :, im_cols[s]] for s in range(n_slabs)]
        for tt in range(steps):
            t = tt if d == 0 else steps - 1 - tt
            rows = slice(t * bsz, (t + 1) * bsz)
            for s in range(n_slabs):
                nr = lam_re[s] * hr[s] - lam_im[s] * hi[s] + h_ref[d, rows, re_cols[s]]
                ni = lam_re[s] * hi[s] + lam_im[s] * hr[s] + h_ref[d, rows, im_cols[s]]
                h_ref[d, rows, re_cols[s]] = nr
                h_ref[d, rows, im_cols[s]] = ni
                hr[s], hi[s] = nr, ni
        for s in range(n_slabs):
            st_ref[d, :, re_cols[s]] = hr[s]
            st_ref[d, :, im_cols[s]] = hi[s]

    def outputs(d):
        for s in range(n_slabs):
            yt_ref[d, s] = _dot(h_ref[d, :, both[s]].astype(BF16), cmat_ref[s])

    def emit(d, o_ref):
        for b in range(bsz):
            for s in range(n_slabs):
                o_ref[:, b * width + s * LANE:b * width + (s + 1) * LANE] = (
                    yt_ref[d, s, pl.ds(b, steps, stride=bsz), :])

    inputs(0, uf_ref)
    inputs(1, ub_ref)
    scan(0)
    outputs(0)
    scan(1)
    outputs(1)
    emit(0, of_ref)
    emit(1, ob_ref)


def _s5_scan(u_t, bmat, lam_re, lam_im, cmat, bsz, n_lat):
    tall, bw = u_t.shape
    width = bw // bsz
    nch, n_lat_chunks = tall // S5_CHUNK, n_lat // S5_CHUNK
    n_state = lam_re.shape[-1]
    rows = S5_CHUNK * bsz
    fwd = lambda s: (_scan_block(s, n_lat_chunks, nch, False), 0)
    bwd = lambda s: (_scan_block(s, n_lat_chunks, nch, True), 0)
    whole = lambda a: pl.BlockSpec(a.shape, lambda s: (0,) * a.ndim)
    lam_re, lam_im = lam_re.reshape(2, n_state), lam_im.reshape(2, n_state)
    return pl.pallas_call(
        functools.partial(_s5_kernel, bsz=bsz),
        grid=(nch,),
        in_specs=[pl.BlockSpec((S5_CHUNK, bw), fwd), pl.BlockSpec((S5_CHUNK, bw), bwd),
                  whole(bmat), whole(lam_re), whole(lam_im), whole(cmat)],
        out_specs=[pl.BlockSpec((S5_CHUNK, bw), fwd), pl.BlockSpec((S5_CHUNK, bw), bwd)],
        out_shape=[jax.ShapeDtypeStruct((tall, bw), F32)] * 2,
        scratch_shapes=[pltpu.VMEM((2, rows, 2 * n_state), F32),
                        pltpu.VMEM((2, width // LANE, rows, LANE), F32),
                        pltpu.VMEM((2, width // LANE, rows, LANE), F32),
                        pltpu.VMEM((2, bsz, 2 * n_state), F32)],
        compiler_params=_cparams(1),
        name="s5_scan",
    )(u_t, u_t, bmat, lam_re, lam_im, cmat)


def _lin_finish(of_ref, ob_ref, gate_ref, nw_ref):
    o = of_ref[0] + ob_ref[0]
    dv = nw_ref.shape[1]
    parts = [_rms(o[:, h * dv:(h + 1) * dv], nw_ref[...]) for h in range(o.shape[1] // dv)]
    return (jnp.concatenate(parts, axis=1) * _silu(gate_ref[0].astype(F32))).astype(BF16)


def _lin_finish_specs(o_dirs, proj, gate_col, norm_w, tm):
    vw = o_dirs[0].shape[2]
    specs = [pl.BlockSpec((1, tm, vw), lambda b, i: (b, i, 0)),
             pl.BlockSpec((1, tm, vw), lambda b, i: (b, i, 0)),
             pl.BlockSpec((1, tm, vw), lambda b, i: (b, i, gate_col)),
             pl.BlockSpec(norm_w.shape, lambda b, i: (0, 0))]
    return specs, [o_dirs[0], o_dirs[1], proj, norm_w]


def _out0_kernel(x_ref, a_ref, of_ref, ob_ref, gate_ref, nw_ref, wa_ref, wb_ref, gl_ref, gc_ref, o_ref,
                 *, tm, n_lat):
    i = pl.program_id(1)
    o = _dot(a_ref[0], wa_ref[...]) + _dot(_lin_finish(of_ref, ob_ref, gate_ref, nw_ref), wb_ref[...])
    o_ref[0] = x_ref[0] + _row_select(i, tm, n_lat, gc_ref[...], gl_ref[0]) * o


def _out_proj0(x_all, mix_a, lin_finish, w_a, w_b, mod_l, mod_c, n_lat, n_rows, tm):
    bsz, _, d = x_all.shape
    tall = n_rows
    lin_specs, lin_args = _lin_finish_specs(*lin_finish, tm)
    return pl.pallas_call(
        functools.partial(_out0_kernel, tm=tm, n_lat=n_lat),
        grid=(bsz, tall // tm),
        in_specs=[pl.BlockSpec((1, tm, d), lambda b, i: (b, i, 0)),
                  pl.BlockSpec((1, tm, mix_a.shape[2]), lambda b, i: (b, i, 0))] + lin_specs + [
                  pl.BlockSpec(w_a.shape, lambda b, i: (0, 0)),
                  pl.BlockSpec(w_b.shape, lambda b, i: (0, 0))] + _mod_specs(d, (2,), 2),
        out_specs=pl.BlockSpec((1, tm, d), lambda b, i: (b, i, 0)),
        out_shape=jax.ShapeDtypeStruct((bsz, tall, d), F32),
        compiler_params=_cparams(2),
        name="out_proj_even",
    )(x_all, mix_a, *lin_args, w_a, w_b, mod_l, mod_c)


def _gelu_tanh(x):
    return 0.5 * x * (1.0 + jnp.tanh(math.sqrt(2.0 / math.pi) * (x + 0.044715 * (x * x * x))))


def _out1_kernel(x_ref, of_ref, ob_ref, gate_ref, nw_ref, yf_ref, yb_ref, u_ref, dsk_ref, gw_ref, gb_ref,
                 wa_ref, wb_ref, gl_ref, gc_ref, o_ref, *, tm, n_lat):
    i = pl.program_id(1)
    y = _gelu_tanh(yf_ref[...] + yb_ref[...] + dsk_ref[...] * u_ref[...])
    glu = _dot(y.astype(BF16), gw_ref[...]) + gb_ref[...]
    y = y * _sigmoid(glu)
    o = (_dot(_lin_finish(of_ref, ob_ref, gate_ref, nw_ref), wa_ref[...])
         + _dot(y.astype(BF16), wb_ref[...]))
    o_ref[0] = x_ref[0] + _row_select(i, tm, n_lat, gc_ref[...], gl_ref[0]) * o


def _out_proj1(x_all, lin_finish, y_dirs, u_t, d_skip, glu_w, glu_b, w_a, w_b, mod_l, mod_c, n_lat, n_rows, tm):
    bsz, _, d = x_all.shape
    tall = n_rows
    width = d_skip.shape[1]
    (y_f, y_b), u2 = y_dirs, u_t
    lin_specs, lin_args = _lin_finish_specs(*lin_finish, tm)
    return pl.pallas_call(
        functools.partial(_out1_kernel, tm=tm, n_lat=n_lat),
        grid=(bsz, tall // tm),
        in_specs=[pl.BlockSpec((1, tm, d), lambda b, i: (b, i, 0))] + lin_specs + [
                  pl.BlockSpec((tm, width), lambda b, i: (i, b)),
                  pl.BlockSpec((tm, width), lambda b, i: (i, b)),
                  pl.BlockSpec((tm, width), lambda b, i: (i, b)),
                  pl.BlockSpec((1, width), lambda b, i: (0, 0)),
                  pl.BlockSpec(glu_w.shape, lambda b, i: (0, 0)),
                  pl.BlockSpec((1, width), lambda b, i: (0, 0)),
                  pl.BlockSpec(w_a.shape, lambda b, i: (0, 0)),
                  pl.BlockSpec(w_b.shape, lambda b, i: (0, 0))] + _mod_specs(d, (2,), 2),
        out_specs=pl.BlockSpec((1, tm, d), lambda b, i: (b, i, 0)),
        out_shape=jax.ShapeDtypeStruct((bsz, tall, d), F32),
        compiler_params=_cparams(2),
        name="out_proj_odd",
    )(x_all, *lin_args, y_f, y_b, u2, d_skip, glu_w, glu_b, w_a, w_b, mod_l, mod_c)


def _mlp_kernel(*refs, tm, n_lat, final):
    if final:
        (x_ref, nw_ref, shl_ref, shc_ref, scl_ref, scc_ref, gl_ref, gc_ref, w1_ref, w2_ref, fw_ref,
         o_ref, h_scr, acc_scr) = refs
    else:
        (x_ref, nw_ref, shl_ref, shc_ref, scl_ref, scc_ref, gl_ref, gc_ref, w1_ref, w2_ref,
         o_ref, h_scr, acc_scr) = refs
    i = pl.program_id(1)
    j = pl.program_id(2)

    @pl.when(j == 0)
    def _():
        _store_norm_modulated(h_scr, x_ref[0], nw_ref[...], shl_ref[0], shc_ref[...], scl_ref[0], scc_ref[...],
                              i, tm, n_lat)
        acc_scr[...] = jnp.zeros_like(acc_scr)

    def partial_product():
        a = jnp.maximum(_dot(h_scr[...], w1_ref[...]), 0.0)
        return _dot((a * a).astype(BF16), w2_ref[...])

    last_j = pl.num_programs(2) - 1

    @pl.when(j < last_j)
    def _():
        acc_scr[...] += partial_product()

    @pl.when(j == last_j)
    def _():
        out = x_ref[0] + _row_select(i, tm, n_lat, gc_ref[...], gl_ref[0]) * (acc_scr[...] + partial_product())
        if final:
            out = _rms(out, fw_ref[...])
        o_ref[0] = out


def _mlp(x_all, norm_w, mod_l, mod_c, w1, w2, final_w, n_lat, tm, tf):
    bsz, tall, d = x_all.shape
    ff = w1.shape[1]
    final = final_w is not None
    in_specs = [pl.BlockSpec((1, tm, d), lambda b, i, j: (b, i, 0)),
                pl.BlockSpec((1, d), lambda b, i, j: (0, 0))]
    in_specs += _mod_specs(d, (3, 4, 5), 3)
    in_specs += [pl.BlockSpec((d, tf), lambda b, i, j: (0, j)),
                 pl.BlockSpec((tf, d), lambda b, i, j: (j, 0))]
    args = [x_all, norm_w.reshape(1, d)] + [mod_l, mod_c] * 3 + [w1, w2]
    if final:
        in_specs.append(pl.BlockSpec((1, d), lambda b, i, j: (0, 0)))
        args.append(final_w.reshape(1, d))
    return pl.pallas_call(
        functools.partial(_mlp_kernel, tm=tm, n_lat=n_lat, final=final),
        grid=(bsz, tall // tm, ff // tf),
        in_specs=in_specs,
        out_specs=pl.BlockSpec((1, tm, d), lambda b, i, j: (b, i, 0)),
        out_shape=jax.ShapeDtypeStruct((bsz, tall, d), F32),
        scratch_shapes=[pltpu.VMEM((tm, d), BF16), pltpu.VMEM((tm, d), F32)],
        compiler_params=_cparams(3),
        name="sq_relu_mlp",
    )(*args)


def _even_in_weight(w_in):
    sizes = (SSD_INNER, SSD_INNER + 2 * SSD_BC, 2 * SSD_HEADS, GLA_KEY, GLA_KEY, GLA_VAL,
             2 * GLA_GATE_RANK, GLA_VAL)
    offs = [0]
    for s in sizes:
        offs.append(offs[-1] + s)
    z, xbc, dt, q, k, v, lr, r = (w_in[:, offs[n]:offs[n + 1]] for n in range(8))
    pad = jnp.zeros((w_in.shape[0], LANE - dt.shape[1] - lr.shape[1]), w_in.dtype)
    main = jnp.concatenate([xbc, z, v, r, q, k], axis=1).astype(BF16)
    aux = jnp.concatenate([dt, lr, pad], axis=1).astype(BF16)
    return main, aux


def _s5_params(a_re, a_im, log_dt, b_re, b_im, c_re, c_im):
    delta = jnp.exp(log_dt.astype(F32))[..., None]
    mag = jnp.exp(a_re * delta)
    lbar_re, lbar_im = mag * jnp.cos(a_im * delta), mag * jnp.sin(a_im * delta)
    den = a_re * a_re + a_im * a_im
    zr = ((lbar_re - 1.0) * a_re + lbar_im * a_im) / den
    zi = (lbar_im * a_re - (lbar_re - 1.0) * a_im) / den
    bb_re = zr[..., None] * b_re - zi[..., None] * b_im
    bb_im = zr[..., None] * b_im + zi[..., None] * b_re
    n_slabs = S5_GROUPS // S5_SLAB
    eye = jnp.eye(S5_SLAB, dtype=F32)
    sw = S5_SLAB * S5_STATE

    def block_in(bb):
        bb = bb.reshape(2, n_slabs, S5_SLAB, S5_STATE, S5_GROUP)
        return jnp.einsum("dsgpc,gh->dsgchp", bb, eye).reshape(2, n_slabs, LANE, sw)

    def block_out(cc):
        cc = cc.reshape(n_slabs, S5_SLAB, S5_GROUP, S5_STATE)
        return jnp.einsum("sgcp,gh->sgphc", cc, eye).reshape(n_slabs, sw, LANE)

    bmat = jnp.concatenate([block_in(bb_re), block_in(bb_im)], axis=3).astype(BF16)
    cmat = jnp.concatenate([block_out(c_re), -block_out(c_im)], axis=1).astype(BF16)
    return (bmat, lbar_re.reshape(2, 1, S5_NSTATE), lbar_im.reshape(2, 1, S5_NSTATE), cmat)


def _layer_even(x_all, mod_l, mod_c, n_lat, tm, n_rows_out, tm_out, norm1_w, w_in, conv_w, conv_b, dt_bias,
                a_log, d_skip, ssd_norm_w, gate_w, gate_b, gla_norm_w, w_out):
    w, w_aux = _even_in_weight(w_in)
    n = w.shape[1]
    proj, aux = _project(x_all, norm1_w, mod_l, mod_c, w, w_aux, n_lat, tm, _largest_divisor(n, 2 * LANE, PROJ_TN),
                         extra_token_major=False)
    n_xbc = SSD_INNER + 2 * SSD_BC
    c_z, c_v, c_r = n_xbc // SSD_INNER, n_xbc // GLA_VAL + 1, n_xbc // GLA_VAL + 2
    c_q = (n_xbc + 3 * SSD_INNER) // GLA_KEY
    xbc = _conv_silu(proj, conv_w, conv_b, n_lat, n_xbc)

    neg_a = -jnp.exp(a_log.astype(F32))
    dt_bias = dt_bias.astype(F32)
    d_wide = jnp.repeat(d_skip.astype(F32), SSD_HEADDIM).reshape(1, SSD_INNER)
    y_f = _ssd_scan(xbc, proj, aux, dt_bias, neg_a, c_z, n_lat, 0)
    y_mix = _ssd_scan(xbc, proj, aux, dt_bias, neg_a, c_z, n_lat, 1,
                      (y_f, d_wide, ssd_norm_w.reshape(1, SSD_INNER)))

    cols = {"q": c_q, "k": c_q + 1, "v": c_v, "gate": c_r}
    gparams = [(gate_w[d].astype(BF16), gate_b[d].reshape(1, GLA_KEY).astype(F32)) for d in range(2)]
    o_dirs = _lin_scan("gla", proj, cols, gparams, n_lat, aux=aux)
    gla_finish = (o_dirs, proj, c_r, gla_norm_w.astype(F32).reshape(1, GLA_DV))

    w_out = w_out.astype(BF16)
    return _out_proj0(x_all, y_mix, gla_finish, w_out[:SSD_INNER], w_out[SSD_INNER:], mod_l, mod_c, n_lat,
                      n_rows_out, tm_out)


def _layer_odd(x_all, mod_l, mod_c, n_lat, tm, n_rows_out, tm_out, norm1_w, w_in, lb, hgrn_norm_w, a_re, a_im,
               log_dt, b_re, b_im, c_re, c_im, d_skip, glu_w, glu_b, w_out):
    bsz = x_all.shape[0]
    n_main = 5 * HGRN_WIDTH
    w_main = w_in[:, :n_main].astype(BF16)
    w_u = w_in[:, n_main:].astype(BF16)
    proj, u_t = _project(x_all, norm1_w, mod_l, mod_c, w_main, w_u, n_lat, tm,
                         _largest_divisor(n_main, 2 * LANE, PROJ_TN))

    lb = lb.astype(F32).reshape(2, 1, HGRN_WIDTH)
    cols = {"q": 0, "v": 1, "aux": 2, "gate": 4}
    o_dirs = _lin_scan("hgrn", proj, cols, [(lb[d], 1.0 - lb[d]) for d in range(2)], n_lat)
    hgrn_finish = (o_dirs, proj, cols["gate"], hgrn_norm_w.astype(F32).reshape(1, HGRN_DV))

    bmat, lam_re, lam_im, cmat = _s5_params(a_re.astype(F32), a_im.astype(F32), log_dt, b_re.astype(F32),
                                            b_im.astype(F32), c_re.astype(F32), c_im.astype(F32))
    y_dirs = _s5_scan(u_t, bmat, lam_re, lam_im, cmat, bsz, n_lat)

    w_out = w_out.astype(BF16)
    return _out_proj1(x_all, hgrn_finish, y_dirs, u_t, d_skip.astype(F32).reshape(1, S5_WIDTH),
                      glu_w.astype(BF16), glu_b.astype(F32).reshape(1, S5_WIDTH),
                      w_out[:HGRN_WIDTH], w_out[HGRN_WIDTH:], mod_l, mod_c, n_lat, n_rows_out, tm_out)


def kernel(x, c, ctx, c_ctx, ada_w, ada_b, norm1_w, norm2_w, ssd_gla_w_in, ssd_conv_w, ssd_conv_b, ssd_dt_bias, ssd_a_log, ssd_d, ssd_norm_w, gla_gate_w, gla_gate_b, gla_norm_w, ssd_gla_w_out, hgrn_s5_w_in, hgrn_lb_logits, hgrn_norm_w, s5_a_re, s5_a_im, s5_log_dt, s5_b_re, s5_b_im, s5_c_re, s5_c_im, s5_d, s5_glu_w, s5_glu_b, hgrn_s5_w_out, mlp_w1, mlp_w2, final_norm_w):
    bsz, n_lat, d = x.shape
    ctx_len = ctx.shape[1]
    depth = ada_w.shape[0]
    tall = n_lat + ctx_len
    assert bsz % 8 == 0 and ctx_len % SCAN_BLOCK == 0 and n_lat % SCAN_BLOCK == 0 and n_lat % GRID_W == 0
    tf = 1024

    n_rows = -(-(bsz + 1) // 8) * 8
    cvec = jnp.concatenate([c, c_ctx[None, :], jnp.zeros((n_rows - bsz - 1, d), c.dtype)], axis=0)
    mod = _modulation(cvec.astype(F32), ada_w, ada_b)

    p_lb = jax.nn.softmax(hgrn_lb_logits.astype(F32), axis=0)
    lb_all = jnp.cumsum(p_lb, axis=0) - p_lb[0]

    x_all = jnp.concatenate([x, ctx], axis=1).astype(F32)
    tm_all = _largest_divisor(tall, 16, 1056)
    for layer in range(depth):
        j = layer // 2
        last = layer == depth - 1
        n_rows = n_lat if last else tall
        tm_out = _largest_divisor(n_rows, 16, 1056)
        mod_l = mod[layer, :bsz].reshape(bsz, 1, N_MOD * d)
        mod_c = mod[layer, bsz:bsz + 1]
        if layer % 2 == 0:
            x_all = _layer_even(x_all, mod_l, mod_c, n_lat, tm_all, n_rows, tm_out, norm1_w[layer],
                                ssd_gla_w_in[j], ssd_conv_w[j], ssd_conv_b[j], ssd_dt_bias[j], ssd_a_log[j],
                                ssd_d[j], ssd_norm_w[j], gla_gate_w[j], gla_gate_b[j], gla_norm_w[j],
                                ssd_gla_w_out[j])
        else:
            x_all = _layer_odd(x_all, mod_l, mod_c, n_lat, tm_all, n_rows, tm_out, norm1_w[layer],
                               hgrn_s5_w_in[j], lb_all[layer], hgrn_norm_w[j], s5_a_re[j], s5_a_im[j],
                               s5_log_dt[j], s5_b_re[j], s5_b_im[j], s5_c_re[j], s5_c_im[j], s5_d[j],
                               s5_glu_w[j], s5_glu_b[j], hgrn_s5_w_out[j])
        x_all = _mlp(x_all, norm2_w[layer], mod_l, mod_c, mlp_w1[layer].astype(BF16),
                     mlp_w2[layer].astype(BF16), final_norm_w if last else None, n_lat, tm_out, tf)
    return x_all.astype(x.dtype)
```

```python
import functools
import math

import jax
import jax.numpy as jnp
from jax import lax
from jax.experimental import pallas as pl
from jax.experimental.pallas import tpu as pltpu

F32 = jnp.float32
BF16 = jnp.bfloat16

GRID_W = 64
NORM_EPS = 1e-6
N_MOD = 6
SSD_HEADDIM = 64
SSD_HEADS = 16
SSD_GROUPS = 4
SSD_STATE = 128
SSD_CHUNK = 128
GLA_HEADS = 8
GLA_DK = 64
GLA_DV = 128
GLA_GATE_RANK = 16
GLA_GATE_NORM = 16.0
HGRN_HEADS = 8
HGRN_DK = 128
HGRN_DV = 128
S5_GROUP = 16
S5_GROUPS = 24
S5_STATE = 64
LIN_CHUNK = 64

SSD_INNER = SSD_HEADS * SSD_HEADDIM
SSD_BC = SSD_GROUPS * SSD_STATE
GLA_KEY = GLA_HEADS * GLA_DK
GLA_VAL = GLA_HEADS * GLA_DV
HGRN_WIDTH = HGRN_HEADS * HGRN_DV
S5_WIDTH = S5_GROUPS * S5_GROUP
S5_NSTATE = S5_GROUPS * S5_STATE

VMEM_LIMIT_BYTES = 56 * 1024 * 1024
LANE = 128
PROJ_TN = 2560
SCAN_BLOCK = 256
S5_CHUNK = 128
S5_SLAB = LANE // S5_GROUP


def _cparams(n_axes):
    return pltpu.CompilerParams(dimension_semantics=("arbitrary",) * n_axes,
                                vmem_limit_bytes=VMEM_LIMIT_BYTES)


def _largest_divisor(n, multiple, cap):
    best = None
    for d in range(multiple, min(n, cap) + 1, multiple):
        if n % d == 0:
            best = d
    assert best is not None, (n, multiple, cap)
    return best


_NEG_LOG2E = -1.4426950408889634


def _sigmoid(x):
    return 1.0 / (1.0 + jnp.exp2(x * _NEG_LOG2E))


def _silu(x):
    return x * _sigmoid(x)


def _softplus(x):
    return jnp.maximum(x, 0.0) + jnp.log1p(jnp.exp(-jnp.abs(x)))


def _log_sigmoid(x):
    return -_softplus(-x)


def _rms(x, w):
    return x * lax.rsqrt(jnp.mean(x * x, axis=-1, keepdims=True) + NORM_EPS) * w


def _dot(a, b, dims=(((1,), (0,)), ((), ())), precision=None):
    return lax.dot_general(a, b, dims, precision=precision, preferred_element_type=F32)


def _split3(v):
    hi = v.astype(BF16)
    r1 = v - hi.astype(F32)
    mid = r1.astype(BF16)
    lo = (r1 - mid.astype(F32)).astype(BF16)
    return hi, mid, lo


def _tri3(mask):
    tri = mask.astype(BF16)
    return jnp.concatenate([tri, tri, tri], axis=1)


def _cumsum_rows(tri2, v):
    hi = v.astype(BF16)
    lo = (v - hi.astype(F32)).astype(BF16)
    return _dot(tri2, jnp.concatenate([hi, lo], axis=0))


_NT = (((1,), (1,)), ((), ()))
_TN = (((0,), (0,)), ((), ()))
_TT = (((0,), (1,)), ((), ()))


def _mod_kernel(c_ref, w_ref, b_ref, o_ref):
    a = _silu(c_ref[...]).astype(BF16)
    o_ref[0] = _dot(a, w_ref[0].astype(BF16)) + b_ref[0]


def _modulation(cvec, ada_w, ada_b):
    depth, d, n = ada_w.shape
    rows = cvec.shape[0]
    tn = _largest_divisor(n, LANE, 1024)
    return pl.pallas_call(
        _mod_kernel,
        grid=(depth, n // tn),
        in_specs=[pl.BlockSpec((rows, d), lambda l, j: (0, 0)),
                  pl.BlockSpec((1, d, tn), lambda l, j: (l, 0, j)),
                  pl.BlockSpec((1, 1, tn), lambda l, j: (l, 0, j))],
        out_specs=pl.BlockSpec((1, rows, tn), lambda l, j: (l, 0, j)),
        out_shape=jax.ShapeDtypeStruct((depth, rows, n), F32),
        compiler_params=_cparams(2),
        name="adaln_mod",
    )(cvec, ada_w, ada_b.reshape(depth, 1, n))


def _row_select(i, tm, n_lat, ctx_val, lat_val):
    row = i * tm + lax.broadcasted_iota(jnp.int32, (tm, 1), 0)
    return jnp.where(row >= n_lat, ctx_val, lat_val)


def _store_norm_modulated(h_ref, x, nw, shift_l, shift_c, scale_l, scale_c, i, tm, n_lat):
    xn = x * lax.rsqrt(jnp.mean(x * x, axis=-1, keepdims=True) + NORM_EPS)
    gain = _row_select(i, tm, n_lat, nw * (1.0 + scale_c), nw * (1.0 + scale_l))
    h_ref[...] = (xn * gain + _row_select(i, tm, n_lat, shift_c, shift_l)).astype(h_ref.dtype)


def _mod_specs(d, cols, n_grid_axes):
    specs = []
    for k in cols:
        if n_grid_axes == 2:
            specs.append(pl.BlockSpec((1, 1, d), lambda b, i, k=k: (b, 0, k)))
            specs.append(pl.BlockSpec((1, d), lambda b, i, k=k: (0, k)))
        else:
            specs.append(pl.BlockSpec((1, 1, d), lambda b, i, j, k=k: (b, 0, k)))
            specs.append(pl.BlockSpec((1, d), lambda b, i, j, k=k: (0, k)))
    return specs


def _proj_kernel(*refs, tm, n_lat, has_extra, silu_cols):
    if has_extra:
        (x_ref, nw_ref, shl_ref, shc_ref, scl_ref, scc_ref, w_ref, wx_ref, o_ref, ox_ref, h_scr) = refs
    else:
        (x_ref, nw_ref, shl_ref, shc_ref, scl_ref, scc_ref, w_ref, o_ref, h_scr) = refs
    i = pl.program_id(1)
    j = pl.program_id(2)

    @pl.when(j == 0)
    def _():
        _store_norm_modulated(h_scr, x_ref[0], nw_ref[...], shl_ref[0], shc_ref[...], scl_ref[0], scc_ref[...],
                              i, tm, n_lat)
        if has_extra:
            ox_ref[...] = _dot(h_scr[...], wx_ref[...]).reshape(ox_ref.shape)

    for jj, ranges in enumerate(silu_cols):
        @pl.when(j == jj)
        def _(ranges=ranges):
            o = _dot(h_scr[...], w_ref[...])
            if not ranges:
                o_ref[0] = o.astype(o_ref.dtype)
            edge = 0
            for start, stop in ranges:
                if start > edge:
                    o_ref[0, :, edge:start] = o[:, edge:start].astype(o_ref.dtype)
                o_ref[0, :, start:stop] = _silu(o[:, start:stop]).astype(o_ref.dtype)
                edge = stop
            if ranges and edge < o.shape[1]:
                o_ref[0, :, edge:] = o[:, edge:].astype(o_ref.dtype)


def _project(x_all, norm_w, mod_l, mod_c, w, w_extra, n_lat, tm, tn, extra_token_major=True, silu_ranges=()):
    bsz, tall, d = x_all.shape
    n = w.shape[1]
    has_extra = w_extra is not None
    silu_cols = tuple(
        tuple((max(a, jj * tn) - jj * tn, min(b, (jj + 1) * tn) - jj * tn)
              for a, b in sorted(silu_ranges) if a < (jj + 1) * tn and b > jj * tn)
        for jj in range(n // tn))
    in_specs = [pl.BlockSpec((1, tm, d), lambda b, i, j: (b, i, 0)),
                pl.BlockSpec((1, d), lambda b, i, j: (0, 0))]
    in_specs += _mod_specs(d, (0, 1), 3)
    in_specs.append(pl.BlockSpec((d, tn), lambda b, i, j: (0, j)))
    args = [x_all, norm_w.reshape(1, d), mod_l, mod_c, mod_l, mod_c, w]
    out_specs = [pl.BlockSpec((1, tm, tn), lambda b, i, j: (b, i, j))]
    out_shape = [jax.ShapeDtypeStruct((bsz, tall, n), BF16)]
    if has_extra:
        nx = w_extra.shape[1]
        in_specs.append(pl.BlockSpec((d, nx), lambda b, i, j: (0, 0)))
        args.append(w_extra)
        if extra_token_major:
            out_specs.append(pl.BlockSpec((tm, nx), lambda b, i, j: (i, b)))
            out_shape.append(jax.ShapeDtypeStruct((tall, bsz * nx), F32))
        else:
            out_specs.append(pl.BlockSpec((1, tm, nx), lambda b, i, j: (b, i, 0)))
            out_shape.append(jax.ShapeDtypeStruct((bsz, tall, nx), F32))
    out = pl.pallas_call(
        functools.partial(_proj_kernel, tm=tm, n_lat=n_lat, has_extra=has_extra, silu_cols=silu_cols),
        grid=(bsz, tall // tm, n // tn),
        in_specs=in_specs,
        out_specs=out_specs,
        out_shape=out_shape,
        scratch_shapes=[pltpu.VMEM((tm, d), BF16)],
        compiler_params=_cparams(3),
        name="norm_mod_proj",
    )(*args)
    return out if has_extra else out[0]


def _conv_kernel(main_ref, prev_ref, next_ref, w_ref, b_ref, o_ref, *, tt, n_lat, tall):
    i = pl.program_id(1)
    te = tt + 2 * GRID_W
    p = i * tt - GRID_W + lax.broadcasted_iota(jnp.int32, (te, 1), 0)
    is_ctx = p >= n_lat
    col = jnp.bitwise_and(p, GRID_W - 1)
    has_left = jnp.where(is_ctx, p - n_lat, col) > 0
    has_right = jnp.where(is_ctx, p - (tall - 1), col - (GRID_W - 1)) < 0
    w = w_ref[...]

    def conv(interior):
        ext = jnp.concatenate([prev_ref[0], main_ref[0], next_ref[0]], axis=0).astype(F32)
        own = slice(GRID_W, GRID_W + tt)
        if interior:
            as_left = jnp.where(has_right, ext, 0.0)
            as_right = jnp.where(has_left, ext, 0.0)
        acc = jnp.zeros((tt, w.shape[1]), F32) + b_ref[...]
        for dy in (-1, 0, 1):
            rs = slice(GRID_W + GRID_W * dy, GRID_W + GRID_W * dy + tt)
            k0 = 3 * (dy + 1)
            if interior:
                left, right = pltpu.roll(as_left[rs], 1, 0), pltpu.roll(as_right[rs], tt - 1, 0)
            else:
                left = jnp.where(has_left[own], pltpu.roll(ext[rs], 1, 0), 0.0)
                right = jnp.where(has_right[own], pltpu.roll(ext[rs], tt - 1, 0), 0.0)
            t = ext[rs] * w[k0 + 1:k0 + 2] + left * w[k0:k0 + 1] + right * w[k0 + 2:k0 + 3]
            if not interior and dy != 0:
                q = p[own]
                if dy == -1:
                    ok = jnp.where(q >= n_lat, 0, q) >= GRID_W
                else:
                    ok = jnp.where(q >= n_lat, n_lat, q) < n_lat - GRID_W
                t = jnp.where(ok, t, 0.0)
            acc = acc + t
        o_ref[0] = _silu(acc).astype(o_ref.dtype)

    interior = jnp.logical_and(i * tt >= GRID_W, (i + 1) * tt <= n_lat - GRID_W)
    pl.when(interior)(lambda: conv(True))
    pl.when(jnp.logical_not(interior))(lambda: conv(False))


def _conv_silu(proj, conv_w, conv_b, n_lat, n_ch):
    bsz, tall, _ = proj.shape
    n_rows = tall // GRID_W
    tt = _largest_divisor(tall, GRID_W, 768)
    assert n_lat % GRID_W == 0 and n_lat // tt == (tall - 1) // tt
    r = tt // GRID_W
    tc = 512
    return pl.pallas_call(
        functools.partial(_conv_kernel, tt=tt, n_lat=n_lat, tall=tall),
        grid=(bsz, tall // tt, n_ch // tc),
        in_specs=[pl.BlockSpec((1, tt, tc), lambda b, i, c: (b, i, c)),
                  pl.BlockSpec((1, GRID_W, tc), lambda b, i, c: (b, jnp.maximum(i * r - 1, 0), c)),
                  pl.BlockSpec((1, GRID_W, tc), lambda b, i, c: (b, jnp.minimum((i + 1) * r, n_rows - 1), c)),
                  pl.BlockSpec((9, tc), lambda b, i, c: (0, c)),
                  pl.BlockSpec((1, tc), lambda b, i, c: (0, c))],
        out_specs=pl.BlockSpec((1, tt, tc), lambda b, i, c: (b, i, c)),
        out_shape=jax.ShapeDtypeStruct((bsz, tall, n_ch), BF16),
        compiler_params=_cparams(3),
        name="dwconv_silu",
    )(proj, proj, proj, conv_w.reshape(9, n_ch), conv_b.reshape(1, n_ch))


def _scan_block(s, n_lat_blocks, n_blocks, reverse):
    n_ctx_blocks = n_blocks - n_lat_blocks
    if not reverse:
        return jnp.where(s < n_ctx_blocks, n_lat_blocks + s, s - n_ctx_blocks)
    return n_blocks - 1 - s


def _tri_mask(c, reverse):
    ri = lax.broadcasted_iota(jnp.int32, (c, c), 0)
    ci = lax.broadcasted_iota(jnp.int32, (c, c), 1)
    return (ci >= ri) if reverse else (ci <= ri)


def _ssd_expand_matrix():
    eye = jnp.eye(SSD_HEADS, dtype=F32)
    e_head = jnp.repeat(eye, SSD_HEADDIM, axis=1)
    e_seg = jnp.repeat(eye, SSD_CHUNK, axis=1)
    zh = jnp.zeros_like(e_head)
    zs = jnp.zeros_like(e_seg)
    blk = jnp.concatenate([
        jnp.concatenate([e_head, zh, zh, zs], axis=1),
        jnp.concatenate([zh, e_head, zh, zs], axis=1),
        jnp.concatenate([zh, zh, e_head, zs], axis=1),
        jnp.concatenate([zh, zh, zh, e_seg], axis=1)], axis=0)
    return jnp.concatenate([blk, blk, blk], axis=0).astype(BF16)


def _ssd_kernel(*refs, direction, finish):
    if finish:
        (x_ref, bm_ref, cm_ref, dtlr_ref, dtb_ref, nega_ref, exp_ref, z_ref, yf_ref, dsk_ref, nw_ref,
         o_ref, st_ref) = refs
    else:
        (x_ref, bm_ref, cm_ref, dtlr_ref, dtb_ref, nega_ref, exp_ref, o_ref, st_ref) = refs
    reverse = direction == 1
    c = SSD_CHUNK
    p = SSD_HEADDIM
    gw = SSD_INNER // SSD_GROUPS
    hpg = SSD_HEADS // SSD_GROUPS

    @pl.when(pl.program_id(1) == 0)
    def _():
        st_ref[...] = jnp.zeros_like(st_ref)

    mask = _tri_mask(c, reverse)
    tri3 = _tri3(mask)
    last = 0 if reverse else c - 1
    n_chunks = x_ref.shape[1] // c
    order = range(n_chunks - 1, -1, -1) if reverse else range(n_chunks)
    groups = range(SSD_GROUPS)
    g_cols = [slice(g * gw, (g + 1) * gw) for g in groups]
    n_cols = [slice(g * SSD_STATE, (g + 1) * SSD_STATE) for g in groups]
    cb = {(ck, g): _dot(cm_ref[0, ck * c:(ck + 1) * c, n_cols[g]], bm_ref[0, ck * c:(ck + 1) * c, n_cols[g]], _NT)
          for ck in order for g in groups}
    prep = {}
    for ck in order:
        rs = slice(ck * c, (ck + 1) * c)
        x = x_ref[0, rs, :].astype(F32)
        dt_raw = dtlr_ref[0, rs, :][:, direction * SSD_HEADS:(direction + 1) * SSD_HEADS].astype(F32)
        dt = _softplus(dt_raw + dtb_ref[...])
        la3 = jnp.concatenate(_split3(dt * nega_ref[...]), axis=0)
        acum = _dot(tri3, la3)
        acum_t = _dot(la3, tri3, _TT)
        a_last = acum[last:last + 1]
        narrow = jnp.concatenate([dt, dt * jnp.exp(a_last - acum), jnp.exp(acum), acum], axis=1)
        wide = _dot(jnp.concatenate(_split3(narrow), axis=1), exp_ref[...])
        ea_w = wide[:, 2 * SSD_INNER:3 * SSD_INNER]
        prep[ck] = dict(
            x=x, bm=bm_ref[0, rs, :], cm=cm_ref[0, rs, :], wide=wide, acum_t=acum_t, ea_w=ea_w,
            xdt=(x * wide[:, :SSD_INNER]).astype(BF16),
            xw=(x * wide[:, SSD_INNER:2 * SSD_INNER]).astype(BF16),
            e_last=ea_w[last:last + 1])
    kv = {(ck, g): _dot(prep[ck]["bm"][:, n_cols[g]], prep[ck]["xw"][:, g_cols[g]], _TN)
          for ck in order for g in groups}
    scores, st_used = {}, {}
    for g in groups:
        st = st_ref[:, g_cols[g]]
        for ck in order:
            st_used[ck, g] = st.astype(BF16)
            st = st * prep[ck]["e_last"][:, g_cols[g]] + kv[ck, g]
        st_ref[:, g_cols[g]] = st
    for ck in order:
        for h in range(SSD_HEADS):
            a_i = prep[ck]["wide"][:, 3 * SSD_INNER + h * c:3 * SSD_INNER + (h + 1) * c]
            decay = jnp.exp(jnp.where(mask, a_i - prep[ck]["acum_t"][h:h + 1, :], -jnp.inf))
            scores[ck, h] = (cb[ck, h // hpg] * decay).astype(BF16)
    for ck in order:
        rs = slice(ck * c, (ck + 1) * c)
        x = prep[ck]["x"]
        y_groups = []
        for g in groups:
            ys = [_dot(scores[ck, h], prep[ck]["xdt"][:, h * p:(h + 1) * p])
                  for h in range(g * hpg, (g + 1) * hpg)]
            y_state = _dot(prep[ck]["cm"][:, n_cols[g]], st_used[ck, g]) * prep[ck]["ea_w"][:, g_cols[g]]
            y_groups.append(jnp.concatenate(ys, axis=1) + y_state)
        y = jnp.concatenate(y_groups, axis=1)
        if finish:
            z_act = z_ref[0, rs, :].astype(F32)
            y = (y + yf_ref[0, rs, :] + dsk_ref[...] * x) * z_act
            outs = []
            for g in range(SSD_GROUPS):
                sl = slice(g * gw, (g + 1) * gw)
                outs.append(_rms(y[:, sl], nw_ref[:, sl]))
            o_ref[0, rs, :] = jnp.concatenate(outs, axis=1).astype(o_ref.dtype)
        else:
            o_ref[0, rs, :] = y


def _ssd_scan(xbc, proj, aux, dt_bias, neg_a, z_col, n_lat, direction, finish_args=None):
    bsz, tall, _ = xbc.shape
    tb = SCAN_BLOCK
    nb, n_lat_blocks = tall // tb, n_lat // tb
    reverse = direction == 1
    finish = finish_args is not None
    expand = _ssd_expand_matrix()

    def tok(col):
        return lambda b, s: (b, _scan_block(s, n_lat_blocks, nb, reverse), col)

    in_specs = [pl.BlockSpec((1, tb, SSD_INNER), tok(0)),
                pl.BlockSpec((1, tb, SSD_BC), tok(SSD_INNER // SSD_BC)),
                pl.BlockSpec((1, tb, SSD_BC), tok(SSD_INNER // SSD_BC + 1)),
                pl.BlockSpec((1, tb, LANE), tok(0)),
                pl.BlockSpec((1, SSD_HEADS), lambda b, s: (0, 0)),
                pl.BlockSpec((1, SSD_HEADS), lambda b, s: (0, 0)),
                pl.BlockSpec(expand.shape, lambda b, s: (0, 0))]
    args = [xbc, xbc, xbc, aux, dt_bias[direction:direction + 1], neg_a[direction:direction + 1], expand]
    if finish:
        y_f, d_skip_wide, norm_w = finish_args
        in_specs += [pl.BlockSpec((1, tb, SSD_INNER), tok(z_col)),
                     pl.BlockSpec((1, tb, SSD_INNER), tok(0)),
                     pl.BlockSpec((1, SSD_INNER), lambda b, s: (0, 0)),
                     pl.BlockSpec((1, SSD_INNER), lambda b, s: (0, 0))]
        args += [proj, y_f, d_skip_wide, norm_w]
    return pl.pallas_call(
        functools.partial(_ssd_kernel, direction=direction, finish=finish),
        grid=(bsz, nb),
        in_specs=in_specs,
        out_specs=pl.BlockSpec((1, tb, SSD_INNER), tok(0)),
        out_shape=jax.ShapeDtypeStruct((bsz, tall, SSD_INNER), BF16 if finish else F32),
        scratch_shapes=[pltpu.VMEM((SSD_STATE, SSD_INNER), F32)],
        compiler_params=_cparams(2),
        name="ssd_scan_bwd" if reverse else "ssd_scan_fwd",
    )(*args)


def _lin_kernel(*refs, mode, heads, dk, dv):
    n_in = 6 if mode == "gla" else 5
    ins = [refs[:n_in], refs[n_in:2 * n_in]]
    o_refs = refs[2 * n_in:2 * n_in + 2]
    st_ref = refs[2 * n_in + 2]
    c = LIN_CHUNK

    @pl.when(pl.program_id(1) == 0)
    def _():
        st_ref[...] = jnp.zeros_like(st_ref)

    tb = o_refs[0].shape[1]
    nc = tb // c
    hpt = LANE // dk
    n_tiles = heads // hpt
    span = n_tiles if mode == "gla" else 1
    gw = span * LANE
    n_groups = n_tiles // span
    hpg = hpt * span
    zeros = jnp.zeros((c, LANE), BF16)
    ri = lax.broadcasted_iota(jnp.int32, (tb, tb), 0)
    ci = lax.broadcasted_iota(jnp.int32, (tb, tb), 1)
    c_shift = c.bit_length() - 1
    same_chunk = jnp.right_shift(ri, c_shift) == jnp.right_shift(ci, c_shift)

    def chunk_blocks(a):
        cols = []
        for b in range(nc):
            cols.append(jnp.concatenate(
                [a[cc * c:(cc + 1) * c] if cc == b else zeros for cc in range(nc)], axis=0))
        return jnp.concatenate(cols, axis=1)

    def tile_cols(h):
        t = (h // hpt) % span
        return slice(t * LANE, (t + 1) * LANE)

    def make_stream(direction):
        if mode == "gla":
            q_ref, k_ref, v_ref, aux_ref, p1_ref, p2_ref = ins[direction]
        else:
            q_ref, v_ref, aux_ref, p1_ref, p2_ref = ins[direction]
            k_ref = None
        o_ref = o_refs[direction]
        reverse = direction == 1
        chunks = range(nc - 1, -1, -1) if reverse else range(nc)
        last = 0 if reverse else c - 1
        if reverse:
            bd_mask = jnp.where(same_chunk, ci - ri, -1) >= 0
        else:
            bd_mask = jnp.where(same_chunk, ci - ri, 1) <= 0
        tri = _tri_mask(c, reverse).astype(BF16)
        tri2 = jnp.concatenate([tri, tri], axis=1)
        groups, work = {}, {}

        def prep(g):
            ls = slice(g * gw, (g + 1) * gw)
            if mode == "gla":
                q = q_ref[0, :, ls].astype(F32)
                k = k_ref[0, :, ls].astype(F32)
                off = 2 * SSD_HEADS + direction * GLA_GATE_RANK
                lr = aux_ref[0][:, off:off + GLA_GATE_RANK].astype(BF16)
                lg = _log_sigmoid(_dot(lr, p1_ref[:, ls]) + p2_ref[:, ls]) * (1.0 / GLA_GATE_NORM)
            else:
                q = q_ref[0, :, ls].astype(F32)
                f_raw = aux_ref[0, :, ls].astype(F32)
                e = jnp.exp2(jnp.abs(f_raw) * _NEG_LOG2E)
                r = 1.0 / (1.0 + e)
                forget = p1_ref[:, ls] + p2_ref[:, ls] * jnp.where(f_raw >= 0.0, r, e * r)
                lg = jnp.log(forget)
                k = 1.0 - forget
            gcum = jnp.concatenate([_cumsum_rows(tri2, lg[cc * c:(cc + 1) * c]) for cc in range(nc)], axis=0)
            e_last = [jnp.exp(gcum[cc * c + last:cc * c + last + 1]) for cc in range(nc)]
            e_rows = jnp.concatenate([jnp.broadcast_to(e, (c, gw)) for e in e_last], axis=0)
            e_gcum = jnp.exp(gcum)
            q_decf = q * e_gcum
            if hpt > 1:
                head_of_lane = jnp.bitwise_and(jnp.right_shift(
                    lax.broadcasted_iota(jnp.int32, (1, gw), 1), dk.bit_length() - 1), hpt - 1)
                q_dec = [jnp.where(head_of_lane == r, q_decf, 0.0).astype(BF16) for r in range(hpt)]
            else:
                q_dec = [q_decf.astype(BF16)]
            k_invf = k * (1.0 / e_gcum)
            groups[g] = dict(q_dec=q_dec, k_inv=k_invf.astype(BF16), k_end=(k_invf * e_rows).astype(BF16),
                             e_last=e_last)

        def products(h):
            gp, ts = groups[h // hpg], tile_cols(h)
            qh = gp["q_dec"][h % hpt][:, ts]
            vh = v_ref[0, :, h * dv:(h + 1) * dv]
            work[h] = dict(
                qh=qh, vh=vh, scores=_dot(qh, gp["k_inv"][:, ts], _NT),
                kv_t=_dot(vh, chunk_blocks(gp["k_end"][:, ts]), _TN))

        def mask_and_chain(h):
            w = work[h]
            e_last = groups[h // hpg]["e_last"]
            w["att"] = jnp.where(bd_mask, w.pop("scores"), 0.0).astype(BF16)
            st = st_ref[direction, h]
            used = [None] * nc
            for cc in chunks:
                used[cc] = st.astype(BF16)
                st = st * e_last[cc][:, tile_cols(h)] + w["kv_t"][:, cc * LANE:(cc + 1) * LANE]
            st_ref[direction, h] = st
            w["used"] = used
            del w["kv_t"]

        def outputs(h):
            w = work[h]
            o_state = jnp.concatenate(
                [_dot(w["qh"][cc * c:(cc + 1) * c], w["used"][cc], _NT) for cc in range(nc)], axis=0)
            w["o"] = _dot(w["att"], w["vh"]) + o_state

        def emit(h):
            o_ref[0, :, h * dv:(h + 1) * dv] = work.pop(h)["o"]

        return dict(prep=prep, products=products, mask_and_chain=mask_and_chain, outputs=outputs, emit=emit)

    streams = [make_stream(0), make_stream(1)]

    def heads_of(g):
        return range(g * hpg, (g + 1) * hpg) if 0 <= g < n_groups else ()

    lag = 2 if n_groups > 1 else 1
    for it in range(n_groups + 4 * lag):
        for stage, delay in (("products", lag), ("outputs", 3 * lag)):
            for st in streams:
                for h in heads_of(it - delay):
                    st[stage](h)
        if it < n_groups:
            for st in streams:
                st["prep"](it)
        for stage, delay in (("mask_and_chain", 2 * lag), ("emit", 4 * lag)):
            for st in streams:
                for h in heads_of(it - delay):
                    st[stage](h)


def _lin_scan(mode, proj, cols, params, n_lat, aux=None):
    bsz, tall, _ = proj.shape
    if mode == "gla":
        heads, dk, dv = GLA_HEADS, GLA_DK, GLA_DV
    else:
        heads, dk, dv = HGRN_HEADS, HGRN_DK, HGRN_DV
    kw, vw = heads * dk, heads * dv
    tb = SCAN_BLOCK
    nb, n_lat_blocks = tall // tb, n_lat // tb

    def const2(shape):
        return pl.BlockSpec(shape, lambda b, s: (0, 0))

    in_specs, args = [], []
    for direction in (0, 1):
        def tok(col, reverse=direction == 1):
            return lambda b, s: (b, _scan_block(s, n_lat_blocks, nb, reverse), col)
        p1, p2 = params[direction]
        if mode == "gla":
            in_specs += [pl.BlockSpec((1, tb, kw), tok(cols["q"])),
                         pl.BlockSpec((1, tb, kw), tok(cols["k"])),
                         pl.BlockSpec((1, tb, vw), tok(cols["v"])),
                         pl.BlockSpec((1, tb, LANE), tok(0)),
                         const2(p1.shape), const2(p2.shape)]
            args += [proj, proj, proj, aux, p1, p2]
        else:
            in_specs += [pl.BlockSpec((1, tb, kw), tok(cols["q"])),
                         pl.BlockSpec((1, tb, vw), tok(cols["v"])),
                         pl.BlockSpec((1, tb, kw), tok(cols["aux"] + direction)),
                         const2(p1.shape), const2(p2.shape)]
            args += [proj, proj, proj, p1, p2]
    out_specs = [pl.BlockSpec((1, tb, vw), lambda b, s: (b, _scan_block(s, n_lat_blocks, nb, False), 0)),
                 pl.BlockSpec((1, tb, vw), lambda b, s: (b, _scan_block(s, n_lat_blocks, nb, True), 0))]
    return pl.pallas_call(
        functools.partial(_lin_kernel, mode=mode, heads=heads, dk=dk, dv=dv),
        grid=(bsz, nb),
        in_specs=in_specs,
        out_specs=out_specs,
        out_shape=[jax.ShapeDtypeStruct((bsz, tall, vw), F32)] * 2,
        scratch_shapes=[pltpu.VMEM((2, heads, dv, LANE), F32)],
        compiler_params=_cparams(2),
        name=f"{mode}_scan",
    )(*args)


def _s5_kernel(uf_ref, ub_ref, bmat_ref, lre_ref, lim_ref, cmat_ref, of_ref, ob_ref, h_ref, ut_ref, yt_ref, st_ref,
               *, bsz):
    steps = S5_CHUNK
    n_slabs = bmat_ref.shape[1]
    sw = bmat_ref.shape[3] // 2
    width = n_slabs * LANE
    re_cols = [slice(2 * s * sw, (2 * s + 1) * sw) for s in range(n_slabs)]
    im_cols = [slice((2 * s + 1) * sw, (2 * s + 2) * sw) for s in range(n_slabs)]
    both = [slice(2 * s * sw, (2 * s + 2) * sw) for s in range(n_slabs)]

    @pl.when(pl.program_id(0) == 0)
    def _():
        st_ref[...] = jnp.zeros_like(st_ref)

    def inputs(d, u_ref):
        for b in range(bsz):
            for s in range(n_slabs):
                ut_ref[d, s, pl.ds(b, steps, stride=bsz), :] = (
                    u_ref[:, b * width + s * LANE:b * width + (s + 1) * LANE])
        for s in range(n_slabs):
            h_ref[d, :, both[s]] = _dot(ut_ref[d, s].astype(BF16), bmat_ref[d, s])

    def scan(d):
        lam_re = [jnp.broadcast_to(lre_ref[d:d + 1, s * sw:(s + 1) * sw], (bsz, sw)) for s in range(n_slabs)]
        lam_im = [jnp.broadcast_to(lim_ref[d:d + 1, s * sw:(s + 1) * sw], (bsz, sw)) for s in range(n_slabs)]
        hr = [st_ref[d, :, re_cols[s]] for s in range(n_slabs)]
        hi = [st_ref[d, :, im_cols[s]] for s in range(n_slabs)]
        for tt in range(steps):
            t = tt if d == 0 else steps - 1 - tt
            rows = slice(t * bsz, (t + 1) * bsz)
            for s in range(n_slabs):
                nr = lam_re[s] * hr[s] - lam_im[s] * hi[s] + h_ref[d, rows, re_cols[s]]
                ni = lam_re[s] * hi[s] + lam_im[s] * hr[s] + h_ref[d, rows, im_cols[s]]
                h_ref[d, rows, re_cols[s]] = nr
                h_ref[d, rows, im_cols[s]] = ni
                hr[s], hi[s] = nr, ni
        for s in range(n_slabs):
            st_ref[d, :, re_cols[s]] = hr[s]
            st_ref[d, :, im_cols[s]] = hi[s]

    def outputs(d):
        for s in range(n_slabs):
            yt_ref[d, s] = _dot(h_ref[d, :, both[s]].astype(BF16), cmat_ref[s])

    def emit(d, o_ref):
        for b in range(bsz):
            for s in range(n_slabs):
                o_ref[:, b * width + s * LANE:b * width + (s + 1) * LANE] = (
                    yt_ref[d, s, pl.ds(b, steps, stride=bsz), :])

    inputs(0, uf_ref)
    inputs(1, ub_ref)
    scan(0)
    outputs(0)
    scan(1)
    outputs(1)
    emit(0, of_ref)
    emit(1, ob_ref)


def _s5_scan(u_t, bmat, lam_re, lam_im, cmat, bsz, n_lat):
    tall, bw = u_t.shape
    width = bw // bsz
    nch, n_lat_chunks = tall // S5_CHUNK, n_lat // S5_CHUNK
    n_state = lam_re.shape[-1]
    rows = S5_CHUNK * bsz
    fwd = lambda s: (_scan_block(s, n_lat_chunks, nch, False), 0)
    bwd = lambda s: (_scan_block(s, n_lat_chunks, nch, True), 0)
    whole = lambda a: pl.BlockSpec(a.shape, lambda s: (0,) * a.ndim)
    lam_re, lam_im = lam_re.reshape(2, n_state), lam_im.reshape(2, n_state)
    return pl.pallas_call(
        functools.partial(_s5_kernel, bsz=bsz),
        grid=(nch,),
        in_specs=[pl.BlockSpec((S5_CHUNK, bw), fwd), pl.BlockSpec((S5_CHUNK, bw), bwd),
                  whole(bmat), whole(lam_re), whole(lam_im), whole(cmat)],
        out_specs=[pl.BlockSpec((S5_CHUNK, bw), fwd), pl.BlockSpec((S5_CHUNK, bw), bwd)],
        out_shape=[jax.ShapeDtypeStruct((tall, bw), F32)] * 2,
        scratch_shapes=[pltpu.VMEM((2, rows, 2 * n_state), F32),
                        pltpu.VMEM((2, width // LANE, rows, LANE), F32),
                        pltpu.VMEM((2, width // LANE, rows, LANE), F32),
                        pltpu.VMEM((2, bsz, 2 * n_state), F32)],
        compiler_params=_cparams(1),
        name="s5_scan",
    )(u_t, u_t, bmat, lam_re, lam_im, cmat)


def _lin_finish(of_ref, ob_ref, gate_ref, nw_ref):
    o = of_ref[0] + ob_ref[0]
    dv = nw_ref.shape[1]
    parts = [_rms(o[:, h * dv:(h + 1) * dv], nw_ref[...]) for h in range(o.shape[1] // dv)]
    return (jnp.concatenate(parts, axis=1) * gate_ref[0].astype(F32)).astype(BF16)


def _lin_finish_specs(o_dirs, proj, gate_col, norm_w, tm):
    vw = o_dirs[0].shape[2]
    specs = [pl.BlockSpec((1, tm, vw), lambda b, i: (b, i, 0)),
             pl.BlockSpec((1, tm, vw), lambda b, i: (b, i, 0)),
             pl.BlockSpec((1, tm, vw), lambda b, i: (b, i, gate_col)),
             pl.BlockSpec(norm_w.shape, lambda b, i: (0, 0))]
    return specs, [o_dirs[0], o_dirs[1], proj, norm_w]


def _out0_kernel(x_ref, a_ref, of_ref, ob_ref, gate_ref, nw_ref, wa_ref, wb_ref, gl_ref, gc_ref, o_ref,
                 *, tm, n_lat):
    i = pl.program_id(1)
    o = _dot(a_ref[0], wa_ref[...]) + _dot(_lin_finish(of_ref, ob_ref, gate_ref, nw_ref), wb_ref[...])
    o_ref[0] = x_ref[0] + _row_select(i, tm, n_lat, gc_ref[...], gl_ref[0]) * o


def _out_proj0(x_all, mix_a, lin_finish, w_a, w_b, mod_l, mod_c, n_lat, n_rows, tm):
    bsz, _, d = x_all.shape
    tall = n_rows
    lin_specs, lin_args = _lin_finish_specs(*lin_finish, tm)
    return pl.pallas_call(
        functools.partial(_out0_kernel, tm=tm, n_lat=n_lat),
        grid=(bsz, tall // tm),
        in_specs=[pl.BlockSpec((1, tm, d), lambda b, i: (b, i, 0)),
                  pl.BlockSpec((1, tm, mix_a.shape[2]), lambda b, i: (b, i, 0))] + lin_specs + [
                  pl.BlockSpec(w_a.shape, lambda b, i: (0, 0)),
                  pl.BlockSpec(w_b.shape, lambda b, i: (0, 0))] + _mod_specs(d, (2,), 2),
        out_specs=pl.BlockSpec((1, tm, d), lambda b, i: (b, i, 0)),
        out_shape=jax.ShapeDtypeStruct((bsz, tall, d), F32),
        compiler_params=_cparams(2),
        name="out_proj_even",
    )(x_all, mix_a, *lin_args, w_a, w_b, mod_l, mod_c)


def _gelu_tanh(x):
    return 0.5 * x * (1.0 + jnp.tanh(math.sqrt(2.0 / math.pi) * (x + 0.044715 * (x * x * x))))


def _out1_kernel(x_ref, of_ref, ob_ref, gate_ref, nw_ref, yf_ref, yb_ref, u_ref, dsk_ref, gw_ref, gb_ref,
                 wa_ref, wb_ref, gl_ref, gc_ref, o_ref, *, tm, n_lat):
    i = pl.program_id(1)
    y = _gelu_tanh(yf_ref[...] + yb_ref[...] + dsk_ref[...] * u_ref[...])
    glu = _dot(y.astype(BF16), gw_ref[...]) + gb_ref[...]
    y = y * _sigmoid(glu)
    o = (_dot(_lin_finish(of_ref, ob_ref, gate_ref, nw_ref), wa_ref[...])
         + _dot(y.astype(BF16), wb_ref[...]))
    o_ref[0] = x_ref[0] + _row_select(i, tm, n_lat, gc_ref[...], gl_ref[0]) * o


def _out_proj1(x_all, lin_finish, y_dirs, u_t, d_skip, glu_w, glu_b, w_a, w_b, mod_l, mod_c, n_lat, n_rows, tm):
    bsz, _, d = x_all.shape
    tall = n_rows
    width = d_skip.shape[1]
    (y_f, y_b), u2 = y_dirs, u_t
    lin_specs, lin_args = _lin_finish_specs(*lin_finish, tm)
    return pl.pallas_call(
        functools.partial(_out1_kernel, tm=tm, n_lat=n_lat),
        grid=(bsz, tall // tm),
        in_specs=[pl.BlockSpec((1, tm, d), lambda b, i: (b, i, 0))] + lin_specs + [
                  pl.BlockSpec((tm, width), lambda b, i: (i, b)),
                  pl.BlockSpec((tm, width), lambda b, i: (i, b)),
                  pl.BlockSpec((tm, width), lambda b, i: (i, b)),
                  pl.BlockSpec((1, width), lambda b, i: (0, 0)),
                  pl.BlockSpec(glu_w.shape, lambda b, i: (0, 0)),
                  pl.BlockSpec((1, width), lambda b, i: (0, 0)),
                  pl.BlockSpec(w_a.shape, lambda b, i: (0, 0)),
                  pl.BlockSpec(w_b.shape, lambda b, i: (0, 0))] + _mod_specs(d, (2,), 2),
        out_specs=pl.BlockSpec((1, tm, d), lambda b, i: (b, i, 0)),
        out_shape=jax.ShapeDtypeStruct((bsz, tall, d), F32),
        compiler_params=_cparams(2),
        name="out_proj_odd",
    )(x_all, *lin_args, y_f, y_b, u2, d_skip, glu_w, glu_b, w_a, w_b, mod_l, mod_c)


def _mlp_kernel(*refs, tm, n_lat, final):
    if final:
        (x_ref, nw_ref, shl_ref, shc_ref, scl_ref, scc_ref, gl_ref, gc_ref, w1_ref, w2_ref, fw_ref,
         o_ref, h_scr, acc_scr) = refs
    else:
        (x_ref, nw_ref, shl_ref, shc_ref, scl_ref, scc_ref, gl_ref, gc_ref, w1_ref, w2_ref,
         o_ref, h_scr, acc_scr) = refs
    i = pl.program_id(1)
    j = pl.program_id(2)

    @pl.when(j == 0)
    def _():
        _store_norm_modulated(h_scr, x_ref[0], nw_ref[...], shl_ref[0], shc_ref[...], scl_ref[0], scc_ref[...],
                              i, tm, n_lat)
        acc_scr[...] = jnp.zeros_like(acc_scr)

    def partial_product():
        a = jnp.maximum(_dot(h_scr[...], w1_ref[...]), 0.0)
        return _dot((a * a).astype(BF16), w2_ref[...])

    last_j = pl.num_programs(2) - 1

    @pl.when(j < last_j)
    def _():
        acc_scr[...] += partial_product()

    @pl.when(j == last_j)
    def _():
        out = x_ref[0] + _row_select(i, tm, n_lat, gc_ref[...], gl_ref[0]) * (acc_scr[...] + partial_product())
        if final:
            out = _rms(out, fw_ref[...])
        o_ref[0] = out


def _mlp(x_all, norm_w, mod_l, mod_c, w1, w2, final_w, n_lat, tm, tf):
    bsz, tall, d = x_all.shape
    ff = w1.shape[1]
    final = final_w is not None
    in_specs = [pl.BlockSpec((1, tm, d), lambda b, i, j: (b, i, 0)),
                pl.BlockSpec((1, d), lambda b, i, j: (0, 0))]
    in_specs += _mod_specs(d, (3, 4, 5), 3)
    in_specs += [pl.BlockSpec((d, tf), lambda b, i, j: (0, j)),
                 pl.BlockSpec((tf, d), lambda b, i, j: (j, 0))]
    args = [x_all, norm_w.reshape(1, d)] + [mod_l, mod_c] * 3 + [w1, w2]
    if final:
        in_specs.append(pl.BlockSpec((1, d), lambda b, i, j: (0, 0)))
        args.append(final_w.reshape(1, d))
    return pl.pallas_call(
        functools.partial(_mlp_kernel, tm=tm, n_lat=n_lat, final=final),
        grid=(bsz, tall // tm, ff // tf),
        in_specs=in_specs,
        out_specs=pl.BlockSpec((1, tm, d), lambda b, i, j: (b, i, 0)),
        out_shape=jax.ShapeDtypeStruct((bsz, tall, d), F32),
        scratch_shapes=[pltpu.VMEM((tm, d), BF16), pltpu.VMEM((tm, d), F32)],
        compiler_params=_cparams(3),
        name="sq_relu_mlp",
    )(*args)


def _even_in_weight(w_in):
    sizes = (SSD_INNER, SSD_INNER + 2 * SSD_BC, 2 * SSD_HEADS, GLA_KEY, GLA_KEY, GLA_VAL,
             2 * GLA_GATE_RANK, GLA_VAL)
    offs = [0]
    for s in sizes:
        offs.append(offs[-1] + s)
    z, xbc, dt, q, k, v, lr, r = (w_in[:, offs[n]:offs[n + 1]] for n in range(8))
    pad = jnp.zeros((w_in.shape[0], LANE - dt.shape[1] - lr.shape[1]), w_in.dtype)
    main = jnp.concatenate([xbc, z, v, r, q * (GLA_DK ** -0.5), k], axis=1).astype(BF16)
    aux = jnp.concatenate([dt, lr, pad], axis=1).astype(BF16)
    return main, aux


def _s5_params(a_re, a_im, log_dt, b_re, b_im, c_re, c_im):
    delta = jnp.exp(log_dt.astype(F32))[..., None]
    mag = jnp.exp(a_re * delta)
    lbar_re, lbar_im = mag * jnp.cos(a_im * delta), mag * jnp.sin(a_im * delta)
    den = a_re * a_re + a_im * a_im
    zr = ((lbar_re - 1.0) * a_re + lbar_im * a_im) / den
    zi = (lbar_im * a_re - (lbar_re - 1.0) * a_im) / den
    bb_re = zr[..., None] * b_re - zi[..., None] * b_im
    bb_im = zr[..., None] * b_im + zi[..., None] * b_re
    n_slabs = S5_GROUPS // S5_SLAB
    eye = jnp.eye(S5_SLAB, dtype=F32)
    sw = S5_SLAB * S5_STATE

    def block_in(bb):
        bb = bb.reshape(2, n_slabs, S5_SLAB, S5_STATE, S5_GROUP)
        return jnp.einsum("dsgpc,gh->dsgchp", bb, eye).reshape(2, n_slabs, LANE, sw)

    def block_out(cc):
        cc = cc.reshape(n_slabs, S5_SLAB, S5_GROUP, S5_STATE)
        return jnp.einsum("sgcp,gh->sgphc", cc, eye).reshape(n_slabs, sw, LANE)

    bmat = jnp.concatenate([block_in(bb_re), block_in(bb_im)], axis=3).astype(BF16)
    cmat = jnp.concatenate([block_out(c_re), -block_out(c_im)], axis=1).astype(BF16)
    return (bmat, lbar_re.reshape(2, 1, S5_NSTATE), lbar_im.reshape(2, 1, S5_NSTATE), cmat)


def _layer_even(x_all, mod_l, mod_c, n_lat, tm, n_rows_out, tm_out, norm1_w, w_in, conv_w, conv_b, dt_bias,
                a_log, d_skip, ssd_norm_w, gate_w, gate_b, gla_norm_w, w_out):
    w, w_aux = _even_in_weight(w_in)
    n = w.shape[1]
    n_xbc = SSD_INNER + 2 * SSD_BC
    gates = ((n_xbc, n_xbc + SSD_INNER), (n_xbc + SSD_INNER + GLA_VAL, n_xbc + SSD_INNER + 2 * GLA_VAL))
    proj, aux = _project(x_all, norm1_w, mod_l, mod_c, w, w_aux, n_lat, tm, _largest_divisor(n, 2 * LANE, PROJ_TN),
                         extra_token_major=False, silu_ranges=gates)
    c_z, c_v, c_r = n_xbc // SSD_INNER, n_xbc // GLA_VAL + 1, n_xbc // GLA_VAL + 2
    c_q = (n_xbc + 3 * SSD_INNER) // GLA_KEY
    xbc = _conv_silu(proj, conv_w, conv_b, n_lat, n_xbc)

    neg_a = -jnp.exp(a_log.astype(F32))
    dt_bias = dt_bias.astype(F32)
    d_wide = jnp.repeat(d_skip.astype(F32), SSD_HEADDIM).reshape(1, SSD_INNER)
    y_f = _ssd_scan(xbc, proj, aux, dt_bias, neg_a, c_z, n_lat, 0)
    y_mix = _ssd_scan(xbc, proj, aux, dt_bias, neg_a, c_z, n_lat, 1,
                      (y_f, d_wide, ssd_norm_w.reshape(1, SSD_INNER)))

    cols = {"q": c_q, "k": c_q + 1, "v": c_v, "gate": c_r}
    gparams = [(gate_w[d].astype(BF16), gate_b[d].reshape(1, GLA_KEY).astype(F32)) for d in range(2)]
    o_dirs = _lin_scan("gla", proj, cols, gparams, n_lat, aux=aux)
    gla_finish = (o_dirs, proj, c_r, gla_norm_w.astype(F32).reshape(1, GLA_DV))

    w_out = w_out.astype(BF16)
    return _out_proj0(x_all, y_mix, gla_finish, w_out[:SSD_INNER], w_out[SSD_INNER:], mod_l, mod_c, n_lat,
                      n_rows_out, tm_out)


def _layer_odd(x_all, mod_l, mod_c, n_lat, tm, n_rows_out, tm_out, norm1_w, w_in, lb, hgrn_norm_w, a_re, a_im,
               log_dt, b_re, b_im, c_re, c_im, d_skip, glu_w, glu_b, w_out):
    bsz = x_all.shape[0]
    n_main = 5 * HGRN_WIDTH
    w_main = w_in[:, :n_main].astype(BF16)
    w_u = w_in[:, n_main:].astype(BF16)
    proj, u_t = _project(x_all, norm1_w, mod_l, mod_c, w_main, w_u, n_lat, tm,
                         _largest_divisor(n_main, 2 * LANE, PROJ_TN),
                         silu_ranges=((0, HGRN_WIDTH), (4 * HGRN_WIDTH, 5 * HGRN_WIDTH)))

    lb = lb.astype(F32).reshape(2, 1, HGRN_WIDTH)
    cols = {"q": 0, "v": 1, "aux": 2, "gate": 4}
    o_dirs = _lin_scan("hgrn", proj, cols, [(lb[d], 1.0 - lb[d]) for d in range(2)], n_lat)
    hgrn_finish = (o_dirs, proj, cols["gate"], hgrn_norm_w.astype(F32).reshape(1, HGRN_DV))

    bmat, lam_re, lam_im, cmat = _s5_params(a_re.astype(F32), a_im.astype(F32), log_dt, b_re.astype(F32),
                                            b_im.astype(F32), c_re.astype(F32), c_im.astype(F32))
    y_dirs = _s5_scan(u_t, bmat, lam_re, lam_im, cmat, bsz, n_lat)

    w_out = w_out.astype(BF16)
    return _out_proj1(x_all, hgrn_finish, y_dirs, u_t, d_skip.astype(F32).reshape(1, S5_WIDTH),
                      glu_w.astype(BF16), glu_b.astype(F32).reshape(1, S5_WIDTH),
                      w_out[:HGRN_WIDTH], w_out[HGRN_WIDTH:], mod_l, mod_c, n_lat, n_rows_out, tm_out)


def kernel(x, c, ctx, c_ctx, ada_w, ada_b, norm1_w, norm2_w, ssd_gla_w_in, ssd_conv_w, ssd_conv_b, ssd_dt_bias, ssd_a_log, ssd_d, ssd_norm_w, gla_gate_w, gla_gate_b, gla_norm_w, ssd_gla_w_out, hgrn_s5_w_in, hgrn_lb_logits, hgrn_norm_w, s5_a_re, s5_a_im, s5_log_dt, s5_b_re, s5_b_im, s5_c_re, s5_c_im, s5_d, s5_glu_w, s5_glu_b, hgrn_s5_w_out, mlp_w1, mlp_w2, final_norm_w):
    bsz, n_lat, d = x.shape
    ctx_len = ctx.shape[1]
    depth = ada_w.shape[0]
    tall = n_lat + ctx_len
    assert bsz % 8 == 0 and ctx_len % SCAN_BLOCK == 0 and n_lat % SCAN_BLOCK == 0 and n_lat % GRID_W == 0
    tf = 2048

    n_rows = -(-(bsz + 1) // 8) * 8
    cvec = jnp.concatenate([c, c_ctx[None, :], jnp.zeros((n_rows - bsz - 1, d), c.dtype)], axis=0)
    mod = _modulation(cvec.astype(F32), ada_w, ada_b)

    p_lb = jax.nn.softmax(hgrn_lb_logits.astype(F32), axis=0)
    lb_all = jnp.cumsum(p_lb, axis=0) - p_lb[0]

    x_all = jnp.concatenate([x, ctx], axis=1).astype(F32)
    tm_all = _largest_divisor(tall, 16, 1056)
    for layer in range(depth):
        j = layer // 2
        last = layer == depth - 1
        n_rows = n_lat if last else tall
        tm_out = _largest_divisor(n_rows, 16, 1056)
        mod_l = mod[layer, :bsz].reshape(bsz, 1, N_MOD * d)
        mod_c = mod[layer, bsz:bsz + 1]
        if layer % 2 == 0:
            x_all = _layer_even(x_all, mod_l, mod_c, n_lat, tm_all, n_rows, tm_out, norm1_w[layer],
                                ssd_gla_w_in[j], ssd_conv_w[j], ssd_conv_b[j], ssd_dt_bias[j], ssd_a_log[j],
                                ssd_d[j], ssd_norm_w[j], gla_gate_w[j], gla_gate_b[j], gla_norm_w[j],
                                ssd_gla_w_out[j])
        else:
            x_all = _layer_odd(x_all, mod_l, mod_c, n_lat, tm_all, n_rows, tm_out, norm1_w[layer],
                               hgrn_s5_w_in[j], lb_all[layer], hgrn_norm_w[j], s5_a_re[j], s5_a_im[j],
                               s5_log_dt[j], s5_b_re[j], s5_b_im[j], s5_c_re[j], s5_c_im[j], s5_d[j],
                               s5_glu_w[j], s5_glu_b[j], hgrn_s5_w_out[j])
        x_all = _mlp(x_all, norm2_w[layer], mod_l, mod_c, mlp_w1[layer].astype(BF16),
                     mlp_w2[layer].astype(BF16), final_norm_w if last else None, n_lat, tm_out, tf)
    return x_all.astype(x.dtype)
```

```python
import functools
import math

import jax
import jax.numpy as jnp
from jax import lax
from jax.experimental import pallas as pl
from jax.experimental.pallas import tpu as pltpu

F32 = jnp.float32
BF16 = jnp.bfloat16

GRID_W = 64
NORM_EPS = 1e-6
N_MOD = 6
SSD_HEADDIM = 64
SSD_HEADS = 16
SSD_GROUPS = 4
SSD_STATE = 128
SSD_CHUNK = 128
GLA_HEADS = 8
GLA_DK = 64
GLA_DV = 128
GLA_GATE_RANK = 16
GLA_GATE_NORM = 16.0
HGRN_HEADS = 8
HGRN_DK = 128
HGRN_DV = 128
S5_GROUP = 16
S5_GROUPS = 24
S5_STATE = 64
LIN_CHUNK = 64

SSD_INNER = SSD_HEADS * SSD_HEADDIM
SSD_BC = SSD_GROUPS * SSD_STATE
GLA_KEY = GLA_HEADS * GLA_DK
GLA_VAL = GLA_HEADS * GLA_DV
HGRN_WIDTH = HGRN_HEADS * HGRN_DV
S5_WIDTH = S5_GROUPS * S5_GROUP
S5_NSTATE = S5_GROUPS * S5_STATE

VMEM_LIMIT_BYTES = 56 * 1024 * 1024
LANE = 128
PROJ_TN = 2560
SCAN_BLOCK = 256
S5_CHUNK = 128
S5_SLAB = LANE // S5_GROUP


def _cparams(n_axes):
    return pltpu.CompilerParams(dimension_semantics=("arbitrary",) * n_axes,
                                vmem_limit_bytes=VMEM_LIMIT_BYTES)


def _largest_divisor(n, multiple, cap):
    best = None
    for d in range(multiple, min(n, cap) + 1, multiple):
        if n % d == 0:
            best = d
    assert best is not None, (n, multiple, cap)
    return best


_NEG_LOG2E = -1.4426950408889634


def _sigmoid(x):
    return 1.0 / (1.0 + jnp.exp2(x * _NEG_LOG2E))


def _silu(x):
    return x * _sigmoid(x)


def _softplus(x):
    return jnp.maximum(x, 0.0) + jnp.log1p(jnp.exp(-jnp.abs(x)))


def _log_sigmoid(x):
    return -_softplus(-x)


def _rms(x, w):
    return x * lax.rsqrt(jnp.mean(x * x, axis=-1, keepdims=True) + NORM_EPS) * w


def _dot(a, b, dims=(((1,), (0,)), ((), ())), precision=None):
    return lax.dot_general(a, b, dims, precision=precision, preferred_element_type=F32)


def _split3(v):
    hi = v.astype(BF16)
    r1 = v - hi.astype(F32)
    mid = r1.astype(BF16)
    lo = (r1 - mid.astype(F32)).astype(BF16)
    return hi, mid, lo


def _tri3(mask):
    tri = mask.astype(BF16)
    return jnp.concatenate([tri, tri, tri], axis=1)


def _cumsum_rows(tri2, v):
    hi = v.astype(BF16)
    lo = (v - hi.astype(F32)).astype(BF16)
    return _dot(tri2, jnp.concatenate([hi, lo], axis=0))


_NT = (((1,), (1,)), ((), ()))
_TN = (((0,), (0,)), ((), ()))
_TT = (((0,), (1,)), ((), ()))


def _mod_kernel(c_ref, w_ref, b_ref, o_ref):
    a = _silu(c_ref[...]).astype(BF16)
    o_ref[0] = _dot(a, w_ref[0].astype(BF16)) + b_ref[0]


def _modulation(cvec, ada_w, ada_b):
    depth, d, n = ada_w.shape
    rows = cvec.shape[0]
    tn = _largest_divisor(n, LANE, 1024)
    return pl.pallas_call(
        _mod_kernel,
        grid=(depth, n // tn),
        in_specs=[pl.BlockSpec((rows, d), lambda l, j: (0, 0)),
                  pl.BlockSpec((1, d, tn), lambda l, j: (l, 0, j)),
                  pl.BlockSpec((1, 1, tn), lambda l, j: (l, 0, j))],
        out_specs=pl.BlockSpec((1, rows, tn), lambda l, j: (l, 0, j)),
        out_shape=jax.ShapeDtypeStruct((depth, rows, n), F32),
        compiler_params=_cparams(2),
        name="adaln_mod",
    )(cvec, ada_w, ada_b.reshape(depth, 1, n))


def _row_select(i, tm, n_lat, ctx_val, lat_val):
    row = i * tm + lax.broadcasted_iota(jnp.int32, (tm, 1), 0)
    return jnp.where(row >= n_lat, ctx_val, lat_val)


def _store_norm_modulated(h_ref, x, nw, shift_l, shift_c, scale_l, scale_c, i, tm, n_lat):
    xn = x * lax.rsqrt(jnp.mean(x * x, axis=-1, keepdims=True) + NORM_EPS)
    gain = _row_select(i, tm, n_lat, nw * (1.0 + scale_c), nw * (1.0 + scale_l))
    h_ref[...] = (xn * gain + _row_select(i, tm, n_lat, shift_c, shift_l)).astype(h_ref.dtype)


def _mod_specs(d, cols, n_grid_axes):
    specs = []
    for k in cols:
        if n_grid_axes == 2:
            specs.append(pl.BlockSpec((1, 1, d), lambda b, i, k=k: (b, 0, k)))
            specs.append(pl.BlockSpec((1, d), lambda b, i, k=k: (0, k)))
        else:
            specs.append(pl.BlockSpec((1, 1, d), lambda b, i, j, k=k: (b, 0, k)))
            specs.append(pl.BlockSpec((1, d), lambda b, i, j, k=k: (0, k)))
    return specs


def _proj_kernel(*refs, tm, n_lat, has_extra, silu_cols):
    if has_extra:
        (x_ref, nw_ref, shl_ref, shc_ref, scl_ref, scc_ref, w_ref, wx_ref, o_ref, ox_ref, h_scr) = refs
    else:
        (x_ref, nw_ref, shl_ref, shc_ref, scl_ref, scc_ref, w_ref, o_ref, h_scr) = refs
    i = pl.program_id(1)
    j = pl.program_id(2)

    @pl.when(j == 0)
    def _():
        _store_norm_modulated(h_scr, x_ref[0], nw_ref[...], shl_ref[0], shc_ref[...], scl_ref[0], scc_ref[...],
                              i, tm, n_lat)
        if has_extra:
            ox_ref[...] = _dot(h_scr[...], wx_ref[...]).reshape(ox_ref.shape)

    for jj, ranges in enumerate(silu_cols):
        @pl.when(j == jj)
        def _(ranges=ranges):
            o = _dot(h_scr[...], w_ref[...])
            if not ranges:
                o_ref[0] = o.astype(o_ref.dtype)
            edge = 0
            for start, stop in ranges:
                if start > edge:
                    o_ref[0, :, edge:start] = o[:, edge:start].astype(o_ref.dtype)
                o_ref[0, :, start:stop] = _silu(o[:, start:stop]).astype(o_ref.dtype)
                edge = stop
            if ranges and edge < o.shape[1]:
                o_ref[0, :, edge:] = o[:, edge:].astype(o_ref.dtype)


def _project(x_all, norm_w, mod_l, mod_c, w, w_extra, n_lat, tm, tn, extra_token_major=True, silu_ranges=()):
    bsz, tall, d = x_all.shape
    n = w.shape[1]
    has_extra = w_extra is not None
    silu_cols = tuple(
        tuple((max(a, jj * tn) - jj * tn, min(b, (jj + 1) * tn) - jj * tn)
              for a, b in sorted(silu_ranges) if a < (jj + 1) * tn and b > jj * tn)
        for jj in range(n // tn))
    in_specs = [pl.BlockSpec((1, tm, d), lambda b, i, j: (b, i, 0)),
                pl.BlockSpec((1, d), lambda b, i, j: (0, 0))]
    in_specs += _mod_specs(d, (0, 1), 3)
    in_specs.append(pl.BlockSpec((d, tn), lambda b, i, j: (0, j)))
    args = [x_all, norm_w.reshape(1, d), mod_l, mod_c, mod_l, mod_c, w]
    out_specs = [pl.BlockSpec((1, tm, tn), lambda b, i, j: (b, i, j))]
    out_shape = [jax.ShapeDtypeStruct((bsz, tall, n), BF16)]
    if has_extra:
        nx = w_extra.shape[1]
        in_specs.append(pl.BlockSpec((d, nx), lambda b, i, j: (0, 0)))
        args.append(w_extra)
        if extra_token_major:
            out_specs.append(pl.BlockSpec((tm, nx), lambda b, i, j: (i, b)))
            out_shape.append(jax.ShapeDtypeStruct((tall, bsz * nx), F32))
        else:
            out_specs.append(pl.BlockSpec((1, tm, nx), lambda b, i, j: (b, i, 0)))
            out_shape.append(jax.ShapeDtypeStruct((bsz, tall, nx), F32))
    out = pl.pallas_call(
        functools.partial(_proj_kernel, tm=tm, n_lat=n_lat, has_extra=has_extra, silu_cols=silu_cols),
        grid=(bsz, tall // tm, n // tn),
        in_specs=in_specs,
        out_specs=out_specs,
        out_shape=out_shape,
        scratch_shapes=[pltpu.VMEM((tm, d), BF16)],
        compiler_params=_cparams(3),
        name="norm_mod_proj",
    )(*args)
    return out if has_extra else out[0]


def _conv_kernel(main_ref, prev_ref, next_ref, w_ref, b_ref, o_ref, *, tt, n_lat, tall):
    i = pl.program_id(1)
    te = tt + 2 * GRID_W
    p = i * tt - GRID_W + lax.broadcasted_iota(jnp.int32, (te, 1), 0)
    is_ctx = p >= n_lat
    col = jnp.bitwise_and(p, GRID_W - 1)
    has_left = jnp.where(is_ctx, p - n_lat, col) > 0
    has_right = jnp.where(is_ctx, p - (tall - 1), col - (GRID_W - 1)) < 0
    w = w_ref[...]

    def conv(interior):
        ext = jnp.concatenate([prev_ref[0], main_ref[0], next_ref[0]], axis=0).astype(F32)
        own = slice(GRID_W, GRID_W + tt)
        if interior:
            as_left = jnp.where(has_right, ext, 0.0)
            as_right = jnp.where(has_left, ext, 0.0)
        acc = jnp.zeros((tt, w.shape[1]), F32) + b_ref[...]
        for dy in (-1, 0, 1):
            rs = slice(GRID_W + GRID_W * dy, GRID_W + GRID_W * dy + tt)
            k0 = 3 * (dy + 1)
            if interior:
                left, right = pltpu.roll(as_left[rs], 1, 0), pltpu.roll(as_right[rs], tt - 1, 0)
            else:
                left = jnp.where(has_left[own], pltpu.roll(ext[rs], 1, 0), 0.0)
                right = jnp.where(has_right[own], pltpu.roll(ext[rs], tt - 1, 0), 0.0)
            t = ext[rs] * w[k0 + 1:k0 + 2] + left * w[k0:k0 + 1] + right * w[k0 + 2:k0 + 3]
            if not interior and dy != 0:
                q = p[own]
                if dy == -1:
                    ok = jnp.where(q >= n_lat, 0, q) >= GRID_W
                else:
                    ok = jnp.where(q >= n_lat, n_lat, q) < n_lat - GRID_W
                t = jnp.where(ok, t, 0.0)
            acc = acc + t
        o_ref[0] = _silu(acc).astype(o_ref.dtype)

    interior = jnp.logical_and(i * tt >= GRID_W, (i + 1) * tt <= n_lat - GRID_W)
    pl.when(interior)(lambda: conv(True))
    pl.when(jnp.logical_not(interior))(lambda: conv(False))


def _conv_silu(proj, conv_w, conv_b, n_lat, n_ch):
    bsz, tall, _ = proj.shape
    n_rows = tall // GRID_W
    tt = _largest_divisor(tall, GRID_W, 768)
    assert n_lat % GRID_W == 0 and n_lat // tt == (tall - 1) // tt
    r = tt // GRID_W
    tc = 512
    return pl.pallas_call(
        functools.partial(_conv_kernel, tt=tt, n_lat=n_lat, tall=tall),
        grid=(bsz, tall // tt, n_ch // tc),
        in_specs=[pl.BlockSpec((1, tt, tc), lambda b, i, c: (b, i, c)),
                  pl.BlockSpec((1, GRID_W, tc), lambda b, i, c: (b, jnp.maximum(i * r - 1, 0), c)),
                  pl.BlockSpec((1, GRID_W, tc), lambda b, i, c: (b, jnp.minimum((i + 1) * r, n_rows - 1), c)),
                  pl.BlockSpec((9, tc), lambda b, i, c: (0, c)),
                  pl.BlockSpec((1, tc), lambda b, i, c: (0, c))],
        out_specs=pl.BlockSpec((1, tt, tc), lambda b, i, c: (b, i, c)),
        out_shape=jax.ShapeDtypeStruct((bsz, tall, n_ch), BF16),
        compiler_params=_cparams(3),
        name="dwconv_silu",
    )(proj, proj, proj, conv_w.reshape(9, n_ch), conv_b.reshape(1, n_ch))


def _scan_block(s, n_lat_blocks, n_blocks, reverse):
    n_ctx_blocks = n_blocks - n_lat_blocks
    if not reverse:
        return jnp.where(s < n_ctx_blocks, n_lat_blocks + s, s - n_ctx_blocks)
    return n_blocks - 1 - s


def _tri_mask(c, reverse):
    ri = lax.broadcasted_iota(jnp.int32, (c, c), 0)
    ci = lax.broadcasted_iota(jnp.int32, (c, c), 1)
    return (ci >= ri) if reverse else (ci <= ri)


def _ssd_expand_matrix():
    eye = jnp.eye(SSD_HEADS, dtype=F32)
    e_head = jnp.repeat(eye, SSD_HEADDIM, axis=1)
    e_seg = jnp.repeat(eye, SSD_CHUNK, axis=1)
    zh = jnp.zeros_like(e_head)
    zs = jnp.zeros_like(e_seg)
    blk = jnp.concatenate([
        jnp.concatenate([e_head, zh, zh, zs], axis=1),
        jnp.concatenate([zh, e_head, zh, zs], axis=1),
        jnp.concatenate([zh, zh, e_head, zs], axis=1),
        jnp.concatenate([zh, zh, zh, e_seg], axis=1)], axis=0)
    return jnp.concatenate([blk, blk, blk], axis=0).astype(BF16)


def _ssd_kernel(*refs, direction, finish):
    if finish:
        (x_ref, bm_ref, cm_ref, dtlr_ref, dtb_ref, nega_ref, exp_ref, z_ref, yf_ref, dsk_ref, nw_ref,
         o_ref, st_ref) = refs
    else:
        (x_ref, bm_ref, cm_ref, dtlr_ref, dtb_ref, nega_ref, exp_ref, o_ref, st_ref) = refs
    reverse = direction == 1
    c = SSD_CHUNK
    p = SSD_HEADDIM
    gw = SSD_INNER // SSD_GROUPS
    hpg = SSD_HEADS // SSD_GROUPS

    @pl.when(pl.program_id(1) == 0)
    def _():
        st_ref[...] = jnp.zeros_like(st_ref)

    mask = _tri_mask(c, reverse)
    tri3 = _tri3(mask)
    last = 0 if reverse else c - 1
    n_chunks = x_ref.shape[1] // c
    order = range(n_chunks - 1, -1, -1) if reverse else range(n_chunks)
    groups = range(SSD_GROUPS)
    g_cols = [slice(g * gw, (g + 1) * gw) for g in groups]
    n_cols = [slice(g * SSD_STATE, (g + 1) * SSD_STATE) for g in groups]
    cb = {(ck, g): _dot(cm_ref[0, ck * c:(ck + 1) * c, n_cols[g]], bm_ref[0, ck * c:(ck + 1) * c, n_cols[g]], _NT)
          for ck in order for g in groups}
    prep = {}
    for ck in order:
        rs = slice(ck * c, (ck + 1) * c)
        x = x_ref[0, rs, :].astype(F32)
        dt_raw = dtlr_ref[0, rs, :][:, direction * SSD_HEADS:(direction + 1) * SSD_HEADS].astype(F32)
        dt = _softplus(dt_raw + dtb_ref[...])
        la3 = jnp.concatenate(_split3(dt * nega_ref[...]), axis=0)
        acum = _dot(tri3, la3)
        acum_t = _dot(la3, tri3, _TT)
        a_last = acum[last:last + 1]
        narrow = jnp.concatenate([dt, dt * jnp.exp(a_last - acum), jnp.exp(acum), acum], axis=1)
        wide = _dot(jnp.concatenate(_split3(narrow), axis=1), exp_ref[...])
        ea_w = wide[:, 2 * SSD_INNER:3 * SSD_INNER]
        prep[ck] = dict(
            x=x, bm=bm_ref[0, rs, :], cm=cm_ref[0, rs, :], wide=wide, acum_t=acum_t, ea_w=ea_w,
            xdt=(x * wide[:, :SSD_INNER]).astype(BF16),
            xw=(x * wide[:, SSD_INNER:2 * SSD_INNER]).astype(BF16),
            e_last=ea_w[last:last + 1])
    kv = {(ck, g): _dot(prep[ck]["bm"][:, n_cols[g]], prep[ck]["xw"][:, g_cols[g]], _TN)
          for ck in order for g in groups}
    scores, st_used = {}, {}
    for g in groups:
        st = st_ref[:, g_cols[g]]
        for ck in order:
            st_used[ck, g] = st.astype(BF16)
            st = st * prep[ck]["e_last"][:, g_cols[g]] + kv[ck, g]
        st_ref[:, g_cols[g]] = st
    for ck in order:
        for h in range(SSD_HEADS):
            a_i = prep[ck]["wide"][:, 3 * SSD_INNER + h * c:3 * SSD_INNER + (h + 1) * c]
            decay = jnp.exp(jnp.where(mask, a_i - prep[ck]["acum_t"][h:h + 1, :], -jnp.inf))
            scores[ck, h] = (cb[ck, h // hpg] * decay).astype(BF16)
    for ck in order:
        rs = slice(ck * c, (ck + 1) * c)
        x = prep[ck]["x"]
        y_groups = []
        for g in groups:
            ys = [_dot(scores[ck, h], prep[ck]["xdt"][:, h * p:(h + 1) * p])
                  for h in range(g * hpg, (g + 1) * hpg)]
            y_state = _dot(prep[ck]["cm"][:, n_cols[g]], st_used[ck, g]) * prep[ck]["ea_w"][:, g_cols[g]]
            y_groups.append(jnp.concatenate(ys, axis=1) + y_state)
        y = jnp.concatenate(y_groups, axis=1)
        if finish:
            z_act = z_ref[0, rs, :].astype(F32)
            y = (y + yf_ref[0, rs, :] + dsk_ref[...] * x) * z_act
            outs = []
            for g in range(SSD_GROUPS):
                sl = slice(g * gw, (g + 1) * gw)
                outs.append(_rms(y[:, sl], nw_ref[:, sl]))
            o_ref[0, rs, :] = jnp.concatenate(outs, axis=1).astype(o_ref.dtype)
        else:
            o_ref[0, rs, :] = y


def _ssd_scan(xbc, proj, aux, dt_bias, neg_a, z_col, n_lat, direction, finish_args=None):
    bsz, tall, _ = xbc.shape
    tb = SCAN_BLOCK
    nb, n_lat_blocks = tall // tb, n_lat // tb
    reverse = direction == 1
    finish = finish_args is not None
    expand = _ssd_expand_matrix()

    def tok(col):
        return lambda b, s: (b, _scan_block(s, n_lat_blocks, nb, reverse), col)

    in_specs = [pl.BlockSpec((1, tb, SSD_INNER), tok(0)),
                pl.BlockSpec((1, tb, SSD_BC), tok(SSD_INNER // SSD_BC)),
                pl.BlockSpec((1, tb, SSD_BC), tok(SSD_INNER // SSD_BC + 1)),
                pl.BlockSpec((1, tb, LANE), tok(0)),
                pl.BlockSpec((1, SSD_HEADS), lambda b, s: (0, 0)),
                pl.BlockSpec((1, SSD_HEADS), lambda b, s: (0, 0)),
                pl.BlockSpec(expand.shape, lambda b, s: (0, 0))]
    args = [xbc, xbc, xbc, aux, dt_bias[direction:direction + 1], neg_a[direction:direction + 1], expand]
    if finish:
        y_f, d_skip_wide, norm_w = finish_args
        in_specs += [pl.BlockSpec((1, tb, SSD_INNER), tok(z_col)),
                     pl.BlockSpec((1, tb, SSD_INNER), tok(0)),
                     pl.BlockSpec((1, SSD_INNER), lambda b, s: (0, 0)),
                     pl.BlockSpec((1, SSD_INNER), lambda b, s: (0, 0))]
        args += [proj, y_f, d_skip_wide, norm_w]
    return pl.pallas_call(
        functools.partial(_ssd_kernel, direction=direction, finish=finish),
        grid=(bsz, nb),
        in_specs=in_specs,
        out_specs=pl.BlockSpec((1, tb, SSD_INNER), tok(0)),
        out_shape=jax.ShapeDtypeStruct((bsz, tall, SSD_INNER), BF16 if finish else F32),
        scratch_shapes=[pltpu.VMEM((SSD_STATE, SSD_INNER), F32)],
        compiler_params=_cparams(2),
        name="ssd_scan_bwd" if reverse else "ssd_scan_fwd",
    )(*args)


def _lin_kernel(*refs, mode, heads, dk, dv):
    n_in = 6 if mode == "gla" else 5
    ins = [refs[:n_in], refs[n_in:2 * n_in]]
    o_refs = refs[2 * n_in:2 * n_in + 2]
    st_ref = refs[2 * n_in + 2]
    c = LIN_CHUNK

    @pl.when(pl.program_id(1) == 0)
    def _():
        st_ref[...] = jnp.zeros_like(st_ref)

    tb = o_refs[0].shape[1]
    nc = tb // c
    hpt = LANE // dk
    n_tiles = heads // hpt
    span = n_tiles if mode == "gla" else 1
    gw = span * LANE
    n_groups = n_tiles // span
    hpg = hpt * span
    zeros = jnp.zeros((c, LANE), BF16)
    ri = lax.broadcasted_iota(jnp.int32, (tb, tb), 0)
    ci = lax.broadcasted_iota(jnp.int32, (tb, tb), 1)
    c_shift = c.bit_length() - 1
    same_chunk = jnp.right_shift(ri, c_shift) == jnp.right_shift(ci, c_shift)

    def chunk_blocks(a):
        cols = []
        for b in range(nc):
            cols.append(jnp.concatenate(
                [a[cc * c:(cc + 1) * c] if cc == b else zeros for cc in range(nc)], axis=0))
        return jnp.concatenate(cols, axis=1)

    def tile_cols(h):
        t = (h // hpt) % span
        return slice(t * LANE, (t + 1) * LANE)

    def make_stream(direction):
        if mode == "gla":
            q_ref, k_ref, v_ref, aux_ref, p1_ref, p2_ref = ins[direction]
        else:
            q_ref, v_ref, aux_ref, p1_ref, p2_ref = ins[direction]
            k_ref = None
        o_ref = o_refs[direction]
        reverse = direction == 1
        chunks = range(nc - 1, -1, -1) if reverse else range(nc)
        last = 0 if reverse else c - 1
        if reverse:
            bd_mask = jnp.where(same_chunk, ci - ri, -1) >= 0
        else:
            bd_mask = jnp.where(same_chunk, ci - ri, 1) <= 0
        tri = _tri_mask(c, reverse).astype(BF16)
        tri2 = jnp.concatenate([tri, tri], axis=1)
        groups, work = {}, {}

        def prep(g):
            ls = slice(g * gw, (g + 1) * gw)
            if mode == "gla":
                q = q_ref[0, :, ls].astype(F32)
                k = k_ref[0, :, ls].astype(F32)
                off = 2 * SSD_HEADS + direction * GLA_GATE_RANK
                lr = aux_ref[0][:, off:off + GLA_GATE_RANK].astype(BF16)
                lg = _log_sigmoid(_dot(lr, p1_ref[:, ls]) + p2_ref[:, ls]) * (1.0 / GLA_GATE_NORM)
            else:
                q = q_ref[0, :, ls].astype(F32)
                f_raw = aux_ref[0, :, ls].astype(F32)
                e = jnp.exp2(jnp.abs(f_raw) * _NEG_LOG2E)
                r = 1.0 / (1.0 + e)
                forget = p1_ref[:, ls] + p2_ref[:, ls] * jnp.where(f_raw >= 0.0, r, e * r)
                lg = jnp.log(forget)
                k = 1.0 - forget
            gcum = jnp.concatenate([_cumsum_rows(tri2, lg[cc * c:(cc + 1) * c]) for cc in range(nc)], axis=0)
            e_last = [jnp.exp(gcum[cc * c + last:cc * c + last + 1]) for cc in range(nc)]
            e_rows = jnp.concatenate([jnp.broadcast_to(e, (c, gw)) for e in e_last], axis=0)
            e_gcum = jnp.exp(gcum)
            q_decf = q * e_gcum
            if hpt > 1:
                head_of_lane = jnp.bitwise_and(jnp.right_shift(
                    lax.broadcasted_iota(jnp.int32, (1, gw), 1), dk.bit_length() - 1), hpt - 1)
                q_dec = [jnp.where(head_of_lane == r, q_decf, 0.0).astype(BF16) for r in range(hpt)]
            else:
                q_dec = [q_decf.astype(BF16)]
            k_invf = k * (1.0 / e_gcum)
            groups[g] = dict(q_dec=q_dec, k_inv=k_invf.astype(BF16), k_end=(k_invf * e_rows).astype(BF16),
                             e_last=e_last)

        def products(h):
            gp, ts = groups[h // hpg], tile_cols(h)
            qh = gp["q_dec"][h % hpt][:, ts]
            vh = v_ref[0, :, h * dv:(h + 1) * dv]
            work[h] = dict(
                qh=qh, vh=vh, scores=_dot(qh, gp["k_inv"][:, ts], _NT),
                kv_t=_dot(vh, chunk_blocks(gp["k_end"][:, ts]), _TN))

        def mask_and_chain(h):
            w = work[h]
            e_last = groups[h // hpg]["e_last"]
            w["att"] = jnp.where(bd_mask, w.pop("scores"), 0.0).astype(BF16)
            st = st_ref[direction, h]
            used = [None] * nc
            for cc in chunks:
                used[cc] = st.astype(BF16)
                st = st * e_last[cc][:, tile_cols(h)] + w["kv_t"][:, cc * LANE:(cc + 1) * LANE]
            st_ref[direction, h] = st
            w["used"] = used
            del w["kv_t"]

        def outputs(h):
            w = work[h]
            o_state = jnp.concatenate(
                [_dot(w["qh"][cc * c:(cc + 1) * c], w["used"][cc], _NT) for cc in range(nc)], axis=0)
            w["o"] = _dot(w["att"], w["vh"]) + o_state

        def emit(h):
            o_ref[0, :, h * dv:(h + 1) * dv] = work.pop(h)["o"].astype(o_ref.dtype)

        return dict(prep=prep, products=products, mask_and_chain=mask_and_chain, outputs=outputs, emit=emit)

    streams = [make_stream(0), make_stream(1)]

    def heads_of(g):
        return range(g * hpg, (g + 1) * hpg) if 0 <= g < n_groups else ()

    lag = 2 if n_groups > 1 else 1
    for it in range(n_groups + 4 * lag):
        for stage, delay in (("products", lag), ("outputs", 3 * lag)):
            for st in streams:
                for h in heads_of(it - delay):
                    st[stage](h)
        if it < n_groups:
            for st in streams:
                st["prep"](it)
        for stage, delay in (("mask_and_chain", 2 * lag), ("emit", 4 * lag)):
            for st in streams:
                for h in heads_of(it - delay):
                    st[stage](h)


def _lin_scan(mode, proj, cols, params, n_lat, aux=None):
    bsz, tall, _ = proj.shape
    if mode == "gla":
        heads, dk, dv = GLA_HEADS, GLA_DK, GLA_DV
    else:
        heads, dk, dv = HGRN_HEADS, HGRN_DK, HGRN_DV
    kw, vw = heads * dk, heads * dv
    tb = SCAN_BLOCK
    nb, n_lat_blocks = tall // tb, n_lat // tb

    def const2(shape):
        return pl.BlockSpec(shape, lambda b, s: (0, 0))

    in_specs, args = [], []
    for direction in (0, 1):
        def tok(col, reverse=direction == 1):
            return lambda b, s: (b, _scan_block(s, n_lat_blocks, nb, reverse), col)
        p1, p2 = params[direction]
        if mode == "gla":
            in_specs += [pl.BlockSpec((1, tb, kw), tok(cols["q"])),
                         pl.BlockSpec((1, tb, kw), tok(cols["k"])),
                         pl.BlockSpec((1, tb, vw), tok(cols["v"])),
                         pl.BlockSpec((1, tb, LANE), tok(0)),
                         const2(p1.shape), const2(p2.shape)]
            args += [proj, proj, proj, aux, p1, p2]
        else:
            in_specs += [pl.BlockSpec((1, tb, kw), tok(cols["q"])),
                         pl.BlockSpec((1, tb, vw), tok(cols["v"])),
                         pl.BlockSpec((1, tb, kw), tok(cols["aux"] + direction)),
                         const2(p1.shape), const2(p2.shape)]
            args += [proj, proj, proj, p1, p2]
    out_specs = [pl.BlockSpec((1, tb, vw), lambda b, s: (b, _scan_block(s, n_lat_blocks, nb, False), 0)),
                 pl.BlockSpec((1, tb, vw), lambda b, s: (b, _scan_block(s, n_lat_blocks, nb, True), 0))]
    return pl.pallas_call(
        functools.partial(_lin_kernel, mode=mode, heads=heads, dk=dk, dv=dv),
        grid=(bsz, nb),
        in_specs=in_specs,
        out_specs=out_specs,
        out_shape=[jax.ShapeDtypeStruct((bsz, tall, vw), BF16)] * 2,
        scratch_shapes=[pltpu.VMEM((2, heads, dv, LANE), F32)],
        compiler_params=_cparams(2),
        name=f"{mode}_scan",
    )(*args)


def _s5_kernel(uf_ref, ub_ref, bmat_ref, lre_ref, lim_ref, cmat_ref, of_ref, ob_ref, h_ref, ut_ref, yt_ref, st_ref,
               *, bsz):
    steps = S5_CHUNK
    n_slabs = bmat_ref.shape[1]
    sw = bmat_ref.shape[3] // 2
    width = n_slabs * LANE
    re_cols = [slice(2 * s * sw, (2 * s + 1) * sw) for s in range(n_slabs)]
    im_cols = [slice((2 * s + 1) * sw, (2 * s + 2) * sw) for s in range(n_slabs)]
    both = [slice(2 * s * sw, (2 * s + 2) * sw) for s in range(n_slabs)]

    @pl.when(pl.program_id(0) == 0)
    def _():
        st_ref[...] = jnp.zeros_like(st_ref)

    def inputs(d, u_ref):
        for b in range(bsz):
            for s in range(n_slabs):
                ut_ref[d, s, pl.ds(b, steps, stride=bsz), :] = (
                    u_ref[:, b * width + s * LANE:b * width + (s + 1) * LANE])
        for s in range(n_slabs):
            h_ref[d, :, both[s]] = _dot(ut_ref[d, s].astype(BF16), bmat_ref[d, s])

    def scan(d):
        lam_re = [jnp.broadcast_to(lre_ref[d:d + 1, s * sw:(s + 1) * sw], (bsz, sw)) for s in range(n_slabs)]
        lam_im = [jnp.broadcast_to(lim_ref[d:d + 1, s * sw:(s + 1) * sw], (bsz, sw)) for s in range(n_slabs)]
        hr = [st_ref[d, :, re_cols[s]] for s in range(n_slabs)]
        hi = [st_ref[d, :, im_cols[s]] for s in range(n_slabs)]
        for tt in range(steps):
            t = tt if d == 0 else steps - 1 - tt
            rows = slice(t * bsz, (t + 1) * bsz)
            for s in range(n_slabs):
                nr = lam_re[s] * hr[s] - lam_im[s] * hi[s] + h_ref[d, rows, re_cols[s]]
                ni = lam_re[s] * hi[s] + lam_im[s] * hr[s] + h_ref[d, rows, im_cols[s]]
                h_ref[d, rows, re_cols[s]] = nr
                h_ref[d, rows, im_cols[s]] = ni
                hr[s], hi[s] = nr, ni
        for s in range(n_slabs):
            st_ref[d, :, re_cols[s]] = hr[s]
            st_ref[d, :, im_cols[s]] = hi[s]

    def outputs(d):
        for s in range(n_slabs):
            yt_ref[d, s] = _dot(h_ref[d, :, both[s]].astype(BF16), cmat_ref[s])

    def emit(d, o_ref):
        for b in range(bsz):
            for s in range(n_slabs):
                o_ref[:, b * width + s * LANE:b * width + (s + 1) * LANE] = (
                    yt_ref[d, s, pl.ds(b, steps, stride=bsz), :])

    inputs(0, uf_ref)
    inputs(1, ub_ref)
    scan(0)
    outputs(0)
    scan(1)
    outputs(1)
    emit(0, of_ref)
    emit(1, ob_ref)


def _s5_scan(u_t, bmat, lam_re, lam_im, cmat, bsz, n_lat):
    tall, bw = u_t.shape
    width = bw // bsz
    nch, n_lat_chunks = tall // S5_CHUNK, n_lat // S5_CHUNK
    n_state = lam_re.shape[-1]
    rows = S5_CHUNK * bsz
    fwd = lambda s: (_scan_block(s, n_lat_chunks, nch, False), 0)
    bwd = lambda s: (_scan_block(s, n_lat_chunks, nch, True), 0)
    whole = lambda a: pl.BlockSpec(a.shape, lambda s: (0,) * a.ndim)
    lam_re, lam_im = lam_re.reshape(2, n_state), lam_im.reshape(2, n_state)
    return pl.pallas_call(
        functools.partial(_s5_kernel, bsz=bsz),
        grid=(nch,),
        in_specs=[pl.BlockSpec((S5_CHUNK, bw), fwd), pl.BlockSpec((S5_CHUNK, bw), bwd),
                  whole(bmat), whole(lam_re), whole(lam_im), whole(cmat)],
        out_specs=[pl.BlockSpec((S5_CHUNK, bw), fwd), pl.BlockSpec((S5_CHUNK, bw), bwd)],
        out_shape=[jax.ShapeDtypeStruct((tall, bw), F32)] * 2,
        scratch_shapes=[pltpu.VMEM((2, rows, 2 * n_state), F32),
                        pltpu.VMEM((2, width // LANE, rows, LANE), F32),
                        pltpu.VMEM((2, width // LANE, rows, LANE), F32),
                        pltpu.VMEM((2, bsz, 2 * n_state), F32)],
        compiler_params=_cparams(1),
        name="s5_scan",
    )(u_t, u_t, bmat, lam_re, lam_im, cmat)


def _lin_finish(of_ref, ob_ref, gate_ref, nw_ref):
    o = of_ref[0].astype(F32) + ob_ref[0].astype(F32)
    dv = nw_ref.shape[1]
    parts = [_rms(o[:, h * dv:(h + 1) * dv], nw_ref[...]) for h in range(o.shape[1] // dv)]
    return (jnp.concatenate(parts, axis=1) * gate_ref[0].astype(F32)).astype(BF16)


def _lin_finish_specs(o_dirs, proj, gate_col, norm_w, tm):
    vw = o_dirs[0].shape[2]
    specs = [pl.BlockSpec((1, tm, vw), lambda b, i: (b, i, 0)),
             pl.BlockSpec((1, tm, vw), lambda b, i: (b, i, 0)),
             pl.BlockSpec((1, tm, vw), lambda b, i: (b, i, gate_col)),
             pl.BlockSpec(norm_w.shape, lambda b, i: (0, 0))]
    return specs, [o_dirs[0], o_dirs[1], proj, norm_w]


def _out0_kernel(x_ref, a_ref, of_ref, ob_ref, gate_ref, nw_ref, wa_ref, wb_ref, gl_ref, gc_ref, o_ref,
                 *, tm, n_lat):
    i = pl.program_id(1)
    o = _dot(a_ref[0], wa_ref[...]) + _dot(_lin_finish(of_ref, ob_ref, gate_ref, nw_ref), wb_ref[...])
    o_ref[0] = x_ref[0] + _row_select(i, tm, n_lat, gc_ref[...], gl_ref[0]) * o


def _out_proj0(x_all, mix_a, lin_finish, w_a, w_b, mod_l, mod_c, n_lat, n_rows, tm):
    bsz, _, d = x_all.shape
    tall = n_rows
    lin_specs, lin_args = _lin_finish_specs(*lin_finish, tm)
    return pl.pallas_call(
        functools.partial(_out0_kernel, tm=tm, n_lat=n_lat),
        grid=(bsz, tall // tm),
        in_specs=[pl.BlockSpec((1, tm, d), lambda b, i: (b, i, 0)),
                  pl.BlockSpec((1, tm, mix_a.shape[2]), lambda b, i: (b, i, 0))] + lin_specs + [
                  pl.BlockSpec(w_a.shape, lambda b, i: (0, 0)),
                  pl.BlockSpec(w_b.shape, lambda b, i: (0, 0))] + _mod_specs(d, (2,), 2),
        out_specs=pl.BlockSpec((1, tm, d), lambda b, i: (b, i, 0)),
        out_shape=jax.ShapeDtypeStruct((bsz, tall, d), F32),
        compiler_params=_cparams(2),
        name="out_proj_even",
    )(x_all, mix_a, *lin_args, w_a, w_b, mod_l, mod_c)


def _gelu_tanh(x):
    return 0.5 * x * (1.0 + jnp.tanh(math.sqrt(2.0 / math.pi) * (x + 0.044715 * (x * x * x))))


def _out1_kernel(x_ref, of_ref, ob_ref, gate_ref, nw_ref, yf_ref, yb_ref, u_ref, dsk_ref, gw_ref, gb_ref,
                 wa_ref, wb_ref, gl_ref, gc_ref, o_ref, *, tm, n_lat):
    i = pl.program_id(1)
    y = _gelu_tanh(yf_ref[...] + yb_ref[...] + dsk_ref[...] * u_ref[...])
    glu = _dot(y.astype(BF16), gw_ref[...]) + gb_ref[...]
    y = y * _sigmoid(glu)
    o = (_dot(_lin_finish(of_ref, ob_ref, gate_ref, nw_ref), wa_ref[...])
         + _dot(y.astype(BF16), wb_ref[...]))
    o_ref[0] = x_ref[0] + _row_select(i, tm, n_lat, gc_ref[...], gl_ref[0]) * o


def _out_proj1(x_all, lin_finish, y_dirs, u_t, d_skip, glu_w, glu_b, w_a, w_b, mod_l, mod_c, n_lat, n_rows, tm):
    bsz, _, d = x_all.shape
    tall = n_rows
    width = d_skip.shape[1]
    (y_f, y_b), u2 = y_dirs, u_t
    lin_specs, lin_args = _lin_finish_specs(*lin_finish, tm)
    return pl.pallas_call(
        functools.partial(_out1_kernel, tm=tm, n_lat=n_lat),
        grid=(bsz, tall // tm),
        in_specs=[pl.BlockSpec((1, tm, d), lambda b, i: (b, i, 0))] + lin_specs + [
                  pl.BlockSpec((tm, width), lambda b, i: (i, b)),
                  pl.BlockSpec((tm, width), lambda b, i: (i, b)),
                  pl.BlockSpec((tm, width), lambda b, i: (i, b)),
                  pl.BlockSpec((1, width), lambda b, i: (0, 0)),
                  pl.BlockSpec(glu_w.shape, lambda b, i: (0, 0)),
                  pl.BlockSpec((1, width), lambda b, i: (0, 0)),
                  pl.BlockSpec(w_a.shape, lambda b, i: (0, 0)),
                  pl.BlockSpec(w_b.shape, lambda b, i: (0, 0))] + _mod_specs(d, (2,), 2),
        out_specs=pl.BlockSpec((1, tm, d), lambda b, i: (b, i, 0)),
        out_shape=jax.ShapeDtypeStruct((bsz, tall, d), F32),
        compiler_params=_cparams(2),
        name="out_proj_odd",
    )(x_all, *lin_args, y_f, y_b, u2, d_skip, glu_w, glu_b, w_a, w_b, mod_l, mod_c)


def _mlp_kernel(*refs, tm, n_lat, final):
    if final:
        (x_ref, nw_ref, shl_ref, shc_ref, scl_ref, scc_ref, gl_ref, gc_ref, w1_ref, w2_ref, fw_ref,
         o_ref, h_scr, acc_scr) = refs
    else:
        (x_ref, nw_ref, shl_ref, shc_ref, scl_ref, scc_ref, gl_ref, gc_ref, w1_ref, w2_ref,
         o_ref, h_scr, acc_scr) = refs
    i = pl.program_id(1)
    j = pl.program_id(2)

    @pl.when(j == 0)
    def _():
        _store_norm_modulated(h_scr, x_ref[0], nw_ref[...], shl_ref[0], shc_ref[...], scl_ref[0], scc_ref[...],
                              i, tm, n_lat)
        acc_scr[...] = jnp.zeros_like(acc_scr)

    def partial_product():
        a = jnp.maximum(_dot(h_scr[...], w1_ref[...]), 0.0)
        return _dot((a * a).astype(BF16), w2_ref[...])

    last_j = pl.num_programs(2) - 1

    @pl.when(j < last_j)
    def _():
        acc_scr[...] += partial_product()

    @pl.when(j == last_j)
    def _():
        out = x_ref[0] + _row_select(i, tm, n_lat, gc_ref[...], gl_ref[0]) * (acc_scr[...] + partial_product())
        if final:
            out = _rms(out, fw_ref[...])
        o_ref[0] = out


def _mlp(x_all, norm_w, mod_l, mod_c, w1, w2, final_w, n_lat, tm, tf):
    bsz, tall, d = x_all.shape
    ff = w1.shape[1]
    final = final_w is not None
    in_specs = [pl.BlockSpec((1, tm, d), lambda b, i, j: (b, i, 0)),
                pl.BlockSpec((1, d), lambda b, i, j: (0, 0))]
    in_specs += _mod_specs(d, (3, 4, 5), 3)
    in_specs += [pl.BlockSpec((d, tf), lambda b, i, j: (0, j)),
                 pl.BlockSpec((tf, d), lambda b, i, j: (j, 0))]
    args = [x_all, norm_w.reshape(1, d)] + [mod_l, mod_c] * 3 + [w1, w2]
    if final:
        in_specs.append(pl.BlockSpec((1, d), lambda b, i, j: (0, 0)))
        args.append(final_w.reshape(1, d))
    return pl.pallas_call(
        functools.partial(_mlp_kernel, tm=tm, n_lat=n_lat, final=final),
        grid=(bsz, tall // tm, ff // tf),
        in_specs=in_specs,
        out_specs=pl.BlockSpec((1, tm, d), lambda b, i, j: (b, i, 0)),
        out_shape=jax.ShapeDtypeStruct((bsz, tall, d), F32),
        scratch_shapes=[pltpu.VMEM((tm, d), BF16), pltpu.VMEM((tm, d), F32)],
        compiler_params=_cparams(3),
        name="sq_relu_mlp",
    )(*args)


def _even_in_weight(w_in):
    sizes = (SSD_INNER, SSD_INNER + 2 * SSD_BC, 2 * SSD_HEADS, GLA_KEY, GLA_KEY, GLA_VAL,
             2 * GLA_GATE_RANK, GLA_VAL)
    offs = [0]
    for s in sizes:
        offs.append(offs[-1] + s)
    z, xbc, dt, q, k, v, lr, r = (w_in[:, offs[n]:offs[n + 1]] for n in range(8))
    pad = jnp.zeros((w_in.shape[0], LANE - dt.shape[1] - lr.shape[1]), w_in.dtype)
    main = jnp.concatenate([xbc, z, v, r, q * (GLA_DK ** -0.5), k], axis=1).astype(BF16)
    aux = jnp.concatenate([dt, lr, pad], axis=1).astype(BF16)
    return main, aux


def _s5_params(a_re, a_im, log_dt, b_re, b_im, c_re, c_im):
    delta = jnp.exp(log_dt.astype(F32))[..., None]
    mag = jnp.exp(a_re * delta)
    lbar_re, lbar_im = mag * jnp.cos(a_im * delta), mag * jnp.sin(a_im * delta)
    den = a_re * a_re + a_im * a_im
    zr = ((lbar_re - 1.0) * a_re + lbar_im * a_im) / den
    zi = (lbar_im * a_re - (lbar_re - 1.0) * a_im) / den
    bb_re = zr[..., None] * b_re - zi[..., None] * b_im
    bb_im = zr[..., None] * b_im + zi[..., None] * b_re
    n_slabs = S5_GROUPS // S5_SLAB
    eye = jnp.eye(S5_SLAB, dtype=F32)
    sw = S5_SLAB * S5_STATE

    def block_in(bb):
        bb = bb.reshape(2, n_slabs, S5_SLAB, S5_STATE, S5_GROUP)
        return jnp.einsum("dsgpc,gh->dsgchp", bb, eye).reshape(2, n_slabs, LANE, sw)

    def block_out(cc):
        cc = cc.reshape(n_slabs, S5_SLAB, S5_GROUP, S5_STATE)
        return jnp.einsum("sgcp,gh->sgphc", cc, eye).reshape(n_slabs, sw, LANE)

    bmat = jnp.concatenate([block_in(bb_re), block_in(bb_im)], axis=3).astype(BF16)
    cmat = jnp.concatenate([block_out(c_re), -block_out(c_im)], axis=1).astype(BF16)
    return (bmat, lbar_re.reshape(2, 1, S5_NSTATE), lbar_im.reshape(2, 1, S5_NSTATE), cmat)


def _layer_even(x_all, mod_l, mod_c, n_lat, tm, n_rows_out, tm_out, norm1_w, w_in, conv_w, conv_b, dt_bias,
                a_log, d_skip, ssd_norm_w, gate_w, gate_b, gla_norm_w, w_out):
    w, w_aux = _even_in_weight(w_in)
    n = w.shape[1]
    n_xbc = SSD_INNER + 2 * SSD_BC
    gates = ((n_xbc, n_xbc + SSD_INNER), (n_xbc + SSD_INNER + GLA_VAL, n_xbc + SSD_INNER + 2 * GLA_VAL))
    proj, aux = _project(x_all, norm1_w, mod_l, mod_c, w, w_aux, n_lat, tm, _largest_divisor(n, 2 * LANE, PROJ_TN),
                         extra_token_major=False, silu_ranges=gates)
    c_z, c_v, c_r = n_xbc // SSD_INNER, n_xbc // GLA_VAL + 1, n_xbc // GLA_VAL + 2
    c_q = (n_xbc + 3 * SSD_INNER) // GLA_KEY
    xbc = _conv_silu(proj, conv_w, conv_b, n_lat, n_xbc)

    neg_a = -jnp.exp(a_log.astype(F32))
    dt_bias = dt_bias.astype(F32)
    d_wide = jnp.repeat(d_skip.astype(F32), SSD_HEADDIM).reshape(1, SSD_INNER)
    y_f = _ssd_scan(xbc, proj, aux, dt_bias, neg_a, c_z, n_lat, 0)
    y_mix = _ssd_scan(xbc, proj, aux, dt_bias, neg_a, c_z, n_lat, 1,
                      (y_f, d_wide, ssd_norm_w.reshape(1, SSD_INNER)))

    cols = {"q": c_q, "k": c_q + 1, "v": c_v, "gate": c_r}
    gparams = [(gate_w[d].astype(BF16), gate_b[d].reshape(1, GLA_KEY).astype(F32)) for d in range(2)]
    o_dirs = _lin_scan("gla", proj, cols, gparams, n_lat, aux=aux)
    gla_finish = (o_dirs, proj, c_r, gla_norm_w.astype(F32).reshape(1, GLA_DV))

    w_out = w_out.astype(BF16)
    return _out_proj0(x_all, y_mix, gla_finish, w_out[:SSD_INNER], w_out[SSD_INNER:], mod_l, mod_c, n_lat,
                      n_rows_out, tm_out)


def _layer_odd(x_all, mod_l, mod_c, n_lat, tm, n_rows_out, tm_out, norm1_w, w_in, lb, hgrn_norm_w, a_re, a_im,
               log_dt, b_re, b_im, c_re, c_im, d_skip, glu_w, glu_b, w_out):
    bsz = x_all.shape[0]
    n_main = 5 * HGRN_WIDTH
    w_main = w_in[:, :n_main].astype(BF16)
    w_u = w_in[:, n_main:].astype(BF16)
    proj, u_t = _project(x_all, norm1_w, mod_l, mod_c, w_main, w_u, n_lat, tm,
                         _largest_divisor(n_main, 2 * LANE, PROJ_TN),
                         silu_ranges=((0, HGRN_WIDTH), (4 * HGRN_WIDTH, 5 * HGRN_WIDTH)))

    lb = lb.astype(F32).reshape(2, 1, HGRN_WIDTH)
    cols = {"q": 0, "v": 1, "aux": 2, "gate": 4}
    o_dirs = _lin_scan("hgrn", proj, cols, [(lb[d], 1.0 - lb[d]) for d in range(2)], n_lat)
    hgrn_finish = (o_dirs, proj, cols["gate"], hgrn_norm_w.astype(F32).reshape(1, HGRN_DV))

    bmat, lam_re, lam_im, cmat = _s5_params(a_re.astype(F32), a_im.astype(F32), log_dt, b_re.astype(F32),
                                            b_im.astype(F32), c_re.astype(F32), c_im.astype(F32))
    y_dirs = _s5_scan(u_t, bmat, lam_re, lam_im, cmat, bsz, n_lat)

    w_out = w_out.astype(BF16)
    return _out_proj1(x_all, hgrn_finish, y_dirs, u_t, d_skip.astype(F32).reshape(1, S5_WIDTH),
                      glu_w.astype(BF16), glu_b.astype(F32).reshape(1, S5_WIDTH),
                      w_out[:HGRN_WIDTH], w_out[HGRN_WIDTH:], mod_l, mod_c, n_lat, n_rows_out, tm_out)


def kernel(x, c, ctx, c_ctx, ada_w, ada_b, norm1_w, norm2_w, ssd_gla_w_in, ssd_conv_w, ssd_conv_b, ssd_dt_bias, ssd_a_log, ssd_d, ssd_norm_w, gla_gate_w, gla_gate_b, gla_norm_w, ssd_gla_w_out, hgrn_s5_w_in, hgrn_lb_logits, hgrn_norm_w, s5_a_re, s5_a_im, s5_log_dt, s5_b_re, s5_b_im, s5_c_re, s5_c_im, s5_d, s5_glu_w, s5_glu_b, hgrn_s5_w_out, mlp_w1, mlp_w2, final_norm_w):
    bsz, n_lat, d = x.shape
    ctx_len = ctx.shape[1]
    depth = ada_w.shape[0]
    tall = n_lat + ctx_len
    assert bsz % 8 == 0 and ctx_len % SCAN_BLOCK == 0 and n_lat % SCAN_BLOCK == 0 and n_lat % GRID_W == 0
    tf = 2048

    n_rows = -(-(bsz + 1) // 8) * 8
    cvec = jnp.concatenate([c, c_ctx[None, :], jnp.zeros((n_rows - bsz - 1, d), c.dtype)], axis=0)
    mod = _modulation(cvec.astype(F32), ada_w, ada_b)

    p_lb = jax.nn.softmax(hgrn_lb_logits.astype(F32), axis=0)
    lb_all = jnp.cumsum(p_lb, axis=0) - p_lb[0]

    x_all = jnp.concatenate([x, ctx], axis=1).astype(F32)
    tm_all = _largest_divisor(tall, 16, 1056)
    for layer in range(depth):
        j = layer // 2
        last = layer == depth - 1
        n_rows = n_lat if last else tall
        tm_out = _largest_divisor(n_rows, 16, 1056)
        mod_l = mod[layer, :bsz].reshape(bsz, 1, N_MOD * d)
        mod_c = mod[layer, bsz:bsz + 1]
        if layer % 2 == 0:
            x_all = _layer_even(x_all, mod_l, mod_c, n_lat, tm_all, n_rows, tm_out, norm1_w[layer],
                                ssd_gla_w_in[j], ssd_conv_w[j], ssd_conv_b[j], ssd_dt_bias[j], ssd_a_log[j],
                                ssd_d[j], ssd_norm_w[j], gla_gate_w[j], gla_gate_b[j], gla_norm_w[j],
                                ssd_gla_w_out[j])
        else:
            x_all = _layer_odd(x_all, mod_l, mod_c, n_lat, tm_all, n_rows, tm_out, norm1_w[layer],
                               hgrn_s5_w_in[j], lb_all[layer], hgrn_norm_w[j], s5_a_re[j], s5_a_im[j],
                               s5_log_dt[j], s5_b_re[j], s5_b_im[j], s5_c_re[j], s5_c_im[j], s5_d[j],
                               s5_glu_w[j], s5_glu_b[j], hgrn_s5_w_out[j])
        x_all = _mlp(x_all, norm2_w[layer], mod_l, mod_c, mlp_w1[layer].astype(BF16),
                     mlp_w2[layer].astype(BF16), final_norm_w if last else None, n_lat, tm_out, tf)
    return x_all.astype(x.dtype)
```

```python
import functools
import math

import jax
import jax.numpy as jnp
from jax import lax
from jax.experimental import pallas as pl
from jax.experimental.pallas import tpu as pltpu

F32 = jnp.float32
BF16 = jnp.bfloat16

GRID_W = 64
NORM_EPS = 1e-6
N_MOD = 6
SSD_HEADDIM = 64
SSD_HEADS = 16
SSD_GROUPS = 4
SSD_STATE = 128
SSD_CHUNK = 128
GLA_HEADS = 8
GLA_DK = 64
GLA_DV = 128
GLA_GATE_RANK = 16
GLA_GATE_NORM = 16.0
HGRN_HEADS = 8
HGRN_DK = 128
HGRN_DV = 128
S5_GROUP = 16
S5_GROUPS = 24
S5_STATE = 64
LIN_CHUNK = 64

SSD_INNER = SSD_HEADS * SSD_HEADDIM
SSD_BC = SSD_GROUPS * SSD_STATE
GLA_KEY = GLA_HEADS * GLA_DK
GLA_VAL = GLA_HEADS * GLA_DV
HGRN_WIDTH = HGRN_HEADS * HGRN_DV
S5_WIDTH = S5_GROUPS * S5_GROUP
S5_NSTATE = S5_GROUPS * S5_STATE

VMEM_LIMIT_BYTES = 56 * 1024 * 1024
LANE = 128
PROJ_TN = 2560
SCAN_BLOCK = 256
SCAN_BATCH = 2
S5_CHUNK = 128
S5_SLAB = LANE // S5_GROUP


def _cparams(n_axes):
    return pltpu.CompilerParams(dimension_semantics=("arbitrary",) * n_axes,
                                vmem_limit_bytes=VMEM_LIMIT_BYTES)


def _largest_divisor(n, multiple, cap):
    best = None
    for d in range(multiple, min(n, cap) + 1, multiple):
        if n % d == 0:
            best = d
    assert best is not None, (n, multiple, cap)
    return best


_NEG_LOG2E = -1.4426950408889634


def _sigmoid(x):
    return 1.0 / (1.0 + jnp.exp2(x * _NEG_LOG2E))


def _silu(x):
    return x * _sigmoid(x)


def _softplus(x):
    return jnp.maximum(x, 0.0) + jnp.log1p(jnp.exp(-jnp.abs(x)))


def _log_sigmoid(x):
    return -_softplus(-x)


def _rms(x, w):
    return x * lax.rsqrt(jnp.mean(x * x, axis=-1, keepdims=True) + NORM_EPS) * w


def _dot(a, b, dims=(((1,), (0,)), ((), ())), precision=None):
    return lax.dot_general(a, b, dims, precision=precision, preferred_element_type=F32)


def _split3(v):
    hi = v.astype(BF16)
    r1 = v - hi.astype(F32)
    mid = r1.astype(BF16)
    lo = (r1 - mid.astype(F32)).astype(BF16)
    return hi, mid, lo


def _tri3(mask):
    tri = mask.astype(BF16)
    return jnp.concatenate([tri, tri, tri], axis=1)


def _cumsum_rows(tri2, v):
    hi = v.astype(BF16)
    lo = (v - hi.astype(F32)).astype(BF16)
    return _dot(tri2, jnp.concatenate([hi, lo], axis=0))


_NT = (((1,), (1,)), ((), ()))
_TN = (((0,), (0,)), ((), ()))
_TT = (((0,), (1,)), ((), ()))


def _mod_kernel(c_ref, w_ref, b_ref, o_ref):
    a = _silu(c_ref[...]).astype(BF16)
    o_ref[0] = _dot(a, w_ref[0].astype(BF16)) + b_ref[0]


def _modulation(cvec, ada_w, ada_b):
    depth, d, n = ada_w.shape
    rows = cvec.shape[0]
    tn = _largest_divisor(n, LANE, 1024)
    return pl.pallas_call(
        _mod_kernel,
        grid=(depth, n // tn),
        in_specs=[pl.BlockSpec((rows, d), lambda l, j: (0, 0)),
                  pl.BlockSpec((1, d, tn), lambda l, j: (l, 0, j)),
                  pl.BlockSpec((1, 1, tn), lambda l, j: (l, 0, j))],
        out_specs=pl.BlockSpec((1, rows, tn), lambda l, j: (l, 0, j)),
        out_shape=jax.ShapeDtypeStruct((depth, rows, n), F32),
        compiler_params=_cparams(2),
        name="adaln_mod",
    )(cvec, ada_w, ada_b.reshape(depth, 1, n))


def _row_select(i, tm, n_lat, ctx_val, lat_val):
    row = i * tm + lax.broadcasted_iota(jnp.int32, (tm, 1), 0)
    return jnp.where(row >= n_lat, ctx_val, lat_val)


def _store_norm_modulated(h_ref, x, nw, shift_l, shift_c, scale_l, scale_c, i, tm, n_lat):
    xn = x * lax.rsqrt(jnp.mean(x * x, axis=-1, keepdims=True) + NORM_EPS)
    gain = _row_select(i, tm, n_lat, nw * (1.0 + scale_c), nw * (1.0 + scale_l))
    h_ref[...] = (xn * gain + _row_select(i, tm, n_lat, shift_c, shift_l)).astype(h_ref.dtype)


def _mod_specs(d, cols, n_grid_axes):
    specs = []
    for k in cols:
        if n_grid_axes == 2:
            specs.append(pl.BlockSpec((1, 1, d), lambda b, i, k=k: (b, 0, k)))
            specs.append(pl.BlockSpec((1, d), lambda b, i, k=k: (0, k)))
        else:
            specs.append(pl.BlockSpec((1, 1, d), lambda b, i, j, k=k: (b, 0, k)))
            specs.append(pl.BlockSpec((1, d), lambda b, i, j, k=k: (0, k)))
    return specs


def _proj_kernel(*refs, tm, n_lat, has_extra, silu_cols):
    if has_extra:
        (x_ref, nw_ref, shl_ref, shc_ref, scl_ref, scc_ref, w_ref, wx_ref, o_ref, ox_ref, h_scr) = refs
    else:
        (x_ref, nw_ref, shl_ref, shc_ref, scl_ref, scc_ref, w_ref, o_ref, h_scr) = refs
    i = pl.program_id(1)
    j = pl.program_id(2)

    @pl.when(j == 0)
    def _():
        _store_norm_modulated(h_scr, x_ref[0], nw_ref[...], shl_ref[0], shc_ref[...], scl_ref[0], scc_ref[...],
                              i, tm, n_lat)
        if has_extra:
            ox_ref[...] = _dot(h_scr[...], wx_ref[...]).reshape(ox_ref.shape)

    for jj, ranges in enumerate(silu_cols):
        @pl.when(j == jj)
        def _(ranges=ranges):
            o = _dot(h_scr[...], w_ref[...])
            if not ranges:
                o_ref[0] = o.astype(o_ref.dtype)
            edge = 0
            for start, stop in ranges:
                if start > edge:
                    o_ref[0, :, edge:start] = o[:, edge:start].astype(o_ref.dtype)
                o_ref[0, :, start:stop] = _silu(o[:, start:stop]).astype(o_ref.dtype)
                edge = stop
            if ranges and edge < o.shape[1]:
                o_ref[0, :, edge:] = o[:, edge:].astype(o_ref.dtype)


def _project(x_all, norm_w, mod_l, mod_c, w, w_extra, n_lat, tm, tn, extra_token_major=True, silu_ranges=()):
    bsz, tall, d = x_all.shape
    n = w.shape[1]
    has_extra = w_extra is not None
    silu_cols = tuple(
        tuple((max(a, jj * tn) - jj * tn, min(b, (jj + 1) * tn) - jj * tn)
              for a, b in sorted(silu_ranges) if a < (jj + 1) * tn and b > jj * tn)
        for jj in range(n // tn))
    in_specs = [pl.BlockSpec((1, tm, d), lambda b, i, j: (b, i, 0)),
                pl.BlockSpec((1, d), lambda b, i, j: (0, 0))]
    in_specs += _mod_specs(d, (0, 1), 3)
    in_specs.append(pl.BlockSpec((d, tn), lambda b, i, j: (0, j)))
    args = [x_all, norm_w.reshape(1, d), mod_l, mod_c, mod_l, mod_c, w]
    out_specs = [pl.BlockSpec((1, tm, tn), lambda b, i, j: (b, i, j))]
    out_shape = [jax.ShapeDtypeStruct((bsz, tall, n), BF16)]
    if has_extra:
        nx = w_extra.shape[1]
        in_specs.append(pl.BlockSpec((d, nx), lambda b, i, j: (0, 0)))
        args.append(w_extra)
        if extra_token_major:
            out_specs.append(pl.BlockSpec((tm, nx), lambda b, i, j: (i, b)))
            out_shape.append(jax.ShapeDtypeStruct((tall, bsz * nx), F32))
        else:
            out_specs.append(pl.BlockSpec((1, tm, nx), lambda b, i, j: (b, i, 0)))
            out_shape.append(jax.ShapeDtypeStruct((bsz, tall, nx), F32))
    out = pl.pallas_call(
        functools.partial(_proj_kernel, tm=tm, n_lat=n_lat, has_extra=has_extra, silu_cols=silu_cols),
        grid=(bsz, tall // tm, n // tn),
        in_specs=in_specs,
        out_specs=out_specs,
        out_shape=out_shape,
        scratch_shapes=[pltpu.VMEM((tm, d), BF16)],
        compiler_params=_cparams(3),
        name="norm_mod_proj",
    )(*args)
    return out if has_extra else out[0]


def _conv_kernel(main_ref, prev_ref, next_ref, w_ref, b_ref, o_ref, *, tt, n_lat, tall):
    i = pl.program_id(1)
    te = tt + 2 * GRID_W
    p = i * tt - GRID_W + lax.broadcasted_iota(jnp.int32, (te, 1), 0)
    is_ctx = p >= n_lat
    col = jnp.bitwise_and(p, GRID_W - 1)
    has_left = jnp.where(is_ctx, p - n_lat, col) > 0
    has_right = jnp.where(is_ctx, p - (tall - 1), col - (GRID_W - 1)) < 0
    w = w_ref[...]

    def conv(interior):
        ext = jnp.concatenate([prev_ref[0], main_ref[0], next_ref[0]], axis=0).astype(F32)
        own = slice(GRID_W, GRID_W + tt)
        if interior:
            as_left = jnp.where(has_right, ext, 0.0)
            as_right = jnp.where(has_left, ext, 0.0)
        acc = jnp.zeros((tt, w.shape[1]), F32) + b_ref[...]
        for dy in (-1, 0, 1):
            rs = slice(GRID_W + GRID_W * dy, GRID_W + GRID_W * dy + tt)
            k0 = 3 * (dy + 1)
            if interior:
                left, right = pltpu.roll(as_left[rs], 1, 0), pltpu.roll(as_right[rs], tt - 1, 0)
            else:
                left = jnp.where(has_left[own], pltpu.roll(ext[rs], 1, 0), 0.0)
                right = jnp.where(has_right[own], pltpu.roll(ext[rs], tt - 1, 0), 0.0)
            t = ext[rs] * w[k0 + 1:k0 + 2] + left * w[k0:k0 + 1] + right * w[k0 + 2:k0 + 3]
            if not interior and dy != 0:
                q = p[own]
                if dy == -1:
                    ok = jnp.where(q >= n_lat, 0, q) >= GRID_W
                else:
                    ok = jnp.where(q >= n_lat, n_lat, q) < n_lat - GRID_W
                t = jnp.where(ok, t, 0.0)
            acc = acc + t
        o_ref[0] = _silu(acc).astype(o_ref.dtype)

    interior = jnp.logical_and(i * tt >= GRID_W, (i + 1) * tt <= n_lat - GRID_W)
    pl.when(interior)(lambda: conv(True))
    pl.when(jnp.logical_not(interior))(lambda: conv(False))


def _conv_silu(proj, conv_w, conv_b, n_lat, n_ch):
    bsz, tall, _ = proj.shape
    n_rows = tall // GRID_W
    tt = _largest_divisor(tall, GRID_W, 768)
    assert n_lat % GRID_W == 0 and n_lat // tt == (tall - 1) // tt
    r = tt // GRID_W
    tc = 1024
    return pl.pallas_call(
        functools.partial(_conv_kernel, tt=tt, n_lat=n_lat, tall=tall),
        grid=(bsz, tall // tt, n_ch // tc),
        in_specs=[pl.BlockSpec((1, tt, tc), lambda b, i, c: (b, i, c)),
                  pl.BlockSpec((1, GRID_W, tc), lambda b, i, c: (b, jnp.maximum(i * r - 1, 0), c)),
                  pl.BlockSpec((1, GRID_W, tc), lambda b, i, c: (b, jnp.minimum((i + 1) * r, n_rows - 1), c)),
                  pl.BlockSpec((9, tc), lambda b, i, c: (0, c)),
                  pl.BlockSpec((1, tc), lambda b, i, c: (0, c))],
        out_specs=pl.BlockSpec((1, tt, tc), lambda b, i, c: (b, i, c)),
        out_shape=jax.ShapeDtypeStruct((bsz, tall, n_ch), BF16),
        compiler_params=_cparams(3),
        name="dwconv_silu",
    )(proj, proj, proj, conv_w.reshape(9, n_ch), conv_b.reshape(1, n_ch))


def _scan_block(s, n_lat_blocks, n_blocks, reverse):
    n_ctx_blocks = n_blocks - n_lat_blocks
    if not reverse:
        return jnp.where(s < n_ctx_blocks, n_lat_blocks + s, s - n_ctx_blocks)
    return n_blocks - 1 - s


def _tri_mask(c, reverse):
    ri = lax.broadcasted_iota(jnp.int32, (c, c), 0)
    ci = lax.broadcasted_iota(jnp.int32, (c, c), 1)
    return (ci >= ri) if reverse else (ci <= ri)


def _ssd_expand_matrix():
    eye = jnp.eye(SSD_HEADS, dtype=F32)
    e_head = jnp.repeat(eye, SSD_HEADDIM, axis=1)
    e_seg = jnp.repeat(eye, SSD_CHUNK, axis=1)
    zh = jnp.zeros_like(e_head)
    zs = jnp.zeros_like(e_seg)
    blk = jnp.concatenate([
        jnp.concatenate([e_head, zh, zh, zs], axis=1),
        jnp.concatenate([zh, e_head, zh, zs], axis=1),
        jnp.concatenate([zh, zh, e_head, zs], axis=1),
        jnp.concatenate([zh, zh, zh, e_seg], axis=1)], axis=0)
    return jnp.concatenate([blk, blk, blk], axis=0).astype(BF16)


def _ssd_kernel(*refs, direction, finish):
    if finish:
        (x_ref, bm_ref, cm_ref, dtlr_ref, dtb_ref, nega_ref, exp_ref, z_ref, yf_ref, dsk_ref, nw_ref,
         o_ref, st_ref) = refs
    else:
        (x_ref, bm_ref, cm_ref, dtlr_ref, dtb_ref, nega_ref, exp_ref, o_ref, st_ref) = refs
    reverse = direction == 1
    c = SSD_CHUNK
    p = SSD_HEADDIM
    gw = SSD_INNER // SSD_GROUPS
    hpg = SSD_HEADS // SSD_GROUPS

    @pl.when(pl.program_id(1) == 0)
    def _():
        st_ref[...] = jnp.zeros_like(st_ref)

    mask = _tri_mask(c, reverse)
    tri3 = _tri3(mask)
    last = 0 if reverse else c - 1
    n_chunks = x_ref.shape[1] // c
    chunk_order = range(n_chunks - 1, -1, -1) if reverse else range(n_chunks)
    order = [(bb, cc) for bb in range(x_ref.shape[0]) for cc in chunk_order]
    rows = {ck: slice(ck[1] * c, (ck[1] + 1) * c) for ck in order}
    groups = range(SSD_GROUPS)
    g_cols = [slice(g * gw, (g + 1) * gw) for g in groups]
    n_cols = [slice(g * SSD_STATE, (g + 1) * SSD_STATE) for g in groups]
    cb = {(ck, g): _dot(cm_ref[ck[0], rows[ck], n_cols[g]], bm_ref[ck[0], rows[ck], n_cols[g]], _NT)
          for ck in order for g in groups}
    prep = {}
    for ck in order:
        bb, rs = ck[0], rows[ck]
        x = x_ref[bb, rs, :].astype(F32)
        dt_raw = dtlr_ref[bb, rs, :][:, direction * SSD_HEADS:(direction + 1) * SSD_HEADS].astype(F32)
        dt = _softplus(dt_raw + dtb_ref[...])
        la3 = jnp.concatenate(_split3(dt * nega_ref[...]), axis=0)
        acum = _dot(tri3, la3)
        acum_t = _dot(la3, tri3, _TT)
        a_last = acum[last:last + 1]
        narrow = jnp.concatenate([dt, dt * jnp.exp(a_last - acum), jnp.exp(acum), acum], axis=1)
        wide = _dot(jnp.concatenate(_split3(narrow), axis=1), exp_ref[...])
        ea_w = wide[:, 2 * SSD_INNER:3 * SSD_INNER]
        prep[ck] = dict(
            x=x, bm=bm_ref[bb, rs, :], cm=cm_ref[bb, rs, :], wide=wide, acum_t=acum_t, ea_w=ea_w,
            xdt=(x * wide[:, :SSD_INNER]).astype(BF16),
            xw=(x * wide[:, SSD_INNER:2 * SSD_INNER]).astype(BF16),
            e_last=ea_w[last:last + 1])
    kv = {(ck, g): _dot(prep[ck]["bm"][:, n_cols[g]], prep[ck]["xw"][:, g_cols[g]], _TN)
          for ck in order for g in groups}
    scores, st_used = {}, {}
    for bb in range(x_ref.shape[0]):
        for g in groups:
            st = st_ref[bb, :, g_cols[g]]
            for cc in chunk_order:
                st_used[(bb, cc), g] = st.astype(BF16)
                st = st * prep[bb, cc]["e_last"][:, g_cols[g]] + kv[(bb, cc), g]
            st_ref[bb, :, g_cols[g]] = st
    for ck in order:
        for h in range(SSD_HEADS):
            a_i = prep[ck]["wide"][:, 3 * SSD_INNER + h * c:3 * SSD_INNER + (h + 1) * c]
            decay = jnp.exp(jnp.where(mask, a_i - prep[ck]["acum_t"][h:h + 1, :], -jnp.inf))
            scores[ck, h] = (cb[ck, h // hpg] * decay).astype(BF16)
    for ck in order:
        bb, rs = ck[0], rows[ck]
        x = prep[ck]["x"]
        y_groups = []
        for g in groups:
            ys = [_dot(scores[ck, h], prep[ck]["xdt"][:, h * p:(h + 1) * p])
                  for h in range(g * hpg, (g + 1) * hpg)]
            y_state = _dot(prep[ck]["cm"][:, n_cols[g]], st_used[ck, g]) * prep[ck]["ea_w"][:, g_cols[g]]
            y_groups.append(jnp.concatenate(ys, axis=1) + y_state)
        y = jnp.concatenate(y_groups, axis=1)
        if finish:
            z_act = z_ref[bb, rs, :].astype(F32)
            y = (y + yf_ref[bb, rs, :] + dsk_ref[...] * x) * z_act
            outs = []
            for g in range(SSD_GROUPS):
                sl = slice(g * gw, (g + 1) * gw)
                outs.append(_rms(y[:, sl], nw_ref[:, sl]))
            o_ref[bb, rs, :] = jnp.concatenate(outs, axis=1).astype(o_ref.dtype)
        else:
            o_ref[bb, rs, :] = y


def _ssd_scan(xbc, proj, aux, dt_bias, neg_a, z_col, n_lat, direction, finish_args=None):
    bsz, tall, _ = xbc.shape
    tb = SCAN_BLOCK
    nb, n_lat_blocks = tall // tb, n_lat // tb
    reverse = direction == 1
    finish = finish_args is not None
    expand = _ssd_expand_matrix()

    def tok(col):
        return lambda b, s: (b, _scan_block(s, n_lat_blocks, nb, reverse), col)

    nbb = SCAN_BATCH
    in_specs = [pl.BlockSpec((nbb, tb, SSD_INNER), tok(0)),
                pl.BlockSpec((nbb, tb, SSD_BC), tok(SSD_INNER // SSD_BC)),
                pl.BlockSpec((nbb, tb, SSD_BC), tok(SSD_INNER // SSD_BC + 1)),
                pl.BlockSpec((nbb, tb, LANE), tok(0)),
                pl.BlockSpec((1, SSD_HEADS), lambda b, s: (0, 0)),
                pl.BlockSpec((1, SSD_HEADS), lambda b, s: (0, 0)),
                pl.BlockSpec(expand.shape, lambda b, s: (0, 0))]
    args = [xbc, xbc, xbc, aux, dt_bias[direction:direction + 1], neg_a[direction:direction + 1], expand]
    if finish:
        y_f, d_skip_wide, norm_w = finish_args
        in_specs += [pl.BlockSpec((nbb, tb, SSD_INNER), tok(z_col)),
                     pl.BlockSpec((nbb, tb, SSD_INNER), tok(0)),
                     pl.BlockSpec((1, SSD_INNER), lambda b, s: (0, 0)),
                     pl.BlockSpec((1, SSD_INNER), lambda b, s: (0, 0))]
        args += [proj, y_f, d_skip_wide, norm_w]
    return pl.pallas_call(
        functools.partial(_ssd_kernel, direction=direction, finish=finish),
        grid=(bsz // nbb, nb),
        in_specs=in_specs,
        out_specs=pl.BlockSpec((nbb, tb, SSD_INNER), tok(0)),
        out_shape=jax.ShapeDtypeStruct((bsz, tall, SSD_INNER), BF16 if finish else F32),
        scratch_shapes=[pltpu.VMEM((nbb, SSD_STATE, SSD_INNER), F32)],
        compiler_params=_cparams(2),
        name="ssd_scan_bwd" if reverse else "ssd_scan_fwd",
    )(*args)


def _lin_kernel(*refs, mode, heads, dk, dv):
    n_in = 6 if mode == "gla" else 5
    ins = [refs[:n_in], refs[n_in:2 * n_in]]
    o_refs = refs[2 * n_in:2 * n_in + 2]
    st_ref = refs[2 * n_in + 2]
    c = LIN_CHUNK

    @pl.when(pl.program_id(1) == 0)
    def _():
        st_ref[...] = jnp.zeros_like(st_ref)

    tb = o_refs[0].shape[1]
    nc = tb // c
    hpt = LANE // dk
    n_tiles = heads // hpt
    span = n_tiles if mode == "gla" else 1
    gw = span * LANE
    n_groups = n_tiles // span
    hpg = hpt * span
    zeros = jnp.zeros((c, LANE), BF16)
    ri = lax.broadcasted_iota(jnp.int32, (tb, tb), 0)
    ci = lax.broadcasted_iota(jnp.int32, (tb, tb), 1)
    c_shift = c.bit_length() - 1
    same_chunk = jnp.right_shift(ri, c_shift) == jnp.right_shift(ci, c_shift)

    def chunk_blocks(a):
        cols = []
        for b in range(nc):
            cols.append(jnp.concatenate(
                [a[cc * c:(cc + 1) * c] if cc == b else zeros for cc in range(nc)], axis=0))
        return jnp.concatenate(cols, axis=1)

    def tile_cols(h):
        t = (h // hpt) % span
        return slice(t * LANE, (t + 1) * LANE)

    def make_stream(direction, bb):
        if mode == "gla":
            q_ref, k_ref, v_ref, aux_ref, p1_ref, p2_ref = ins[direction]
        else:
            q_ref, v_ref, aux_ref, p1_ref, p2_ref = ins[direction]
            k_ref = None
        o_ref = o_refs[direction]
        reverse = direction == 1
        chunks = range(nc - 1, -1, -1) if reverse else range(nc)
        last = 0 if reverse else c - 1
        if reverse:
            bd_mask = jnp.where(same_chunk, ci - ri, -1) >= 0
        else:
            bd_mask = jnp.where(same_chunk, ci - ri, 1) <= 0
        tri = _tri_mask(c, reverse).astype(BF16)
        tri2 = jnp.concatenate([tri, tri], axis=1)
        groups, work = {}, {}

        def prep(g):
            ls = slice(g * gw, (g + 1) * gw)
            if mode == "gla":
                q = q_ref[bb, :, ls].astype(F32)
                k = k_ref[bb, :, ls].astype(F32)
                off = 2 * SSD_HEADS + direction * GLA_GATE_RANK
                lr = aux_ref[bb][:, off:off + GLA_GATE_RANK].astype(BF16)
                lg = _log_sigmoid(_dot(lr, p1_ref[:, ls]) + p2_ref[:, ls]) * (1.0 / GLA_GATE_NORM)
            else:
                q = q_ref[bb, :, ls].astype(F32)
                f_raw = aux_ref[bb, :, ls].astype(F32)
                e = jnp.exp2(jnp.abs(f_raw) * _NEG_LOG2E)
                r = 1.0 / (1.0 + e)
                forget = p1_ref[:, ls] + p2_ref[:, ls] * jnp.where(f_raw >= 0.0, r, e * r)
                lg = jnp.log(forget)
                k = 1.0 - forget
            gcum = jnp.concatenate([_cumsum_rows(tri2, lg[cc * c:(cc + 1) * c]) for cc in range(nc)], axis=0)
            e_last = [jnp.exp(gcum[cc * c + last:cc * c + last + 1]) for cc in range(nc)]
            e_rows = jnp.concatenate([jnp.broadcast_to(e, (c, gw)) for e in e_last], axis=0)
            e_gcum = jnp.exp(gcum)
            q_decf = q * e_gcum
            if hpt > 1:
                head_of_lane = jnp.bitwise_and(jnp.right_shift(
                    lax.broadcasted_iota(jnp.int32, (1, gw), 1), dk.bit_length() - 1), hpt - 1)
                q_dec = [jnp.where(head_of_lane == r, q_decf, 0.0).astype(BF16) for r in range(hpt)]
            else:
                q_dec = [q_decf.astype(BF16)]
            k_invf = k * (1.0 / e_gcum)
            groups[g] = dict(q_dec=q_dec, k_inv=k_invf.astype(BF16), k_end=(k_invf * e_rows).astype(BF16),
                             e_last=e_last)

        def products(h):
            gp, ts = groups[h // hpg], tile_cols(h)
            qh = gp["q_dec"][h % hpt][:, ts]
            vh = v_ref[bb, :, h * dv:(h + 1) * dv]
            work[h] = dict(
                qh=qh, vh=vh, scores=_dot(qh, gp["k_inv"][:, ts], _NT),
                kv_t=_dot(vh, chunk_blocks(gp["k_end"][:, ts]), _TN))

        def mask_and_chain(h):
            w = work[h]
            e_last = groups[h // hpg]["e_last"]
            w["att"] = jnp.where(bd_mask, w.pop("scores"), 0.0).astype(BF16)
            st = st_ref[direction, bb, h]
            used = [None] * nc
            for cc in chunks:
                used[cc] = st.astype(BF16)
                st = st * e_last[cc][:, tile_cols(h)] + w["kv_t"][:, cc * LANE:(cc + 1) * LANE]
            st_ref[direction, bb, h] = st
            w["used"] = used
            del w["kv_t"]

        def outputs(h):
            w = work[h]
            o_state = jnp.concatenate(
                [_dot(w["qh"][cc * c:(cc + 1) * c], w["used"][cc], _NT) for cc in range(nc)], axis=0)
            w["o"] = _dot(w["att"], w["vh"]) + o_state

        def emit(h):
            o_ref[bb, :, h * dv:(h + 1) * dv] = work.pop(h)["o"]

        return dict(prep=prep, products=products, mask_and_chain=mask_and_chain, outputs=outputs, emit=emit)

    streams = [make_stream(d, bb) for bb in range(o_refs[0].shape[0]) for d in (0, 1)]

    def heads_of(g):
        return range(g * hpg, (g + 1) * hpg) if 0 <= g < n_groups else ()

    lag = 2 if n_groups > 1 else 1
    for it in range(n_groups + 4 * lag):
        for stage, delay in (("products", lag), ("outputs", 3 * lag)):
            for st in streams:
                for h in heads_of(it - delay):
                    st[stage](h)
        if it < n_groups:
            for st in streams:
                st["prep"](it)
        for stage, delay in (("mask_and_chain", 2 * lag), ("emit", 4 * lag)):
            for st in streams:
                for h in heads_of(it - delay):
                    st[stage](h)


def _lin_scan(mode, proj, cols, params, n_lat, aux=None):
    bsz, tall, _ = proj.shape
    if mode == "gla":
        heads, dk, dv = GLA_HEADS, GLA_DK, GLA_DV
    else:
        heads, dk, dv = HGRN_HEADS, HGRN_DK, HGRN_DV
    kw, vw = heads * dk, heads * dv
    tb, nbb = SCAN_BLOCK, SCAN_BATCH
    nb, n_lat_blocks = tall // tb, n_lat // tb

    def const2(shape):
        return pl.BlockSpec(shape, lambda b, s: (0, 0))

    in_specs, args = [], []
    for direction in (0, 1):
        def tok(col, reverse=direction == 1):
            return lambda b, s: (b, _scan_block(s, n_lat_blocks, nb, reverse), col)
        p1, p2 = params[direction]
        if mode == "gla":
            in_specs += [pl.BlockSpec((nbb, tb, kw), tok(cols["q"])),
                         pl.BlockSpec((nbb, tb, kw), tok(cols["k"])),
                         pl.BlockSpec((nbb, tb, vw), tok(cols["v"])),
                         pl.BlockSpec((nbb, tb, LANE), tok(0)),
                         const2(p1.shape), const2(p2.shape)]
            args += [proj, proj, proj, aux, p1, p2]
        else:
            in_specs += [pl.BlockSpec((nbb, tb, kw), tok(cols["q"])),
                         pl.BlockSpec((nbb, tb, vw), tok(cols["v"])),
                         pl.BlockSpec((nbb, tb, kw), tok(cols["aux"] + direction)),
                         const2(p1.shape), const2(p2.shape)]
            args += [proj, proj, proj, p1, p2]
    out_specs = [pl.BlockSpec((nbb, tb, vw), lambda b, s: (b, _scan_block(s, n_lat_blocks, nb, False), 0)),
                 pl.BlockSpec((nbb, tb, vw), lambda b, s: (b, _scan_block(s, n_lat_blocks, nb, True), 0))]
    return pl.pallas_call(
        functools.partial(_lin_kernel, mode=mode, heads=heads, dk=dk, dv=dv),
        grid=(bsz // nbb, nb),
        in_specs=in_specs,
        out_specs=out_specs,
        out_shape=[jax.ShapeDtypeStruct((bsz, tall, vw), F32)] * 2,
        scratch_shapes=[pltpu.VMEM((2, SCAN_BATCH, heads, dv, LANE), F32)],
        compiler_params=_cparams(2),
        name=f"{mode}_scan",
    )(*args)


def _s5_kernel(uf_ref, ub_ref, bmat_ref, lre_ref, lim_ref, cmat_ref, of_ref, ob_ref, h_ref, ut_ref, yt_ref, st_ref,
               *, bsz):
    steps = S5_CHUNK
    n_slabs = bmat_ref.shape[1]
    sw = bmat_ref.shape[3] // 2
    width = n_slabs * LANE
    re_cols = [slice(2 * s * sw, (2 * s + 1) * sw) for s in range(n_slabs)]
    im_cols = [slice((2 * s + 1) * sw, (2 * s + 2) * sw) for s in range(n_slabs)]
    both = [slice(2 * s * sw, (2 * s + 2) * sw) for s in range(n_slabs)]

    @pl.when(pl.program_id(0) == 0)
    def _():
        st_ref[...] = jnp.zeros_like(st_ref)

    def inputs(d, u_ref):
        for b in range(bsz):
            for s in range(n_slabs):
                ut_ref[d, s, pl.ds(b, steps, stride=bsz), :] = (
                    u_ref[:, b * width + s * LANE:b * width + (s + 1) * LANE])
        for s in range(n_slabs):
            h_ref[d, :, both[s]] = _dot(ut_ref[d, s].astype(BF16), bmat_ref[d, s])

    def scan(d):
        lam_re = [jnp.broadcast_to(lre_ref[d:d + 1, s * sw:(s + 1) * sw], (bsz, sw)) for s in range(n_slabs)]
        lam_im = [jnp.broadcast_to(lim_ref[d:d + 1, s * sw:(s + 1) * sw], (bsz, sw)) for s in range(n_slabs)]
        hr = [st_ref[d, :, re_cols[s]] for s in range(n_slabs)]
        hi = [st_ref[d, :, im_cols[s]] for s in range(n_slabs)]
        for tt in range(steps):
            t = tt if d == 0 else steps - 1 - tt
            rows = slice(t * bsz, (t + 1) * bsz)
            for s in range(n_slabs):
                nr = lam_re[s] * hr[s] - lam_im[s] * hi[s] + h_ref[d, rows, re_cols[s]]
                ni = lam_re[s] * hi[s] + lam_im[s] * hr[s] + h_ref[d, rows, im_cols[s]]
                h_ref[d, rows, re_cols[s]] = nr
                h_ref[d, rows, im_cols[s]] = ni
                hr[s], hi[s] = nr, ni
        for s in range(n_slabs):
            st_ref[d, :, re_cols[s]] = hr[s]
            st_ref[d, :, im_cols[s]] = hi[s]

    def outputs(d):
        for s in range(n_slabs):
            yt_ref[d, s] = _dot(h_ref[d, :, both[s]].astype(BF16), cmat_ref[s])

    def emit(d, o_ref):
        for b in range(bsz):
            for s in range(n_slabs):
                o_ref[:, b * width + s * LANE:b * width + (s + 1) * LANE] = (
                    yt_ref[d, s, pl.ds(b, steps, stride=bsz), :])

    inputs(0, uf_ref)
    inputs(1, ub_ref)
    scan(0)
    outputs(0)
    scan(1)
    outputs(1)
    emit(0, of_ref)
    emit(1, ob_ref)


def _s5_scan(u_t, bmat, lam_re, lam_im, cmat, bsz, n_lat):
    tall, bw = u_t.shape
    width = bw // bsz
    nch, n_lat_chunks = tall // S5_CHUNK, n_lat // S5_CHUNK
    n_state = lam_re.shape[-1]
    rows = S5_CHUNK * bsz
    fwd = lambda s: (_scan_block(s, n_lat_chunks, nch, False), 0)
    bwd = lambda s: (_scan_block(s, n_lat_chunks, nch, True), 0)
    whole = lambda a: pl.BlockSpec(a.shape, lambda s: (0,) * a.ndim)
    lam_re, lam_im = lam_re.reshape(2, n_state), lam_im.reshape(2, n_state)
    return pl.pallas_call(
        functools.partial(_s5_kernel, bsz=bsz),
        grid=(nch,),
        in_specs=[pl.BlockSpec((S5_CHUNK, bw), fwd), pl.BlockSpec((S5_CHUNK, bw), bwd),
                  whole(bmat), whole(lam_re), whole(lam_im), whole(cmat)],
        out_specs=[pl.BlockSpec((S5_CHUNK, bw), fwd), pl.BlockSpec((S5_CHUNK, bw), bwd)],
        out_shape=[jax.ShapeDtypeStruct((tall, bw), F32)] * 2,
        scratch_shapes=[pltpu.VMEM((2, rows, 2 * n_state), F32),
                        pltpu.VMEM((2, width // LANE, rows, LANE), F32),
                        pltpu.VMEM((2, width // LANE, rows, LANE), F32),
                        pltpu.VMEM((2, bsz, 2 * n_state), F32)],
        compiler_params=_cparams(1),
        name="s5_scan",
    )(u_t, u_t, bmat, lam_re, lam_im, cmat)


def _lin_finish(of_ref, ob_ref, gate_ref, nw_ref):
    o = of_ref[0] + ob_ref[0]
    dv = nw_ref.shape[1]
    parts = [_rms(o[:, h * dv:(h + 1) * dv], nw_ref[...]) for h in range(o.shape[1] // dv)]
    return (jnp.concatenate(parts, axis=1) * gate_ref[0].astype(F32)).astype(BF16)


def _lin_finish_specs(o_dirs, proj, gate_col, norm_w, tm):
    vw = o_dirs[0].shape[2]
    specs = [pl.BlockSpec((1, tm, vw), lambda b, i: (b, i, 0)),
             pl.BlockSpec((1, tm, vw), lambda b, i: (b, i, 0)),
             pl.BlockSpec((1, tm, vw), lambda b, i: (b, i, gate_col)),
             pl.BlockSpec(norm_w.shape, lambda b, i: (0, 0))]
    return specs, [o_dirs[0], o_dirs[1], proj, norm_w]


def _out0_kernel(x_ref, a_ref, of_ref, ob_ref, gate_ref, nw_ref, wa_ref, wb_ref, gl_ref, gc_ref, o_ref,
                 *, tm, n_lat):
    i = pl.program_id(1)
    o = _dot(a_ref[0], wa_ref[...]) + _dot(_lin_finish(of_ref, ob_ref, gate_ref, nw_ref), wb_ref[...])
    o_ref[0] = x_ref[0] + _row_select(i, tm, n_lat, gc_ref[...], gl_ref[0]) * o


def _out_proj0(x_all, mix_a, lin_finish, w_a, w_b, mod_l, mod_c, n_lat, n_rows, tm):
    bsz, _, d = x_all.shape
    tall = n_rows
    lin_specs, lin_args = _lin_finish_specs(*lin_finish, tm)
    return pl.pallas_call(
        functools.partial(_out0_kernel, tm=tm, n_lat=n_lat),
        grid=(bsz, tall // tm),
        in_specs=[pl.BlockSpec((1, tm, d), lambda b, i: (b, i, 0)),
                  pl.BlockSpec((1, tm, mix_a.shape[2]), lambda b, i: (b, i, 0))] + lin_specs + [
                  pl.BlockSpec(w_a.shape, lambda b, i: (0, 0)),
                  pl.BlockSpec(w_b.shape, lambda b, i: (0, 0))] + _mod_specs(d, (2,), 2),
        out_specs=pl.BlockSpec((1, tm, d), lambda b, i: (b, i, 0)),
        out_shape=jax.ShapeDtypeStruct((bsz, tall, d), F32),
        compiler_params=_cparams(2),
        name="out_proj_even",
    )(x_all, mix_a, *lin_args, w_a, w_b, mod_l, mod_c)


def _gelu_tanh(x):
    return 0.5 * x * (1.0 + jnp.tanh(math.sqrt(2.0 / math.pi) * (x + 0.044715 * (x * x * x))))


def _out1_kernel(x_ref, of_ref, ob_ref, gate_ref, nw_ref, yf_ref, yb_ref, u_ref, dsk_ref, gw_ref, gb_ref,
                 wa_ref, wb_ref, gl_ref, gc_ref, o_ref, *, tm, n_lat):
    i = pl.program_id(1)
    y = _gelu_tanh(yf_ref[...] + yb_ref[...] + dsk_ref[...] * u_ref[...])
    glu = _dot(y.astype(BF16), gw_ref[...]) + gb_ref[...]
    y = y * _sigmoid(glu)
    o = (_dot(_lin_finish(of_ref, ob_ref, gate_ref, nw_ref), wa_ref[...])
         + _dot(y.astype(BF16), wb_ref[...]))
    o_ref[0] = x_ref[0] + _row_select(i, tm, n_lat, gc_ref[...], gl_ref[0]) * o


def _out_proj1(x_all, lin_finish, y_dirs, u_t, d_skip, glu_w, glu_b, w_a, w_b, mod_l, mod_c, n_lat, n_rows, tm):
    bsz, _, d = x_all.shape
    tall = n_rows
    width = d_skip.shape[1]
    (y_f, y_b), u2 = y_dirs, u_t
    lin_specs, lin_args = _lin_finish_specs(*lin_finish, tm)
    return pl.pallas_call(
        functools.partial(_out1_kernel, tm=tm, n_lat=n_lat),
        grid=(bsz, tall // tm),
        in_specs=[pl.BlockSpec((1, tm, d), lambda b, i: (b, i, 0))] + lin_specs + [
                  pl.BlockSpec((tm, width), lambda b, i: (i, b)),
                  pl.BlockSpec((tm, width), lambda b, i: (i, b)),
                  pl.BlockSpec((tm, width), lambda b, i: (i, b)),
                  pl.BlockSpec((1, width), lambda b, i: (0, 0)),
                  pl.BlockSpec(glu_w.shape, lambda b, i: (0, 0)),
                  pl.BlockSpec((1, width), lambda b, i: (0, 0)),
                  pl.BlockSpec(w_a.shape, lambda b, i: (0, 0)),
                  pl.BlockSpec(w_b.shape, lambda b, i: (0, 0))] + _mod_specs(d, (2,), 2),
        out_specs=pl.BlockSpec((1, tm, d), lambda b, i: (b, i, 0)),
        out_shape=jax.ShapeDtypeStruct((bsz, tall, d), F32),
        compiler_params=_cparams(2),
        name="out_proj_odd",
    )(x_all, *lin_args, y_f, y_b, u2, d_skip, glu_w, glu_b, w_a, w_b, mod_l, mod_c)


def _mlp_kernel(*refs, tm, n_lat, final):
    if final:
        (x_ref, nw_ref, shl_ref, shc_ref, scl_ref, scc_ref, gl_ref, gc_ref, w1_ref, w2_ref, fw_ref,
         o_ref, h_scr, acc_scr) = refs
    else:
        (x_ref, nw_ref, shl_ref, shc_ref, scl_ref, scc_ref, gl_ref, gc_ref, w1_ref, w2_ref,
         o_ref, h_scr, acc_scr) = refs
    i = pl.program_id(1)
    j = pl.program_id(2)

    @pl.when(j == 0)
    def _():
        _store_norm_modulated(h_scr, x_ref[0], nw_ref[...], shl_ref[0], shc_ref[...], scl_ref[0], scc_ref[...],
                              i, tm, n_lat)
        acc_scr[...] = jnp.zeros_like(acc_scr)

    def partial_product():
        a = jnp.maximum(_dot(h_scr[...], w1_ref[...]), 0.0)
        return _dot((a * a).astype(BF16), w2_ref[...])

    last_j = pl.num_programs(2) - 1

    @pl.when(j < last_j)
    def _():
        acc_scr[...] += partial_product()

    @pl.when(j == last_j)
    def _():
        out = x_ref[0] + _row_select(i, tm, n_lat, gc_ref[...], gl_ref[0]) * (acc_scr[...] + partial_product())
        if final:
            out = _rms(out, fw_ref[...])
        o_ref[0] = out


def _mlp(x_all, norm_w, mod_l, mod_c, w1, w2, final_w, n_lat, tm, tf):
    bsz, tall, d = x_all.shape
    ff = w1.shape[1]
    final = final_w is not None
    in_specs = [pl.BlockSpec((1, tm, d), lambda b, i, j: (b, i, 0)),
                pl.BlockSpec((1, d), lambda b, i, j: (0, 0))]
    in_specs += _mod_specs(d, (3, 4, 5), 3)
    in_specs += [pl.BlockSpec((d, tf), lambda b, i, j: (0, j)),
                 pl.BlockSpec((tf, d), lambda b, i, j: (j, 0))]
    args = [x_all, norm_w.reshape(1, d)] + [mod_l, mod_c] * 3 + [w1, w2]
    if final:
        in_specs.append(pl.BlockSpec((1, d), lambda b, i, j: (0, 0)))
        args.append(final_w.reshape(1, d))
    return pl.pallas_call(
        functools.partial(_mlp_kernel, tm=tm, n_lat=n_lat, final=final),
        grid=(bsz, tall // tm, ff // tf),
        in_specs=in_specs,
        out_specs=pl.BlockSpec((1, tm, d), lambda b, i, j: (b, i, 0)),
        out_shape=jax.ShapeDtypeStruct((bsz, tall, d), F32),
        scratch_shapes=[pltpu.VMEM((tm, d), BF16), pltpu.VMEM((tm, d), F32)],
        compiler_params=_cparams(3),
        name="sq_relu_mlp",
    )(*args)


def _even_in_weight(w_in):
    sizes = (SSD_INNER, SSD_INNER + 2 * SSD_BC, 2 * SSD_HEADS, GLA_KEY, GLA_KEY, GLA_VAL,
             2 * GLA_GATE_RANK, GLA_VAL)
    offs = [0]
    for s in sizes:
        offs.append(offs[-1] + s)
    z, xbc, dt, q, k, v, lr, r = (w_in[:, offs[n]:offs[n + 1]] for n in range(8))
    pad = jnp.zeros((w_in.shape[0], LANE - dt.shape[1] - lr.shape[1]), w_in.dtype)
    main = jnp.concatenate([xbc, z, v, r, q * (GLA_DK ** -0.5), k], axis=1).astype(BF16)
    aux = jnp.concatenate([dt, lr, pad], axis=1).astype(BF16)
    return main, aux


def _s5_params(a_re, a_im, log_dt, b_re, b_im, c_re, c_im):
    delta = jnp.exp(log_dt.astype(F32))[..., None]
    mag = jnp.exp(a_re * delta)
    lbar_re, lbar_im = mag * jnp.cos(a_im * delta), mag * jnp.sin(a_im * delta)
    den = a_re * a_re + a_im * a_im
    zr = ((lbar_re - 1.0) * a_re + lbar_im * a_im) / den
    zi = (lbar_im * a_re - (lbar_re - 1.0) * a_im) / den
    bb_re = zr[..., None] * b_re - zi[..., None] * b_im
    bb_im = zr[..., None] * b_im + zi[..., None] * b_re
    n_slabs = S5_GROUPS // S5_SLAB
    eye = jnp.eye(S5_SLAB, dtype=F32)
    sw = S5_SLAB * S5_STATE

    def block_in(bb):
        bb = bb.reshape(2, n_slabs, S5_SLAB, S5_STATE, S5_GROUP)
        return jnp.einsum("dsgpc,gh->dsgchp", bb, eye).reshape(2, n_slabs, LANE, sw)

    def block_out(cc):
        cc = cc.reshape(n_slabs, S5_SLAB, S5_GROUP, S5_STATE)
        return jnp.einsum("sgcp,gh->sgphc", cc, eye).reshape(n_slabs, sw, LANE)

    bmat = jnp.concatenate([block_in(bb_re), block_in(bb_im)], axis=3).astype(BF16)
    cmat = jnp.concatenate([block_out(c_re), -block_out(c_im)], axis=1).astype(BF16)
    return (bmat, lbar_re.reshape(2, 1, S5_NSTATE), lbar_im.reshape(2, 1, S5_NSTATE), cmat)


def _layer_even(x_all, mod_l, mod_c, n_lat, tm, n_rows_out, tm_out, norm1_w, w_in, conv_w, conv_b, dt_bias,
                a_log, d_skip, ssd_norm_w, gate_w, gate_b, gla_norm_w, w_out):
    w, w_aux = _even_in_weight(w_in)
    n = w.shape[1]
    n_xbc = SSD_INNER + 2 * SSD_BC
    gates = ((n_xbc, n_xbc + SSD_INNER), (n_xbc + SSD_INNER + GLA_VAL, n_xbc + SSD_INNER + 2 * GLA_VAL))
    proj, aux = _project(x_all, norm1_w, mod_l, mod_c, w, w_aux, n_lat, tm, _largest_divisor(n, 2 * LANE, PROJ_TN),
                         extra_token_major=False, silu_ranges=gates)
    c_z, c_v, c_r = n_xbc // SSD_INNER, n_xbc // GLA_VAL + 1, n_xbc // GLA_VAL + 2
    c_q = (n_xbc + 3 * SSD_INNER) // GLA_KEY
    xbc = _conv_silu(proj, conv_w, conv_b, n_lat, n_xbc)

    neg_a = -jnp.exp(a_log.astype(F32))
    dt_bias = dt_bias.astype(F32)
    d_wide = jnp.repeat(d_skip.astype(F32), SSD_HEADDIM).reshape(1, SSD_INNER)
    y_f = _ssd_scan(xbc, proj, aux, dt_bias, neg_a, c_z, n_lat, 0)
    y_mix = _ssd_scan(xbc, proj, aux, dt_bias, neg_a, c_z, n_lat, 1,
                      (y_f, d_wide, ssd_norm_w.reshape(1, SSD_INNER)))

    cols = {"q": c_q, "k": c_q + 1, "v": c_v, "gate": c_r}
    gparams = [(gate_w[d].astype(BF16), gate_b[d].reshape(1, GLA_KEY).astype(F32)) for d in range(2)]
    o_dirs = _lin_scan("gla", proj, cols, gparams, n_lat, aux=aux)
    gla_finish = (o_dirs, proj, c_r, gla_norm_w.astype(F32).reshape(1, GLA_DV))

    w_out = w_out.astype(BF16)
    return _out_proj0(x_all, y_mix, gla_finish, w_out[:SSD_INNER], w_out[SSD_INNER:], mod_l, mod_c, n_lat,
                      n_rows_out, tm_out)


def _layer_odd(x_all, mod_l, mod_c, n_lat, tm, n_rows_out, tm_out, norm1_w, w_in, lb, hgrn_norm_w, a_re, a_im,
               log_dt, b_re, b_im, c_re, c_im, d_skip, glu_w, glu_b, w_out):
    bsz = x_all.shape[0]
    n_main = 5 * HGRN_WIDTH
    w_main = w_in[:, :n_main].astype(BF16)
    w_u = w_in[:, n_main:].astype(BF16)
    proj, u_t = _project(x_all, norm1_w, mod_l, mod_c, w_main, w_u, n_lat, tm,
                         _largest_divisor(n_main, 2 * LANE, PROJ_TN),
                         silu_ranges=((0, HGRN_WIDTH), (4 * HGRN_WIDTH, 5 * HGRN_WIDTH)))

    lb = lb.astype(F32).reshape(2, 1, HGRN_WIDTH)
    cols = {"q": 0, "v": 1, "aux": 2, "gate": 4}
    o_dirs = _lin_scan("hgrn", proj, cols, [(lb[d], 1.0 - lb[d]) for d in range(2)], n_lat)
    hgrn_finish = (o_dirs, proj, cols["gate"], hgrn_norm_w.astype(F32).reshape(1, HGRN_DV))

    bmat, lam_re, lam_im, cmat = _s5_params(a_re.astype(F32), a_im.astype(F32), log_dt, b_re.astype(F32),
                                            b_im.astype(F32), c_re.astype(F32), c_im.astype(F32))
    y_dirs = _s5_scan(u_t, bmat, lam_re, lam_im, cmat, bsz, n_lat)

    w_out = w_out.astype(BF16)
    return _out_proj1(x_all, hgrn_finish, y_dirs, u_t, d_skip.astype(F32).reshape(1, S5_WIDTH),
                      glu_w.astype(BF16), glu_b.astype(F32).reshape(1, S5_WIDTH),
                      w_out[:HGRN_WIDTH], w_out[HGRN_WIDTH:], mod_l, mod_c, n_lat, n_rows_out, tm_out)


def kernel(x, c, ctx, c_ctx, ada_w, ada_b, norm1_w, norm2_w, ssd_gla_w_in, ssd_conv_w, ssd_conv_b, ssd_dt_bias, ssd_a_log, ssd_d, ssd_norm_w, gla_gate_w, gla_gate_b, gla_norm_w, ssd_gla_w_out, hgrn_s5_w_in, hgrn_lb_logits, hgrn_norm_w, s5_a_re, s5_a_im, s5_log_dt, s5_b_re, s5_b_im, s5_c_re, s5_c_im, s5_d, s5_glu_w, s5_glu_b, hgrn_s5_w_out, mlp_w1, mlp_w2, final_norm_w):
    bsz, n_lat, d = x.shape
    ctx_len = ctx.shape[1]
    depth = ada_w.shape[0]
    tall = n_lat + ctx_len
    assert bsz % 8 == 0 and ctx_len % SCAN_BLOCK == 0 and n_lat % SCAN_BLOCK == 0 and n_lat % GRID_W == 0
    tf = 2048

    n_rows = -(-(bsz + 1) // 8) * 8
    cvec = jnp.concatenate([c, c_ctx[None, :], jnp.zeros((n_rows - bsz - 1, d), c.dtype)], axis=0)
    mod = _modulation(cvec.astype(F32), ada_w, ada_b)

    p_lb = jax.nn.softmax(hgrn_lb_logits.astype(F32), axis=0)
    lb_all = jnp.cumsum(p_lb, axis=0) - p_lb[0]

    x_all = jnp.concatenate([x, ctx], axis=1).astype(F32)
    tm_all = _largest_divisor(tall, 16, 1056)
    for layer in range(depth):
        j = layer // 2
        last = layer == depth - 1
        n_rows = n_lat if last else tall
        tm_out = _largest_divisor(n_rows, 16, 1056)
        mod_l = mod[layer, :bsz].reshape(bsz, 1, N_MOD * d)
        mod_c = mod[layer, bsz:bsz + 1]
        if layer % 2 == 0:
            x_all = _layer_even(x_all, mod_l, mod_c, n_lat, tm_all, n_rows, tm_out, norm1_w[layer],
                                ssd_gla_w_in[j], ssd_conv_w[j], ssd_conv_b[j], ssd_dt_bias[j], ssd_a_log[j],
                                ssd_d[j], ssd_norm_w[j], gla_gate_w[j], gla_gate_b[j], gla_norm_w[j],
                                ssd_gla_w_out[j])
        else:
            x_all = _layer_odd(x_all, mod_l, mod_c, n_lat, tm_all, n_rows, tm_out, norm1_w[layer],
                               hgrn_s5_w_in[j], lb_all[layer], hgrn_norm_w[j], s5_a_re[j], s5_a_im[j],
                               s5_log_dt[j], s5_b_re[j], s5_b_im[j], s5_c_re[j], s5_c_im[j], s5_d[j],
                               s5_glu_w[j], s5_glu_b[j], hgrn_s5_w_out[j])
        x_all = _mlp(x_all, norm2_w[layer], mod_l, mod_c, mlp_w1[layer].astype(BF16),
                     mlp_w2[layer].astype(BF16), final_norm_w if last else None, n_lat, tm_out, tf)
    return x_all.astype(x.dtype)
```

```python
import functools
import math

import jax
import jax.numpy as jnp
from jax import lax
from jax.experimental import pallas as pl
from jax.experimental.pallas import tpu as pltpu

F32 = jnp.float32
BF16 = jnp.bfloat16

GRID_W = 64
NORM_EPS = 1e-6
N_MOD = 6
SSD_HEADDIM = 64
SSD_HEADS = 16
SSD_GROUPS = 4
SSD_STATE = 128
SSD_CHUNK = 128
GLA_HEADS = 8
GLA_DK = 64
GLA_DV = 128
GLA_GATE_RANK = 16
GLA_GATE_NORM = 16.0
HGRN_HEADS = 8
HGRN_DK = 128
HGRN_DV = 128
S5_GROUP = 16
S5_GROUPS = 24
S5_STATE = 64
LIN_CHUNK = 64

SSD_INNER = SSD_HEADS * SSD_HEADDIM
SSD_BC = SSD_GROUPS * SSD_STATE
GLA_KEY = GLA_HEADS * GLA_DK
GLA_VAL = GLA_HEADS * GLA_DV
HGRN_WIDTH = HGRN_HEADS * HGRN_DV
S5_WIDTH = S5_GROUPS * S5_GROUP
S5_NSTATE = S5_GROUPS * S5_STATE

VMEM_LIMIT_BYTES = 56 * 1024 * 1024
LANE = 128
PROJ_TN = 2560
SCAN_BLOCK = 256
SCAN_BATCH = {"ssd_fwd": 2, "ssd_bwd": 1, "gla": 2, "hgrn": 1}
S5_CHUNK = 128
S5_SLAB = LANE // S5_GROUP


def _cparams(n_axes):
    return pltpu.CompilerParams(dimension_semantics=("arbitrary",) * n_axes,
                                vmem_limit_bytes=VMEM_LIMIT_BYTES)


def _largest_divisor(n, multiple, cap):
    best = None
    for d in range(multiple, min(n, cap) + 1, multiple):
        if n % d == 0:
            best = d
    assert best is not None, (n, multiple, cap)
    return best


_NEG_LOG2E = -1.4426950408889634


def _sigmoid(x):
    return 1.0 / (1.0 + jnp.exp2(x * _NEG_LOG2E))


def _silu(x):
    return x * _sigmoid(x)


def _softplus(x):
    return jnp.maximum(x, 0.0) + jnp.log1p(jnp.exp(-jnp.abs(x)))


def _log_sigmoid(x):
    return -_softplus(-x)


def _rms(x, w):
    return x * lax.rsqrt(jnp.mean(x * x, axis=-1, keepdims=True) + NORM_EPS) * w


def _dot(a, b, dims=(((1,), (0,)), ((), ())), precision=None):
    return lax.dot_general(a, b, dims, precision=precision, preferred_element_type=F32)


def _split3(v):
    hi = v.astype(BF16)
    r1 = v - hi.astype(F32)
    mid = r1.astype(BF16)
    lo = (r1 - mid.astype(F32)).astype(BF16)
    return hi, mid, lo


def _tri3(mask):
    tri = mask.astype(BF16)
    return jnp.concatenate([tri, tri, tri], axis=1)


def _cumsum_rows(tri2, v):
    hi = v.astype(BF16)
    lo = (v - hi.astype(F32)).astype(BF16)
    return _dot(tri2, jnp.concatenate([hi, lo], axis=0))


_NT = (((1,), (1,)), ((), ()))
_TN = (((0,), (0,)), ((), ()))
_TT = (((0,), (1,)), ((), ()))


def _mod_kernel(c_ref, w_ref, b_ref, o_ref):
    a = _silu(c_ref[...]).astype(BF16)
    o_ref[0] = _dot(a, w_ref[0].astype(BF16)) + b_ref[0]


def _modulation(cvec, ada_w, ada_b):
    depth, d, n = ada_w.shape
    rows = cvec.shape[0]
    tn = _largest_divisor(n, LANE, 1024)
    return pl.pallas_call(
        _mod_kernel,
        grid=(depth, n // tn),
        in_specs=[pl.BlockSpec((rows, d), lambda l, j: (0, 0)),
                  pl.BlockSpec((1, d, tn), lambda l, j: (l, 0, j)),
                  pl.BlockSpec((1, 1, tn), lambda l, j: (l, 0, j))],
        out_specs=pl.BlockSpec((1, rows, tn), lambda l, j: (l, 0, j)),
        out_shape=jax.ShapeDtypeStruct((depth, rows, n), F32),
        compiler_params=_cparams(2),
        name="adaln_mod",
    )(cvec, ada_w, ada_b.reshape(depth, 1, n))


def _row_select(i, tm, n_lat, ctx_val, lat_val):
    row = i * tm + lax.broadcasted_iota(jnp.int32, (tm, 1), 0)
    return jnp.where(row >= n_lat, ctx_val, lat_val)


def _store_norm_modulated(h_ref, x, nw, shift_l, shift_c, scale_l, scale_c, i, tm, n_lat):
    xn = x * lax.rsqrt(jnp.mean(x * x, axis=-1, keepdims=True) + NORM_EPS)
    gain = _row_select(i, tm, n_lat, nw * (1.0 + scale_c), nw * (1.0 + scale_l))
    h_ref[...] = (xn * gain + _row_select(i, tm, n_lat, shift_c, shift_l)).astype(h_ref.dtype)


def _mod_specs(d, cols, n_grid_axes):
    specs = []
    for k in cols:
        if n_grid_axes == 2:
            specs.append(pl.BlockSpec((1, 1, d), lambda b, i, k=k: (b, 0, k)))
            specs.append(pl.BlockSpec((1, d), lambda b, i, k=k: (0, k)))
        else:
            specs.append(pl.BlockSpec((1, 1, d), lambda b, i, j, k=k: (b, 0, k)))
            specs.append(pl.BlockSpec((1, d), lambda b, i, j, k=k: (0, k)))
    return specs


def _proj_kernel(*refs, tm, n_lat, has_extra, silu_cols):
    if has_extra:
        (x_ref, nw_ref, shl_ref, shc_ref, scl_ref, scc_ref, w_ref, wx_ref, o_ref, ox_ref, h_scr) = refs
    else:
        (x_ref, nw_ref, shl_ref, shc_ref, scl_ref, scc_ref, w_ref, o_ref, h_scr) = refs
    i = pl.program_id(1)
    j = pl.program_id(2)

    @pl.when(j == 0)
    def _():
        _store_norm_modulated(h_scr, x_ref[0], nw_ref[...], shl_ref[0], shc_ref[...], scl_ref[0], scc_ref[...],
                              i, tm, n_lat)
        if has_extra:
            ox_ref[...] = _dot(h_scr[...], wx_ref[...]).reshape(ox_ref.shape)

    for jj, ranges in enumerate(silu_cols):
        @pl.when(j == jj)
        def _(ranges=ranges):
            o = _dot(h_scr[...], w_ref[...])
            if not ranges:
                o_ref[0] = o.astype(o_ref.dtype)
            edge = 0
            for start, stop in ranges:
                if start > edge:
                    o_ref[0, :, edge:start] = o[:, edge:start].astype(o_ref.dtype)
                o_ref[0, :, start:stop] = _silu(o[:, start:stop]).astype(o_ref.dtype)
                edge = stop
            if ranges and edge < o.shape[1]:
                o_ref[0, :, edge:] = o[:, edge:].astype(o_ref.dtype)


def _project(x_all, norm_w, mod_l, mod_c, w, w_extra, n_lat, tm, tn, extra_token_major=True, silu_ranges=()):
    bsz, tall, d = x_all.shape
    n = w.shape[1]
    has_extra = w_extra is not None
    silu_cols = tuple(
        tuple((max(a, jj * tn) - jj * tn, min(b, (jj + 1) * tn) - jj * tn)
              for a, b in sorted(silu_ranges) if a < (jj + 1) * tn and b > jj * tn)
        for jj in range(n // tn))
    in_specs = [pl.BlockSpec((1, tm, d), lambda b, i, j: (b, i, 0)),
                pl.BlockSpec((1, d), lambda b, i, j: (0, 0))]
    in_specs += _mod_specs(d, (0, 1), 3)
    in_specs.append(pl.BlockSpec((d, tn), lambda b, i, j: (0, j)))
    args = [x_all, norm_w.reshape(1, d), mod_l, mod_c, mod_l, mod_c, w]
    out_specs = [pl.BlockSpec((1, tm, tn), lambda b, i, j: (b, i, j))]
    out_shape = [jax.ShapeDtypeStruct((bsz, tall, n), BF16)]
    if has_extra:
        nx = w_extra.shape[1]
        in_specs.append(pl.BlockSpec((d, nx), lambda b, i, j: (0, 0)))
        args.append(w_extra)
        if extra_token_major:
            out_specs.append(pl.BlockSpec((tm, nx), lambda b, i, j: (i, b)))
            out_shape.append(jax.ShapeDtypeStruct((tall, bsz * nx), F32))
        else:
            out_specs.append(pl.BlockSpec((1, tm, nx), lambda b, i, j: (b, i, 0)))
            out_shape.append(jax.ShapeDtypeStruct((bsz, tall, nx), F32))
    out = pl.pallas_call(
        functools.partial(_proj_kernel, tm=tm, n_lat=n_lat, has_extra=has_extra, silu_cols=silu_cols),
        grid=(bsz, tall // tm, n // tn),
        in_specs=in_specs,
        out_specs=out_specs,
        out_shape=out_shape,
        scratch_shapes=[pltpu.VMEM((tm, d), BF16)],
        compiler_params=_cparams(3),
        name="norm_mod_proj",
    )(*args)
    return out if has_extra else out[0]


def _conv_kernel(main_ref, prev_ref, next_ref, w_ref, b_ref, o_ref, *, tt, n_lat, tall):
    i = pl.program_id(1)
    te = tt + 2 * GRID_W
    p = i * tt - GRID_W + lax.broadcasted_iota(jnp.int32, (te, 1), 0)
    is_ctx = p >= n_lat
    col = jnp.bitwise_and(p, GRID_W - 1)
    has_left = jnp.where(is_ctx, p - n_lat, col) > 0
    has_right = jnp.where(is_ctx, p - (tall - 1), col - (GRID_W - 1)) < 0
    w = w_ref[...]

    def conv(interior):
        ext = jnp.concatenate([prev_ref[0], main_ref[0], next_ref[0]], axis=0).astype(F32)
        own = slice(GRID_W, GRID_W + tt)
        if interior:
            as_left = jnp.where(has_right, ext, 0.0)
            as_right = jnp.where(has_left, ext, 0.0)
        acc = jnp.zeros((tt, w.shape[1]), F32) + b_ref[...]
        for dy in (-1, 0, 1):
            rs = slice(GRID_W + GRID_W * dy, GRID_W + GRID_W * dy + tt)
            k0 = 3 * (dy + 1)
            if interior:
                left, right = pltpu.roll(as_left[rs], 1, 0), pltpu.roll(as_right[rs], tt - 1, 0)
            else:
                left = jnp.where(has_left[own], pltpu.roll(ext[rs], 1, 0), 0.0)
                right = jnp.where(has_right[own], pltpu.roll(ext[rs], tt - 1, 0), 0.0)
            t = ext[rs] * w[k0 + 1:k0 + 2] + left * w[k0:k0 + 1] + right * w[k0 + 2:k0 + 3]
            if not interior and dy != 0:
                q = p[own]
                if dy == -1:
                    ok = jnp.where(q >= n_lat, 0, q) >= GRID_W
                else:
                    ok = jnp.where(q >= n_lat, n_lat, q) < n_lat - GRID_W
                t = jnp.where(ok, t, 0.0)
            acc = acc + t
        o_ref[0] = _silu(acc).astype(o_ref.dtype)

    interior = jnp.logical_and(i * tt >= GRID_W, (i + 1) * tt <= n_lat - GRID_W)
    pl.when(interior)(lambda: conv(True))
    pl.when(jnp.logical_not(interior))(lambda: conv(False))


def _conv_silu(proj, conv_w, conv_b, n_lat, n_ch):
    bsz, tall, _ = proj.shape
    n_rows = tall // GRID_W
    tt = _largest_divisor(tall, GRID_W, 768)
    assert n_lat % GRID_W == 0 and n_lat // tt == (tall - 1) // tt
    r = tt // GRID_W
    tc = 1024
    return pl.pallas_call(
        functools.partial(_conv_kernel, tt=tt, n_lat=n_lat, tall=tall),
        grid=(bsz, tall // tt, n_ch // tc),
        in_specs=[pl.BlockSpec((1, tt, tc), lambda b, i, c: (b, i, c)),
                  pl.BlockSpec((1, GRID_W, tc), lambda b, i, c: (b, jnp.maximum(i * r - 1, 0), c)),
                  pl.BlockSpec((1, GRID_W, tc), lambda b, i, c: (b, jnp.minimum((i + 1) * r, n_rows - 1), c)),
                  pl.BlockSpec((9, tc), lambda b, i, c: (0, c)),
                  pl.BlockSpec((1, tc), lambda b, i, c: (0, c))],
        out_specs=pl.BlockSpec((1, tt, tc), lambda b, i, c: (b, i, c)),
        out_shape=jax.ShapeDtypeStruct((bsz, tall, n_ch), BF16),
        compiler_params=_cparams(3),
        name="dwconv_silu",
    )(proj, proj, proj, conv_w.reshape(9, n_ch), conv_b.reshape(1, n_ch))


def _scan_block(s, n_lat_blocks, n_blocks, reverse):
    n_ctx_blocks = n_blocks - n_lat_blocks
    if not reverse:
        return jnp.where(s < n_ctx_blocks, n_lat_blocks + s, s - n_ctx_blocks)
    return n_blocks - 1 - s


def _tri_mask(c, reverse):
    ri = lax.broadcasted_iota(jnp.int32, (c, c), 0)
    ci = lax.broadcasted_iota(jnp.int32, (c, c), 1)
    return (ci >= ri) if reverse else (ci <= ri)


def _ssd_expand_matrix():
    eye = jnp.eye(SSD_HEADS, dtype=F32)
    e_head = jnp.repeat(eye, SSD_HEADDIM, axis=1)
    e_seg = jnp.repeat(eye, SSD_CHUNK, axis=1)
    zh = jnp.zeros_like(e_head)
    zs = jnp.zeros_like(e_seg)
    blk = jnp.concatenate([
        jnp.concatenate([e_head, zh, zh, zs], axis=1),
        jnp.concatenate([zh, e_head, zh, zs], axis=1),
        jnp.concatenate([zh, zh, e_head, zs], axis=1),
        jnp.concatenate([zh, zh, zh, e_seg], axis=1)], axis=0)
    return jnp.concatenate([blk, blk, blk], axis=0).astype(BF16)


def _ssd_kernel(*refs, direction, finish):
    if finish:
        (x_ref, bm_ref, cm_ref, dtlr_ref, dtb_ref, nega_ref, exp_ref, z_ref, yf_ref, dsk_ref, nw_ref,
         o_ref, st_ref) = refs
    else:
        (x_ref, bm_ref, cm_ref, dtlr_ref, dtb_ref, nega_ref, exp_ref, o_ref, st_ref) = refs
    reverse = direction == 1
    c = SSD_CHUNK
    p = SSD_HEADDIM
    gw = SSD_INNER // SSD_GROUPS
    hpg = SSD_HEADS // SSD_GROUPS

    @pl.when(pl.program_id(1) == 0)
    def _():
        st_ref[...] = jnp.zeros_like(st_ref)

    mask = _tri_mask(c, reverse)
    tri3 = _tri3(mask)
    last = 0 if reverse else c - 1
    n_chunks = x_ref.shape[1] // c
    chunk_order = range(n_chunks - 1, -1, -1) if reverse else range(n_chunks)
    order = [(bb, cc) for bb in range(x_ref.shape[0]) for cc in chunk_order]
    rows = {ck: slice(ck[1] * c, (ck[1] + 1) * c) for ck in order}
    groups = range(SSD_GROUPS)
    g_cols = [slice(g * gw, (g + 1) * gw) for g in groups]
    n_cols = [slice(g * SSD_STATE, (g + 1) * SSD_STATE) for g in groups]
    cb = {(ck, g): _dot(cm_ref[ck[0], rows[ck], n_cols[g]], bm_ref[ck[0], rows[ck], n_cols[g]], _NT)
          for ck in order for g in groups}
    prep = {}
    for ck in order:
        bb, rs = ck[0], rows[ck]
        x = x_ref[bb, rs, :].astype(F32)
        dt_raw = dtlr_ref[bb, rs, :][:, direction * SSD_HEADS:(direction + 1) * SSD_HEADS].astype(F32)
        dt = _softplus(dt_raw + dtb_ref[...])
        la3 = jnp.concatenate(_split3(dt * nega_ref[...]), axis=0)
        acum = _dot(tri3, la3)
        acum_t = _dot(la3, tri3, _TT)
        a_last = acum[last:last + 1]
        narrow = jnp.concatenate([dt, dt * jnp.exp(a_last - acum), jnp.exp(acum), acum], axis=1)
        wide = _dot(jnp.concatenate(_split3(narrow), axis=1), exp_ref[...])
        ea_w = wide[:, 2 * SSD_INNER:3 * SSD_INNER]
        prep[ck] = dict(
            x=x, bm=bm_ref[bb, rs, :], cm=cm_ref[bb, rs, :], wide=wide, acum_t=acum_t, ea_w=ea_w,
            xdt=(x * wide[:, :SSD_INNER]).astype(BF16),
            xw=(x * wide[:, SSD_INNER:2 * SSD_INNER]).astype(BF16),
            e_last=ea_w[last:last + 1])
    kv = {(ck, g): _dot(prep[ck]["bm"][:, n_cols[g]], prep[ck]["xw"][:, g_cols[g]], _TN)
          for ck in order for g in groups}
    scores, st_used = {}, {}
    for bb in range(x_ref.shape[0]):
        for g in groups:
            st = st_ref[bb, :, g_cols[g]]
            for cc in chunk_order:
                st_used[(bb, cc), g] = st.astype(BF16)
                st = st * prep[bb, cc]["e_last"][:, g_cols[g]] + kv[(bb, cc), g]
            st_ref[bb, :, g_cols[g]] = st
    for ck in order:
        for h in range(SSD_HEADS):
            a_i = prep[ck]["wide"][:, 3 * SSD_INNER + h * c:3 * SSD_INNER + (h + 1) * c]
            decay = jnp.exp(jnp.where(mask, a_i - prep[ck]["acum_t"][h:h + 1, :], -jnp.inf))
            scores[ck, h] = (cb[ck, h // hpg] * decay).astype(BF16)
    for ck in order:
        bb, rs = ck[0], rows[ck]
        x = prep[ck]["x"]
        y_groups = []
        for g in groups:
            ys = [_dot(scores[ck, h], prep[ck]["xdt"][:, h * p:(h + 1) * p])
                  for h in range(g * hpg, (g + 1) * hpg)]
            y_state = _dot(prep[ck]["cm"][:, n_cols[g]], st_used[ck, g]) * prep[ck]["ea_w"][:, g_cols[g]]
            y_groups.append(jnp.concatenate(ys, axis=1) + y_state)
        y = jnp.concatenate(y_groups, axis=1)
        if finish:
            z_act = z_ref[bb, rs, :].astype(F32)
            y = (y + yf_ref[bb, rs, :] + dsk_ref[...] * x) * z_act
            outs = []
            for g in range(SSD_GROUPS):
                sl = slice(g * gw, (g + 1) * gw)
                outs.append(_rms(y[:, sl], nw_ref[:, sl]))
            o_ref[bb, rs, :] = jnp.concatenate(outs, axis=1).astype(o_ref.dtype)
        else:
            o_ref[bb, rs, :] = y


def _ssd_scan(xbc, proj, aux, dt_bias, neg_a, z_col, n_lat, direction, finish_args=None):
    bsz, tall, _ = xbc.shape
    tb = SCAN_BLOCK
    nb, n_lat_blocks = tall // tb, n_lat // tb
    reverse = direction == 1
    finish = finish_args is not None
    expand = _ssd_expand_matrix()

    def tok(col):
        return lambda b, s: (b, _scan_block(s, n_lat_blocks, nb, reverse), col)

    nbb = SCAN_BATCH["ssd_bwd" if finish else "ssd_fwd"]
    in_specs = [pl.BlockSpec((nbb, tb, SSD_INNER), tok(0)),
                pl.BlockSpec((nbb, tb, SSD_BC), tok(SSD_INNER // SSD_BC)),
                pl.BlockSpec((nbb, tb, SSD_BC), tok(SSD_INNER // SSD_BC + 1)),
                pl.BlockSpec((nbb, tb, LANE), tok(0)),
                pl.BlockSpec((1, SSD_HEADS), lambda b, s: (0, 0)),
                pl.BlockSpec((1, SSD_HEADS), lambda b, s: (0, 0)),
                pl.BlockSpec(expand.shape, lambda b, s: (0, 0))]
    args = [xbc, xbc, xbc, aux, dt_bias[direction:direction + 1], neg_a[direction:direction + 1], expand]
    if finish:
        y_f, d_skip_wide, norm_w = finish_args
        in_specs += [pl.BlockSpec((nbb, tb, SSD_INNER), tok(z_col)),
                     pl.BlockSpec((nbb, tb, SSD_INNER), tok(0)),
                     pl.BlockSpec((1, SSD_INNER), lambda b, s: (0, 0)),
                     pl.BlockSpec((1, SSD_INNER), lambda b, s: (0, 0))]
        args += [proj, y_f, d_skip_wide, norm_w]
    return pl.pallas_call(
        functools.partial(_ssd_kernel, direction=direction, finish=finish),
        grid=(bsz // nbb, nb),
        in_specs=in_specs,
        out_specs=pl.BlockSpec((nbb, tb, SSD_INNER), tok(0)),
        out_shape=jax.ShapeDtypeStruct((bsz, tall, SSD_INNER), BF16 if finish else F32),
        scratch_shapes=[pltpu.VMEM((nbb, SSD_STATE, SSD_INNER), F32)],
        compiler_params=_cparams(2),
        name="ssd_scan_bwd" if reverse else "ssd_scan_fwd",
    )(*args)


def _lin_kernel(*refs, mode, heads, dk, dv):
    n_in = 6 if mode == "gla" else 5
    ins = [refs[:n_in], refs[n_in:2 * n_in]]
    o_refs = refs[2 * n_in:2 * n_in + 2]
    st_ref = refs[2 * n_in + 2]
    c = LIN_CHUNK

    @pl.when(pl.program_id(1) == 0)
    def _():
        st_ref[...] = jnp.zeros_like(st_ref)

    tb = o_refs[0].shape[1]
    nc = tb // c
    hpt = LANE // dk
    n_tiles = heads // hpt
    span = n_tiles if mode == "gla" else 1
    gw = span * LANE
    n_groups = n_tiles // span
    hpg = hpt * span
    zeros = jnp.zeros((c, LANE), BF16)
    ri = lax.broadcasted_iota(jnp.int32, (tb, tb), 0)
    ci = lax.broadcasted_iota(jnp.int32, (tb, tb), 1)
    c_shift = c.bit_length() - 1
    same_chunk = jnp.right_shift(ri, c_shift) == jnp.right_shift(ci, c_shift)

    def chunk_blocks(a):
        cols = []
        for b in range(nc):
            cols.append(jnp.concatenate(
                [a[cc * c:(cc + 1) * c] if cc == b else zeros for cc in range(nc)], axis=0))
        return jnp.concatenate(cols, axis=1)

    def tile_cols(h):
        t = (h // hpt) % span
        return slice(t * LANE, (t + 1) * LANE)

    def make_stream(direction, bb):
        if mode == "gla":
            q_ref, k_ref, v_ref, aux_ref, p1_ref, p2_ref = ins[direction]
        else:
            q_ref, v_ref, aux_ref, p1_ref, p2_ref = ins[direction]
            k_ref = None
        o_ref = o_refs[direction]
        reverse = direction == 1
        chunks = range(nc - 1, -1, -1) if reverse else range(nc)
        last = 0 if reverse else c - 1
        if reverse:
            bd_mask = jnp.where(same_chunk, ci - ri, -1) >= 0
        else:
            bd_mask = jnp.where(same_chunk, ci - ri, 1) <= 0
        tri = _tri_mask(c, reverse).astype(BF16)
        tri2 = jnp.concatenate([tri, tri], axis=1)
        groups, work = {}, {}

        def prep(g):
            ls = slice(g * gw, (g + 1) * gw)
            if mode == "gla":
                q = q_ref[bb, :, ls].astype(F32)
                k = k_ref[bb, :, ls].astype(F32)
                off = 2 * SSD_HEADS + direction * GLA_GATE_RANK
                lr = aux_ref[bb][:, off:off + GLA_GATE_RANK].astype(BF16)
                lg = _log_sigmoid(_dot(lr, p1_ref[:, ls]) + p2_ref[:, ls]) * (1.0 / GLA_GATE_NORM)
            else:
                q = q_ref[bb, :, ls].astype(F32)
                f_raw = aux_ref[bb, :, ls].astype(F32)
                e = jnp.exp2(jnp.abs(f_raw) * _NEG_LOG2E)
                r = 1.0 / (1.0 + e)
                forget = p1_ref[:, ls] + p2_ref[:, ls] * jnp.where(f_raw >= 0.0, r, e * r)
                lg = jnp.log(forget)
                k = 1.0 - forget
            gcum = jnp.concatenate([_cumsum_rows(tri2, lg[cc * c:(cc + 1) * c]) for cc in range(nc)], axis=0)
            e_last = [jnp.exp(gcum[cc * c + last:cc * c + last + 1]) for cc in range(nc)]
            e_rows = jnp.concatenate([jnp.broadcast_to(e, (c, gw)) for e in e_last], axis=0)
            e_gcum = jnp.exp(gcum)
            q_decf = q * e_gcum
            if hpt > 1:
                head_of_lane = jnp.bitwise_and(jnp.right_shift(
                    lax.broadcasted_iota(jnp.int32, (1, gw), 1), dk.bit_length() - 1), hpt - 1)
                q_dec = [jnp.where(head_of_lane == r, q_decf, 0.0).astype(BF16) for r in range(hpt)]
            else:
                q_dec = [q_decf.astype(BF16)]
            k_invf = k * (1.0 / e_gcum)
            groups[g] = dict(q_dec=q_dec, k_inv=k_invf.astype(BF16), k_end=(k_invf * e_rows).astype(BF16),
                             e_last=e_last)

        def products(h):
            gp, ts = groups[h // hpg], tile_cols(h)
            qh = gp["q_dec"][h % hpt][:, ts]
            vh = v_ref[bb, :, h * dv:(h + 1) * dv]
            work[h] = dict(
                qh=qh, vh=vh, scores=_dot(qh, gp["k_inv"][:, ts], _NT),
                kv_t=_dot(vh, chunk_blocks(gp["k_end"][:, ts]), _TN))

        def mask_and_chain(h):
            w = work[h]
            e_last = groups[h // hpg]["e_last"]
            w["att"] = jnp.where(bd_mask, w.pop("scores"), 0.0).astype(BF16)
            st = st_ref[direction, bb, h]
            used = [None] * nc
            for cc in chunks:
                used[cc] = st.astype(BF16)
                st = st * e_last[cc][:, tile_cols(h)] + w["kv_t"][:, cc * LANE:(cc + 1) * LANE]
            st_ref[direction, bb, h] = st
            w["used"] = used
            del w["kv_t"]

        def outputs(h):
            w = work[h]
            o_state = jnp.concatenate(
                [_dot(w["qh"][cc * c:(cc + 1) * c], w["used"][cc], _NT) for cc in range(nc)], axis=0)
            w["o"] = _dot(w["att"], w["vh"]) + o_state

        def emit(h):
            o_ref[bb, :, h * dv:(h + 1) * dv] = work.pop(h)["o"]

        return dict(prep=prep, products=products, mask_and_chain=mask_and_chain, outputs=outputs, emit=emit)

    streams = [make_stream(d, bb) for bb in range(o_refs[0].shape[0]) for d in (0, 1)]

    def heads_of(g):
        return range(g * hpg, (g + 1) * hpg) if 0 <= g < n_groups else ()

    lag = 2 if n_groups > 1 else 1
    for it in range(n_groups + 4 * lag):
        for stage, delay in (("products", lag), ("outputs", 3 * lag)):
            for st in streams:
                for h in heads_of(it - delay):
                    st[stage](h)
        if it < n_groups:
            for st in streams:
                st["prep"](it)
        for stage, delay in (("mask_and_chain", 2 * lag), ("emit", 4 * lag)):
            for st in streams:
                for h in heads_of(it - delay):
                    st[stage](h)


def _lin_scan(mode, proj, cols, params, n_lat, aux=None):
    bsz, tall, _ = proj.shape
    if mode == "gla":
        heads, dk, dv = GLA_HEADS, GLA_DK, GLA_DV
    else:
        heads, dk, dv = HGRN_HEADS, HGRN_DK, HGRN_DV
    kw, vw = heads * dk, heads * dv
    tb, nbb = SCAN_BLOCK, SCAN_BATCH[mode]
    nb, n_lat_blocks = tall // tb, n_lat // tb

    def const2(shape):
        return pl.BlockSpec(shape, lambda b, s: (0, 0))

    in_specs, args = [], []
    for direction in (0, 1):
        def tok(col, reverse=direction == 1):
            return lambda b, s: (b, _scan_block(s, n_lat_blocks, nb, reverse), col)
        p1, p2 = params[direction]
        if mode == "gla":
            in_specs += [pl.BlockSpec((nbb, tb, kw), tok(cols["q"])),
                         pl.BlockSpec((nbb, tb, kw), tok(cols["k"])),
                         pl.BlockSpec((nbb, tb, vw), tok(cols["v"])),
                         pl.BlockSpec((nbb, tb, LANE), tok(0)),
                         const2(p1.shape), const2(p2.shape)]
            args += [proj, proj, proj, aux, p1, p2]
        else:
            in_specs += [pl.BlockSpec((nbb, tb, kw), tok(cols["q"])),
                         pl.BlockSpec((nbb, tb, vw), tok(cols["v"])),
                         pl.BlockSpec((nbb, tb, kw), tok(cols["aux"] + direction)),
                         const2(p1.shape), const2(p2.shape)]
            args += [proj, proj, proj, p1, p2]
    out_specs = [pl.BlockSpec((nbb, tb, vw), lambda b, s: (b, _scan_block(s, n_lat_blocks, nb, False), 0)),
                 pl.BlockSpec((nbb, tb, vw), lambda b, s: (b, _scan_block(s, n_lat_blocks, nb, True), 0))]
    return pl.pallas_call(
        functools.partial(_lin_kernel, mode=mode, heads=heads, dk=dk, dv=dv),
        grid=(bsz // nbb, nb),
        in_specs=in_specs,
        out_specs=out_specs,
        out_shape=[jax.ShapeDtypeStruct((bsz, tall, vw), F32)] * 2,
        scratch_shapes=[pltpu.VMEM((2, nbb, heads, dv, LANE), F32)],
        compiler_params=_cparams(2),
        name=f"{mode}_scan",
    )(*args)


def _s5_kernel(uf_ref, ub_ref, bmat_ref, lre_ref, lim_ref, cmat_ref, of_ref, ob_ref, h_ref, ut_ref, yt_ref, st_ref,
               *, bsz):
    steps = S5_CHUNK
    n_slabs = bmat_ref.shape[1]
    sw = bmat_ref.shape[3] // 2
    width = n_slabs * LANE
    re_cols = [slice(2 * s * sw, (2 * s + 1) * sw) for s in range(n_slabs)]
    im_cols = [slice((2 * s + 1) * sw, (2 * s + 2) * sw) for s in range(n_slabs)]
    both = [slice(2 * s * sw, (2 * s + 2) * sw) for s in range(n_slabs)]

    @pl.when(pl.program_id(0) == 0)
    def _():
        st_ref[...] = jnp.zeros_like(st_ref)

    def inputs(d, u_ref):
        for b in range(bsz):
            for s in range(n_slabs):
                ut_ref[d, s, pl.ds(b, steps, stride=bsz), :] = (
                    u_ref[:, b * width + s * LANE:b * width + (s + 1) * LANE])
        for s in range(n_slabs):
            h_ref[d, :, both[s]] = _dot(ut_ref[d, s].astype(BF16), bmat_ref[d, s])

    def scan(d):
        lam_re = [jnp.broadcast_to(lre_ref[d:d + 1, s * sw:(s + 1) * sw], (bsz, sw)) for s in range(n_slabs)]
        lam_im = [jnp.broadcast_to(lim_ref[d:d + 1, s * sw:(s + 1) * sw], (bsz, sw)) for s in range(n_slabs)]
        hr = [st_ref[d, :, re_cols[s]] for s in range(n_slabs)]
        hi = [st_ref[d, :, im_cols[s]] for s in range(n_slabs)]
        for tt in range(steps):
            t = tt if d == 0 else steps - 1 - tt
            rows = slice(t * bsz, (t + 1) * bsz)
            for s in range(n_slabs):
                nr = lam_re[s] * hr[s] - lam_im[s] * hi[s] + h_ref[d, rows, re_cols[s]]
                ni = lam_re[s] * hi[s] + lam_im[s] * hr[s] + h_ref[d, rows, im_cols[s]]
                h_ref[d, rows, re_cols[s]] = nr
                h_ref[d, rows, im_cols[s]] = ni
                hr[s], hi[s] = nr, ni
        for s in range(n_slabs):
            st_ref[d, :, re_cols[s]] = hr[s]
            st_ref[d, :, im_cols[s]] = hi[s]

    def outputs(d):
        for s in range(n_slabs):
            yt_ref[d, s] = _dot(h_ref[d, :, both[s]].astype(BF16), cmat_ref[s])

    def emit(d, o_ref):
        for b in range(bsz):
            for s in range(n_slabs):
                o_ref[:, b * width + s * LANE:b * width + (s + 1) * LANE] = (
                    yt_ref[d, s, pl.ds(b, steps, stride=bsz), :])

    inputs(0, uf_ref)
    inputs(1, ub_ref)
    scan(0)
    outputs(0)
    scan(1)
    outputs(1)
    emit(0, of_ref)
    emit(1, ob_ref)


def _s5_scan(u_t, bmat, lam_re, lam_im, cmat, bsz, n_lat):
    tall, bw = u_t.shape
    width = bw // bsz
    nch, n_lat_chunks = tall // S5_CHUNK, n_lat // S5_CHUNK
    n_state = lam_re.shape[-1]
    rows = S5_CHUNK * bsz
    fwd = lambda s: (_scan_block(s, n_lat_chunks, nch, False), 0)
    bwd = lambda s: (_scan_block(s, n_lat_chunks, nch, True), 0)
    whole = lambda a: pl.BlockSpec(a.shape, lambda s: (0,) * a.ndim)
    lam_re, lam_im = lam_re.reshape(2, n_state), lam_im.reshape(2, n_state)
    return pl.pallas_call(
        functools.partial(_s5_kernel, bsz=bsz),
        grid=(nch,),
        in_specs=[pl.BlockSpec((S5_CHUNK, bw), fwd), pl.BlockSpec((S5_CHUNK, bw), bwd),
                  whole(bmat), whole(lam_re), whole(lam_im), whole(cmat)],
        out_specs=[pl.BlockSpec((S5_CHUNK, bw), fwd), pl.BlockSpec((S5_CHUNK, bw), bwd)],
        out_shape=[jax.ShapeDtypeStruct((tall, bw), F32)] * 2,
        scratch_shapes=[pltpu.VMEM((2, rows, 2 * n_state), F32),
                        pltpu.VMEM((2, width // LANE, rows, LANE), F32),
                        pltpu.VMEM((2, width // LANE, rows, LANE), F32),
                        pltpu.VMEM((2, bsz, 2 * n_state), F32)],
        compiler_params=_cparams(1),
        name="s5_scan",
    )(u_t, u_t, bmat, lam_re, lam_im, cmat)


def _lin_finish(of_ref, ob_ref, gate_ref, nw_ref):
    o = of_ref[0] + ob_ref[0]
    dv = nw_ref.shape[1]
    parts = [_rms(o[:, h * dv:(h + 1) * dv], nw_ref[...]) for h in range(o.shape[1] // dv)]
    return (jnp.concatenate(parts, axis=1) * gate_ref[0].astype(F32)).astype(BF16)


def _lin_finish_specs(o_dirs, proj, gate_col, norm_w, tm):
    vw = o_dirs[0].shape[2]
    specs = [pl.BlockSpec((1, tm, vw), lambda b, i: (b, i, 0)),
             pl.BlockSpec((1, tm, vw), lambda b, i: (b, i, 0)),
             pl.BlockSpec((1, tm, vw), lambda b, i: (b, i, gate_col)),
             pl.BlockSpec(norm_w.shape, lambda b, i: (0, 0))]
    return specs, [o_dirs[0], o_dirs[1], proj, norm_w]


def _out0_kernel(x_ref, a_ref, of_ref, ob_ref, gate_ref, nw_ref, wa_ref, wb_ref, gl_ref, gc_ref, o_ref,
                 *, tm, n_lat):
    i = pl.program_id(1)
    o = _dot(a_ref[0], wa_ref[...]) + _dot(_lin_finish(of_ref, ob_ref, gate_ref, nw_ref), wb_ref[...])
    o_ref[0] = x_ref[0] + _row_select(i, tm, n_lat, gc_ref[...], gl_ref[0]) * o


def _out_proj0(x_all, mix_a, lin_finish, w_a, w_b, mod_l, mod_c, n_lat, n_rows, tm):
    bsz, _, d = x_all.shape
    tall = n_rows
    lin_specs, lin_args = _lin_finish_specs(*lin_finish, tm)
    return pl.pallas_call(
        functools.partial(_out0_kernel, tm=tm, n_lat=n_lat),
        grid=(bsz, tall // tm),
        in_specs=[pl.BlockSpec((1, tm, d), lambda b, i: (b, i, 0)),
                  pl.BlockSpec((1, tm, mix_a.shape[2]), lambda b, i: (b, i, 0))] + lin_specs + [
                  pl.BlockSpec(w_a.shape, lambda b, i: (0, 0)),
                  pl.BlockSpec(w_b.shape, lambda b, i: (0, 0))] + _mod_specs(d, (2,), 2),
        out_specs=pl.BlockSpec((1, tm, d), lambda b, i: (b, i, 0)),
        out_shape=jax.ShapeDtypeStruct((bsz, tall, d), F32),
        compiler_params=_cparams(2),
        name="out_proj_even",
    )(x_all, mix_a, *lin_args, w_a, w_b, mod_l, mod_c)


def _gelu_tanh(x):
    return 0.5 * x * (1.0 + jnp.tanh(math.sqrt(2.0 / math.pi) * (x + 0.044715 * (x * x * x))))


def _out1_kernel(x_ref, of_ref, ob_ref, gate_ref, nw_ref, yf_ref, yb_ref, u_ref, dsk_ref, gw_ref, gb_ref,
                 wa_ref, wb_ref, gl_ref, gc_ref, o_ref, *, tm, n_lat):
    i = pl.program_id(1)
    y = _gelu_tanh(yf_ref[...] + yb_ref[...] + dsk_ref[...] * u_ref[...])
    glu = _dot(y.astype(BF16), gw_ref[...]) + gb_ref[...]
    y = y * _sigmoid(glu)
    o = (_dot(_lin_finish(of_ref, ob_ref, gate_ref, nw_ref), wa_ref[...])
         + _dot(y.astype(BF16), wb_ref[...]))
    o_ref[0] = x_ref[0] + _row_select(i, tm, n_lat, gc_ref[...], gl_ref[0]) * o


def _out_proj1(x_all, lin_finish, y_dirs, u_t, d_skip, glu_w, glu_b, w_a, w_b, mod_l, mod_c, n_lat, n_rows, tm):
    bsz, _, d = x_all.shape
    tall = n_rows
    width = d_skip.shape[1]
    (y_f, y_b), u2 = y_dirs, u_t
    lin_specs, lin_args = _lin_finish_specs(*lin_finish, tm)
    return pl.pallas_call(
        functools.partial(_out1_kernel, tm=tm, n_lat=n_lat),
        grid=(bsz, tall // tm),
        in_specs=[pl.BlockSpec((1, tm, d), lambda b, i: (b, i, 0))] + lin_specs + [
                  pl.BlockSpec((tm, width), lambda b, i: (i, b)),
                  pl.BlockSpec((tm, width), lambda b, i: (i, b)),
                  pl.BlockSpec((tm, width), lambda b, i: (i, b)),
                  pl.BlockSpec((1, width), lambda b, i: (0, 0)),
                  pl.BlockSpec(glu_w.shape, lambda b, i: (0, 0)),
                  pl.BlockSpec((1, width), lambda b, i: (0, 0)),
                  pl.BlockSpec(w_a.shape, lambda b, i: (0, 0)),
                  pl.BlockSpec(w_b.shape, lambda b, i: (0, 0))] + _mod_specs(d, (2,), 2),
        out_specs=pl.BlockSpec((1, tm, d), lambda b, i: (b, i, 0)),
        out_shape=jax.ShapeDtypeStruct((bsz, tall, d), F32),
        compiler_params=_cparams(2),
        name="out_proj_odd",
    )(x_all, *lin_args, y_f, y_b, u2, d_skip, glu_w, glu_b, w_a, w_b, mod_l, mod_c)


def _mlp_kernel(*refs, tm, n_lat, final):
    if final:
        (x_ref, nw_ref, shl_ref, shc_ref, scl_ref, scc_ref, gl_ref, gc_ref, w1_ref, w2_ref, fw_ref,
         o_ref, h_scr, acc_scr) = refs
    else:
        (x_ref, nw_ref, shl_ref, shc_ref, scl_ref, scc_ref, gl_ref, gc_ref, w1_ref, w2_ref,
         o_ref, h_scr, acc_scr) = refs
    i = pl.program_id(1)
    j = pl.program_id(2)

    @pl.when(j == 0)
    def _():
        _store_norm_modulated(h_scr, x_ref[0], nw_ref[...], shl_ref[0], shc_ref[...], scl_ref[0], scc_ref[...],
                              i, tm, n_lat)
        acc_scr[...] = jnp.zeros_like(acc_scr)

    def partial_product():
        a = jnp.maximum(_dot(h_scr[...], w1_ref[...]), 0.0)
        return _dot((a * a).astype(BF16), w2_ref[...])

    last_j = pl.num_programs(2) - 1

    @pl.when(j < last_j)
    def _():
        acc_scr[...] += partial_product()

    @pl.when(j == last_j)
    def _():
        out = x_ref[0] + _row_select(i, tm, n_lat, gc_ref[...], gl_ref[0]) * (acc_scr[...] + partial_product())
        if final:
            out = _rms(out, fw_ref[...])
        o_ref[0] = out


def _mlp(x_all, norm_w, mod_l, mod_c, w1, w2, final_w, n_lat, tm, tf):
    bsz, tall, d = x_all.shape
    ff = w1.shape[1]
    final = final_w is not None
    in_specs = [pl.BlockSpec((1, tm, d), lambda b, i, j: (b, i, 0)),
                pl.BlockSpec((1, d), lambda b, i, j: (0, 0))]
    in_specs += _mod_specs(d, (3, 4, 5), 3)
    in_specs += [pl.BlockSpec((d, tf), lambda b, i, j: (0, j)),
                 pl.BlockSpec((tf, d), lambda b, i, j: (j, 0))]
    args = [x_all, norm_w.reshape(1, d)] + [mod_l, mod_c] * 3 + [w1, w2]
    if final:
        in_specs.append(pl.BlockSpec((1, d), lambda b, i, j: (0, 0)))
        args.append(final_w.reshape(1, d))
    return pl.pallas_call(
        functools.partial(_mlp_kernel, tm=tm, n_lat=n_lat, final=final),
        grid=(bsz, tall // tm, ff // tf),
        in_specs=in_specs,
        out_specs=pl.BlockSpec((1, tm, d), lambda b, i, j: (b, i, 0)),
        out_shape=jax.ShapeDtypeStruct((bsz, tall, d), F32),
        scratch_shapes=[pltpu.VMEM((tm, d), BF16), pltpu.VMEM((tm, d), F32)],
        compiler_params=_cparams(3),
        name="sq_relu_mlp",
    )(*args)


def _even_in_weight(w_in):
    sizes = (SSD_INNER, SSD_INNER + 2 * SSD_BC, 2 * SSD_HEADS, GLA_KEY, GLA_KEY, GLA_VAL,
             2 * GLA_GATE_RANK, GLA_VAL)
    offs = [0]
    for s in sizes:
        offs.append(offs[-1] + s)
    z, xbc, dt, q, k, v, lr, r = (w_in[:, offs[n]:offs[n + 1]] for n in range(8))
    pad = jnp.zeros((w_in.shape[0], LANE - dt.shape[1] - lr.shape[1]), w_in.dtype)
    main = jnp.concatenate([xbc, z, v, r, q * (GLA_DK ** -0.5), k], axis=1).astype(BF16)
    aux = jnp.concatenate([dt, lr, pad], axis=1).astype(BF16)
    return main, aux


def _s5_params(a_re, a_im, log_dt, b_re, b_im, c_re, c_im):
    delta = jnp.exp(log_dt.astype(F32))[..., None]
    mag = jnp.exp(a_re * delta)
    lbar_re, lbar_im = mag * jnp.cos(a_im * delta), mag * jnp.sin(a_im * delta)
    den = a_re * a_re + a_im * a_im
    zr = ((lbar_re - 1.0) * a_re + lbar_im * a_im) / den
    zi = (lbar_im * a_re - (lbar_re - 1.0) * a_im) / den
    bb_re = zr[..., None] * b_re - zi[..., None] * b_im
    bb_im = zr[..., None] * b_im + zi[..., None] * b_re
    n_slabs = S5_GROUPS // S5_SLAB
    eye = jnp.eye(S5_SLAB, dtype=F32)
    sw = S5_SLAB * S5_STATE

    def block_in(bb):
        bb = bb.reshape(2, n_slabs, S5_SLAB, S5_STATE, S5_GROUP)
        return jnp.einsum("dsgpc,gh->dsgchp", bb, eye).reshape(2, n_slabs, LANE, sw)

    def block_out(cc):
        cc = cc.reshape(n_slabs, S5_SLAB, S5_GROUP, S5_STATE)
        return jnp.einsum("sgcp,gh->sgphc", cc, eye).reshape(n_slabs, sw, LANE)

    bmat = jnp.concatenate([block_in(bb_re), block_in(bb_im)], axis=3).astype(BF16)
    cmat = jnp.concatenate([block_out(c_re), -block_out(c_im)], axis=1).astype(BF16)
    return (bmat, lbar_re.reshape(2, 1, S5_NSTATE), lbar_im.reshape(2, 1, S5_NSTATE), cmat)


def _layer_even(x_all, mod_l, mod_c, n_lat, tm, n_rows_out, tm_out, norm1_w, w_in, conv_w, conv_b, dt_bias,
                a_log, d_skip, ssd_norm_w, gate_w, gate_b, gla_norm_w, w_out):
    w, w_aux = _even_in_weight(w_in)
    n = w.shape[1]
    n_xbc = SSD_INNER + 2 * SSD_BC
    gates = ((n_xbc, n_xbc + SSD_INNER), (n_xbc + SSD_INNER + GLA_VAL, n_xbc + SSD_INNER + 2 * GLA_VAL))
    proj, aux = _project(x_all, norm1_w, mod_l, mod_c, w, w_aux, n_lat, tm, _largest_divisor(n, 2 * LANE, PROJ_TN),
                         extra_token_major=False, silu_ranges=gates)
    c_z, c_v, c_r = n_xbc // SSD_INNER, n_xbc // GLA_VAL + 1, n_xbc // GLA_VAL + 2
    c_q = (n_xbc + 3 * SSD_INNER) // GLA_KEY
    xbc = _conv_silu(proj, conv_w, conv_b, n_lat, n_xbc)

    neg_a = -jnp.exp(a_log.astype(F32))
    dt_bias = dt_bias.astype(F32)
    d_wide = jnp.repeat(d_skip.astype(F32), SSD_HEADDIM).reshape(1, SSD_INNER)
    y_f = _ssd_scan(xbc, proj, aux, dt_bias, neg_a, c_z, n_lat, 0)
    y_mix = _ssd_scan(xbc, proj, aux, dt_bias, neg_a, c_z, n_lat, 1,
                      (y_f, d_wide, ssd_norm_w.reshape(1, SSD_INNER)))

    cols = {"q": c_q, "k": c_q + 1, "v": c_v, "gate": c_r}
    gparams = [(gate_w[d].astype(BF16), gate_b[d].reshape(1, GLA_KEY).astype(F32)) for d in range(2)]
    o_dirs = _lin_scan("gla", proj, cols, gparams, n_lat, aux=aux)
    gla_finish = (o_dirs, proj, c_r, gla_norm_w.astype(F32).reshape(1, GLA_DV))

    w_out = w_out.astype(BF16)
    return _out_proj0(x_all, y_mix, gla_finish, w_out[:SSD_INNER], w_out[SSD_INNER:], mod_l, mod_c, n_lat,
                      n_rows_out, tm_out)


def _layer_odd(x_all, mod_l, mod_c, n_lat, tm, n_rows_out, tm_out, norm1_w, w_in, lb, hgrn_norm_w, a_re, a_im,
               log_dt, b_re, b_im, c_re, c_im, d_skip, glu_w, glu_b, w_out):
    bsz = x_all.shape[0]
    n_main = 5 * HGRN_WIDTH
    w_main = w_in[:, :n_main].astype(BF16)
    w_u = w_in[:, n_main:].astype(BF16)
    proj, u_t = _project(x_all, norm1_w, mod_l, mod_c, w_main, w_u, n_lat, tm,
                         _largest_divisor(n_main, 2 * LANE, PROJ_TN),
                         silu_ranges=((0, HGRN_WIDTH), (4 * HGRN_WIDTH, 5 * HGRN_WIDTH)))

    lb = lb.astype(F32).reshape(2, 1, HGRN_WIDTH)
    cols = {"q": 0, "v": 1, "aux": 2, "gate": 4}
    o_dirs = _lin_scan("hgrn", proj, cols, [(lb[d], 1.0 - lb[d]) for d in range(2)], n_lat)
    hgrn_finish = (o_dirs, proj, cols["gate"], hgrn_norm_w.astype(F32).reshape(1, HGRN_DV))

    bmat, lam_re, lam_im, cmat = _s5_params(a_re.astype(F32), a_im.astype(F32), log_dt, b_re.astype(F32),
                                            b_im.astype(F32), c_re.astype(F32), c_im.astype(F32))
    y_dirs = _s5_scan(u_t, bmat, lam_re, lam_im, cmat, bsz, n_lat)

    w_out = w_out.astype(BF16)
    return _out_proj1(x_all, hgrn_finish, y_dirs, u_t, d_skip.astype(F32).reshape(1, S5_WIDTH),
                      glu_w.astype(BF16), glu_b.astype(F32).reshape(1, S5_WIDTH),
                      w_out[:HGRN_WIDTH], w_out[HGRN_WIDTH:], mod_l, mod_c, n_lat, n_rows_out, tm_out)


def kernel(x, c, ctx, c_ctx, ada_w, ada_b, norm1_w, norm2_w, ssd_gla_w_in, ssd_conv_w, ssd_conv_b, ssd_dt_bias, ssd_a_log, ssd_d, ssd_norm_w, gla_gate_w, gla_gate_b, gla_norm_w, ssd_gla_w_out, hgrn_s5_w_in, hgrn_lb_logits, hgrn_norm_w, s5_a_re, s5_a_im, s5_log_dt, s5_b_re, s5_b_im, s5_c_re, s5_c_im, s5_d, s5_glu_w, s5_glu_b, hgrn_s5_w_out, mlp_w1, mlp_w2, final_norm_w):
    bsz, n_lat, d = x.shape
    ctx_len = ctx.shape[1]
    depth = ada_w.shape[0]
    tall = n_lat + ctx_len
    assert bsz % 8 == 0 and ctx_len % SCAN_BLOCK == 0 and n_lat % SCAN_BLOCK == 0 and n_lat % GRID_W == 0
    tf = 2048

    n_rows = -(-(bsz + 1) // 8) * 8
    cvec = jnp.concatenate([c, c_ctx[None, :], jnp.zeros((n_rows - bsz - 1, d), c.dtype)], axis=0)
    mod = _modulation(cvec.astype(F32), ada_w, ada_b)

    p_lb = jax.nn.softmax(hgrn_lb_logits.astype(F32), axis=0)
    lb_all = jnp.cumsum(p_lb, axis=0) - p_lb[0]

    x_all = jnp.concatenate([x, ctx], axis=1).astype(F32)
    tm_all = _largest_divisor(tall, 16, 1056)
    for layer in range(depth):
        j = layer // 2
        last = layer == depth - 1
        n_rows = n_lat if last else tall
        tm_out = _largest_divisor(n_rows, 16, 1056)
        mod_l = mod[layer, :bsz].reshape(bsz, 1, N_MOD * d)
        mod_c = mod[layer, bsz:bsz + 1]
        if layer % 2 == 0:
            x_all = _layer_even(x_all, mod_l, mod_c, n_lat, tm_all, n_rows, tm_out, norm1_w[layer],
                                ssd_gla_w_in[j], ssd_conv_w[j], ssd_conv_b[j], ssd_dt_bias[j], ssd_a_log[j],
                                ssd_d[j], ssd_norm_w[j], gla_gate_w[j], gla_gate_b[j], gla_norm_w[j],
                                ssd_gla_w_out[j])
        else:
            x_all = _layer_odd(x_all, mod_l, mod_c, n_lat, tm_all, n_rows, tm_out, norm1_w[layer],
                               hgrn_s5_w_in[j], lb_all[layer], hgrn_norm_w[j], s5_a_re[j], s5_a_im[j],
                               s5_log_dt[j], s5_b_re[j], s5_b_im[j], s5_c_re[j], s5_c_im[j], s5_d[j],
                               s5_glu_w[j], s5_glu_b[j], hgrn_s5_w_out[j])
        x_all = _mlp(x_all, norm2_w[layer], mod_l, mod_c, mlp_w1[layer].astype(BF16),
                     mlp_w2[layer].astype(BF16), final_norm_w if last else None, n_lat, tm_out, tf)
    return x_all.astype(x.dtype)
```

```python
import functools
import math

import jax
import jax.numpy as jnp
from jax import lax
from jax.experimental import pallas as pl
from jax.experimental.pallas import tpu as pltpu

F32 = jnp.float32
BF16 = jnp.bfloat16

GRID_W = 64
NORM_EPS = 1e-6
N_MOD = 6
SSD_HEADDIM = 64
SSD_HEADS = 16
SSD_GROUPS = 4
SSD_STATE = 128
SSD_CHUNK = 128
GLA_HEADS = 8
GLA_DK = 64
GLA_DV = 128
GLA_GATE_RANK = 16
GLA_GATE_NORM = 16.0
HGRN_HEADS = 8
HGRN_DK = 128
HGRN_DV = 128
S5_GROUP = 16
S5_GROUPS = 24
S5_STATE = 64
LIN_CHUNK = 64

SSD_INNER = SSD_HEADS * SSD_HEADDIM
SSD_BC = SSD_GROUPS * SSD_STATE
GLA_KEY = GLA_HEADS * GLA_DK
GLA_VAL = GLA_HEADS * GLA_DV
HGRN_WIDTH = HGRN_HEADS * HGRN_DV
S5_WIDTH = S5_GROUPS * S5_GROUP
S5_NSTATE = S5_GROUPS * S5_STATE

VMEM_LIMIT_BYTES = 56 * 1024 * 1024
LANE = 128
PROJ_TN = 3072
SCAN_BLOCK = 256
SCAN_BATCH = {"ssd_fwd": 2, "ssd_bwd": 1, "gla": 2, "hgrn": 1}
S5_CHUNK = 128
S5_SLAB = LANE // S5_GROUP


def _cparams(n_axes):
    return pltpu.CompilerParams(dimension_semantics=("arbitrary",) * n_axes,
                                vmem_limit_bytes=VMEM_LIMIT_BYTES)


def _largest_divisor(n, multiple, cap):
    best = None
    for d in range(multiple, min(n, cap) + 1, multiple):
        if n % d == 0:
            best = d
    assert best is not None, (n, multiple, cap)
    return best


_NEG_LOG2E = -1.4426950408889634


def _sigmoid(x):
    return 1.0 / (1.0 + jnp.exp2(x * _NEG_LOG2E))


def _silu(x):
    return x * _sigmoid(x)


def _softplus(x):
    return jnp.maximum(x, 0.0) + jnp.log1p(jnp.exp(-jnp.abs(x)))


def _log_sigmoid(x):
    return -_softplus(-x)


def _rms(x, w):
    return x * lax.rsqrt(jnp.mean(x * x, axis=-1, keepdims=True) + NORM_EPS) * w


def _dot(a, b, dims=(((1,), (0,)), ((), ())), precision=None):
    return lax.dot_general(a, b, dims, precision=precision, preferred_element_type=F32)


def _split3(v):
    hi = v.astype(BF16)
    r1 = v - hi.astype(F32)
    mid = r1.astype(BF16)
    lo = (r1 - mid.astype(F32)).astype(BF16)
    return hi, mid, lo


def _tri3(mask):
    tri = mask.astype(BF16)
    return jnp.concatenate([tri, tri, tri], axis=1)


def _cumsum_rows(tri2, v):
    hi = v.astype(BF16)
    lo = (v - hi.astype(F32)).astype(BF16)
    return _dot(tri2, jnp.concatenate([hi, lo], axis=0))


_NT = (((1,), (1,)), ((), ()))
_TN = (((0,), (0,)), ((), ()))
_TT = (((0,), (1,)), ((), ()))


def _mod_kernel(c_ref, w_ref, b_ref, o_ref):
    a = _silu(c_ref[...]).astype(BF16)
    o_ref[0] = _dot(a, w_ref[0].astype(BF16)) + b_ref[0]


def _modulation(cvec, ada_w, ada_b):
    depth, d, n = ada_w.shape
    rows = cvec.shape[0]
    tn = _largest_divisor(n, LANE, 1024)
    return pl.pallas_call(
        _mod_kernel,
        grid=(depth, n // tn),
        in_specs=[pl.BlockSpec((rows, d), lambda l, j: (0, 0)),
                  pl.BlockSpec((1, d, tn), lambda l, j: (l, 0, j)),
                  pl.BlockSpec((1, 1, tn), lambda l, j: (l, 0, j))],
        out_specs=pl.BlockSpec((1, rows, tn), lambda l, j: (l, 0, j)),
        out_shape=jax.ShapeDtypeStruct((depth, rows, n), F32),
        compiler_params=_cparams(2),
        name="adaln_mod",
    )(cvec, ada_w, ada_b.reshape(depth, 1, n))


def _row_select(i, tm, n_lat, ctx_val, lat_val):
    row = i * tm + lax.broadcasted_iota(jnp.int32, (tm, 1), 0)
    return jnp.where(row >= n_lat, ctx_val, lat_val)


def _store_norm_modulated(h_ref, x, nw, shift_l, shift_c, scale_l, scale_c, i, tm, n_lat):
    xn = x * lax.rsqrt(jnp.mean(x * x, axis=-1, keepdims=True) + NORM_EPS)
    gain = _row_select(i, tm, n_lat, nw * (1.0 + scale_c), nw * (1.0 + scale_l))
    h_ref[...] = (xn * gain + _row_select(i, tm, n_lat, shift_c, shift_l)).astype(h_ref.dtype)


def _mod_specs(d, cols, n_grid_axes):
    specs = []
    for k in cols:
        if n_grid_axes == 2:
            specs.append(pl.BlockSpec((1, 1, d), lambda b, i, k=k: (b, 0, k)))
            specs.append(pl.BlockSpec((1, d), lambda b, i, k=k: (0, k)))
        else:
            specs.append(pl.BlockSpec((1, 1, d), lambda b, i, j, k=k: (b, 0, k)))
            specs.append(pl.BlockSpec((1, d), lambda b, i, j, k=k: (0, k)))
    return specs


def _proj_kernel(*refs, tm, n_lat, has_extra, silu_cols, n_prior):
    n_in = 8 if has_extra else 7
    refs = refs[:n_in] + refs[n_in + n_prior:]
    if has_extra:
        (x_ref, nw_ref, shl_ref, shc_ref, scl_ref, scc_ref, w_ref, wx_ref, o_ref, ox_ref, h_scr) = refs
    else:
        (x_ref, nw_ref, shl_ref, shc_ref, scl_ref, scc_ref, w_ref, o_ref, h_scr) = refs
    i = pl.program_id(1)
    j = pl.program_id(2)

    @pl.when(j == 0)
    def _():
        _store_norm_modulated(h_scr, x_ref[0], nw_ref[...], shl_ref[0], shc_ref[...], scl_ref[0], scc_ref[...],
                              i, tm, n_lat)
        if has_extra:
            ox_ref[...] = _dot(h_scr[...], wx_ref[...]).reshape(ox_ref.shape)

    for jj, ranges in enumerate(silu_cols):
        @pl.when(j == jj)
        def _(ranges=ranges):
            o = _dot(h_scr[...], w_ref[...])
            if not ranges:
                o_ref[0] = o.astype(o_ref.dtype)
            edge = 0
            for start, stop in ranges:
                if start > edge:
                    o_ref[0, :, edge:start] = o[:, edge:start].astype(o_ref.dtype)
                o_ref[0, :, start:stop] = _silu(o[:, start:stop]).astype(o_ref.dtype)
                edge = stop
            if ranges and edge < o.shape[1]:
                o_ref[0, :, edge:] = o[:, edge:].astype(o_ref.dtype)


def _project(x_all, norm_w, mod_l, mod_c, w, w_extra, n_lat, tm, tn, extra_token_major=True, silu_ranges=(),
             tall=None, row_block0=0, prior=None):
    bsz, n_rows, d = x_all.shape
    tall = n_rows if tall is None else tall
    n = w.shape[1]
    has_extra = w_extra is not None
    silu_cols = tuple(
        tuple((max(a, jj * tn) - jj * tn, min(b, (jj + 1) * tn) - jj * tn)
              for a, b in sorted(silu_ranges) if a < (jj + 1) * tn and b > jj * tn)
        for jj in range(n // tn))
    in_specs = [pl.BlockSpec((1, tm, d), lambda b, i, j: (b, i, 0)),
                pl.BlockSpec((1, d), lambda b, i, j: (0, 0))]
    in_specs += _mod_specs(d, (0, 1), 3)
    in_specs.append(pl.BlockSpec((d, tn), lambda b, i, j: (0, j)))
    args = [x_all, norm_w.reshape(1, d), mod_l, mod_c, mod_l, mod_c, w]
    out_specs = [pl.BlockSpec((1, tm, tn), lambda b, i, j: (b, row_block0 + i, j))]
    out_shape = [jax.ShapeDtypeStruct((bsz, tall, n), BF16)]
    if has_extra:
        nx = w_extra.shape[1]
        in_specs.append(pl.BlockSpec((d, nx), lambda b, i, j: (0, 0)))
        args.append(w_extra)
        if extra_token_major:
            out_specs.append(pl.BlockSpec((tm, nx), lambda b, i, j: (row_block0 + i, b)))
            out_shape.append(jax.ShapeDtypeStruct((tall, bsz * nx), F32))
        else:
            out_specs.append(pl.BlockSpec((1, tm, nx), lambda b, i, j: (b, row_block0 + i, 0)))
            out_shape.append(jax.ShapeDtypeStruct((bsz, tall, nx), F32))
    prior = list(prior) if prior is not None else []
    aliases = {len(args) + k: k for k in range(len(prior))}
    in_specs += [pl.BlockSpec(memory_space=pl.ANY)] * len(prior)
    args += prior
    out = pl.pallas_call(
        functools.partial(_proj_kernel, tm=tm, n_lat=n_lat, has_extra=has_extra, silu_cols=silu_cols,
                          n_prior=len(prior)),
        grid=(bsz, n_rows // tm, n // tn),
        in_specs=in_specs,
        out_specs=out_specs,
        out_shape=out_shape,
        input_output_aliases=aliases,
        scratch_shapes=[pltpu.VMEM((tm, d), BF16)],
        compiler_params=_cparams(3),
        name="norm_mod_proj",
    )(*args)
    return out if has_extra else out[0]


def _conv_kernel(main_ref, prev_ref, next_ref, w_ref, b_ref, o_ref, *, tt, n_lat, tall):
    i = pl.program_id(1)
    te = tt + 2 * GRID_W
    p = i * tt - GRID_W + lax.broadcasted_iota(jnp.int32, (te, 1), 0)
    is_ctx = p >= n_lat
    col = jnp.bitwise_and(p, GRID_W - 1)
    has_left = jnp.where(is_ctx, p - n_lat, col) > 0
    has_right = jnp.where(is_ctx, p - (tall - 1), col - (GRID_W - 1)) < 0
    w = w_ref[...]

    def conv(interior):
        ext = jnp.concatenate([prev_ref[0], main_ref[0], next_ref[0]], axis=0).astype(F32)
        own = slice(GRID_W, GRID_W + tt)
        if interior:
            as_left = jnp.where(has_right, ext, 0.0)
            as_right = jnp.where(has_left, ext, 0.0)
        acc = jnp.zeros((tt, w.shape[1]), F32) + b_ref[...]
        for dy in (-1, 0, 1):
            rs = slice(GRID_W + GRID_W * dy, GRID_W + GRID_W * dy + tt)
            k0 = 3 * (dy + 1)
            if interior:
                left, right = pltpu.roll(as_left[rs], 1, 0), pltpu.roll(as_right[rs], tt - 1, 0)
            else:
                left = jnp.where(has_left[own], pltpu.roll(ext[rs], 1, 0), 0.0)
                right = jnp.where(has_right[own], pltpu.roll(ext[rs], tt - 1, 0), 0.0)
            t = ext[rs] * w[k0 + 1:k0 + 2] + left * w[k0:k0 + 1] + right * w[k0 + 2:k0 + 3]
            if not interior and dy != 0:
                q = p[own]
                if dy == -1:
                    ok = jnp.where(q >= n_lat, 0, q) >= GRID_W
                else:
                    ok = jnp.where(q >= n_lat, n_lat, q) < n_lat - GRID_W
                t = jnp.where(ok, t, 0.0)
            acc = acc + t
        o_ref[0] = _silu(acc).astype(o_ref.dtype)

    interior = jnp.logical_and(i * tt >= GRID_W, (i + 1) * tt <= n_lat - GRID_W)
    pl.when(interior)(lambda: conv(True))
    pl.when(jnp.logical_not(interior))(lambda: conv(False))


def _conv_silu(proj, conv_w, conv_b, n_lat, n_ch):
    bsz, tall, _ = proj.shape
    n_rows = tall // GRID_W
    tt = _largest_divisor(tall, GRID_W, 768)
    assert n_lat % GRID_W == 0 and n_lat // tt == (tall - 1) // tt
    r = tt // GRID_W
    tc = 1024
    return pl.pallas_call(
        functools.partial(_conv_kernel, tt=tt, n_lat=n_lat, tall=tall),
        grid=(bsz, tall // tt, n_ch // tc),
        in_specs=[pl.BlockSpec((1, tt, tc), lambda b, i, c: (b, i, c)),
                  pl.BlockSpec((1, GRID_W, tc), lambda b, i, c: (b, jnp.maximum(i * r - 1, 0), c)),
                  pl.BlockSpec((1, GRID_W, tc), lambda b, i, c: (b, jnp.minimum((i + 1) * r, n_rows - 1), c)),
                  pl.BlockSpec((9, tc), lambda b, i, c: (0, c)),
                  pl.BlockSpec((1, tc), lambda b, i, c: (0, c))],
        out_specs=pl.BlockSpec((1, tt, tc), lambda b, i, c: (b, i, c)),
        out_shape=jax.ShapeDtypeStruct((bsz, tall, n_ch), BF16),
        compiler_params=_cparams(3),
        name="dwconv_silu",
    )(proj, proj, proj, conv_w.reshape(9, n_ch), conv_b.reshape(1, n_ch))


def _scan_block(s, n_lat_blocks, n_blocks, reverse):
    n_ctx_blocks = n_blocks - n_lat_blocks
    if not reverse:
        return jnp.where(s < n_ctx_blocks, n_lat_blocks + s, s - n_ctx_blocks)
    return n_blocks - 1 - s


def _tri_mask(c, reverse):
    ri = lax.broadcasted_iota(jnp.int32, (c, c), 0)
    ci = lax.broadcasted_iota(jnp.int32, (c, c), 1)
    return (ci >= ri) if reverse else (ci <= ri)


def _ssd_expand_matrix():
    eye = jnp.eye(SSD_HEADS, dtype=F32)
    e_head = jnp.repeat(eye, SSD_HEADDIM, axis=1)
    e_seg = jnp.repeat(eye, SSD_CHUNK, axis=1)
    zh = jnp.zeros_like(e_head)
    zs = jnp.zeros_like(e_seg)
    blk = jnp.concatenate([
        jnp.concatenate([e_head, zh, zh, zs], axis=1),
        jnp.concatenate([zh, e_head, zh, zs], axis=1),
        jnp.concatenate([zh, zh, e_head, zs], axis=1),
        jnp.concatenate([zh, zh, zh, e_seg], axis=1)], axis=0)
    return jnp.concatenate([blk, blk, blk], axis=0).astype(BF16)


def _ssd_kernel(*refs, direction, finish):
    if finish:
        (x_ref, bm_ref, cm_ref, dtlr_ref, dtb_ref, nega_ref, exp_ref, z_ref, yf_ref, dsk_ref, nw_ref,
         o_ref, st_ref) = refs
    else:
        (x_ref, bm_ref, cm_ref, dtlr_ref, dtb_ref, nega_ref, exp_ref, o_ref, st_ref) = refs
    reverse = direction == 1
    c = SSD_CHUNK
    p = SSD_HEADDIM
    gw = SSD_INNER // SSD_GROUPS
    hpg = SSD_HEADS // SSD_GROUPS

    @pl.when(pl.program_id(1) == 0)
    def _():
        st_ref[...] = jnp.zeros_like(st_ref)

    mask = _tri_mask(c, reverse)
    tri3 = _tri3(mask)
    last = 0 if reverse else c - 1
    n_chunks = x_ref.shape[1] // c
    chunk_order = range(n_chunks - 1, -1, -1) if reverse else range(n_chunks)
    order = [(bb, cc) for bb in range(x_ref.shape[0]) for cc in chunk_order]
    rows = {ck: slice(ck[1] * c, (ck[1] + 1) * c) for ck in order}
    groups = range(SSD_GROUPS)
    g_cols = [slice(g * gw, (g + 1) * gw) for g in groups]
    n_cols = [slice(g * SSD_STATE, (g + 1) * SSD_STATE) for g in groups]
    cb = {(ck, g): _dot(cm_ref[ck[0], rows[ck], n_cols[g]], bm_ref[ck[0], rows[ck], n_cols[g]], _NT)
          for ck in order for g in groups}
    prep = {}
    for ck in order:
        bb, rs = ck[0], rows[ck]
        x = x_ref[bb, rs, :].astype(F32)
        dt_raw = dtlr_ref[bb, rs, :][:, direction * SSD_HEADS:(direction + 1) * SSD_HEADS].astype(F32)
        dt = _softplus(dt_raw + dtb_ref[...])
        la3 = jnp.concatenate(_split3(dt * nega_ref[...]), axis=0)
        acum = _dot(tri3, la3)
        acum_t = _dot(la3, tri3, _TT)
        a_last = acum[last:last + 1]
        narrow = jnp.concatenate([dt, dt * jnp.exp(a_last - acum), jnp.exp(acum), acum], axis=1)
        wide = _dot(jnp.concatenate(_split3(narrow), axis=1), exp_ref[...])
        ea_w = wide[:, 2 * SSD_INNER:3 * SSD_INNER]
        prep[ck] = dict(
            x=x, bm=bm_ref[bb, rs, :], cm=cm_ref[bb, rs, :], wide=wide, acum_t=acum_t, ea_w=ea_w,
            xdt=(x * wide[:, :SSD_INNER]).astype(BF16),
            xw=(x * wide[:, SSD_INNER:2 * SSD_INNER]).astype(BF16),
            e_last=ea_w[last:last + 1])
    kv = {(ck, g): _dot(prep[ck]["bm"][:, n_cols[g]], prep[ck]["xw"][:, g_cols[g]], _TN)
          for ck in order for g in groups}
    scores, st_used = {}, {}
    for bb in range(x_ref.shape[0]):
        for g in groups:
            st = st_ref[bb, :, g_cols[g]]
            for cc in chunk_order:
                st_used[(bb, cc), g] = st.astype(BF16)
                st = st * prep[bb, cc]["e_last"][:, g_cols[g]] + kv[(bb, cc), g]
            st_ref[bb, :, g_cols[g]] = st
    for ck in order:
        for h in range(SSD_HEADS):
            a_i = prep[ck]["wide"][:, 3 * SSD_INNER + h * c:3 * SSD_INNER + (h + 1) * c]
            decay = jnp.exp(jnp.where(mask, a_i - prep[ck]["acum_t"][h:h + 1, :], -jnp.inf))
            scores[ck, h] = (cb[ck, h // hpg] * decay).astype(BF16)
    for ck in order:
        bb, rs = ck[0], rows[ck]
        x = prep[ck]["x"]
        y_groups = []
        for g in groups:
            ys = [_dot(scores[ck, h], prep[ck]["xdt"][:, h * p:(h + 1) * p])
                  for h in range(g * hpg, (g + 1) * hpg)]
            y_state = _dot(prep[ck]["cm"][:, n_cols[g]], st_used[ck, g]) * prep[ck]["ea_w"][:, g_cols[g]]
            y_groups.append(jnp.concatenate(ys, axis=1) + y_state)
        y = jnp.concatenate(y_groups, axis=1)
        if finish:
            z_act = z_ref[bb, rs, :].astype(F32)
            y = (y + yf_ref[bb, rs, :] + dsk_ref[...] * x) * z_act
            outs = []
            for g in range(SSD_GROUPS):
                sl = slice(g * gw, (g + 1) * gw)
                outs.append(_rms(y[:, sl], nw_ref[:, sl]))
            o_ref[bb, rs, :] = jnp.concatenate(outs, axis=1).astype(o_ref.dtype)
        else:
            o_ref[bb, rs, :] = y


def _ssd_scan(xbc, proj, aux, dt_bias, neg_a, z_col, n_lat, direction, finish_args=None):
    bsz, tall, _ = xbc.shape
    tb = SCAN_BLOCK
    nb, n_lat_blocks = tall // tb, n_lat // tb
    reverse = direction == 1
    finish = finish_args is not None
    expand = _ssd_expand_matrix()

    def tok(col):
        return lambda b, s: (b, _scan_block(s, n_lat_blocks, nb, reverse), col)

    nbb = SCAN_BATCH["ssd_bwd" if finish else "ssd_fwd"]
    in_specs = [pl.BlockSpec((nbb, tb, SSD_INNER), tok(0)),
                pl.BlockSpec((nbb, tb, SSD_BC), tok(SSD_INNER // SSD_BC)),
                pl.BlockSpec((nbb, tb, SSD_BC), tok(SSD_INNER // SSD_BC + 1)),
                pl.BlockSpec((nbb, tb, LANE), tok(0)),
                pl.BlockSpec((1, SSD_HEADS), lambda b, s: (0, 0)),
                pl.BlockSpec((1, SSD_HEADS), lambda b, s: (0, 0)),
                pl.BlockSpec(expand.shape, lambda b, s: (0, 0))]
    args = [xbc, xbc, xbc, aux, dt_bias[direction:direction + 1], neg_a[direction:direction + 1], expand]
    if finish:
        y_f, d_skip_wide, norm_w = finish_args
        in_specs += [pl.BlockSpec((nbb, tb, SSD_INNER), tok(z_col)),
                     pl.BlockSpec((nbb, tb, SSD_INNER), tok(0)),
                     pl.BlockSpec((1, SSD_INNER), lambda b, s: (0, 0)),
                     pl.BlockSpec((1, SSD_INNER), lambda b, s: (0, 0))]
        args += [proj, y_f, d_skip_wide, norm_w]
    return pl.pallas_call(
        functools.partial(_ssd_kernel, direction=direction, finish=finish),
        grid=(bsz // nbb, nb),
        in_specs=in_specs,
        out_specs=pl.BlockSpec((nbb, tb, SSD_INNER), tok(0)),
        out_shape=jax.ShapeDtypeStruct((bsz, tall, SSD_INNER), BF16 if finish else F32),
        scratch_shapes=[pltpu.VMEM((nbb, SSD_STATE, SSD_INNER), F32)],
        compiler_params=_cparams(2),
        name="ssd_scan_bwd" if reverse else "ssd_scan_fwd",
    )(*args)


def _lin_kernel(*refs, mode, heads, dk, dv):
    n_in = 6 if mode == "gla" else 5
    ins = [refs[:n_in], refs[n_in:2 * n_in]]
    o_refs = refs[2 * n_in:2 * n_in + 2]
    st_ref = refs[2 * n_in + 2]
    c = LIN_CHUNK

    @pl.when(pl.program_id(1) == 0)
    def _():
        st_ref[...] = jnp.zeros_like(st_ref)

    tb = o_refs[0].shape[1]
    nc = tb // c
    hpt = LANE // dk
    n_tiles = heads // hpt
    span = n_tiles if mode == "gla" else 1
    gw = span * LANE
    n_groups = n_tiles // span
    hpg = hpt * span
    zeros = jnp.zeros((c, LANE), BF16)
    ri = lax.broadcasted_iota(jnp.int32, (tb, tb), 0)
    ci = lax.broadcasted_iota(jnp.int32, (tb, tb), 1)
    c_shift = c.bit_length() - 1
    same_chunk = jnp.right_shift(ri, c_shift) == jnp.right_shift(ci, c_shift)

    def chunk_blocks(a):
        cols = []
        for b in range(nc):
            cols.append(jnp.concatenate(
                [a[cc * c:(cc + 1) * c] if cc == b else zeros for cc in range(nc)], axis=0))
        return jnp.concatenate(cols, axis=1)

    def tile_cols(h):
        t = (h // hpt) % span
        return slice(t * LANE, (t + 1) * LANE)

    def make_stream(direction, bb):
        if mode == "gla":
            q_ref, k_ref, v_ref, aux_ref, p1_ref, p2_ref = ins[direction]
        else:
            q_ref, v_ref, aux_ref, p1_ref, p2_ref = ins[direction]
            k_ref = None
        o_ref = o_refs[direction]
        reverse = direction == 1
        chunks = range(nc - 1, -1, -1) if reverse else range(nc)
        last = 0 if reverse else c - 1
        if reverse:
            bd_mask = jnp.where(same_chunk, ci - ri, -1) >= 0
        else:
            bd_mask = jnp.where(same_chunk, ci - ri, 1) <= 0
        tri = _tri_mask(c, reverse).astype(BF16)
        tri2 = jnp.concatenate([tri, tri], axis=1)
        groups, work = {}, {}

        def prep(g):
            ls = slice(g * gw, (g + 1) * gw)
            if mode == "gla":
                q = q_ref[bb, :, ls].astype(F32)
                k = k_ref[bb, :, ls].astype(F32)
                off = 2 * SSD_HEADS + direction * GLA_GATE_RANK
                lr = aux_ref[bb][:, off:off + GLA_GATE_RANK].astype(BF16)
                lg = _log_sigmoid(_dot(lr, p1_ref[:, ls]) + p2_ref[:, ls]) * (1.0 / GLA_GATE_NORM)
            else:
                q = q_ref[bb, :, ls].astype(F32)
                f_raw = aux_ref[bb, :, ls].astype(F32)
                e = jnp.exp2(jnp.abs(f_raw) * _NEG_LOG2E)
                r = 1.0 / (1.0 + e)
                forget = p1_ref[:, ls] + p2_ref[:, ls] * jnp.where(f_raw >= 0.0, r, e * r)
                lg = jnp.log(forget)
                k = 1.0 - forget
            gcum = jnp.concatenate([_cumsum_rows(tri2, lg[cc * c:(cc + 1) * c]) for cc in range(nc)], axis=0)
            e_last = [jnp.exp(gcum[cc * c + last:cc * c + last + 1]) for cc in range(nc)]
            e_rows = jnp.concatenate([jnp.broadcast_to(e, (c, gw)) for e in e_last], axis=0)
            e_gcum = jnp.exp(gcum)
            q_decf = q * e_gcum
            if hpt > 1:
                head_of_lane = jnp.bitwise_and(jnp.right_shift(
                    lax.broadcasted_iota(jnp.int32, (1, gw), 1), dk.bit_length() - 1), hpt - 1)
                q_dec = [jnp.where(head_of_lane == r, q_decf, 0.0).astype(BF16) for r in range(hpt)]
            else:
                q_dec = [q_decf.astype(BF16)]
            k_invf = k * (1.0 / e_gcum)
            groups[g] = dict(q_dec=q_dec, k_inv=k_invf.astype(BF16), k_end=(k_invf * e_rows).astype(BF16),
                             e_last=e_last)

        def products(h):
            gp, ts = groups[h // hpg], tile_cols(h)
            qh = gp["q_dec"][h % hpt][:, ts]
            vh = v_ref[bb, :, h * dv:(h + 1) * dv]
            work[h] = dict(
                qh=qh, vh=vh, scores=_dot(qh, gp["k_inv"][:, ts], _NT),
                kv_t=_dot(vh, chunk_blocks(gp["k_end"][:, ts]), _TN))

        def mask_and_chain(h):
            w = work[h]
            e_last = groups[h // hpg]["e_last"]
            w["att"] = jnp.where(bd_mask, w.pop("scores"), 0.0).astype(BF16)
            st = st_ref[direction, bb, h]
            used = [None] * nc
            for cc in chunks:
                used[cc] = st.astype(BF16)
                st = st * e_last[cc][:, tile_cols(h)] + w["kv_t"][:, cc * LANE:(cc + 1) * LANE]
            st_ref[direction, bb, h] = st
            w["used"] = used
            del w["kv_t"]

        def outputs(h):
            w = work[h]
            o_state = jnp.concatenate(
                [_dot(w["qh"][cc * c:(cc + 1) * c], w["used"][cc], _NT) for cc in range(nc)], axis=0)
            w["o"] = _dot(w["att"], w["vh"]) + o_state

        def emit(h):
            o_ref[bb, :, h * dv:(h + 1) * dv] = work.pop(h)["o"]

        return dict(prep=prep, products=products, mask_and_chain=mask_and_chain, outputs=outputs, emit=emit)

    streams = [make_stream(d, bb) for bb in range(o_refs[0].shape[0]) for d in (0, 1)]

    def heads_of(g):
        return range(g * hpg, (g + 1) * hpg) if 0 <= g < n_groups else ()

    lag = 2 if n_groups > 1 else 1
    for it in range(n_groups + 4 * lag):
        for stage, delay in (("products", lag), ("outputs", 3 * lag)):
            for st in streams:
                for h in heads_of(it - delay):
                    st[stage](h)
        if it < n_groups:
            for st in streams:
                st["prep"](it)
        for stage, delay in (("mask_and_chain", 2 * lag), ("emit", 4 * lag)):
            for st in streams:
                for h in heads_of(it - delay):
                    st[stage](h)


def _lin_scan(mode, proj, cols, params, n_lat, aux=None):
    bsz, tall, _ = proj.shape
    if mode == "gla":
        heads, dk, dv = GLA_HEADS, GLA_DK, GLA_DV
    else:
        heads, dk, dv = HGRN_HEADS, HGRN_DK, HGRN_DV
    kw, vw = heads * dk, heads * dv
    tb, nbb = SCAN_BLOCK, SCAN_BATCH[mode]
    nb, n_lat_blocks = tall // tb, n_lat // tb

    def const2(shape):
        return pl.BlockSpec(shape, lambda b, s: (0, 0))

    in_specs, args = [], []
    for direction in (0, 1):
        def tok(col, reverse=direction == 1):
            return lambda b, s: (b, _scan_block(s, n_lat_blocks, nb, reverse), col)
        p1, p2 = params[direction]
        if mode == "gla":
            in_specs += [pl.BlockSpec((nbb, tb, kw), tok(cols["q"])),
                         pl.BlockSpec((nbb, tb, kw), tok(cols["k"])),
                         pl.BlockSpec((nbb, tb, vw), tok(cols["v"])),
                         pl.BlockSpec((nbb, tb, LANE), tok(0)),
                         const2(p1.shape), const2(p2.shape)]
            args += [proj, proj, proj, aux, p1, p2]
        else:
            in_specs += [pl.BlockSpec((nbb, tb, kw), tok(cols["q"])),
                         pl.BlockSpec((nbb, tb, vw), tok(cols["v"])),
                         pl.BlockSpec((nbb, tb, kw), tok(cols["aux"] + direction)),
                         const2(p1.shape), const2(p2.shape)]
            args += [proj, proj, proj, p1, p2]
    out_specs = [pl.BlockSpec((nbb, tb, vw), lambda b, s: (b, _scan_block(s, n_lat_blocks, nb, False), 0)),
                 pl.BlockSpec((nbb, tb, vw), lambda b, s: (b, _scan_block(s, n_lat_blocks, nb, True), 0))]
    return pl.pallas_call(
        functools.partial(_lin_kernel, mode=mode, heads=heads, dk=dk, dv=dv),
        grid=(bsz // nbb, nb),
        in_specs=in_specs,
        out_specs=out_specs,
        out_shape=[jax.ShapeDtypeStruct((bsz, tall, vw), F32)] * 2,
        scratch_shapes=[pltpu.VMEM((2, nbb, heads, dv, LANE), F32)],
        compiler_params=_cparams(2),
        name=f"{mode}_scan",
    )(*args)


def _s5_kernel(uf_ref, ub_ref, bmat_ref, lre_ref, lim_ref, cmat_ref, of_ref, ob_ref, h_ref, ut_ref, yt_ref, st_ref,
               *, bsz):
    steps = S5_CHUNK
    n_slabs = bmat_ref.shape[1]
    sw = bmat_ref.shape[3] // 2
    width = n_slabs * LANE
    re_cols = [slice(2 * s * sw, (2 * s + 1) * sw) for s in range(n_slabs)]
    im_cols = [slice((2 * s + 1) * sw, (2 * s + 2) * sw) for s in range(n_slabs)]
    both = [slice(2 * s * sw, (2 * s + 2) * sw) for s in range(n_slabs)]

    @pl.when(pl.program_id(0) == 0)
    def _():
        st_ref[...] = jnp.zeros_like(st_ref)

    def inputs(d, u_ref):
        for b in range(bsz):
            for s in range(n_slabs):
                ut_ref[d, s, pl.ds(b, steps, stride=bsz), :] = (
                    u_ref[:, b * width + s * LANE:b * width + (s + 1) * LANE])
        for s in range(n_slabs):
            h_ref[d, :, both[s]] = _dot(ut_ref[d, s].astype(BF16), bmat_ref[d, s])

    def scan(d):
        lam_re = [jnp.broadcast_to(lre_ref[d:d + 1, s * sw:(s + 1) * sw], (bsz, sw)) for s in range(n_slabs)]
        lam_im = [jnp.broadcast_to(lim_ref[d:d + 1, s * sw:(s + 1) * sw], (bsz, sw)) for s in range(n_slabs)]
        hr = [st_ref[d, :, re_cols[s]] for s in range(n_slabs)]
        hi = [st_ref[d, :, im_cols[s]] for s in range(n_slabs)]
        for tt in range(steps):
            t = tt if d == 0 else steps - 1 - tt
            rows = slice(t * bsz, (t + 1) * bsz)
            for s in range(n_slabs):
                nr = lam_re[s] * hr[s] - lam_im[s] * hi[s] + h_ref[d, rows, re_cols[s]]
                ni = lam_re[s] * hi[s] + lam_im[s] * hr[s] + h_ref[d, rows, im_cols[s]]
                h_ref[d, rows, re_cols[s]] = nr
                h_ref[d, rows, im_cols[s]] = ni
                hr[s], hi[s] = nr, ni
        for s in range(n_slabs):
            st_ref[d, :, re_cols[s]] = hr[s]
            st_ref[d, :, im_cols[s]] = hi[s]

    def outputs(d):
        for s in range(n_slabs):
            yt_ref[d, s] = _dot(h_ref[d, :, both[s]].astype(BF16), cmat_ref[s])

    def emit(d, o_ref):
        for b in range(bsz):
            for s in range(n_slabs):
                o_ref[:, b * width + s * LANE:b * width + (s + 1) * LANE] = (
                    yt_ref[d, s, pl.ds(b, steps, stride=bsz), :])

    inputs(0, uf_ref)
    inputs(1, ub_ref)
    scan(0)
    outputs(0)
    scan(1)
    outputs(1)
    emit(0, of_ref)
    emit(1, ob_ref)


def _s5_scan(u_t, bmat, lam_re, lam_im, cmat, bsz, n_lat):
    tall, bw = u_t.shape
    width = bw // bsz
    nch, n_lat_chunks = tall // S5_CHUNK, n_lat // S5_CHUNK
    n_state = lam_re.shape[-1]
    rows = S5_CHUNK * bsz
    fwd = lambda s: (_scan_block(s, n_lat_chunks, nch, False), 0)
    bwd = lambda s: (_scan_block(s, n_lat_chunks, nch, True), 0)
    whole = lambda a: pl.BlockSpec(a.shape, lambda s: (0,) * a.ndim)
    lam_re, lam_im = lam_re.reshape(2, n_state), lam_im.reshape(2, n_state)
    return pl.pallas_call(
        functools.partial(_s5_kernel, bsz=bsz),
        grid=(nch,),
        in_specs=[pl.BlockSpec((S5_CHUNK, bw), fwd), pl.BlockSpec((S5_CHUNK, bw), bwd),
                  whole(bmat), whole(lam_re), whole(lam_im), whole(cmat)],
        out_specs=[pl.BlockSpec((S5_CHUNK, bw), fwd), pl.BlockSpec((S5_CHUNK, bw), bwd)],
        out_shape=[jax.ShapeDtypeStruct((tall, bw), F32)] * 2,
        scratch_shapes=[pltpu.VMEM((2, rows, 2 * n_state), F32),
                        pltpu.VMEM((2, width // LANE, rows, LANE), F32),
                        pltpu.VMEM((2, width // LANE, rows, LANE), F32),
                        pltpu.VMEM((2, bsz, 2 * n_state), F32)],
        compiler_params=_cparams(1),
        name="s5_scan",
    )(u_t, u_t, bmat, lam_re, lam_im, cmat)


def _lin_finish(of_ref, ob_ref, gate_ref, nw_ref):
    o = of_ref[0] + ob_ref[0]
    dv = nw_ref.shape[1]
    parts = [_rms(o[:, h * dv:(h + 1) * dv], nw_ref[...]) for h in range(o.shape[1] // dv)]
    return (jnp.concatenate(parts, axis=1) * gate_ref[0].astype(F32)).astype(BF16)


def _lin_finish_specs(o_dirs, proj, gate_col, norm_w, tm, row_block0=0):
    vw = o_dirs[0].shape[2]
    specs = [pl.BlockSpec((1, tm, vw), lambda b, i: (b, row_block0 + i, 0)),
             pl.BlockSpec((1, tm, vw), lambda b, i: (b, row_block0 + i, 0)),
             pl.BlockSpec((1, tm, vw), lambda b, i: (b, row_block0 + i, gate_col)),
             pl.BlockSpec(norm_w.shape, lambda b, i: (0, 0))]
    return specs, [o_dirs[0], o_dirs[1], proj, norm_w]


def _out0_kernel(x_ref, a_ref, of_ref, ob_ref, gate_ref, nw_ref, wa_ref, wb_ref, gl_ref, gc_ref, *rest,
                 tm, n_lat):
    o_ref = rest[-1]
    i = pl.program_id(1)
    o = _dot(a_ref[0], wa_ref[...]) + _dot(_lin_finish(of_ref, ob_ref, gate_ref, nw_ref), wb_ref[...])
    o_ref[0] = x_ref[0] + _row_select(i, tm, n_lat, gc_ref[...], gl_ref[0]) * o


def _out_proj0(x_rows, mix_a, lin_finish, w_a, w_b, mod_l, mod_c, n_lat, n_rows, tm, tall=None,
               row_block0=0, prior=None):
    bsz, _, d = x_rows.shape
    tall = n_rows if tall is None else tall
    lin_specs, lin_args = _lin_finish_specs(*lin_finish, tm, row_block0)
    in_specs = [pl.BlockSpec((1, tm, d), lambda b, i: (b, i, 0)),
                pl.BlockSpec((1, tm, mix_a.shape[2]), lambda b, i: (b, row_block0 + i, 0))] + lin_specs + [
                pl.BlockSpec(w_a.shape, lambda b, i: (0, 0)),
                pl.BlockSpec(w_b.shape, lambda b, i: (0, 0))] + _mod_specs(d, (2,), 2)
    args = [x_rows, mix_a, *lin_args, w_a, w_b, mod_l, mod_c]
    aliases = {}
    if prior is not None:
        aliases = {len(args): 0}
        in_specs.append(pl.BlockSpec(memory_space=pl.ANY))
        args.append(prior)
    return pl.pallas_call(
        functools.partial(_out0_kernel, tm=tm, n_lat=n_lat),
        grid=(bsz, n_rows // tm),
        in_specs=in_specs,
        out_specs=pl.BlockSpec((1, tm, d), lambda b, i: (b, row_block0 + i, 0)),
        out_shape=jax.ShapeDtypeStruct((bsz, tall, d), F32),
        input_output_aliases=aliases,
        compiler_params=_cparams(2),
        name="out_proj_even",
    )(*args)


def _gelu_tanh(x):
    return 0.5 * x * (1.0 + jnp.tanh(math.sqrt(2.0 / math.pi) * (x + 0.044715 * (x * x * x))))


def _out1_kernel(x_ref, of_ref, ob_ref, gate_ref, nw_ref, yf_ref, yb_ref, u_ref, dsk_ref, gw_ref, gb_ref,
                 wa_ref, wb_ref, gl_ref, gc_ref, o_ref, *, tm, n_lat):
    i = pl.program_id(1)
    y = _gelu_tanh(yf_ref[...] + yb_ref[...] + dsk_ref[...] * u_ref[...])
    glu = _dot(y.astype(BF16), gw_ref[...]) + gb_ref[...]
    y = y * _sigmoid(glu)
    o = (_dot(_lin_finish(of_ref, ob_ref, gate_ref, nw_ref), wa_ref[...])
         + _dot(y.astype(BF16), wb_ref[...]))
    o_ref[0] = x_ref[0] + _row_select(i, tm, n_lat, gc_ref[...], gl_ref[0]) * o


def _out_proj1(x_all, lin_finish, y_dirs, u_t, d_skip, glu_w, glu_b, w_a, w_b, mod_l, mod_c, n_lat, n_rows, tm):
    bsz, _, d = x_all.shape
    tall = n_rows
    width = d_skip.shape[1]
    (y_f, y_b), u2 = y_dirs, u_t
    lin_specs, lin_args = _lin_finish_specs(*lin_finish, tm)
    return pl.pallas_call(
        functools.partial(_out1_kernel, tm=tm, n_lat=n_lat),
        grid=(bsz, tall // tm),
        in_specs=[pl.BlockSpec((1, tm, d), lambda b, i: (b, i, 0))] + lin_specs + [
                  pl.BlockSpec((tm, width), lambda b, i: (i, b)),
                  pl.BlockSpec((tm, width), lambda b, i: (i, b)),
                  pl.BlockSpec((tm, width), lambda b, i: (i, b)),
                  pl.BlockSpec((1, width), lambda b, i: (0, 0)),
                  pl.BlockSpec(glu_w.shape, lambda b, i: (0, 0)),
                  pl.BlockSpec((1, width), lambda b, i: (0, 0)),
                  pl.BlockSpec(w_a.shape, lambda b, i: (0, 0)),
                  pl.BlockSpec(w_b.shape, lambda b, i: (0, 0))] + _mod_specs(d, (2,), 2),
        out_specs=pl.BlockSpec((1, tm, d), lambda b, i: (b, i, 0)),
        out_shape=jax.ShapeDtypeStruct((bsz, tall, d), F32),
        compiler_params=_cparams(2),
        name="out_proj_odd",
    )(x_all, *lin_args, y_f, y_b, u2, d_skip, glu_w, glu_b, w_a, w_b, mod_l, mod_c)


def _mlp_kernel(*refs, tm, n_lat, final):
    if final:
        (x_ref, nw_ref, shl_ref, shc_ref, scl_ref, scc_ref, gl_ref, gc_ref, w1_ref, w2_ref, fw_ref,
         o_ref, h_scr, acc_scr) = refs
    else:
        (x_ref, nw_ref, shl_ref, shc_ref, scl_ref, scc_ref, gl_ref, gc_ref, w1_ref, w2_ref,
         o_ref, h_scr, acc_scr) = refs
    i = pl.program_id(1)
    j = pl.program_id(2)

    @pl.when(j == 0)
    def _():
        _store_norm_modulated(h_scr, x_ref[0], nw_ref[...], shl_ref[0], shc_ref[...], scl_ref[0], scc_ref[...],
                              i, tm, n_lat)
        acc_scr[...] = jnp.zeros_like(acc_scr)

    def partial_product():
        a = jnp.maximum(_dot(h_scr[...], w1_ref[...]), 0.0)
        return _dot((a * a).astype(BF16), w2_ref[...])

    last_j = pl.num_programs(2) - 1

    @pl.when(j < last_j)
    def _():
        acc_scr[...] += partial_product()

    @pl.when(j == last_j)
    def _():
        out = x_ref[0] + _row_select(i, tm, n_lat, gc_ref[...], gl_ref[0]) * (acc_scr[...] + partial_product())
        if final:
            out = _rms(out, fw_ref[...])
        o_ref[0] = out


def _mlp(x_all, norm_w, mod_l, mod_c, w1, w2, final_w, n_lat, tm, tf):
    bsz, tall, d = x_all.shape
    ff = w1.shape[1]
    final = final_w is not None
    in_specs = [pl.BlockSpec((1, tm, d), lambda b, i, j: (b, i, 0)),
                pl.BlockSpec((1, d), lambda b, i, j: (0, 0))]
    in_specs += _mod_specs(d, (3, 4, 5), 3)
    in_specs += [pl.BlockSpec((d, tf), lambda b, i, j: (0, j)),
                 pl.BlockSpec((tf, d), lambda b, i, j: (j, 0))]
    args = [x_all, norm_w.reshape(1, d)] + [mod_l, mod_c] * 3 + [w1, w2]
    if final:
        in_specs.append(pl.BlockSpec((1, d), lambda b, i, j: (0, 0)))
        args.append(final_w.reshape(1, d))
    return pl.pallas_call(
        functools.partial(_mlp_kernel, tm=tm, n_lat=n_lat, final=final),
        grid=(bsz, tall // tm, ff // tf),
        in_specs=in_specs,
        out_specs=pl.BlockSpec((1, tm, d), lambda b, i, j: (b, i, 0)),
        out_shape=jax.ShapeDtypeStruct((bsz, tall, d), F32),
        scratch_shapes=[pltpu.VMEM((tm, d), BF16), pltpu.VMEM((tm, d), F32)],
        compiler_params=_cparams(3),
        name="sq_relu_mlp",
    )(*args)


def _even_in_weight(w_in):
    sizes = (SSD_INNER, SSD_INNER + 2 * SSD_BC, 2 * SSD_HEADS, GLA_KEY, GLA_KEY, GLA_VAL,
             2 * GLA_GATE_RANK, GLA_VAL)
    offs = [0]
    for s in sizes:
        offs.append(offs[-1] + s)
    z, xbc, dt, q, k, v, lr, r = (w_in[:, offs[n]:offs[n + 1]] for n in range(8))
    pad = jnp.zeros((w_in.shape[0], LANE - dt.shape[1] - lr.shape[1]), w_in.dtype)
    main = jnp.concatenate([xbc, z, v, r, q * (GLA_DK ** -0.5), k], axis=1).astype(BF16)
    aux = jnp.concatenate([dt, lr, pad], axis=1).astype(BF16)
    return main, aux


def _s5_params(a_re, a_im, log_dt, b_re, b_im, c_re, c_im):
    delta = jnp.exp(log_dt.astype(F32))[..., None]
    mag = jnp.exp(a_re * delta)
    lbar_re, lbar_im = mag * jnp.cos(a_im * delta), mag * jnp.sin(a_im * delta)
    den = a_re * a_re + a_im * a_im
    zr = ((lbar_re - 1.0) * a_re + lbar_im * a_im) / den
    zi = (lbar_im * a_re - (lbar_re - 1.0) * a_im) / den
    bb_re = zr[..., None] * b_re - zi[..., None] * b_im
    bb_im = zr[..., None] * b_im + zi[..., None] * b_re
    n_slabs = S5_GROUPS // S5_SLAB
    eye = jnp.eye(S5_SLAB, dtype=F32)
    sw = S5_SLAB * S5_STATE

    def block_in(bb):
        bb = bb.reshape(2, n_slabs, S5_SLAB, S5_STATE, S5_GROUP)
        return jnp.einsum("dsgpc,gh->dsgchp", bb, eye).reshape(2, n_slabs, LANE, sw)

    def block_out(cc):
        cc = cc.reshape(n_slabs, S5_SLAB, S5_GROUP, S5_STATE)
        return jnp.einsum("sgcp,gh->sgphc", cc, eye).reshape(n_slabs, sw, LANE)

    bmat = jnp.concatenate([block_in(bb_re), block_in(bb_im)], axis=3).astype(BF16)
    cmat = jnp.concatenate([block_out(c_re), -block_out(c_im)], axis=1).astype(BF16)
    return (bmat, lbar_re.reshape(2, 1, S5_NSTATE), lbar_im.reshape(2, 1, S5_NSTATE), cmat)


def _row_segments(x_in, n_lat, tm):
    if not isinstance(x_in, tuple):
        return [(x_in, n_lat, 0, tm)]
    x_lat, x_ctx = x_in
    t_lat = _largest_divisor(n_lat, 16, 1056)
    t_ctx = _largest_divisor(x_ctx.shape[1], 16, 1056)
    assert n_lat % t_ctx == 0
    return [(x_lat, n_lat, 0, t_lat), (x_ctx, 0, n_lat // t_ctx, t_ctx)]


def _layer_even(x_in, tall, mod_l, mod_c, n_lat, tm, n_rows_out, tm_out, norm1_w, w_in, conv_w, conv_b, dt_bias,
                a_log, d_skip, ssd_norm_w, gate_w, gate_b, gla_norm_w, w_out):
    w, w_aux = _even_in_weight(w_in)
    n = w.shape[1]
    n_xbc = SSD_INNER + 2 * SSD_BC
    gates = ((n_xbc, n_xbc + SSD_INNER), (n_xbc + SSD_INNER + GLA_VAL, n_xbc + SSD_INNER + 2 * GLA_VAL))
    segments = _row_segments(x_in, n_lat, tm)
    outs = None
    for rows, n_lat_seen, row_block0, tm_seg in segments:
        outs = _project(rows, norm1_w, mod_l, mod_c, w, w_aux, n_lat_seen, tm_seg,
                        _largest_divisor(n, 2 * LANE, PROJ_TN), extra_token_major=False, silu_ranges=gates,
                        tall=tall, row_block0=row_block0, prior=outs)
    proj, aux = outs
    c_z, c_v, c_r = n_xbc // SSD_INNER, n_xbc // GLA_VAL + 1, n_xbc // GLA_VAL + 2
    c_q = (n_xbc + 3 * SSD_INNER) // GLA_KEY
    xbc = _conv_silu(proj, conv_w, conv_b, n_lat, n_xbc)

    neg_a = -jnp.exp(a_log.astype(F32))
    dt_bias = dt_bias.astype(F32)
    d_wide = jnp.repeat(d_skip.astype(F32), SSD_HEADDIM).reshape(1, SSD_INNER)
    y_f = _ssd_scan(xbc, proj, aux, dt_bias, neg_a, c_z, n_lat, 0)
    y_mix = _ssd_scan(xbc, proj, aux, dt_bias, neg_a, c_z, n_lat, 1,
                      (y_f, d_wide, ssd_norm_w.reshape(1, SSD_INNER)))

    cols = {"q": c_q, "k": c_q + 1, "v": c_v, "gate": c_r}
    gparams = [(gate_w[d].astype(BF16), gate_b[d].reshape(1, GLA_KEY).astype(F32)) for d in range(2)]
    o_dirs = _lin_scan("gla", proj, cols, gparams, n_lat, aux=aux)
    gla_finish = (o_dirs, proj, c_r, gla_norm_w.astype(F32).reshape(1, GLA_DV))

    w_out = w_out.astype(BF16)
    if not isinstance(x_in, tuple):
        return _out_proj0(x_in, y_mix, gla_finish, w_out[:SSD_INNER], w_out[SSD_INNER:], mod_l, mod_c, n_lat,
                          n_rows_out, tm_out)
    x1 = None
    for rows, n_lat_seen, row_block0, tm_seg in (segments if n_rows_out > n_lat else segments[:1]):
        x1 = _out_proj0(rows, y_mix, gla_finish, w_out[:SSD_INNER], w_out[SSD_INNER:], mod_l, mod_c,
                        n_lat_seen, rows.shape[1], tm_seg, tall=n_rows_out, row_block0=row_block0, prior=x1)
    return x1


def _layer_odd(x_all, mod_l, mod_c, n_lat, tm, n_rows_out, tm_out, norm1_w, w_in, lb, hgrn_norm_w, a_re, a_im,
               log_dt, b_re, b_im, c_re, c_im, d_skip, glu_w, glu_b, w_out):
    bsz = x_all.shape[0]
    n_main = 5 * HGRN_WIDTH
    w_main = w_in[:, :n_main].astype(BF16)
    w_u = w_in[:, n_main:].astype(BF16)
    proj, u_t = _project(x_all, norm1_w, mod_l, mod_c, w_main, w_u, n_lat, tm,
                         _largest_divisor(n_main, 2 * LANE, PROJ_TN),
                         silu_ranges=((0, HGRN_WIDTH), (4 * HGRN_WIDTH, 5 * HGRN_WIDTH)))

    lb = lb.astype(F32).reshape(2, 1, HGRN_WIDTH)
    cols = {"q": 0, "v": 1, "aux": 2, "gate": 4}
    o_dirs = _lin_scan("hgrn", proj, cols, [(lb[d], 1.0 - lb[d]) for d in range(2)], n_lat)
    hgrn_finish = (o_dirs, proj, cols["gate"], hgrn_norm_w.astype(F32).reshape(1, HGRN_DV))

    bmat, lam_re, lam_im, cmat = _s5_params(a_re.astype(F32), a_im.astype(F32), log_dt, b_re.astype(F32),
                                            b_im.astype(F32), c_re.astype(F32), c_im.astype(F32))
    y_dirs = _s5_scan(u_t, bmat, lam_re, lam_im, cmat, bsz, n_lat)

    w_out = w_out.astype(BF16)
    return _out_proj1(x_all, hgrn_finish, y_dirs, u_t, d_skip.astype(F32).reshape(1, S5_WIDTH),
                      glu_w.astype(BF16), glu_b.astype(F32).reshape(1, S5_WIDTH),
                      w_out[:HGRN_WIDTH], w_out[HGRN_WIDTH:], mod_l, mod_c, n_lat, n_rows_out, tm_out)


def kernel(x, c, ctx, c_ctx, ada_w, ada_b, norm1_w, norm2_w, ssd_gla_w_in, ssd_conv_w, ssd_conv_b, ssd_dt_bias, ssd_a_log, ssd_d, ssd_norm_w, gla_gate_w, gla_gate_b, gla_norm_w, ssd_gla_w_out, hgrn_s5_w_in, hgrn_lb_logits, hgrn_norm_w, s5_a_re, s5_a_im, s5_log_dt, s5_b_re, s5_b_im, s5_c_re, s5_c_im, s5_d, s5_glu_w, s5_glu_b, hgrn_s5_w_out, mlp_w1, mlp_w2, final_norm_w):
    bsz, n_lat, d = x.shape
    ctx_len = ctx.shape[1]
    depth = ada_w.shape[0]
    tall = n_lat + ctx_len
    assert bsz % 8 == 0 and ctx_len % SCAN_BLOCK == 0 and n_lat % SCAN_BLOCK == 0 and n_lat % GRID_W == 0
    tf = 2048

    n_rows = -(-(bsz + 1) // 8) * 8
    cvec = jnp.concatenate([c, c_ctx[None, :], jnp.zeros((n_rows - bsz - 1, d), c.dtype)], axis=0)
    mod = _modulation(cvec.astype(F32), ada_w, ada_b)

    p_lb = jax.nn.softmax(hgrn_lb_logits.astype(F32), axis=0)
    lb_all = jnp.cumsum(p_lb, axis=0) - p_lb[0]

    x_all = (x.astype(F32), ctx.astype(F32))
    tm_all = _largest_divisor(tall, 16, 1056)
    for layer in range(depth):
        j = layer // 2
        last = layer == depth - 1
        n_rows = n_lat if last else tall
        tm_out = _largest_divisor(n_rows, 16, 1056)
        mod_l = mod[layer, :bsz].reshape(bsz, 1, N_MOD * d)
        mod_c = mod[layer, bsz:bsz + 1]
        if layer % 2 == 0:
            x_all = _layer_even(x_all, tall, mod_l, mod_c, n_lat, tm_all, n_rows, tm_out, norm1_w[layer],
                                ssd_gla_w_in[j], ssd_conv_w[j], ssd_conv_b[j], ssd_dt_bias[j], ssd_a_log[j],
                                ssd_d[j], ssd_norm_w[j], gla_gate_w[j], gla_gate_b[j], gla_norm_w[j],
                                ssd_gla_w_out[j])
        else:
            x_all = _layer_odd(x_all, mod_l, mod_c, n_lat, tm_all, n_rows, tm_out, norm1_w[layer],
                               hgrn_s5_w_in[j], lb_all[layer], hgrn_norm_w[j], s5_a_re[j], s5_a_im[j],
                               s5_log_dt[j], s5_b_re[j], s5_b_im[j], s5_c_re[j], s5_c_im[j], s5_d[j],
                               s5_glu_w[j], s5_glu_b[j], hgrn_s5_w_out[j])
        x_all = _mlp(x_all, norm2_w[layer], mod_l, mod_c, mlp_w1[layer].astype(BF16),
                     mlp_w2[layer].astype(BF16), final_norm_w if last else None, n_lat, tm_out, tf)
    return x_all.astype(x.dtype)
```
